```python
import math
import jax, jax.numpy as jnp
from jax import lax
import numpy as np

D_MODEL = 1024
BATCH = 4
SEQ = 4096
DEPTH = 2
DEC_BATCH = 8
DEC_SEQ = 16
PAST_LEN = 4096

CHUNK = 64
GMLP_CHUNK = 128
GMLP_WIDTH = D_MODEL // 2
GMLP_HEADS = 4
GMLP_HEAD_DIM = GMLP_WIDTH // GMLP_HEADS
SSM_WIDTH = D_MODEL - GMLP_WIDTH
SSM_GROUP = 16
SSM_GROUPS = SSM_WIDTH // SSM_GROUP
SSM_STATE = 64
MIX_WIDTH = GMLP_WIDTH + SSM_WIDTH
IN_WIDTH = 2 * GMLP_WIDTH + SSM_WIDTH
N_EXPERTS = 16
N_EXPERT_GROUPS = 4
EXPERTS_PER_GROUP = N_EXPERTS // N_EXPERT_GROUPS
TOP_K = 2
D_EXPERT = D_MODEL // 2
EPS = 1e-6
DT_MIN = 1e-3
DT_MAX = 1e-1

kernel_name = 'hybrid_gmlp_s5_moe_stream_step'

F32 = jnp.float32


def rmsnorm(x, g):
    xf = x.astype(F32)
    y = xf * lax.rsqrt(jnp.mean(xf * xf, axis=-1, keepdims=True) + EPS)
    return (y * g.astype(F32)).astype(x.dtype)


def gmlp_mix(uv, v_gain, w_s, b_s):
    z = jax.nn.gelu(uv.astype(F32))
    u, v = z[..., :GMLP_WIDTH], z[..., GMLP_WIDTH:]
    vc = v - jnp.mean(v, axis=-1, keepdims=True)
    vn = vc * lax.rsqrt(jnp.mean(vc * vc, axis=-1, keepdims=True) + EPS) * v_gain.astype(F32)
    bt, seq_len, _ = vn.shape
    pad = (-seq_len) % GMLP_CHUNK
    n_chunks = (seq_len + pad) // GMLP_CHUNK
    vp = jnp.pad(vn, ((0, 0), (0, pad), (0, 0)))
    vp = vp.reshape(bt, n_chunks, GMLP_CHUNK, GMLP_HEADS, GMLP_HEAD_DIM)
    pos = jnp.arange(GMLP_CHUNK)
    mask = (pos[None, :] // CHUNK) <= (pos[:, None] // CHUNK)
    w = jnp.where(mask[None], w_s.astype(F32), 0.0)
    mixed = jnp.einsum('hij,bcjhd->bcihd', w, vp) + b_s.astype(F32).T[None, None, :, :, None]
    mixed = mixed.reshape(bt, n_chunks * GMLP_CHUNK, GMLP_WIDTH)[:, :seq_len]
    return u * mixed, vn


def s5_combine(e1, e2):
    ar1, ai1, br1, bi1 = e1
    ar2, ai2, br2, bi2 = e2
    ar = ar1 * ar2 - ai1 * ai2
    ai = ar1 * ai2 + ai1 * ar2
    br = ar2 * br1 - ai2 * bi1 + br2
    bi = ar2 * bi1 + ai2 * br1 + bi2
    return (ar, ai, br, bi)


def s5_mix(u, h0_re, h0_im, a_re, a_im, log_dt, b_re, b_im, c_re, c_im, d_skip):
    a_re = a_re.astype(F32)
    a_im = a_im.astype(F32)
    dt = jnp.exp(log_dt.astype(F32))[:, None]
    mag = jnp.exp(a_re * dt)
    ang = a_im * dt
    lb_re = mag * jnp.cos(ang)
    lb_im = mag * jnp.sin(ang)
    num_re = lb_re - 1.0
    num_im = lb_im
    den = a_re * a_re + a_im * a_im
    coef_re = (num_re * a_re + num_im * a_im) / den
    coef_im = (num_im * a_re - num_re * a_im) / den
    b_re = b_re.astype(F32)
    b_im = b_im.astype(F32)
    bb_re = coef_re[..., None] * b_re - coef_im[..., None] * b_im
    bb_im = coef_re[..., None] * b_im + coef_im[..., None] * b_re
    bu_re = jnp.einsum('blgq,gpq->blgp', u, bb_re)
    bu_im = jnp.einsum('blgq,gpq->blgp', u, bb_im)
    elems = (jnp.broadcast_to(lb_re, bu_re.shape), jnp.broadcast_to(lb_im, bu_re.shape), bu_re, bu_im)
    p_re, p_im, h_re, h_im = lax.associative_scan(s5_combine, elems, axis=1)
    if h0_re is not None:
        s_re = h0_re.astype(F32)[:, None]
        s_im = h0_im.astype(F32)[:, None]
        h_re = h_re + p_re * s_re - p_im * s_im
        h_im = h_im + p_re * s_im + p_im * s_re
    y = (jnp.einsum('blgp,gqp->blgq', h_re, c_re.astype(F32))
         - jnp.einsum('blgp,gqp->blgq', h_im, c_im.astype(F32))
         + d_skip.astype(F32) * u)
    return y, h_re[:, -1], h_im[:, -1]


def moe(h, w_router, b_router, w_gate, w_up, w_down):
    logits = h.astype(F32) @ w_router.astype(F32) + b_router.astype(F32)
    scores = jax.nn.softmax(logits, axis=-1)
    n_tok = scores.shape[0]
    grouped = scores.reshape(n_tok, N_EXPERT_GROUPS, EXPERTS_PER_GROUP)
    group_score = jnp.sum(lax.top_k(grouped, TOP_K)[0], axis=-1)
    g_idx = jnp.argmax(group_score, axis=-1)
    in_group = (jnp.arange(N_EXPERTS) // EXPERTS_PER_GROUP)[None, :] == g_idx[:, None]
    vals, idx = lax.top_k(jnp.where(in_group, scores, -jnp.inf), TOP_K)
    weights = vals / jnp.sum(vals, axis=-1, keepdims=True)
    gate = jnp.sum(weights[..., None] * jax.nn.one_hot(idx, N_EXPERTS, dtype=F32), axis=1)
    y = jnp.zeros(h.shape, F32)
    for e in range(N_EXPERTS):
        he = jax.nn.silu(h @ w_gate[e]) * (h @ w_up[e])
        y = y + gate[:, e:e + 1] * (he @ w_down[e]).astype(F32)
    return y.astype(h.dtype)


def layer(x, c, h0_re, h0_im, w_ada, b_ada, g_mix, g_ffn, w_in, v_gain, w_s, b_s,
          a_re, a_im, log_dt, b_re, b_im, c_re, c_im, d_skip, w_glu, b_glu, w_out,
          w_router, b_router, w_gate, w_up, w_down):
    bt, seq_len, _ = x.shape
    mod = (jax.nn.silu(c) @ w_ada + b_ada)[:, None, :]
    shift_m, scale_m, gate_m, shift_f, scale_f, gate_f = jnp.split(mod, 6, axis=-1)
    h = rmsnorm(x, g_mix) * (1 + scale_m) + shift_m
    proj = h @ w_in
    a_out, v_rows = gmlp_mix(proj[..., :2 * GMLP_WIDTH], v_gain, w_s, b_s)
    ub = proj[..., 2 * GMLP_WIDTH:].reshape(bt, seq_len, SSM_GROUPS, SSM_GROUP).astype(F32)
    y_ssm, h_re, h_im = s5_mix(ub, h0_re, h0_im, a_re, a_im, log_dt, b_re, b_im, c_re, c_im, d_skip)
    y_ssm = jax.nn.gelu(y_ssm.reshape(bt, seq_len, SSM_WIDTH))
    b_out = y_ssm * jax.nn.sigmoid(y_ssm @ w_glu.astype(F32) + b_glu.astype(F32))
    mixed = jnp.concatenate([a_out, b_out], axis=-1).astype(x.dtype)
    x = x + gate_m * (mixed @ w_out)
    h2 = rmsnorm(x, g_ffn) * (1 + scale_f) + shift_f
    ffn = moe(h2.reshape(bt * seq_len, D_MODEL), w_router, b_router, w_gate, w_up, w_down)
    x = x + gate_f * ffn.reshape(bt, seq_len, D_MODEL)
    return x, h_re, h_im, v_rows.astype(x.dtype)


def setup_inputs(seed: int = 0) -> dict:
    key = jax.random.key(seed)
    ks = jax.random.split(key, 32)

    def nrm(k, shape, scale):
        return jax.random.normal(k, shape, F32) * scale

    n_idx = jnp.arange(SSM_STATE, dtype=F32)
    ssm_state_shape = (DEPTH, DEC_BATCH, SSM_GROUPS, SSM_STATE)
    return {
        'x_prompt': nrm(ks[0], (BATCH, SEQ, D_MODEL), 1.0),
        'x_sample': nrm(ks[1], (DEC_BATCH, DEC_SEQ, D_MODEL), 1.0),
        'c_prompt': nrm(ks[2], (BATCH, D_MODEL), 1.0),
        'c_sample': nrm(ks[3], (DEC_BATCH, D_MODEL), 1.0),
        'state_s5_re': nrm(ks[4], ssm_state_shape, 0.5),
        'state_s5_im': nrm(ks[5], ssm_state_shape, 0.5),
        'w_ada': nrm(ks[6], (DEPTH, D_MODEL, 6 * D_MODEL), 0.5 * D_MODEL ** -0.5),
        'b_ada': nrm(ks[7], (DEPTH, 6 * D_MODEL), 0.02),
        'g_norm_mix': 1.0 + nrm(ks[8], (DEPTH, D_MODEL), 0.02),
        'g_norm_ffn': 1.0 + nrm(ks[9], (DEPTH, D_MODEL), 0.02),
        'w_in': nrm(ks[10], (DEPTH, D_MODEL, IN_WIDTH), D_MODEL ** -0.5),
        'gmlp_v_gain': 1.0 + nrm(ks[11], (DEPTH, GMLP_WIDTH), 0.02),
        'gmlp_w_spatial': nrm(ks[12], (DEPTH, GMLP_HEADS, GMLP_CHUNK, GMLP_CHUNK), 0.5 * GMLP_CHUNK ** -0.5),
        'gmlp_b_spatial': 1.0 + nrm(ks[13], (DEPTH, GMLP_HEADS, GMLP_CHUNK), 0.1),
        's5_a_re': -0.5 + nrm(ks[14], (DEPTH, SSM_GROUPS, SSM_STATE), 0.01),
        's5_a_im': math.pi * n_idx + nrm(ks[15], (DEPTH, SSM_GROUPS, SSM_STATE), 0.01),
        's5_log_dt': jax.random.uniform(ks[16], (DEPTH, SSM_GROUPS), F32, math.log(DT_MIN), math.log(DT_MAX)),
        's5_b_re': nrm(ks[17], (DEPTH, SSM_GROUPS, SSM_STATE, SSM_GROUP), (2 * SSM_GROUP) ** -0.5),
        's5_b_im': nrm(ks[18], (DEPTH, SSM_GROUPS, SSM_STATE, SSM_GROUP), (2 * SSM_GROUP) ** -0.5),
        's5_c_re': nrm(ks[19], (DEPTH, SSM_GROUPS, SSM_GROUP, SSM_STATE), SSM_STATE ** -0.5),
        's5_c_im': nrm(ks[20], (DEPTH, SSM_GROUPS, SSM_GROUP, SSM_STATE), SSM_STATE ** -0.5),
        's5_d': nrm(ks[21], (DEPTH, SSM_GROUPS, SSM_GROUP), 1.0),
        's5_w_glu': nrm(ks[22], (DEPTH, SSM_WIDTH, SSM_WIDTH), SSM_WIDTH ** -0.5),
        's5_b_glu': nrm(ks[23], (DEPTH, SSM_WIDTH), 0.02),
        'w_out': nrm(ks[24], (DEPTH, MIX_WIDTH, D_MODEL), MIX_WIDTH ** -0.5),
        'w_router': nrm(ks[25], (D_MODEL, N_EXPERTS), D_MODEL ** -0.5),
        'b_router': nrm(ks[26], (N_EXPERTS,), 0.01),
        'w_gate': nrm(ks[27], (DEPTH, N_EXPERTS, D_MODEL, D_EXPERT), D_MODEL ** -0.5),
        'w_up': nrm(ks[28], (DEPTH, N_EXPERTS, D_MODEL, D_EXPERT), D_MODEL ** -0.5),
        'w_down': nrm(ks[29], (DEPTH, N_EXPERTS, D_EXPERT, D_MODEL), D_EXPERT ** -0.5),
        'g_final': 1.0 + nrm(ks[30], (D_MODEL,), 0.02),
    }


def reference(x_prompt, x_sample, c_prompt, c_sample, state_s5_re, state_s5_im,
              w_ada, b_ada, g_norm_mix, g_norm_ffn, w_in, gmlp_v_gain, gmlp_w_spatial, gmlp_b_spatial,
              s5_a_re, s5_a_im, s5_log_dt, s5_b_re, s5_b_im, s5_c_re, s5_c_im, s5_d, s5_w_glu, s5_b_glu,
              w_out, w_router, b_router, w_gate, w_up, w_down, g_final):
    xp = x_prompt
    xs = x_sample
    sp_re, sp_im, ss_re, ss_im, v_new = [], [], [], [], []
    for l in range(DEPTH):
        p = (w_ada[l], b_ada[l], g_norm_mix[l], g_norm_ffn[l], w_in[l], gmlp_v_gain[l],
             gmlp_w_spatial[l], gmlp_b_spatial[l], s5_a_re[l], s5_a_im[l], s5_log_dt[l],
             s5_b_re[l], s5_b_im[l], s5_c_re[l], s5_c_im[l], s5_d[l], s5_w_glu[l], s5_b_glu[l],
             w_out[l], w_router, b_router, w_gate[l], w_up[l], w_down[l])
        xp, hp_re, hp_im, _ = layer(xp, c_prompt, None, None, *p)
        xs, hs_re, hs_im, vs = layer(xs, c_sample, state_s5_re[l], state_s5_im[l], *p)
        sp_re.append(hp_re)
        sp_im.append(hp_im)
        ss_re.append(hs_re)
        ss_im.append(hs_im)
        v_new.append(vs)
    y_prompt = rmsnorm(xp, g_final)
    y_sample = rmsnorm(xs, g_final)
    return (y_prompt, y_sample, jnp.stack(sp_re), jnp.stack(sp_im), jnp.stack(ss_re), jnp.stack(ss_im), jnp.stack(v_new))
```

```python
import functools

import jax
import jax.numpy as jnp
from jax import lax
from jax.experimental import pallas as pl
from jax.experimental.pallas import tpu as pltpu

F32 = jnp.float32
BF16 = jnp.bfloat16

D_MODEL = 1024
DEPTH = 2
CHUNK = 64
GMLP_CHUNK = 128
GMLP_WIDTH = 512
GMLP_HEADS = 4
GMLP_HEAD_DIM = 128
SSM_WIDTH = 512
SSM_GROUP = 16
SSM_GROUPS = 32
SSM_STATE = 64
IN_WIDTH = 1536
N_EXPERTS = 16
EXPERTS_PER_GROUP = 4
N_EXPERT_GROUPS = 4
D_EXPERT = 512
EPS = 1e-6

LANES = 128
S5_CHUNK = 16
GROUP_BLOCKS = 4
GROUPS_PER_BLOCK = SSM_GROUPS // GROUP_BLOCKS
STATE_BLOCK = GROUPS_PER_BLOCK * SSM_STATE
S5_ROW = S5_CHUNK * LANES
ADA_ROWS = 16
VMEM_LIMIT = 56 * 1024 * 1024


def _cparams(*sem):
    return pltpu.CompilerParams(dimension_semantics=sem, vmem_limit_bytes=VMEM_LIMIT)


def _ada_kernel(c_ref, w_ref, b_ref, o_ref):
    c = c_ref[...]
    s = (c * jax.nn.sigmoid(c)).astype(BF16)
    o_ref[0] = jnp.dot(s, w_ref[0].astype(BF16), preferred_element_type=F32) + b_ref[0]


def _ada(c_all, w_ada, b_ada):
    nblk = 6
    return pl.pallas_call(
        _ada_kernel,
        grid=(DEPTH, nblk),
        in_specs=[
            pl.BlockSpec((ADA_ROWS, D_MODEL), lambda l, j: (0, 0)),
            pl.BlockSpec((1, D_MODEL, D_MODEL), lambda l, j: (l, 0, j)),
            pl.BlockSpec((1, 1, D_MODEL), lambda l, j: (l, 0, j)),
        ],
        out_specs=pl.BlockSpec((1, ADA_ROWS, D_MODEL), lambda l, j: (l, 0, j)),
        out_shape=jax.ShapeDtypeStruct((DEPTH, ADA_ROWS, 6 * D_MODEL), F32),
        compiler_params=_cparams("parallel", "parallel"),
        name="ada",
    )(c_all, w_ada, b_ada.reshape(DEPTH, 1, 6 * D_MODEL))


def _prep_kernel(are_ref, aim_ref, ldt_ref, bre_ref, bim_ref, cre_ref, cim_ref,
                 msre_ref, msim_ref, nt_ref, wrev_ref, a16re_ref, a16im_ref):
    a_re = are_ref[0]
    a_im = aim_ref[0]
    dt = jnp.exp(ldt_ref[0])
    rho = a_re * dt
    th = a_im * dt
    kk = jnp.minimum(lax.broadcasted_iota(jnp.int32, (24, STATE_BLOCK), 0), S5_CHUNK).astype(F32)
    mag = jnp.exp(kk * rho)
    pw_re = mag * jnp.cos(kk * th)
    pw_im = mag * jnp.sin(kk * th)

    lb_re = pw_re[1:2]
    lb_im = pw_im[1:2]
    num_re = lb_re - 1.0
    num_im = lb_im
    den = a_re * a_re + a_im * a_im
    coef_re = (num_re * a_re + num_im * a_im) / den
    coef_im = (num_im * a_re - num_re * a_im) / den
    b_re = bre_ref[0]
    b_im = bim_ref[0]
    bb_re = coef_re * b_re - coef_im * b_im
    bb_im = coef_re * b_im + coef_im * b_re

    rows = lax.broadcasted_iota(jnp.int32, (LANES, STATE_BLOCK), 0)
    cols = lax.broadcasted_iota(jnp.int32, (LANES, STATE_BLOCK), 1)
    same_group = (rows >> 4) == (cols >> 6)

    def blockdiag(x16):
        return jnp.where(same_group, jnp.concatenate([x16] * GROUPS_PER_BLOCK, axis=0), 0.0)

    for s in range(S5_CHUNK):
        k = S5_CHUNK - 1 - s
        p_re = pw_re[k:k + 1]
        p_im = pw_im[k:k + 1]
        msre_ref[0, s * LANES:(s + 1) * LANES, :] = blockdiag(p_re * bb_re - p_im * bb_im).astype(BF16)
        msim_ref[0, s * LANES:(s + 1) * LANES, :] = blockdiag(p_re * bb_im + p_im * bb_re).astype(BF16)

    bcat = jnp.concatenate([blockdiag(bb_re), blockdiag(bb_im)], axis=1)
    c_re = cre_ref[0]
    c_im = cim_ref[0]
    for k in range(S5_CHUNK + 1):
        p_re = pw_re[k:k + 1]
        p_im = pw_im[k:k + 1]
        cl = jnp.concatenate([blockdiag(c_re * p_re - c_im * p_im),
                              -blockdiag(c_re * p_im + c_im * p_re)], axis=1)
        if k >= 1:
            nt_ref[0, (k - 1) * LANES:k * LANES, :] = cl.astype(BF16)
        if k < S5_CHUNK:
            wl = lax.dot_general(bcat, cl, (((1,), (1,)), ((), ())),
                                 precision=lax.Precision.HIGHEST, preferred_element_type=F32)
            j = S5_CHUNK - 1 - k
            wrev_ref[0, j * LANES:(j + 1) * LANES, :] = wl.astype(BF16)

    a16re_ref[0] = pw_re[S5_CHUNK:S5_CHUNK + 1]
    a16im_ref[0] = pw_im[S5_CHUNK:S5_CHUNK + 1]


def _s5_prep(a_re, a_im, log_dt, b_re, b_im, c_re, c_im):
    nstate = SSM_GROUPS * SSM_STATE

    def lane_row(v):
        return v.reshape(GROUP_BLOCKS, 1, STATE_BLOCK)

    def rows16(v):
        return v.reshape(SSM_GROUP, GROUP_BLOCKS, STATE_BLOCK).transpose(1, 0, 2)

    ldt = jnp.repeat(log_dt, SSM_STATE).reshape(SSM_GROUPS, SSM_STATE)
    bt_re = rows16(b_re.transpose(2, 0, 1).reshape(SSM_GROUP, nstate))
    bt_im = rows16(b_im.transpose(2, 0, 1).reshape(SSM_GROUP, nstate))
    ct_re = rows16(c_re.transpose(1, 0, 2).reshape(SSM_GROUP, nstate))
    ct_im = rows16(c_im.transpose(1, 0, 2).reshape(SSM_GROUP, nstate))
    row_spec = pl.BlockSpec((1, 1, STATE_BLOCK), lambda g: (g, 0, 0))
    r16_spec = pl.BlockSpec((1, SSM_GROUP, STATE_BLOCK), lambda g: (g, 0, 0))
    return pl.pallas_call(
        _prep_kernel,
        grid=(GROUP_BLOCKS,),
        in_specs=[row_spec, row_spec, row_spec, r16_spec, r16_spec, r16_spec, r16_spec],
        out_specs=[
            pl.BlockSpec((1, S5_ROW, STATE_BLOCK), lambda g: (g, 0, 0)),
            pl.BlockSpec((1, S5_ROW, STATE_BLOCK), lambda g: (g, 0, 0)),
            pl.BlockSpec((1, S5_ROW, 2 * STATE_BLOCK), lambda g: (g, 0, 0)),
            pl.BlockSpec((1, S5_ROW, LANES), lambda g: (g, 0, 0)),
            row_spec, row_spec,
        ],
        out_shape=[
            jax.ShapeDtypeStruct((GROUP_BLOCKS, S5_ROW, STATE_BLOCK), BF16),
            jax.ShapeDtypeStruct((GROUP_BLOCKS, S5_ROW, STATE_BLOCK), BF16),
            jax.ShapeDtypeStruct((GROUP_BLOCKS, S5_ROW, 2 * STATE_BLOCK), BF16),
            jax.ShapeDtypeStruct((GROUP_BLOCKS, S5_ROW, LANES), BF16),
            jax.ShapeDtypeStruct((GROUP_BLOCKS, 1, STATE_BLOCK), F32),
            jax.ShapeDtypeStruct((GROUP_BLOCKS, 1, STATE_BLOCK), F32),
        ],
        compiler_params=_cparams("parallel"),
        name="s5_prep",
    )(lane_row(a_re), lane_row(a_im), lane_row(ldt), bt_re, bt_im, ct_re, ct_im)


def _rms(x, g):
    return x * lax.rsqrt(jnp.mean(x * x, axis=-1, keepdims=True) + EPS) * g


def _mixer_in_kernel(x_ref, mod_ref, g_ref, win_ref, vg_ref, ws_ref, bs_ref,
                     a_ref, u_ref, *v_ref, tm):
    x = x_ref[...]
    mod = mod_ref[0]
    shift = mod[:, :D_MODEL]
    scale = mod[:, D_MODEL:]
    h = _rms(x, g_ref[...]) * (1.0 + scale) + shift
    proj = jnp.dot(h.astype(BF16), win_ref[...], preferred_element_type=F32)
    z = jax.nn.gelu(proj[:, :2 * GMLP_WIDTH])
    u = z[:, :GMLP_WIDTH]
    v = z[:, GMLP_WIDTH:]
    vc = v - jnp.mean(v, axis=-1, keepdims=True)
    vn = vc * lax.rsqrt(jnp.mean(vc * vc, axis=-1, keepdims=True) + EPS) * vg_ref[...]
    if v_ref:
        v_ref[0][...] = vn
    vb = vn.astype(BF16)
    bias = bs_ref[...]
    for c in range(tm // GMLP_CHUNK):
        r0 = c * GMLP_CHUNK
        for hh in range(GMLP_HEADS):
            l0 = hh * GMLP_HEAD_DIM
            mixed = jnp.dot(ws_ref[hh], vb[r0:r0 + GMLP_CHUNK, l0:l0 + GMLP_HEAD_DIM],
                            preferred_element_type=F32) + bias[:, l0:l0 + GMLP_HEAD_DIM]
            a_ref[r0:r0 + GMLP_CHUNK, l0:l0 + GMLP_HEAD_DIM] = (
                u[r0:r0 + GMLP_CHUNK, l0:l0 + GMLP_HEAD_DIM] * mixed).astype(BF16)
    for gb in range(GROUP_BLOCKS):
        l0 = 2 * GMLP_WIDTH + gb * LANES
        u_ref[gb] = proj[:, l0:l0 + LANES]


def _mixer_in(x2d, mod, g_mix, w_in, v_gain, ws, bs, *, tm, tiles_per_mod, want_v):
    t = x2d.shape[0]
    rmod = mod.shape[1]
    const2 = lambda i: (0, 0)
    out_shape = [jax.ShapeDtypeStruct((t, GMLP_WIDTH), BF16),
                 jax.ShapeDtypeStruct((GROUP_BLOCKS, t, LANES), F32)]
    out_specs = [pl.BlockSpec((tm, GMLP_WIDTH), lambda i: (i, 0)),
                 pl.BlockSpec((GROUP_BLOCKS, tm, LANES), lambda i: (0, i, 0))]
    if want_v:
        out_shape.append(jax.ShapeDtypeStruct((t, GMLP_WIDTH), F32))
        out_specs.append(pl.BlockSpec((tm, GMLP_WIDTH), lambda i: (i, 0)))
    return pl.pallas_call(
        functools.partial(_mixer_in_kernel, tm=tm),
        grid=(t // tm,),
        in_specs=[
            pl.BlockSpec((tm, D_MODEL), lambda i: (i, 0)),
            pl.BlockSpec((1, rmod, 2 * D_MODEL), lambda i: (i // tiles_per_mod, 0, 0)),
            pl.BlockSpec((1, D_MODEL), const2),
            pl.BlockSpec((D_MODEL, IN_WIDTH), const2),
            pl.BlockSpec((1, GMLP_WIDTH), const2),
            pl.BlockSpec((GMLP_HEADS, GMLP_CHUNK, GMLP_CHUNK), lambda i: (0, 0, 0)),
            pl.BlockSpec((GMLP_CHUNK, GMLP_WIDTH), const2),
        ],
        out_specs=out_specs,
        out_shape=out_shape,
        compiler_params=_cparams("parallel"),
        name="mixer_in",
    )(x2d, mod, g_mix, w_in, v_gain, ws, bs)


def _s5_kernel(u_ref, msre_ref, msim_ref, nt_ref, wrev_ref, a16re_ref, a16im_ref, d_ref,
               h0re_ref, h0im_ref, y_ref, hnre_ref, hnim_ref, *scratch, scan):
    u = u_ref[0]
    ub = u.astype(BF16)
    s_re = jnp.dot(ub, msre_ref[0], preferred_element_type=F32)
    s_im = jnp.dot(ub, msim_ref[0], preferred_element_type=F32)
    a_re = a16re_ref[0]
    a_im = a16im_ref[0]
    h0_re = h0re_ref[0, 0]
    h0_im = h0im_ref[0, 0]
    if scan:
        sre_scr, sim_scr, hre_scr, him_scr = scratch
        sre_scr[...] = s_re
        sim_scr[...] = s_im

        def body(r, carry):
            hr, hi = carry
            hre_scr[pl.ds(r, 1), :] = hr
            him_scr[pl.ds(r, 1), :] = hi
            sr = sre_scr[pl.ds(r, 1), :]
            si = sim_scr[pl.ds(r, 1), :]
            return (a_re * hr - a_im * hi + sr, a_re * hi + a_im * hr + si)

        hn_re, hn_im = lax.fori_loop(0, u.shape[0], body, (h0_re, h0_im))
        h_re = hre_scr[...]
        h_im = him_scr[...]
    else:
        h_re = h0_re
        h_im = h0_im
        hn_re = a_re * h_re - a_im * h_im + s_re
        hn_im = a_re * h_im + a_im * h_re + s_im
    hnre_ref[0, 0] = hn_re
    hnim_ref[0, 0] = hn_im
    hcat = jnp.concatenate([h_re, h_im], axis=1).astype(BF16)
    inter = lax.dot_general(hcat, nt_ref[0], (((1,), (1,)), ((), ())),
                            preferred_element_type=F32)
    d = d_ref[0]
    for t in range(S5_CHUNK):
        k0 = (S5_CHUNK - 1 - t) * LANES
        intra = jnp.dot(ub[:, :(t + 1) * LANES], wrev_ref[0, k0:, :], preferred_element_type=F32)
        sl = slice(t * LANES, (t + 1) * LANES)
        y_ref[0, :, sl] = intra + inter[:, sl] + d[:, sl] * u[:, sl]


def _s5(u4, prep, d_tiled, h0_re, h0_im, *, rows, nblk, scan):
    ms_re, ms_im, nt, wrev, a16_re, a16_im = prep
    rh = h0_re.shape[2]
    wspec = lambda shape: pl.BlockSpec((1,) + shape, lambda g, b: (g, 0, 0))
    hspec = pl.BlockSpec((1, 1, rh, STATE_BLOCK), lambda g, b: (g, b, 0, 0))
    scratch = [pltpu.VMEM((rows, STATE_BLOCK), F32)] * 4 if scan else []
    return pl.pallas_call(
        functools.partial(_s5_kernel, scan=scan),
        grid=(GROUP_BLOCKS, nblk),
        in_specs=[
            pl.BlockSpec((1, rows, S5_ROW), lambda g, b: (g, b, 0)),
            wspec((S5_ROW, STATE_BLOCK)), wspec((S5_ROW, STATE_BLOCK)),
            wspec((S5_ROW, 2 * STATE_BLOCK)), wspec((S5_ROW, LANES)),
            wspec((1, STATE_BLOCK)), wspec((1, STATE_BLOCK)), wspec((1, S5_ROW)),
            hspec, hspec,
        ],
        out_specs=[pl.BlockSpec((1, rows, S5_ROW), lambda g, b: (g, b, 0)), hspec, hspec],
        out_shape=[
            jax.ShapeDtypeStruct(u4.shape, F32),
            jax.ShapeDtypeStruct(h0_re.shape, F32),
            jax.ShapeDtypeStruct(h0_re.shape, F32),
        ],
        scratch_shapes=scratch,
        compiler_params=_cparams("parallel", "parallel"),
        name="s5",
    )(u4, ms_re, ms_im, nt, wrev, a16_re, a16_im, d_tiled, h0_re, h0_im)


def _split_bf16(x):
    hi = x.astype(BF16)
    return hi, (x - hi.astype(F32)).astype(BF16)


def _top2_of4(a):
    m1 = jnp.maximum(jnp.maximum(a[0], a[1]), jnp.maximum(a[2], a[3]))
    i1 = jnp.where(a[0] == m1, 0, jnp.where(a[1] == m1, 1, jnp.where(a[2] == m1, 2, 3)))
    b = [jnp.where(i1 == j, -jnp.inf, a[j]) for j in range(4)]
    m2 = jnp.maximum(jnp.maximum(b[0], b[1]), jnp.maximum(b[2], b[3]))
    i2 = jnp.where(b[0] == m2, 0, jnp.where(b[1] == m2, 1, jnp.where(b[2] == m2, 2, 3)))
    return m1, i1, m2, i2


def _mixer_out_kernel(x_ref, a_ref, y_ref, mod_ref, g_ref, wglu_ref, bglu_ref, wout_ref,
                      wrh_ref, wrl_ref, br_ref,
                      xmid_ref, h2_ref, lo_ref, hi_ref, glo_ref, ghi_ref):
    x = x_ref[...]
    mod = mod_ref[0]
    gate_m = mod[:, :D_MODEL]
    shift_f = mod[:, D_MODEL:2 * D_MODEL]
    scale_f = mod[:, 2 * D_MODEL:]
    ys = jax.nn.gelu(jnp.concatenate([y_ref[gb] for gb in range(GROUP_BLOCKS)], axis=1))
    glu = jnp.dot(ys.astype(BF16), wglu_ref[...], preferred_element_type=F32) + bglu_ref[...]
    b_out = ys * jax.nn.sigmoid(glu)
    mixed = jnp.concatenate([a_ref[...], b_out.astype(BF16)], axis=1)
    xmid = x + gate_m * jnp.dot(mixed, wout_ref[...], preferred_element_type=F32)
    xmid_ref[...] = xmid
    h2 = _rms(xmid, g_ref[...]) * (1.0 + scale_f) + shift_f
    h2_ref[...] = h2

    h_hi, h_lo = _split_bf16(h2)
    w_hi = wrh_ref[...]
    logits = (jnp.dot(h_hi, w_hi, preferred_element_type=F32)
              + jnp.dot(h_lo, w_hi, preferred_element_type=F32)
              + jnp.dot(h_hi, wrl_ref[...], preferred_element_type=F32))
    lt = logits.T[:N_EXPERTS] + br_ref[...]
    rows = [lt[e:e + 1] for e in range(N_EXPERTS)]
    mx = functools.reduce(jnp.maximum, rows)
    ex = [jnp.exp(r - mx) for r in rows]
    tot = functools.reduce(lambda p, q: p + q, ex)
    scores = [e / tot for e in ex]
    best = None
    for g in range(N_EXPERT_GROUPS):
        m1, i1, m2, i2 = _top2_of4(scores[g * EXPERTS_PER_GROUP:(g + 1) * EXPERTS_PER_GROUP])
        cand = (m1 + m2, m1, i1 + g * EXPERTS_PER_GROUP, m2, i2 + g * EXPERTS_PER_GROUP)
        if best is None:
            best = cand
        else:
            better = cand[0] > best[0]
            best = tuple(jnp.where(better, c, b) for c, b in zip(cand, best))
    _, v1, e1, v2, e2 = best
    den = v1 + v2
    w1 = v1 / den
    w2 = v2 / den
    first_lo = e1 < e2
    lo_ref[...] = jnp.where(first_lo, e1, e2)
    hi_ref[...] = jnp.where(first_lo, e2, e1)
    glo_ref[...] = jnp.where(first_lo, w1, w2)
    ghi_ref[...] = jnp.where(first_lo, w2, w1)


def _mixer_out(x2d, a_out, y4, mod, g_ffn, w_glu, b_glu, w_out, wr_hi, wr_lo, b_r,
               *, tm, tiles_per_mod):
    t = x2d.shape[0]
    rmod = mod.shape[1]
    const2 = lambda i: (0, 0)
    tok = pl.BlockSpec((tm, D_MODEL), lambda i: (i, 0))
    row = pl.BlockSpec((1, tm), lambda i: (0, i))
    return pl.pallas_call(
        _mixer_out_kernel,
        grid=(t // tm,),
        in_specs=[
            tok,
            pl.BlockSpec((tm, GMLP_WIDTH), lambda i: (i, 0)),
            pl.BlockSpec((GROUP_BLOCKS, tm, LANES), lambda i: (0, i, 0)),
            pl.BlockSpec((1, rmod, 3 * D_MODEL), lambda i: (i // tiles_per_mod, 0, 0)),
            pl.BlockSpec((1, D_MODEL), const2),
            pl.BlockSpec((SSM_WIDTH, SSM_WIDTH), const2),
            pl.BlockSpec((1, SSM_WIDTH), const2),
            pl.BlockSpec((D_MODEL, D_MODEL), const2),
            pl.BlockSpec((D_MODEL, LANES), const2),
            pl.BlockSpec((D_MODEL, LANES), const2),
            pl.BlockSpec((N_EXPERTS, 1), const2),
        ],
        out_specs=[tok, tok, row, row, row, row],
        out_shape=[
            jax.ShapeDtypeStruct((t, D_MODEL), F32),
            jax.ShapeDtypeStruct((t, D_MODEL), F32),
            jax.ShapeDtypeStruct((1, t), jnp.int32),
            jax.ShapeDtypeStruct((1, t), jnp.int32),
            jax.ShapeDtypeStruct((1, t), F32),
            jax.ShapeDtypeStruct((1, t), F32),
        ],
        compiler_params=_cparams("parallel"),
        name="mixer_out",
    )(x2d, a_out, y4, mod, g_ffn, w_glu, b_glu, w_out, wr_hi, wr_lo, b_r)


def _moe_kernel(h2_ref, gates_ref, wg_ref, wu_ref, wd_ref, xmid_ref, mod_ref, gfin_ref,
                o_ref, acc_ref, *, final_norm):
    e = pl.program_id(1)

    @pl.when(e == 0)
    def _():
        acc_ref[...] = jnp.zeros_like(acc_ref)

    h = h2_ref[...].astype(BF16)
    he = (jax.nn.silu(jnp.dot(h, wg_ref[0], preferred_element_type=F32))
          * jnp.dot(h, wu_ref[0], preferred_element_type=F32))
    y = jnp.dot(he.astype(BF16), wd_ref[0], preferred_element_type=F32)
    gates = gates_ref[...]
    lane = lax.broadcasted_iota(jnp.int32, gates.shape, 1)
    gcol = jnp.sum(jnp.where(lane == e, gates, 0.0), axis=1, keepdims=True)
    acc_ref[...] += gcol * y

    @pl.when(e == N_EXPERTS - 1)
    def _():
        x = xmid_ref[...] + mod_ref[0] * acc_ref[...]
        if final_norm:
            x = _rms(x, gfin_ref[...])
        o_ref[...] = x


def _moe(h2, gates, wg, wu, wd, xmid, mod, g_final, *, tm, tiles_per_mod, final_norm):
    t = h2.shape[0]
    rmod = mod.shape[1]
    tok = pl.BlockSpec((tm, D_MODEL), lambda i, e: (i, 0))
    return pl.pallas_call(
        functools.partial(_moe_kernel, final_norm=final_norm),
        grid=(t // tm, N_EXPERTS),
        in_specs=[
            tok,
            pl.BlockSpec((tm, N_EXPERTS), lambda i, e: (i, 0)),
            pl.BlockSpec((1, D_MODEL, D_EXPERT), lambda i, e: (e, 0, 0)),
            pl.BlockSpec((1, D_MODEL, D_EXPERT), lambda i, e: (e, 0, 0)),
            pl.BlockSpec((1, D_EXPERT, D_MODEL), lambda i, e: (e, 0, 0)),
            tok,
            pl.BlockSpec((1, rmod, D_MODEL), lambda i, e: (i // tiles_per_mod, 0, 0)),
            pl.BlockSpec((1, D_MODEL), lambda i, e: (0, 0)),
        ],
        out_specs=tok,
        out_shape=jax.ShapeDtypeStruct((t, D_MODEL), F32),
        scratch_shapes=[pltpu.VMEM((tm, D_MODEL), F32)],
        compiler_params=_cparams("parallel", "arbitrary"),
        name="moe",
    )(h2, gates, wg, wu, wd, xmid, mod, g_final)


def _layer(x2d, mod, lw, h0_re, h0_im, *, seq_len, nseq, sample, final_norm):
    t = x2d.shape[0]
    if sample:
        tm, tiles_per_mod, tm_moe = t, 1, t
        s5_rows, s5_blocks = nseq, 1
    else:
        tm = min(512, seq_len)
        tiles_per_mod = seq_len // tm
        tm_moe = min(512, seq_len)
        s5_rows, s5_blocks = seq_len // S5_CHUNK, nseq
    outs = _mixer_in(x2d, mod[..., :2 * D_MODEL], lw["g_mix"], lw["w_in"], lw["v_gain"],
                     lw["ws_sample"] if sample else lw["ws"], lw["bs_sample"] if sample else lw["bs"],
                     tm=tm, tiles_per_mod=tiles_per_mod, want_v=sample)
    a_out, u4 = outs[0], outs[1]
    v_rows = outs[2] if sample else None
    y4, hn_re, hn_im = _s5(u4.reshape(GROUP_BLOCKS, t // S5_CHUNK, S5_ROW), lw["prep"], lw["d_tiled"],
                           h0_re, h0_im, rows=s5_rows, nblk=s5_blocks, scan=not sample)
    xmid, h2, lo, hi, glo, ghi = _mixer_out(
        x2d, a_out, y4.reshape(GROUP_BLOCKS, t, LANES), mod[..., 2 * D_MODEL:5 * D_MODEL],
        lw["g_ffn"], lw["w_glu"], lw["b_glu"], lw["w_out"], lw["wr_hi"], lw["wr_lo"], lw["b_r"],
        tm=tm, tiles_per_mod=tiles_per_mod)
    eids = jnp.arange(N_EXPERTS, dtype=jnp.int32)[None, :]
    gates = (jnp.where(lo[0][:, None] == eids, glo[0][:, None], 0.0)
             + jnp.where(hi[0][:, None] == eids, ghi[0][:, None], 0.0))
    x_new = _moe(h2, gates, lw["wg"], lw["wu"], lw["wd"], xmid, mod[..., 5 * D_MODEL:],
                 lw["g_final"], tm=tm_moe, tiles_per_mod=seq_len // tm_moe if not sample else 1,
                 final_norm=final_norm)
    return x_new, hn_re, hn_im, v_rows


def _state_out(h, nseq):
    return h.reshape(GROUP_BLOCKS, nseq, GROUPS_PER_BLOCK, SSM_STATE).transpose(1, 0, 2, 3).reshape(
        nseq, SSM_GROUPS, SSM_STATE)


def _state_in(h, nblk, rh):
    nseq = h.shape[0]
    return h.reshape(nseq, GROUP_BLOCKS, STATE_BLOCK).transpose(1, 0, 2).reshape(
        GROUP_BLOCKS, nblk, rh, STATE_BLOCK)


def kernel(x_prompt, x_sample, c_prompt, c_sample, state_s5_re, state_s5_im, w_ada, b_ada, g_norm_mix, g_norm_ffn, w_in, gmlp_v_gain, gmlp_w_spatial, gmlp_b_spatial, s5_a_re, s5_a_im, s5_log_dt, s5_b_re, s5_b_im, s5_c_re, s5_c_im, s5_d, s5_w_glu, s5_b_glu, w_out, w_router, b_router, w_gate, w_up, w_down, g_final):
    nb, seq_len, _ = x_prompt.shape
    ns, dec_len, _ = x_sample.shape
    assert dec_len == S5_CHUNK and ns * dec_len == GMLP_CHUNK and nb + ns <= ADA_ROWS
    assert seq_len % GMLP_CHUNK == 0

    c_all = jnp.concatenate([c_prompt, c_sample, jnp.zeros((ADA_ROWS - nb - ns, D_MODEL), F32)], axis=0)
    mod_all = _ada(c_all, w_ada, b_ada)

    pos = jnp.arange(GMLP_CHUNK)
    causal = (pos[None, :] // CHUNK) <= (pos[:, None] // CHUNK)
    wr_pad = jnp.pad(w_router, ((0, 0), (0, LANES - N_EXPERTS)))
    wr_hi = wr_pad.astype(BF16)
    wr_lo = (wr_pad - wr_hi.astype(F32)).astype(BF16)
    b_r = b_router.reshape(N_EXPERTS, 1)
    g_fin = g_final.reshape(1, D_MODEL)
    eye_s = jnp.eye(ns, dtype=F32)

    xp = x_prompt.reshape(nb * seq_len, D_MODEL)
    xs = x_sample.reshape(ns * dec_len, D_MODEL)
    zeros_p = jnp.zeros((GROUP_BLOCKS, nb, 1, STATE_BLOCK), F32)
    sp_re, sp_im, ss_re, ss_im, v_new = [], [], [], [], []
    for l in range(DEPTH):
        ws = jnp.where(causal[None], gmlp_w_spatial[l], 0.0)
        ws_sample = jnp.einsum("ab,hij->haibj", eye_s, ws[:, :dec_len, :dec_len]).reshape(
            GMLP_HEADS, GMLP_CHUNK, GMLP_CHUNK)
        bs = jnp.repeat(gmlp_b_spatial[l].T, GMLP_HEAD_DIM, axis=1)
        lw = dict(
            g_mix=g_norm_mix[l].reshape(1, D_MODEL), g_ffn=g_norm_ffn[l].reshape(1, D_MODEL),
            w_in=w_in[l].astype(BF16), v_gain=gmlp_v_gain[l].reshape(1, GMLP_WIDTH),
            ws=ws.astype(BF16), ws_sample=ws_sample.astype(BF16),
            bs=bs, bs_sample=jnp.tile(bs[:dec_len], (ns, 1)),
            prep=_s5_prep(s5_a_re[l], s5_a_im[l], s5_log_dt[l], s5_b_re[l], s5_b_im[l],
                          s5_c_re[l], s5_c_im[l]),
            d_tiled=jnp.tile(s5_d[l].reshape(GROUP_BLOCKS, 1, LANES), (1, 1, S5_CHUNK)),
            w_glu=s5_w_glu[l].astype(BF16), b_glu=s5_b_glu[l].reshape(1, SSM_WIDTH),
            w_out=w_out[l].astype(BF16), wr_hi=wr_hi, wr_lo=wr_lo, b_r=b_r,
            wg=w_gate[l].astype(BF16), wu=w_up[l].astype(BF16), wd=w_down[l].astype(BF16),
            g_final=g_fin,
        )
        last = l == DEPTH - 1
        mod_p = mod_all[l, :nb].reshape(nb, 1, 6 * D_MODEL)
        mod_s = jnp.repeat(mod_all[l, nb:nb + ns], dec_len, axis=0).reshape(1, ns * dec_len, 6 * D_MODEL)
        xp, hp_re, hp_im, _ = _layer(xp, mod_p, lw, zeros_p, zeros_p, seq_len=seq_len, nseq=nb,
                                     sample=False, final_norm=last)
        xs, hs_re, hs_im, vs = _layer(xs, mod_s, lw, _state_in(state_s5_re[l], 1, ns),
                                      _state_in(state_s5_im[l], 1, ns), seq_len=dec_len, nseq=ns,
                                      sample=True, final_norm=last)
        sp_re.append(_state_out(hp_re, nb))
        sp_im.append(_state_out(hp_im, nb))
        ss_re.append(_state_out(hs_re, ns))
        ss_im.append(_state_out(hs_im, ns))
        v_new.append(vs.reshape(ns, dec_len, GMLP_WIDTH))
    return (xp.reshape(nb, seq_len, D_MODEL), xs.reshape(ns, dec_len, D_MODEL),
            jnp.stack(sp_re), jnp.stack(sp_im), jnp.stack(ss_re), jnp.stack(ss_im), jnp.stack(v_new))
```

```python
import functools

import jax
import jax.numpy as jnp
from jax import lax
from jax.experimental import pallas as pl
from jax.experimental.pallas import tpu as pltpu

F32 = jnp.float32
BF16 = jnp.bfloat16

D_MODEL = 1024
DEPTH = 2
CHUNK = 64
GMLP_CHUNK = 128
GMLP_WIDTH = 512
GMLP_HEADS = 4
GMLP_HEAD_DIM = 128
SSM_WIDTH = 512
SSM_GROUP = 16
SSM_GROUPS = 32
SSM_STATE = 64
IN_WIDTH = 1536
N_EXPERTS = 16
EXPERTS_PER_GROUP = 4
N_EXPERT_GROUPS = 4
D_EXPERT = 512
EPS = 1e-6

LANES = 128
S5_CHUNK = 16
GROUP_BLOCKS = 4
GROUPS_PER_BLOCK = SSM_GROUPS // GROUP_BLOCKS
STATE_BLOCK = GROUPS_PER_BLOCK * SSM_STATE
S5_ROW = S5_CHUNK * LANES
ADA_ROWS = 16
PAIRS_PER_GROUP = 6
N_CLASSES = N_EXPERT_GROUPS * PAIRS_PER_GROUP
CLASS_ROWS = 32
_PAIRS = [(a, b) for a in range(EXPERTS_PER_GROUP) for b in range(a + 1, EXPERTS_PER_GROUP)]
CLASS_LO = [g * EXPERTS_PER_GROUP + a for g in range(N_EXPERT_GROUPS) for a, _ in _PAIRS]
CLASS_HI = [g * EXPERTS_PER_GROUP + b for g in range(N_EXPERT_GROUPS) for _, b in _PAIRS]
MOE_TILE = 256
XEXT_WIDTH = D_MODEL + LANES
VMEM_LIMIT = 56 * 1024 * 1024


def _cparams(*sem):
    return pltpu.CompilerParams(dimension_semantics=sem, vmem_limit_bytes=VMEM_LIMIT)


def _ada_kernel(c_ref, w_ref, b_ref, o_ref):
    c = c_ref[...]
    s = (c * jax.nn.sigmoid(c)).astype(BF16)
    o_ref[0] = jnp.dot(s, w_ref[0].astype(BF16), preferred_element_type=F32) + b_ref[0]


def _ada(c_all, w_ada, b_ada):
    nblk = 6
    return pl.pallas_call(
        _ada_kernel,
        grid=(DEPTH, nblk),
        in_specs=[
            pl.BlockSpec((ADA_ROWS, D_MODEL), lambda l, j: (0, 0)),
            pl.BlockSpec((1, D_MODEL, D_MODEL), lambda l, j: (l, 0, j)),
            pl.BlockSpec((1, 1, D_MODEL), lambda l, j: (l, 0, j)),
        ],
        out_specs=pl.BlockSpec((1, ADA_ROWS, D_MODEL), lambda l, j: (l, 0, j)),
        out_shape=jax.ShapeDtypeStruct((DEPTH, ADA_ROWS, 6 * D_MODEL), F32),
        compiler_params=_cparams("parallel", "parallel"),
        name="ada",
    )(c_all, w_ada, b_ada.reshape(DEPTH, 1, 6 * D_MODEL))


def _prep_kernel(are_ref, aim_ref, ldt_ref, bre_ref, bim_ref, cre_ref, cim_ref,
                 msre_ref, msim_ref, nt_ref, wrev_ref, a16re_ref, a16im_ref):
    a_re = are_ref[0]
    a_im = aim_ref[0]
    dt = jnp.exp(ldt_ref[0])
    rho = a_re * dt
    th = a_im * dt
    kk = jnp.minimum(lax.broadcasted_iota(jnp.int32, (24, STATE_BLOCK), 0), S5_CHUNK).astype(F32)
    mag = jnp.exp(kk * rho)
    pw_re = mag * jnp.cos(kk * th)
    pw_im = mag * jnp.sin(kk * th)

    lb_re = pw_re[1:2]
    lb_im = pw_im[1:2]
    num_re = lb_re - 1.0
    num_im = lb_im
    den = a_re * a_re + a_im * a_im
    coef_re = (num_re * a_re + num_im * a_im) / den
    coef_im = (num_im * a_re - num_re * a_im) / den
    b_re = bre_ref[0]
    b_im = bim_ref[0]
    bb_re = coef_re * b_re - coef_im * b_im
    bb_im = coef_re * b_im + coef_im * b_re

    rows = lax.broadcasted_iota(jnp.int32, (LANES, STATE_BLOCK), 0)
    cols = lax.broadcasted_iota(jnp.int32, (LANES, STATE_BLOCK), 1)
    same_group = (rows >> 4) == (cols >> 6)

    def blockdiag(x16):
        return jnp.where(same_group, jnp.concatenate([x16] * GROUPS_PER_BLOCK, axis=0), 0.0)

    for s in range(S5_CHUNK):
        k = S5_CHUNK - 1 - s
        p_re = pw_re[k:k + 1]
        p_im = pw_im[k:k + 1]
        msre_ref[0, s * LANES:(s + 1) * LANES, :] = blockdiag(p_re * bb_re - p_im * bb_im).astype(BF16)
        msim_ref[0, s * LANES:(s + 1) * LANES, :] = blockdiag(p_re * bb_im + p_im * bb_re).astype(BF16)

    bcat = jnp.concatenate([blockdiag(bb_re), blockdiag(bb_im)], axis=1)
    c_re = cre_ref[0]
    c_im = cim_ref[0]
    for k in range(S5_CHUNK + 1):
        p_re = pw_re[k:k + 1]
        p_im = pw_im[k:k + 1]
        cl = jnp.concatenate([blockdiag(c_re * p_re - c_im * p_im),
                              -blockdiag(c_re * p_im + c_im * p_re)], axis=1)
        if k >= 1:
            nt_ref[0, (k - 1) * LANES:k * LANES, :] = cl.astype(BF16)
        if k < S5_CHUNK:
            wl = lax.dot_general(bcat, cl, (((1,), (1,)), ((), ())),
                                 precision=lax.Precision.HIGHEST, preferred_element_type=F32)
            j = S5_CHUNK - 1 - k
            wrev_ref[0, j * LANES:(j + 1) * LANES, :] = wl.astype(BF16)

    a16re_ref[0] = pw_re[S5_CHUNK:S5_CHUNK + 1]
    a16im_ref[0] = pw_im[S5_CHUNK:S5_CHUNK + 1]


def _s5_prep(a_re, a_im, log_dt, b_re, b_im, c_re, c_im):
    nstate = SSM_GROUPS * SSM_STATE

    def lane_row(v):
        return v.reshape(GROUP_BLOCKS, 1, STATE_BLOCK)

    def rows16(v):
        return v.reshape(SSM_GROUP, GROUP_BLOCKS, STATE_BLOCK).transpose(1, 0, 2)

    ldt = jnp.repeat(log_dt, SSM_STATE).reshape(SSM_GROUPS, SSM_STATE)
    bt_re = rows16(b_re.transpose(2, 0, 1).reshape(SSM_GROUP, nstate))
    bt_im = rows16(b_im.transpose(2, 0, 1).reshape(SSM_GROUP, nstate))
    ct_re = rows16(c_re.transpose(1, 0, 2).reshape(SSM_GROUP, nstate))
    ct_im = rows16(c_im.transpose(1, 0, 2).reshape(SSM_GROUP, nstate))
    row_spec = pl.BlockSpec((1, 1, STATE_BLOCK), lambda g: (g, 0, 0))
    r16_spec = pl.BlockSpec((1, SSM_GROUP, STATE_BLOCK), lambda g: (g, 0, 0))
    return pl.pallas_call(
        _prep_kernel,
        grid=(GROUP_BLOCKS,),
        in_specs=[row_spec, row_spec, row_spec, r16_spec, r16_spec, r16_spec, r16_spec],
        out_specs=[
            pl.BlockSpec((1, S5_ROW, STATE_BLOCK), lambda g: (g, 0, 0)),
            pl.BlockSpec((1, S5_ROW, STATE_BLOCK), lambda g: (g, 0, 0)),
            pl.BlockSpec((1, S5_ROW, 2 * STATE_BLOCK), lambda g: (g, 0, 0)),
            pl.BlockSpec((1, S5_ROW, LANES), lambda g: (g, 0, 0)),
            row_spec, row_spec,
        ],
        out_shape=[
            jax.ShapeDtypeStruct((GROUP_BLOCKS, S5_ROW, STATE_BLOCK), BF16),
            jax.ShapeDtypeStruct((GROUP_BLOCKS, S5_ROW, STATE_BLOCK), BF16),
            jax.ShapeDtypeStruct((GROUP_BLOCKS, S5_ROW, 2 * STATE_BLOCK), BF16),
            jax.ShapeDtypeStruct((GROUP_BLOCKS, S5_ROW, LANES), BF16),
            jax.ShapeDtypeStruct((GROUP_BLOCKS, 1, STATE_BLOCK), F32),
            jax.ShapeDtypeStruct((GROUP_BLOCKS, 1, STATE_BLOCK), F32),
        ],
        compiler_params=_cparams("parallel"),
        name="s5_prep",
    )(lane_row(a_re), lane_row(a_im), lane_row(ldt), bt_re, bt_im, ct_re, ct_im)


def _rms(x, g):
    return x * lax.rsqrt(jnp.mean(x * x, axis=-1, keepdims=True) + EPS) * g


def _mixer_in_kernel(x_ref, mod_ref, g_ref, win_ref, vg_ref, ws_ref, bs_ref,
                     a_ref, u_ref, *v_ref, tm):
    x = x_ref[...]
    mod = mod_ref[0]
    shift = mod[:, :D_MODEL]
    scale = mod[:, D_MODEL:]
    h = _rms(x, g_ref[...]) * (1.0 + scale) + shift
    proj = jnp.dot(h.astype(BF16), win_ref[...], preferred_element_type=F32)
    z = jax.nn.gelu(proj[:, :2 * GMLP_WIDTH])
    u = z[:, :GMLP_WIDTH]
    v = z[:, GMLP_WIDTH:]
    vc = v - jnp.mean(v, axis=-1, keepdims=True)
    vn = vc * lax.rsqrt(jnp.mean(vc * vc, axis=-1, keepdims=True) + EPS) * vg_ref[...]
    if v_ref:
        v_ref[0][...] = vn
    vb = vn.astype(BF16)
    bias = bs_ref[...]
    for c in range(tm // GMLP_CHUNK):
        r0 = c * GMLP_CHUNK
        for hh in range(GMLP_HEADS):
            l0 = hh * GMLP_HEAD_DIM
            mixed = jnp.dot(ws_ref[hh], vb[r0:r0 + GMLP_CHUNK, l0:l0 + GMLP_HEAD_DIM],
                            preferred_element_type=F32) + bias[:, l0:l0 + GMLP_HEAD_DIM]
            a_ref[r0:r0 + GMLP_CHUNK, l0:l0 + GMLP_HEAD_DIM] = (
                u[r0:r0 + GMLP_CHUNK, l0:l0 + GMLP_HEAD_DIM] * mixed).astype(BF16)
    for gb in range(GROUP_BLOCKS):
        l0 = 2 * GMLP_WIDTH + gb * LANES
        u_ref[gb] = proj[:, l0:l0 + LANES]


def _mixer_in(x2d, mod, g_mix, w_in, v_gain, ws, bs, *, tm, tiles_per_mod, want_v):
    t = x2d.shape[0]
    rmod = mod.shape[1]
    const2 = lambda i: (0, 0)
    out_shape = [jax.ShapeDtypeStruct((t, GMLP_WIDTH), BF16),
                 jax.ShapeDtypeStruct((GROUP_BLOCKS, t, LANES), F32)]
    out_specs = [pl.BlockSpec((tm, GMLP_WIDTH), lambda i: (i, 0)),
                 pl.BlockSpec((GROUP_BLOCKS, tm, LANES), lambda i: (0, i, 0))]
    if want_v:
        out_shape.append(jax.ShapeDtypeStruct((t, GMLP_WIDTH), F32))
        out_specs.append(pl.BlockSpec((tm, GMLP_WIDTH), lambda i: (i, 0)))
    return pl.pallas_call(
        functools.partial(_mixer_in_kernel, tm=tm),
        grid=(t // tm,),
        in_specs=[
            pl.BlockSpec((tm, D_MODEL), lambda i: (i, 0)),
            pl.BlockSpec((1, rmod, 2 * D_MODEL), lambda i: (i // tiles_per_mod, 0, 0)),
            pl.BlockSpec((1, D_MODEL), const2),
            pl.BlockSpec((D_MODEL, IN_WIDTH), const2),
            pl.BlockSpec((1, GMLP_WIDTH), const2),
            pl.BlockSpec((GMLP_HEADS, GMLP_CHUNK, GMLP_CHUNK), lambda i: (0, 0, 0)),
            pl.BlockSpec((GMLP_CHUNK, GMLP_WIDTH), const2),
        ],
        out_specs=out_specs,
        out_shape=out_shape,
        compiler_params=_cparams("parallel"),
        name="mixer_in",
    )(x2d, mod, g_mix, w_in, v_gain, ws, bs)


def _s5_kernel(u_ref, msre_ref, msim_ref, nt_ref, wrev_ref, a16re_ref, a16im_ref, d_ref,
               h0re_ref, h0im_ref, y_ref, hnre_ref, hnim_ref, *scratch, scan):
    rows = u_ref.shape[1] // S5_CHUNK
    u = jnp.concatenate([u_ref[0, pl.ds(s, rows, stride=S5_CHUNK), :] for s in range(S5_CHUNK)], axis=1)
    ub = u.astype(BF16)
    s_re = jnp.dot(ub, msre_ref[0], preferred_element_type=F32)
    s_im = jnp.dot(ub, msim_ref[0], preferred_element_type=F32)
    a_re = a16re_ref[0]
    a_im = a16im_ref[0]
    h0_re = h0re_ref[0, 0]
    h0_im = h0im_ref[0, 0]
    if scan:
        sre_scr, sim_scr, hre_scr, him_scr = scratch
        sre_scr[...] = s_re
        sim_scr[...] = s_im

        def body(r, carry):
            hr, hi = carry
            hre_scr[pl.ds(r, 1), :] = hr
            him_scr[pl.ds(r, 1), :] = hi
            sr = sre_scr[pl.ds(r, 1), :]
            si = sim_scr[pl.ds(r, 1), :]
            return (a_re * hr - a_im * hi + sr, a_re * hi + a_im * hr + si)

        hn_re, hn_im = lax.fori_loop(0, rows, body, (h0_re, h0_im))
        h_re = hre_scr[...]
        h_im = him_scr[...]
    else:
        h_re = h0_re
        h_im = h0_im
        hn_re = a_re * h_re - a_im * h_im + s_re
        hn_im = a_re * h_im + a_im * h_re + s_im
    hnre_ref[0, 0] = hn_re
    hnim_ref[0, 0] = hn_im
    hcat = jnp.concatenate([h_re, h_im], axis=1).astype(BF16)
    inter = lax.dot_general(hcat, nt_ref[0], (((1,), (1,)), ((), ())),
                            preferred_element_type=F32)
    d = d_ref[0]
    for t in range(S5_CHUNK):
        k0 = (S5_CHUNK - 1 - t) * LANES
        intra = jnp.dot(ub[:, :(t + 1) * LANES], wrev_ref[0, k0:, :], preferred_element_type=F32)
        sl = slice(t * LANES, (t + 1) * LANES)
        y_ref[0, pl.ds(t, rows, stride=S5_CHUNK), :] = intra + inter[:, sl] + d[:, sl] * u[:, sl]


def _s5(u4, prep, d_tiled, h0_re, h0_im, *, rows, nblk, scan):
    ms_re, ms_im, nt, wrev, a16_re, a16_im = prep
    rh = h0_re.shape[2]
    wspec = lambda shape: pl.BlockSpec((1,) + shape, lambda g, b: (g, 0, 0))
    hspec = pl.BlockSpec((1, 1, rh, STATE_BLOCK), lambda g, b: (g, b, 0, 0))
    scratch = [pltpu.VMEM((rows, STATE_BLOCK), F32)] * 4 if scan else []
    return pl.pallas_call(
        functools.partial(_s5_kernel, scan=scan),
        grid=(GROUP_BLOCKS, nblk),
        in_specs=[
            pl.BlockSpec((1, rows * S5_CHUNK, LANES), lambda g, b: (g, b, 0)),
            wspec((S5_ROW, STATE_BLOCK)), wspec((S5_ROW, STATE_BLOCK)),
            wspec((S5_ROW, 2 * STATE_BLOCK)), wspec((S5_ROW, LANES)),
            wspec((1, STATE_BLOCK)), wspec((1, STATE_BLOCK)), wspec((1, S5_ROW)),
            hspec, hspec,
        ],
        out_specs=[pl.BlockSpec((1, rows * S5_CHUNK, LANES), lambda g, b: (g, b, 0)), hspec, hspec],
        out_shape=[
            jax.ShapeDtypeStruct(u4.shape, F32),
            jax.ShapeDtypeStruct(h0_re.shape, F32),
            jax.ShapeDtypeStruct(h0_re.shape, F32),
        ],
        scratch_shapes=scratch,
        compiler_params=_cparams("parallel", "parallel"),
        name="s5",
    )(u4, ms_re, ms_im, nt, wrev, a16_re, a16_im, d_tiled, h0_re, h0_im)


def _split_bf16(x):
    hi = x.astype(BF16)
    return hi, (x - hi.astype(F32)).astype(BF16)


def _top2_of4(a):
    m1 = jnp.maximum(jnp.maximum(a[0], a[1]), jnp.maximum(a[2], a[3]))
    i1 = jnp.where(a[0] == m1, 0, jnp.where(a[1] == m1, 1, jnp.where(a[2] == m1, 2, 3)))
    b = [jnp.where(i1 == j, -jnp.inf, a[j]) for j in range(4)]
    m2 = jnp.maximum(jnp.maximum(b[0], b[1]), jnp.maximum(b[2], b[3]))
    i2 = jnp.where(b[0] == m2, 0, jnp.where(b[1] == m2, 1, jnp.where(b[2] == m2, 2, 3)))
    return m1, i1, m2, i2


def _route_rows(h2, wrh_ref, wrl_ref, br_ref):
    h_hi, h_lo = _split_bf16(h2)
    w_hi = wrh_ref[...]
    logits = (jnp.dot(h_hi, w_hi, preferred_element_type=F32)
              + jnp.dot(h_lo, w_hi, preferred_element_type=F32)
              + jnp.dot(h_hi, wrl_ref[...], preferred_element_type=F32))
    lt = logits.T[:N_EXPERTS] + br_ref[...]
    rows = [lt[e:e + 1] for e in range(N_EXPERTS)]
    mx = functools.reduce(jnp.maximum, rows)
    ex = [jnp.exp(r - mx) for r in rows]
    tot = functools.reduce(lambda p, q: p + q, ex)
    scores = [e / tot for e in ex]
    best = None
    for g in range(N_EXPERT_GROUPS):
        m1, i1, m2, i2 = _top2_of4(scores[g * EXPERTS_PER_GROUP:(g + 1) * EXPERTS_PER_GROUP])
        cand = (m1 + m2, m1, i1 + g * EXPERTS_PER_GROUP, m2, i2 + g * EXPERTS_PER_GROUP)
        if best is None:
            best = cand
        else:
            better = cand[0] > best[0]
            best = tuple(jnp.where(better, c, b) for c, b in zip(cand, best))
    _, v1, e1, v2, e2 = best
    den = v1 + v2
    w1 = v1 / den
    w2 = v2 / den
    first_lo = e1 < e2
    return (jnp.where(first_lo, e1, e2), jnp.where(first_lo, e2, e1),
            jnp.where(first_lo, w1, w2), jnp.where(first_lo, w2, w1))


def _mixer_out_kernel(x_ref, a_ref, y_ref, mod_ref, g_ref, wglu_ref, bglu_ref, wout_ref,
                      wrh_ref, wrl_ref, br_ref, *rest, route, tiles_per_mod):
    x = x_ref[...]
    mod = mod_ref[0]
    gate_m = mod[:, :D_MODEL]
    shift_f = mod[:, D_MODEL:2 * D_MODEL]
    scale_f = mod[:, 2 * D_MODEL:]
    ys = jax.nn.gelu(jnp.concatenate([y_ref[gb] for gb in range(GROUP_BLOCKS)], axis=1))
    glu = jnp.dot(ys.astype(BF16), wglu_ref[...], preferred_element_type=F32) + bglu_ref[...]
    b_out = ys * jax.nn.sigmoid(glu)
    mixed = jnp.concatenate([a_ref[...], b_out.astype(BF16)], axis=1)
    xmid = x + gate_m * jnp.dot(mixed, wout_ref[...], preferred_element_type=F32)
    h2 = _rms(xmid, g_ref[...]) * (1.0 + scale_f) + shift_f
    lo, hi, glo, ghi = _route_rows(h2, wrh_ref, wrl_ref, br_ref)
    if not route:
        xmid_ref, h2_ref, lo_ref, hi_ref, glo_ref, ghi_ref = rest
        xmid_ref[...] = xmid
        h2_ref[...] = h2
        lo_ref[...] = lo
        hi_ref[...] = hi
        glo_ref[...] = glo
        ghi_ref[...] = ghi
        return

    tri_ref, xext_ref, cls_ref, rank_ref, cnt_ref, run_ref = rest
    i = pl.program_id(0)
    tm = x.shape[0]

    @pl.when(i == 0)
    def _():
        run_ref[...] = jnp.zeros_like(run_ref)

    a = lo & (EXPERTS_PER_GROUP - 1)
    b = hi & (EXPERTS_PER_GROUP - 1)
    pair = jnp.where(a == 0, 0, jnp.where(a == 1, 3, 5)) + (b - a - 1)
    cls = (lo >> 2) * PAIRS_PER_GROUP + pair
    cls_ref[...] = cls
    onehot = lax.broadcasted_iota(jnp.int32, (CLASS_ROWS, tm), 0) == cls
    prefix = jnp.dot(jnp.where(onehot, 1.0, 0.0).astype(BF16), tri_ref[...],
                     preferred_element_type=F32)
    run = run_ref[...]
    rank = jnp.sum(jnp.where(onehot, prefix - 1.0 + run[:, :1], 0.0), axis=0, keepdims=True)
    rank_ref[...] = rank.astype(jnp.int32)
    run = run + prefix[:, tm - 1:tm]
    run_ref[...] = run
    cnt_ref[...] = run
    bid = jnp.full((1, tm), i // tiles_per_mod, jnp.int32).astype(F32)
    ext = jnp.concatenate([glo, ghi, bid, jnp.zeros((LANES - 3, tm), F32)], axis=0).T
    xext_ref[:, :D_MODEL] = xmid
    xext_ref[:, D_MODEL:] = ext


def _mixer_out(x2d, a_out, y4, mod, g_ffn, w_glu, b_glu, w_out, wr_hi, wr_lo, b_r,
               *, tm, tiles_per_mod, route):
    t = x2d.shape[0]
    rmod = mod.shape[1]
    const2 = lambda i: (0, 0)
    tok = pl.BlockSpec((tm, D_MODEL), lambda i: (i, 0))
    row = pl.BlockSpec((1, tm), lambda i: (0, i))
    in_specs = [
        tok,
        pl.BlockSpec((tm, GMLP_WIDTH), lambda i: (i, 0)),
        pl.BlockSpec((GROUP_BLOCKS, tm, LANES), lambda i: (0, i, 0)),
        pl.BlockSpec((1, rmod, 3 * D_MODEL), lambda i: (i // tiles_per_mod, 0, 0)),
        pl.BlockSpec((1, D_MODEL), const2),
        pl.BlockSpec((SSM_WIDTH, SSM_WIDTH), const2),
        pl.BlockSpec((1, SSM_WIDTH), const2),
        pl.BlockSpec((D_MODEL, D_MODEL), const2),
        pl.BlockSpec((D_MODEL, LANES), const2),
        pl.BlockSpec((D_MODEL, LANES), const2),
        pl.BlockSpec((N_EXPERTS, 1), const2),
    ]
    args = [x2d, a_out, y4, mod, g_ffn, w_glu, b_glu, w_out, wr_hi, wr_lo, b_r]
    if route:
        ids = jnp.arange(tm)
        args.append((ids[:, None] <= ids[None, :]).astype(BF16))
        in_specs.append(pl.BlockSpec((tm, tm), const2))
        out_specs = [pl.BlockSpec((tm, XEXT_WIDTH), lambda i: (i, 0)), row, row,
                     pl.BlockSpec((CLASS_ROWS, LANES), const2)]
        out_shape = [jax.ShapeDtypeStruct((t, XEXT_WIDTH), F32),
                     jax.ShapeDtypeStruct((1, t), jnp.int32),
                     jax.ShapeDtypeStruct((1, t), jnp.int32),
                     jax.ShapeDtypeStruct((CLASS_ROWS, LANES), F32)]
        scratch = [pltpu.VMEM((CLASS_ROWS, LANES), F32)]
    else:
        out_specs = [tok, tok, row, row, row, row]
        out_shape = [jax.ShapeDtypeStruct((t, D_MODEL), F32),
                     jax.ShapeDtypeStruct((t, D_MODEL), F32),
                     jax.ShapeDtypeStruct((1, t), jnp.int32),
                     jax.ShapeDtypeStruct((1, t), jnp.int32),
                     jax.ShapeDtypeStruct((1, t), F32),
                     jax.ShapeDtypeStruct((1, t), F32)]
        scratch = []
    return pl.pallas_call(
        functools.partial(_mixer_out_kernel, route=route, tiles_per_mod=tiles_per_mod),
        grid=(t // tm,),
        in_specs=in_specs,
        out_specs=out_specs,
        out_shape=out_shape,
        scratch_shapes=scratch,
        compiler_params=_cparams("arbitrary" if route else "parallel"),
        name="mixer_out",
    )(*args)


def _plan_kernel(pos_ref, zeros_hbm, src_ref, sem):
    fill = pltpu.make_async_copy(zeros_hbm, src_ref, sem)
    fill.start()
    fill.wait()

    def body(t, carry):
        src_ref[pos_ref[t]] = t
        return carry

    lax.fori_loop(0, pos_ref.shape[0], body, 0, unroll=8)


def _plan(pos, nslots):
    return pl.pallas_call(
        _plan_kernel,
        in_specs=[pl.BlockSpec(memory_space=pltpu.SMEM), pl.BlockSpec(memory_space=pl.ANY)],
        out_specs=pl.BlockSpec(memory_space=pltpu.SMEM),
        out_shape=jax.ShapeDtypeStruct((nslots,), jnp.int32),
        scratch_shapes=[pltpu.SemaphoreType.DMA(())],
        name="moe_plan",
    )(pos, jnp.zeros((nslots,), jnp.int32))


def _moe_routed_kernel(tlo_ref, thi_ref, tval_ref, src_ref,
                       xext_hbm, wgl_ref, wul_ref, wdl_ref, wgh_ref, wuh_ref, wdh_ref,
                       mod_ref, g_ref, gfin_ref, out_hbm, buf_ref, obuf_ref, sem_in, sem_out,
                       *, nseq, final_norm):
    i = pl.program_id(0)
    nv = tval_ref[i]
    base = i * MOE_TILE

    @pl.when(i == 0)
    def _():
        buf_ref[...] = jnp.zeros_like(buf_ref)

    def row_in(j, tok):
        return pltpu.make_async_copy(xext_hbm.at[pl.ds(tok, 1), :], buf_ref.at[pl.ds(j, 1), :], sem_in)

    def row_out(j, tok):
        return pltpu.make_async_copy(obuf_ref.at[pl.ds(j, 1), :], out_hbm.at[pl.ds(tok, 1), :], sem_out)

    @pl.when(nv > 0)
    def _():
        def start_in(j, c):
            row_in(j, src_ref[base + j]).start()
            return c

        def wait_in(j, c):
            row_in(j, 0).wait()
            return c

        lax.fori_loop(0, nv, start_in, 0)
        lax.fori_loop(0, nv, wait_in, 0)

        buf = buf_ref[...]
        xm = buf[:, :D_MODEL]
        glo = buf[:, D_MODEL:D_MODEL + 1]
        ghi = buf[:, D_MODEL + 1:D_MODEL + 2]
        bid = buf[:, D_MODEL + 2:D_MODEL + 3]
        mod = mod_ref[...]

        def per_row(k):
            out = mod[nseq - 1:nseq, k * D_MODEL:(k + 1) * D_MODEL]
            for s in range(nseq - 2, -1, -1):
                out = jnp.where(bid == float(s), mod[s:s + 1, k * D_MODEL:(k + 1) * D_MODEL], out)
            return out

        shift_f, scale_f, gate_f = per_row(0), per_row(1), per_row(2)
        h = (_rms(xm, g_ref[...]) * (1.0 + scale_f) + shift_f).astype(BF16)
        ffn = None
        for wg, wu, wd, gate in ((wgl_ref, wul_ref, wdl_ref, glo), (wgh_ref, wuh_ref, wdh_ref, ghi)):
            he = (jax.nn.silu(jnp.dot(h, wg[0], preferred_element_type=F32))
                  * jnp.dot(h, wu[0], preferred_element_type=F32))
            y = gate * jnp.dot(he.astype(BF16), wd[0], preferred_element_type=F32)
            ffn = y if ffn is None else ffn + y
        x_new = xm + gate_f * ffn
        if final_norm:
            x_new = _rms(x_new, gfin_ref[...])
        obuf_ref[...] = x_new

        def start_out(j, c):
            row_out(j, src_ref[base + j]).start()
            return c

        def wait_out(j, c):
            row_out(j, 0).wait()
            return c

        lax.fori_loop(0, nv, start_out, 0)
        lax.fori_loop(0, nv, wait_out, 0)


def _moe_routed(xext, cls, rank, counts, wg, wu, wd, mod, g_ffn, g_final, *, final_norm):
    t = xext.shape[0]
    nseq = mod.shape[0]
    ntiles = t // MOE_TILE + N_CLASSES
    nslots = ntiles * MOE_TILE
    counts = counts[:N_CLASSES, 0].astype(jnp.int32)
    tiles_c = (counts + MOE_TILE - 1) // MOE_TILE
    tile_end = jnp.cumsum(tiles_c)
    tile_start = tile_end - tiles_c
    used = tile_end[-1]
    pos = (tile_start * MOE_TILE)[cls[0]] + rank[0]
    src = _plan(pos, nslots)
    tid = jnp.arange(ntiles, dtype=jnp.int32)
    tcls = jnp.searchsorted(tile_end, jnp.minimum(tid, used - 1), side="right").astype(jnp.int32)
    tval = jnp.where(tid < used,
                     jnp.clip(counts[tcls] - (tid - tile_start[tcls]) * MOE_TILE, 0, MOE_TILE), 0)
    tlo = jnp.asarray(CLASS_LO, jnp.int32)[tcls]
    thi = jnp.asarray(CLASS_HI, jnp.int32)[tcls]
    wspec_lo = lambda shape: pl.BlockSpec(shape, lambda i, tlo, thi, tval, src: (tlo[i], 0, 0))
    wspec_hi = lambda shape: pl.BlockSpec(shape, lambda i, tlo, thi, tval, src: (thi[i], 0, 0))
    const2 = lambda i, tlo, thi, tval, src: (0, 0)
    up = (1, D_MODEL, D_EXPERT)
    down = (1, D_EXPERT, D_MODEL)
    return pl.pallas_call(
        functools.partial(_moe_routed_kernel, nseq=nseq, final_norm=final_norm),
        grid_spec=pltpu.PrefetchScalarGridSpec(
            num_scalar_prefetch=4,
            grid=(ntiles,),
            in_specs=[
                pl.BlockSpec(memory_space=pl.ANY),
                wspec_lo(up), wspec_lo(up), wspec_lo(down),
                wspec_hi(up), wspec_hi(up), wspec_hi(down),
                pl.BlockSpec((nseq, 3 * D_MODEL), const2),
                pl.BlockSpec((1, D_MODEL), const2),
                pl.BlockSpec((1, D_MODEL), const2),
            ],
            out_specs=pl.BlockSpec(memory_space=pl.ANY),
            scratch_shapes=[
                pltpu.VMEM((MOE_TILE, XEXT_WIDTH), F32),
                pltpu.VMEM((MOE_TILE, D_MODEL), F32),
                pltpu.SemaphoreType.DMA(()),
                pltpu.SemaphoreType.DMA(()),
            ],
        ),
        out_shape=jax.ShapeDtypeStruct((t, D_MODEL), F32),
        compiler_params=_cparams("arbitrary"),
        name="moe_routed",
    )(tlo, thi, tval, src, xext, wg, wu, wd, wg, wu, wd, mod, g_ffn, g_final)


def _moe_kernel(h2_ref, gates_ref, wg_ref, wu_ref, wd_ref, xmid_ref, mod_ref, gfin_ref,
                o_ref, acc_ref, *, final_norm):
    e = pl.program_id(1)

    @pl.when(e == 0)
    def _():
        acc_ref[...] = jnp.zeros_like(acc_ref)

    h = h2_ref[...].astype(BF16)
    he = (jax.nn.silu(jnp.dot(h, wg_ref[0], preferred_element_type=F32))
          * jnp.dot(h, wu_ref[0], preferred_element_type=F32))
    y = jnp.dot(he.astype(BF16), wd_ref[0], preferred_element_type=F32)
    gates = gates_ref[...]
    lane = lax.broadcasted_iota(jnp.int32, gates.shape, 1)
    gcol = jnp.sum(jnp.where(lane == e, gates, 0.0), axis=1, keepdims=True)
    acc_ref[...] += gcol * y

    @pl.when(e == N_EXPERTS - 1)
    def _():
        x = xmid_ref[...] + mod_ref[0] * acc_ref[...]
        if final_norm:
            x = _rms(x, gfin_ref[...])
        o_ref[...] = x


def _moe(h2, gates, wg, wu, wd, xmid, mod, g_final, *, tm, tiles_per_mod, final_norm):
    t = h2.shape[0]
    rmod = mod.shape[1]
    tok = pl.BlockSpec((tm, D_MODEL), lambda i, e: (i, 0))
    return pl.pallas_call(
        functools.partial(_moe_kernel, final_norm=final_norm),
        grid=(t // tm, N_EXPERTS),
        in_specs=[
            tok,
            pl.BlockSpec((tm, N_EXPERTS), lambda i, e: (i, 0)),
            pl.BlockSpec((1, D_MODEL, D_EXPERT), lambda i, e: (e, 0, 0)),
            pl.BlockSpec((1, D_MODEL, D_EXPERT), lambda i, e: (e, 0, 0)),
            pl.BlockSpec((1, D_EXPERT, D_MODEL), lambda i, e: (e, 0, 0)),
            tok,
            pl.BlockSpec((1, rmod, D_MODEL), lambda i, e: (i // tiles_per_mod, 0, 0)),
            pl.BlockSpec((1, D_MODEL), lambda i, e: (0, 0)),
        ],
        out_specs=tok,
        out_shape=jax.ShapeDtypeStruct((t, D_MODEL), F32),
        scratch_shapes=[pltpu.VMEM((tm, D_MODEL), F32)],
        compiler_params=_cparams("parallel", "arbitrary"),
        name="moe",
    )(h2, gates, wg, wu, wd, xmid, mod, g_final)


def _layer(x2d, mod, lw, h0_re, h0_im, *, seq_len, nseq, sample, final_norm):
    t = x2d.shape[0]
    if sample:
        tm, tiles_per_mod, tm_moe = t, 1, t
        s5_rows, s5_blocks = nseq, 1
    else:
        tm = min(512, seq_len)
        tiles_per_mod = seq_len // tm
        tm_moe = min(512, seq_len)
        s5_rows, s5_blocks = seq_len // S5_CHUNK, nseq
    outs = _mixer_in(x2d, mod[..., :2 * D_MODEL], lw["g_mix"], lw["w_in"], lw["v_gain"],
                     lw["ws_sample"] if sample else lw["ws"], lw["bs_sample"] if sample else lw["bs"],
                     tm=tm, tiles_per_mod=tiles_per_mod, want_v=sample)
    a_out, u4 = outs[0], outs[1]
    v_rows = outs[2] if sample else None
    y4, hn_re, hn_im = _s5(u4, lw["prep"], lw["d_tiled"],
                           h0_re, h0_im, rows=s5_rows, nblk=s5_blocks, scan=not sample)
    mo = _mixer_out(x2d, a_out, y4, mod[..., 2 * D_MODEL:5 * D_MODEL],
                    lw["g_ffn"], lw["w_glu"], lw["b_glu"], lw["w_out"], lw["wr_hi"], lw["wr_lo"], lw["b_r"],
                    tm=tm, tiles_per_mod=tiles_per_mod, route=not sample)
    if sample:
        xmid, h2, lo, hi, glo, ghi = mo
        eids = jnp.arange(N_EXPERTS, dtype=jnp.int32)[None, :]
        gates = (jnp.where(lo[0][:, None] == eids, glo[0][:, None], 0.0)
                 + jnp.where(hi[0][:, None] == eids, ghi[0][:, None], 0.0))
        x_new = _moe(h2, gates, lw["wg"], lw["wu"], lw["wd"], xmid, mod[..., 5 * D_MODEL:],
                     lw["g_final"], tm=tm_moe, tiles_per_mod=1, final_norm=final_norm)
    else:
        xext, cls, rank, counts = mo
        x_new = _moe_routed(xext, cls, rank, counts, lw["wg"], lw["wu"], lw["wd"],
                            mod[:, 0, 3 * D_MODEL:], lw["g_ffn"], lw["g_final"], final_norm=final_norm)
    return x_new, hn_re, hn_im, v_rows


def _state_out(h, nseq):
    return h.reshape(GROUP_BLOCKS, nseq, GROUPS_PER_BLOCK, SSM_STATE).transpose(1, 0, 2, 3).reshape(
        nseq, SSM_GROUPS, SSM_STATE)


def _state_in(h, nblk, rh):
    nseq = h.shape[0]
    return h.reshape(nseq, GROUP_BLOCKS, STATE_BLOCK).transpose(1, 0, 2).reshape(
        GROUP_BLOCKS, nblk, rh, STATE_BLOCK)


def kernel(x_prompt, x_sample, c_prompt, c_sample, state_s5_re, state_s5_im, w_ada, b_ada, g_norm_mix, g_norm_ffn, w_in, gmlp_v_gain, gmlp_w_spatial, gmlp_b_spatial, s5_a_re, s5_a_im, s5_log_dt, s5_b_re, s5_b_im, s5_c_re, s5_c_im, s5_d, s5_w_glu, s5_b_glu, w_out, w_router, b_router, w_gate, w_up, w_down, g_final):
    nb, seq_len, _ = x_prompt.shape
    ns, dec_len, _ = x_sample.shape
    assert dec_len == S5_CHUNK and ns * dec_len == GMLP_CHUNK and nb + ns <= ADA_ROWS
    assert seq_len % GMLP_CHUNK == 0

    c_all = jnp.concatenate([c_prompt, c_sample, jnp.zeros((ADA_ROWS - nb - ns, D_MODEL), F32)], axis=0)
    mod_all = _ada(c_all, w_ada, b_ada)

    pos = jnp.arange(GMLP_CHUNK)
    causal = (pos[None, :] // CHUNK) <= (pos[:, None] // CHUNK)
    wr_pad = jnp.pad(w_router, ((0, 0), (0, LANES - N_EXPERTS)))
    wr_hi = wr_pad.astype(BF16)
    wr_lo = (wr_pad - wr_hi.astype(F32)).astype(BF16)
    b_r = b_router.reshape(N_EXPERTS, 1)
    g_fin = g_final.reshape(1, D_MODEL)
    eye_s = jnp.eye(ns, dtype=F32)

    xp = x_prompt.reshape(nb * seq_len, D_MODEL)
    xs = x_sample.reshape(ns * dec_len, D_MODEL)
    zeros_p = jnp.zeros((GROUP_BLOCKS, nb, 1, STATE_BLOCK), F32)
    sp_re, sp_im, ss_re, ss_im, v_new = [], [], [], [], []
    for l in range(DEPTH):
        ws = jnp.where(causal[None], gmlp_w_spatial[l], 0.0)
        ws_sample = jnp.einsum("ab,hij->haibj", eye_s, ws[:, :dec_len, :dec_len]).reshape(
            GMLP_HEADS, GMLP_CHUNK, GMLP_CHUNK)
        bs = jnp.repeat(gmlp_b_spatial[l].T, GMLP_HEAD_DIM, axis=1)
        lw = dict(
            g_mix=g_norm_mix[l].reshape(1, D_MODEL), g_ffn=g_norm_ffn[l].reshape(1, D_MODEL),
            w_in=w_in[l].astype(BF16), v_gain=gmlp_v_gain[l].reshape(1, GMLP_WIDTH),
            ws=ws.astype(BF16), ws_sample=ws_sample.astype(BF16),
            bs=bs, bs_sample=jnp.tile(bs[:dec_len], (ns, 1)),
            prep=_s5_prep(s5_a_re[l], s5_a_im[l], s5_log_dt[l], s5_b_re[l], s5_b_im[l],
                          s5_c_re[l], s5_c_im[l]),
            d_tiled=jnp.tile(s5_d[l].reshape(GROUP_BLOCKS, 1, LANES), (1, 1, S5_CHUNK)),
            w_glu=s5_w_glu[l].astype(BF16), b_glu=s5_b_glu[l].reshape(1, SSM_WIDTH),
            w_out=w_out[l].astype(BF16), wr_hi=wr_hi, wr_lo=wr_lo, b_r=b_r,
            wg=w_gate[l].astype(BF16), wu=w_up[l].astype(BF16), wd=w_down[l].astype(BF16),
            g_final=g_fin,
        )
        last = l == DEPTH - 1
        mod_p = mod_all[l, :nb].reshape(nb, 1, 6 * D_MODEL)
        mod_s = jnp.repeat(mod_all[l, nb:nb + ns], dec_len, axis=0).reshape(1, ns * dec_len, 6 * D_MODEL)
        xp, hp_re, hp_im, _ = _layer(xp, mod_p, lw, zeros_p, zeros_p, seq_len=seq_len, nseq=nb,
                                     sample=False, final_norm=last)
        xs, hs_re, hs_im, vs = _layer(xs, mod_s, lw, _state_in(state_s5_re[l], 1, ns),
                                      _state_in(state_s5_im[l], 1, ns), seq_len=dec_len, nseq=ns,
                                      sample=True, final_norm=last)
        sp_re.append(_state_out(hp_re, nb))
        sp_im.append(_state_out(hp_im, nb))
        ss_re.append(_state_out(hs_re, ns))
        ss_im.append(_state_out(hs_im, ns))
        v_new.append(vs.reshape(ns, dec_len, GMLP_WIDTH))
    return (xp.reshape(nb, seq_len, D_MODEL), xs.reshape(ns, dec_len, D_MODEL),
            jnp.stack(sp_re), jnp.stack(sp_im), jnp.stack(ss_re), jnp.stack(ss_im), jnp.stack(v_new))
```

```python
import functools

import jax
import jax.numpy as jnp
from jax import lax
from jax.experimental import pallas as pl
from jax.experimental.pallas import tpu as pltpu

F32 = jnp.float32
BF16 = jnp.bfloat16

D_MODEL = 1024
DEPTH = 2
CHUNK = 64
GMLP_CHUNK = 128
GMLP_WIDTH = 512
GMLP_HEADS = 4
GMLP_HEAD_DIM = 128
SSM_WIDTH = 512
SSM_GROUP = 16
SSM_GROUPS = 32
SSM_STATE = 64
IN_WIDTH = 1536
N_EXPERTS = 16
EXPERTS_PER_GROUP = 4
N_EXPERT_GROUPS = 4
D_EXPERT = 512
EPS = 1e-6

LANES = 128
S5_CHUNK = 16
GROUP_BLOCKS = 4
GROUPS_PER_BLOCK = SSM_GROUPS // GROUP_BLOCKS
STATE_BLOCK = GROUPS_PER_BLOCK * SSM_STATE
S5_ROW = S5_CHUNK * LANES
ADA_ROWS = 16
PAIRS_PER_GROUP = 6
N_CLASSES = N_EXPERT_GROUPS * PAIRS_PER_GROUP
CLASS_ROWS = 32
_PAIRS = [(a, b) for a in range(EXPERTS_PER_GROUP) for b in range(a + 1, EXPERTS_PER_GROUP)]
CLASS_LO = [g * EXPERTS_PER_GROUP + a for g in range(N_EXPERT_GROUPS) for a, _ in _PAIRS]
CLASS_HI = [g * EXPERTS_PER_GROUP + b for g in range(N_EXPERT_GROUPS) for _, b in _PAIRS]
MOE_TILE = 256
XEXT_WIDTH = D_MODEL + LANES
VMEM_LIMIT = 56 * 1024 * 1024


def _cparams(*sem):
    return pltpu.CompilerParams(dimension_semantics=sem, vmem_limit_bytes=VMEM_LIMIT)


def _ada_kernel(c_ref, w_ref, b_ref, o_ref):
    c = c_ref[...]
    s = (c * jax.nn.sigmoid(c)).astype(BF16)
    o_ref[0] = jnp.dot(s, w_ref[0].astype(BF16), preferred_element_type=F32) + b_ref[0]


def _ada(c_all, w_ada, b_ada):
    nblk = 6
    return pl.pallas_call(
        _ada_kernel,
        grid=(DEPTH, nblk),
        in_specs=[
            pl.BlockSpec((ADA_ROWS, D_MODEL), lambda l, j: (0, 0)),
            pl.BlockSpec((1, D_MODEL, D_MODEL), lambda l, j: (l, 0, j)),
            pl.BlockSpec((1, 1, D_MODEL), lambda l, j: (l, 0, j)),
        ],
        out_specs=pl.BlockSpec((1, ADA_ROWS, D_MODEL), lambda l, j: (l, 0, j)),
        out_shape=jax.ShapeDtypeStruct((DEPTH, ADA_ROWS, 6 * D_MODEL), F32),
        compiler_params=_cparams("parallel", "parallel"),
        name="ada",
    )(c_all, w_ada, b_ada.reshape(DEPTH, 1, 6 * D_MODEL))


def _prep_kernel(are_ref, aim_ref, ldt_ref, bre_ref, bim_ref, cre_ref, cim_ref,
                 msre_ref, msim_ref, nt_ref, wrev_ref, a16re_ref, a16im_ref):
    a_re = are_ref[0]
    a_im = aim_ref[0]
    dt = jnp.exp(ldt_ref[0])
    rho = a_re * dt
    th = a_im * dt
    kk = jnp.minimum(lax.broadcasted_iota(jnp.int32, (24, STATE_BLOCK), 0), S5_CHUNK).astype(F32)
    mag = jnp.exp(kk * rho)
    pw_re = mag * jnp.cos(kk * th)
    pw_im = mag * jnp.sin(kk * th)

    lb_re = pw_re[1:2]
    lb_im = pw_im[1:2]
    num_re = lb_re - 1.0
    num_im = lb_im
    den = a_re * a_re + a_im * a_im
    coef_re = (num_re * a_re + num_im * a_im) / den
    coef_im = (num_im * a_re - num_re * a_im) / den
    b_re = bre_ref[0]
    b_im = bim_ref[0]
    bb_re = coef_re * b_re - coef_im * b_im
    bb_im = coef_re * b_im + coef_im * b_re

    rows = lax.broadcasted_iota(jnp.int32, (LANES, STATE_BLOCK), 0)
    cols = lax.broadcasted_iota(jnp.int32, (LANES, STATE_BLOCK), 1)
    same_group = (rows >> 4) == (cols >> 6)

    def blockdiag(x16):
        return jnp.where(same_group, jnp.concatenate([x16] * GROUPS_PER_BLOCK, axis=0), 0.0)

    for s in range(S5_CHUNK):
        k = S5_CHUNK - 1 - s
        p_re = pw_re[k:k + 1]
        p_im = pw_im[k:k + 1]
        msre_ref[0, s * LANES:(s + 1) * LANES, :] = blockdiag(p_re * bb_re - p_im * bb_im).astype(BF16)
        msim_ref[0, s * LANES:(s + 1) * LANES, :] = blockdiag(p_re * bb_im + p_im * bb_re).astype(BF16)

    bcat = jnp.concatenate([blockdiag(bb_re), blockdiag(bb_im)], axis=1)
    c_re = cre_ref[0]
    c_im = cim_ref[0]
    for k in range(S5_CHUNK + 1):
        p_re = pw_re[k:k + 1]
        p_im = pw_im[k:k + 1]
        cl = jnp.concatenate([blockdiag(c_re * p_re - c_im * p_im),
                              -blockdiag(c_re * p_im + c_im * p_re)], axis=1)
        if k >= 1:
            nt_ref[0, (k - 1) * LANES:k * LANES, :] = cl.astype(BF16)
        if k < S5_CHUNK:
            wl = lax.dot_general(bcat, cl, (((1,), (1,)), ((), ())),
                                 precision=lax.Precision.HIGHEST, preferred_element_type=F32)
            j = S5_CHUNK - 1 - k
            wrev_ref[0, j * LANES:(j + 1) * LANES, :] = wl.astype(BF16)

    a16re_ref[0] = pw_re[S5_CHUNK:S5_CHUNK + 1]
    a16im_ref[0] = pw_im[S5_CHUNK:S5_CHUNK + 1]


def _s5_prep(a_re, a_im, log_dt, b_re, b_im, c_re, c_im):
    nstate = SSM_GROUPS * SSM_STATE

    def lane_row(v):
        return v.reshape(GROUP_BLOCKS, 1, STATE_BLOCK)

    def rows16(v):
        return v.reshape(SSM_GROUP, GROUP_BLOCKS, STATE_BLOCK).transpose(1, 0, 2)

    ldt = jnp.repeat(log_dt, SSM_STATE).reshape(SSM_GROUPS, SSM_STATE)
    bt_re = rows16(b_re.transpose(2, 0, 1).reshape(SSM_GROUP, nstate))
    bt_im = rows16(b_im.transpose(2, 0, 1).reshape(SSM_GROUP, nstate))
    ct_re = rows16(c_re.transpose(1, 0, 2).reshape(SSM_GROUP, nstate))
    ct_im = rows16(c_im.transpose(1, 0, 2).reshape(SSM_GROUP, nstate))
    row_spec = pl.BlockSpec((1, 1, STATE_BLOCK), lambda g: (g, 0, 0))
    r16_spec = pl.BlockSpec((1, SSM_GROUP, STATE_BLOCK), lambda g: (g, 0, 0))
    return pl.pallas_call(
        _prep_kernel,
        grid=(GROUP_BLOCKS,),
        in_specs=[row_spec, row_spec, row_spec, r16_spec, r16_spec, r16_spec, r16_spec],
        out_specs=[
            pl.BlockSpec((1, S5_ROW, STATE_BLOCK), lambda g: (g, 0, 0)),
            pl.BlockSpec((1, S5_ROW, STATE_BLOCK), lambda g: (g, 0, 0)),
            pl.BlockSpec((1, S5_ROW, 2 * STATE_BLOCK), lambda g: (g, 0, 0)),
            pl.BlockSpec((1, S5_ROW, LANES), lambda g: (g, 0, 0)),
            row_spec, row_spec,
        ],
        out_shape=[
            jax.ShapeDtypeStruct((GROUP_BLOCKS, S5_ROW, STATE_BLOCK), BF16),
            jax.ShapeDtypeStruct((GROUP_BLOCKS, S5_ROW, STATE_BLOCK), BF16),
            jax.ShapeDtypeStruct((GROUP_BLOCKS, S5_ROW, 2 * STATE_BLOCK), BF16),
            jax.ShapeDtypeStruct((GROUP_BLOCKS, S5_ROW, LANES), BF16),
            jax.ShapeDtypeStruct((GROUP_BLOCKS, 1, STATE_BLOCK), F32),
            jax.ShapeDtypeStruct((GROUP_BLOCKS, 1, STATE_BLOCK), F32),
        ],
        compiler_params=_cparams("parallel"),
        name="s5_prep",
    )(lane_row(a_re), lane_row(a_im), lane_row(ldt), bt_re, bt_im, ct_re, ct_im)


def _rms(x, g):
    return x * lax.rsqrt(jnp.mean(x * x, axis=-1, keepdims=True) + EPS) * g


def _mixer_in_kernel(x_ref, mod_ref, g_ref, win_ref, vg_ref, ws_ref, bs_ref,
                     a_ref, u_ref, *v_ref, tm):
    x = x_ref[...]
    mod = mod_ref[0]
    shift = mod[:, :D_MODEL]
    scale = mod[:, D_MODEL:]
    h = _rms(x, g_ref[...]) * (1.0 + scale) + shift
    proj = jnp.dot(h.astype(BF16), win_ref[...], preferred_element_type=F32)
    z = jax.nn.gelu(proj[:, :2 * GMLP_WIDTH])
    u = z[:, :GMLP_WIDTH]
    v = z[:, GMLP_WIDTH:]
    vc = v - jnp.mean(v, axis=-1, keepdims=True)
    vn = vc * lax.rsqrt(jnp.mean(vc * vc, axis=-1, keepdims=True) + EPS) * vg_ref[...]
    if v_ref:
        v_ref[0][...] = vn
    vb = vn.astype(BF16)
    bias = bs_ref[...]
    for c in range(tm // GMLP_CHUNK):
        r0 = c * GMLP_CHUNK
        for hh in range(GMLP_HEADS):
            l0 = hh * GMLP_HEAD_DIM
            mixed = jnp.dot(ws_ref[hh], vb[r0:r0 + GMLP_CHUNK, l0:l0 + GMLP_HEAD_DIM],
                            preferred_element_type=F32) + bias[:, l0:l0 + GMLP_HEAD_DIM]
            a_ref[r0:r0 + GMLP_CHUNK, l0:l0 + GMLP_HEAD_DIM] = (
                u[r0:r0 + GMLP_CHUNK, l0:l0 + GMLP_HEAD_DIM] * mixed).astype(BF16)
    for gb in range(GROUP_BLOCKS):
        l0 = 2 * GMLP_WIDTH + gb * LANES
        u_ref[gb] = proj[:, l0:l0 + LANES]


def _mixer_in(x2d, mod, g_mix, w_in, v_gain, ws, bs, *, tm, tiles_per_mod, want_v):
    t = x2d.shape[0]
    rmod = mod.shape[1]
    const2 = lambda i: (0, 0)
    out_shape = [jax.ShapeDtypeStruct((t, GMLP_WIDTH), BF16),
                 jax.ShapeDtypeStruct((GROUP_BLOCKS, t, LANES), F32)]
    out_specs = [pl.BlockSpec((tm, GMLP_WIDTH), lambda i: (i, 0)),
                 pl.BlockSpec((GROUP_BLOCKS, tm, LANES), lambda i: (0, i, 0))]
    if want_v:
        out_shape.append(jax.ShapeDtypeStruct((t, GMLP_WIDTH), F32))
        out_specs.append(pl.BlockSpec((tm, GMLP_WIDTH), lambda i: (i, 0)))
    return pl.pallas_call(
        functools.partial(_mixer_in_kernel, tm=tm),
        grid=(t // tm,),
        in_specs=[
            pl.BlockSpec((tm, D_MODEL), lambda i: (i, 0)),
            pl.BlockSpec((1, rmod, 2 * D_MODEL), lambda i: (i // tiles_per_mod, 0, 0)),
            pl.BlockSpec((1, D_MODEL), const2),
            pl.BlockSpec((D_MODEL, IN_WIDTH), const2),
            pl.BlockSpec((1, GMLP_WIDTH), const2),
            pl.BlockSpec((GMLP_HEADS, GMLP_CHUNK, GMLP_CHUNK), lambda i: (0, 0, 0)),
            pl.BlockSpec((GMLP_CHUNK, GMLP_WIDTH), const2),
        ],
        out_specs=out_specs,
        out_shape=out_shape,
        compiler_params=_cparams("parallel"),
        name="mixer_in",
    )(x2d, mod, g_mix, w_in, v_gain, ws, bs)


def _s5_kernel(u_ref, msre_ref, msim_ref, nt_ref, wrev_ref, a16re_ref, a16im_ref, d_ref,
               h0re_ref, h0im_ref, y_ref, hnre_ref, hnim_ref, *scratch, scan):
    rows = u_ref.shape[1] // S5_CHUNK
    u = jnp.concatenate([u_ref[0, pl.ds(s, rows, stride=S5_CHUNK), :] for s in range(S5_CHUNK)], axis=1)
    ub = u.astype(BF16)
    s_re = jnp.dot(ub, msre_ref[0], preferred_element_type=F32)
    s_im = jnp.dot(ub, msim_ref[0], preferred_element_type=F32)
    a_re = a16re_ref[0]
    a_im = a16im_ref[0]
    h0_re = h0re_ref[0, 0]
    h0_im = h0im_ref[0, 0]
    if scan:
        sre_scr, sim_scr, hre_scr, him_scr = scratch
        sre_scr[...] = s_re
        sim_scr[...] = s_im

        def body(r, carry):
            hr, hi = carry
            hre_scr[pl.ds(r, 1), :] = hr
            him_scr[pl.ds(r, 1), :] = hi
            sr = sre_scr[pl.ds(r, 1), :]
            si = sim_scr[pl.ds(r, 1), :]
            return (a_re * hr - a_im * hi + sr, a_re * hi + a_im * hr + si)

        hn_re, hn_im = lax.fori_loop(0, rows, body, (h0_re, h0_im))
        h_re = hre_scr[...]
        h_im = him_scr[...]
    else:
        h_re = h0_re
        h_im = h0_im
        hn_re = a_re * h_re - a_im * h_im + s_re
        hn_im = a_re * h_im + a_im * h_re + s_im
    hnre_ref[0, 0] = hn_re
    hnim_ref[0, 0] = hn_im
    hcat = jnp.concatenate([h_re, h_im], axis=1).astype(BF16)
    inter = lax.dot_general(hcat, nt_ref[0], (((1,), (1,)), ((), ())),
                            preferred_element_type=F32)
    d = d_ref[0]
    for t in range(S5_CHUNK):
        k0 = (S5_CHUNK - 1 - t) * LANES
        intra = jnp.dot(ub[:, :(t + 1) * LANES], wrev_ref[0, k0:, :], preferred_element_type=F32)
        sl = slice(t * LANES, (t + 1) * LANES)
        y_ref[0, pl.ds(t, rows, stride=S5_CHUNK), :] = intra + inter[:, sl] + d[:, sl] * u[:, sl]


def _s5(u4, prep, d_tiled, h0_re, h0_im, *, rows, nblk, scan):
    ms_re, ms_im, nt, wrev, a16_re, a16_im = prep
    rh = h0_re.shape[2]
    wspec = lambda shape: pl.BlockSpec((1,) + shape, lambda g, b: (g, 0, 0))
    hspec = pl.BlockSpec((1, 1, rh, STATE_BLOCK), lambda g, b: (g, b, 0, 0))
    scratch = [pltpu.VMEM((rows, STATE_BLOCK), F32)] * 4 if scan else []
    return pl.pallas_call(
        functools.partial(_s5_kernel, scan=scan),
        grid=(GROUP_BLOCKS, nblk),
        in_specs=[
            pl.BlockSpec((1, rows * S5_CHUNK, LANES), lambda g, b: (g, b, 0)),
            wspec((S5_ROW, STATE_BLOCK)), wspec((S5_ROW, STATE_BLOCK)),
            wspec((S5_ROW, 2 * STATE_BLOCK)), wspec((S5_ROW, LANES)),
            wspec((1, STATE_BLOCK)), wspec((1, STATE_BLOCK)), wspec((1, S5_ROW)),
            hspec, hspec,
        ],
        out_specs=[pl.BlockSpec((1, rows * S5_CHUNK, LANES), lambda g, b: (g, b, 0)), hspec, hspec],
        out_shape=[
            jax.ShapeDtypeStruct(u4.shape, F32),
            jax.ShapeDtypeStruct(h0_re.shape, F32),
            jax.ShapeDtypeStruct(h0_re.shape, F32),
        ],
        scratch_shapes=scratch,
        compiler_params=_cparams("parallel", "parallel"),
        name="s5",
    )(u4, ms_re, ms_im, nt, wrev, a16_re, a16_im, d_tiled, h0_re, h0_im)


def _split_bf16(x):
    hi = x.astype(BF16)
    return hi, (x - hi.astype(F32)).astype(BF16)


def _top2_of4(a):
    m1 = jnp.maximum(jnp.maximum(a[0], a[1]), jnp.maximum(a[2], a[3]))
    i1 = jnp.where(a[0] == m1, 0, jnp.where(a[1] == m1, 1, jnp.where(a[2] == m1, 2, 3)))
    b = [jnp.where(i1 == j, -jnp.inf, a[j]) for j in range(4)]
    m2 = jnp.maximum(jnp.maximum(b[0], b[1]), jnp.maximum(b[2], b[3]))
    i2 = jnp.where(b[0] == m2, 0, jnp.where(b[1] == m2, 1, jnp.where(b[2] == m2, 2, 3)))
    return m1, i1, m2, i2


def _route_rows(h2, wrh_ref, wrl_ref, br_ref):
    h_hi, h_lo = _split_bf16(h2)
    w_hi = wrh_ref[...]
    logits = (jnp.dot(h_hi, w_hi, preferred_element_type=F32)
              + jnp.dot(h_lo, w_hi, preferred_element_type=F32)
              + jnp.dot(h_hi, wrl_ref[...], preferred_element_type=F32))
    lt = logits.T[:N_EXPERTS] + br_ref[...]
    rows = [lt[e:e + 1] for e in range(N_EXPERTS)]
    mx = functools.reduce(jnp.maximum, rows)
    ex = [jnp.exp(r - mx) for r in rows]
    tot = functools.reduce(lambda p, q: p + q, ex)
    scores = [e / tot for e in ex]
    best = None
    for g in range(N_EXPERT_GROUPS):
        m1, i1, m2, i2 = _top2_of4(scores[g * EXPERTS_PER_GROUP:(g + 1) * EXPERTS_PER_GROUP])
        cand = (m1 + m2, m1, i1 + g * EXPERTS_PER_GROUP, m2, i2 + g * EXPERTS_PER_GROUP)
        if best is None:
            best = cand
        else:
            better = cand[0] > best[0]
            best = tuple(jnp.where(better, c, b) for c, b in zip(cand, best))
    _, v1, e1, v2, e2 = best
    den = v1 + v2
    w1 = v1 / den
    w2 = v2 / den
    first_lo = e1 < e2
    return (jnp.where(first_lo, e1, e2), jnp.where(first_lo, e2, e1),
            jnp.where(first_lo, w1, w2), jnp.where(first_lo, w2, w1))


def _mixer_out_kernel(x_ref, a_ref, y_ref, mod_ref, g_ref, wglu_ref, bglu_ref, wout_ref,
                      wrh_ref, wrl_ref, br_ref, *rest, route, tiles_per_mod):
    x = x_ref[...]
    mod = mod_ref[0]
    gate_m = mod[:, :D_MODEL]
    shift_f = mod[:, D_MODEL:2 * D_MODEL]
    scale_f = mod[:, 2 * D_MODEL:]
    ys = jax.nn.gelu(jnp.concatenate([y_ref[gb] for gb in range(GROUP_BLOCKS)], axis=1))
    glu = jnp.dot(ys.astype(BF16), wglu_ref[...], preferred_element_type=F32) + bglu_ref[...]
    b_out = ys * jax.nn.sigmoid(glu)
    mixed = jnp.concatenate([a_ref[...], b_out.astype(BF16)], axis=1)
    xmid = x + gate_m * jnp.dot(mixed, wout_ref[...], preferred_element_type=F32)
    h2 = _rms(xmid, g_ref[...]) * (1.0 + scale_f) + shift_f
    lo, hi, glo, ghi = _route_rows(h2, wrh_ref, wrl_ref, br_ref)
    if not route:
        xmid_ref, h2_ref, lo_ref, hi_ref, glo_ref, ghi_ref = rest
        xmid_ref[...] = xmid
        h2_ref[...] = h2
        lo_ref[...] = lo
        hi_ref[...] = hi
        glo_ref[...] = glo
        ghi_ref[...] = ghi
        return

    tri_ref, xext_ref, cls_ref, rank_ref, cnt_ref, run_ref = rest
    i = pl.program_id(0)
    tm = x.shape[0]

    @pl.when(i == 0)
    def _():
        run_ref[...] = jnp.zeros_like(run_ref)

    a = lo & (EXPERTS_PER_GROUP - 1)
    b = hi & (EXPERTS_PER_GROUP - 1)
    pair = jnp.where(a == 0, 0, jnp.where(a == 1, 3, 5)) + (b - a - 1)
    cls = (lo >> 2) * PAIRS_PER_GROUP + pair
    cls_ref[...] = cls
    onehot = lax.broadcasted_iota(jnp.int32, (CLASS_ROWS, tm), 0) == cls
    prefix = jnp.dot(jnp.where(onehot, 1.0, 0.0).astype(BF16), tri_ref[...],
                     preferred_element_type=F32)
    run = run_ref[...]
    rank = jnp.sum(jnp.where(onehot, prefix - 1.0 + run[:, :1], 0.0), axis=0, keepdims=True)
    rank_ref[...] = rank.astype(jnp.int32)
    run = run + prefix[:, tm - 1:tm]
    run_ref[...] = run
    cnt_ref[...] = run
    bid = jnp.full((1, tm), i // tiles_per_mod, jnp.int32).astype(F32)
    ext = jnp.concatenate([glo, ghi, bid, jnp.zeros((LANES - 3, tm), F32)], axis=0).T
    xext_ref[:, :D_MODEL] = xmid
    xext_ref[:, D_MODEL:] = ext


def _mixer_out(x2d, a_out, y4, mod, g_ffn, w_glu, b_glu, w_out, wr_hi, wr_lo, b_r,
               *, tm, tiles_per_mod, route):
    t = x2d.shape[0]
    rmod = mod.shape[1]
    const2 = lambda i: (0, 0)
    tok = pl.BlockSpec((tm, D_MODEL), lambda i: (i, 0))
    row = pl.BlockSpec((1, tm), lambda i: (0, i))
    in_specs = [
        tok,
        pl.BlockSpec((tm, GMLP_WIDTH), lambda i: (i, 0)),
        pl.BlockSpec((GROUP_BLOCKS, tm, LANES), lambda i: (0, i, 0)),
        pl.BlockSpec((1, rmod, 3 * D_MODEL), lambda i: (i // tiles_per_mod, 0, 0)),
        pl.BlockSpec((1, D_MODEL), const2),
        pl.BlockSpec((SSM_WIDTH, SSM_WIDTH), const2),
        pl.BlockSpec((1, SSM_WIDTH), const2),
        pl.BlockSpec((D_MODEL, D_MODEL), const2),
        pl.BlockSpec((D_MODEL, LANES), const2),
        pl.BlockSpec((D_MODEL, LANES), const2),
        pl.BlockSpec((N_EXPERTS, 1), const2),
    ]
    args = [x2d, a_out, y4, mod, g_ffn, w_glu, b_glu, w_out, wr_hi, wr_lo, b_r]
    if route:
        ids = jnp.arange(tm)
        args.append((ids[:, None] <= ids[None, :]).astype(BF16))
        in_specs.append(pl.BlockSpec((tm, tm), const2))
        out_specs = [pl.BlockSpec((tm, XEXT_WIDTH), lambda i: (i, 0)), row, row,
                     pl.BlockSpec((CLASS_ROWS, LANES), const2)]
        out_shape = [jax.ShapeDtypeStruct((t, XEXT_WIDTH), F32),
                     jax.ShapeDtypeStruct((1, t), jnp.int32),
                     jax.ShapeDtypeStruct((1, t), jnp.int32),
                     jax.ShapeDtypeStruct((CLASS_ROWS, LANES), F32)]
        scratch = [pltpu.VMEM((CLASS_ROWS, LANES), F32)]
    else:
        out_specs = [tok, tok, row, row, row, row]
        out_shape = [jax.ShapeDtypeStruct((t, D_MODEL), F32),
                     jax.ShapeDtypeStruct((t, D_MODEL), F32),
                     jax.ShapeDtypeStruct((1, t), jnp.int32),
                     jax.ShapeDtypeStruct((1, t), jnp.int32),
                     jax.ShapeDtypeStruct((1, t), F32),
                     jax.ShapeDtypeStruct((1, t), F32)]
        scratch = []
    return pl.pallas_call(
        functools.partial(_mixer_out_kernel, route=route, tiles_per_mod=tiles_per_mod),
        grid=(t // tm,),
        in_specs=in_specs,
        out_specs=out_specs,
        out_shape=out_shape,
        scratch_shapes=scratch,
        compiler_params=_cparams("arbitrary" if route else "parallel"),
        name="mixer_out",
    )(*args)


def _plan_kernel(pos_ref, zeros_hbm, src_ref, sem):
    fill = pltpu.make_async_copy(zeros_hbm, src_ref, sem)
    fill.start()
    fill.wait()

    def body(t, carry):
        src_ref[pos_ref[t]] = t
        return carry

    lax.fori_loop(0, pos_ref.shape[0], body, 0, unroll=8)


def _plan(pos, nslots):
    return pl.pallas_call(
        _plan_kernel,
        in_specs=[pl.BlockSpec(memory_space=pltpu.SMEM), pl.BlockSpec(memory_space=pl.ANY)],
        out_specs=pl.BlockSpec(memory_space=pltpu.SMEM),
        out_shape=jax.ShapeDtypeStruct((nslots,), jnp.int32),
        scratch_shapes=[pltpu.SemaphoreType.DMA(())],
        name="moe_plan",
    )(pos, jnp.zeros((nslots,), jnp.int32))


def _moe_routed_kernel(tlo_ref, thi_ref, tval_ref, src_ref,
                       xext_hbm, wgl_ref, wul_ref, wdl_ref, wgh_ref, wuh_ref, wdh_ref,
                       mod_ref, g_ref, gfin_ref, out_hbm, buf_ref, obuf_ref, sem_in, sem_out,
                       *, nseq, final_norm):
    i = pl.program_id(0)
    nt = pl.num_programs(0)
    slot = i % 2
    other = 1 - slot
    nv = tval_ref[i]
    nv_next = jnp.where(i + 1 < nt, tval_ref[jnp.minimum(i + 1, nt - 1)], 0)
    nv_prev = jnp.where(i >= 1, tval_ref[jnp.maximum(i - 1, 0)], 0)
    nv_prev2 = jnp.where(i >= 2, tval_ref[jnp.maximum(i - 2, 0)], 0)

    def wait_rows(n, copy):
        n8 = pl.multiple_of((n >> 3) << 3, 8)

        @pl.when(n8 > 0)
        def _():
            copy(0, n8).wait()

        def one(j, c):
            copy(0, 1).wait()
            return c

        lax.fori_loop(n8, n, one, 0)

    def gather(tile, sl, n, wait):
        if wait:
            wait_rows(n, lambda r0, nr: pltpu.make_async_copy(
                xext_hbm.at[pl.ds(r0, nr), :], buf_ref.at[sl, pl.ds(r0, nr), :], sem_in.at[sl]))
            return

        def start(j, c):
            tok = src_ref[tile * MOE_TILE + j]
            pltpu.make_async_copy(xext_hbm.at[pl.ds(tok, 1), :], buf_ref.at[sl, pl.ds(j, 1), :],
                                  sem_in.at[sl]).start()
            return c

        lax.fori_loop(0, n, start, 0)

    def scatter(tile, sl, n, wait):
        if wait:
            wait_rows(n, lambda r0, nr: pltpu.make_async_copy(
                obuf_ref.at[sl, pl.ds(r0, nr), :], out_hbm.at[pl.ds(r0, nr), :], sem_out.at[sl]))
            return

        def start(j, c):
            tok = src_ref[tile * MOE_TILE + j]
            pltpu.make_async_copy(obuf_ref.at[sl, pl.ds(j, 1), :], out_hbm.at[pl.ds(tok, 1), :],
                                  sem_out.at[sl]).start()
            return c

        lax.fori_loop(0, n, start, 0)

    @pl.when(i == 0)
    def _():
        buf_ref[...] = jnp.zeros_like(buf_ref)
        gather(0, 0, nv, wait=False)

    @pl.when(nv_next > 0)
    def _():
        gather(i + 1, other, nv_next, wait=False)

    @pl.when(nv_prev2 > 0)
    def _():
        scatter(i - 2, slot, nv_prev2, wait=True)

    @pl.when(nv > 0)
    def _():
        gather(i, slot, nv, wait=True)
        buf = buf_ref[slot]
        xm = buf[:, :D_MODEL]
        glo = buf[:, D_MODEL:D_MODEL + 1]
        ghi = buf[:, D_MODEL + 1:D_MODEL + 2]
        bid = buf[:, D_MODEL + 2:D_MODEL + 3]
        mod = mod_ref[...]

        def per_row(k):
            out = mod[nseq - 1:nseq, k * D_MODEL:(k + 1) * D_MODEL]
            for s in range(nseq - 2, -1, -1):
                out = jnp.where(bid == float(s), mod[s:s + 1, k * D_MODEL:(k + 1) * D_MODEL], out)
            return out

        shift_f, scale_f, gate_f = per_row(0), per_row(1), per_row(2)
        h = (_rms(xm, g_ref[...]) * (1.0 + scale_f) + shift_f).astype(BF16)
        ffn = None
        for wg, wu, wd, gate in ((wgl_ref, wul_ref, wdl_ref, glo), (wgh_ref, wuh_ref, wdh_ref, ghi)):
            he = (jax.nn.silu(jnp.dot(h, wg[0], preferred_element_type=F32))
                  * jnp.dot(h, wu[0], preferred_element_type=F32))
            y = gate * jnp.dot(he.astype(BF16), wd[0], preferred_element_type=F32)
            ffn = y if ffn is None else ffn + y
        x_new = xm + gate_f * ffn
        if final_norm:
            x_new = _rms(x_new, gfin_ref[...])
        obuf_ref[slot] = x_new
        scatter(i, slot, nv, wait=False)

    @pl.when(i == nt - 1)
    def _():
        @pl.when(nv_prev > 0)
        def _():
            scatter(i - 1, other, nv_prev, wait=True)

        @pl.when(nv > 0)
        def _():
            scatter(i, slot, nv, wait=True)


def _moe_routed(xext, cls, rank, counts, wg, wu, wd, mod, g_ffn, g_final, *, final_norm):
    t = xext.shape[0]
    nseq = mod.shape[0]
    ntiles = t // MOE_TILE + N_CLASSES
    nslots = ntiles * MOE_TILE
    counts = counts[:N_CLASSES, 0].astype(jnp.int32)
    tiles_c = (counts + MOE_TILE - 1) // MOE_TILE
    tile_end = jnp.cumsum(tiles_c)
    tile_start = tile_end - tiles_c
    used = tile_end[-1]
    pos = (tile_start * MOE_TILE)[cls[0]] + rank[0]
    src = _plan(pos, nslots)
    tid = jnp.arange(ntiles, dtype=jnp.int32)
    tcls = jnp.sum(tile_end[None, :] <= jnp.minimum(tid, used - 1)[:, None], axis=1).astype(jnp.int32)
    tval = jnp.where(tid < used,
                     jnp.clip(counts[tcls] - (tid - tile_start[tcls]) * MOE_TILE, 0, MOE_TILE), 0)
    tlo = jnp.asarray(CLASS_LO, jnp.int32)[tcls]
    thi = jnp.asarray(CLASS_HI, jnp.int32)[tcls]
    wspec_lo = lambda shape: pl.BlockSpec(shape, lambda i, tlo, thi, tval, src: (tlo[i], 0, 0))
    wspec_hi = lambda shape: pl.BlockSpec(shape, lambda i, tlo, thi, tval, src: (thi[i], 0, 0))
    const2 = lambda i, tlo, thi, tval, src: (0, 0)
    up = (1, D_MODEL, D_EXPERT)
    down = (1, D_EXPERT, D_MODEL)
    return pl.pallas_call(
        functools.partial(_moe_routed_kernel, nseq=nseq, final_norm=final_norm),
        grid_spec=pltpu.PrefetchScalarGridSpec(
            num_scalar_prefetch=4,
            grid=(ntiles,),
            in_specs=[
                pl.BlockSpec(memory_space=pl.ANY),
                wspec_lo(up), wspec_lo(up), wspec_lo(down),
                wspec_hi(up), wspec_hi(up), wspec_hi(down),
                pl.BlockSpec((nseq, 3 * D_MODEL), const2),
                pl.BlockSpec((1, D_MODEL), const2),
                pl.BlockSpec((1, D_MODEL), const2),
            ],
            out_specs=pl.BlockSpec(memory_space=pl.ANY),
            scratch_shapes=[
                pltpu.VMEM((2, MOE_TILE, XEXT_WIDTH), F32),
                pltpu.VMEM((2, MOE_TILE, D_MODEL), F32),
                pltpu.SemaphoreType.DMA((2,)),
                pltpu.SemaphoreType.DMA((2,)),
            ],
        ),
        out_shape=jax.ShapeDtypeStruct((t, D_MODEL), F32),
        compiler_params=_cparams("arbitrary"),
        name="moe_routed",
    )(tlo, thi, tval, src, xext, wg, wu, wd, wg, wu, wd, mod, g_ffn, g_final)


def _moe_kernel(h2_ref, gates_ref, wg_ref, wu_ref, wd_ref, xmid_ref, mod_ref, gfin_ref,
                o_ref, acc_ref, *, final_norm):
    e = pl.program_id(1)

    @pl.when(e == 0)
    def _():
        acc_ref[...] = jnp.zeros_like(acc_ref)

    h = h2_ref[...].astype(BF16)
    he = (jax.nn.silu(jnp.dot(h, wg_ref[0], preferred_element_type=F32))
          * jnp.dot(h, wu_ref[0], preferred_element_type=F32))
    y = jnp.dot(he.astype(BF16), wd_ref[0], preferred_element_type=F32)
    gates = gates_ref[...]
    lane = lax.broadcasted_iota(jnp.int32, gates.shape, 1)
    gcol = jnp.sum(jnp.where(lane == e, gates, 0.0), axis=1, keepdims=True)
    acc_ref[...] += gcol * y

    @pl.when(e == N_EXPERTS - 1)
    def _():
        x = xmid_ref[...] + mod_ref[0] * acc_ref[...]
        if final_norm:
            x = _rms(x, gfin_ref[...])
        o_ref[...] = x


def _moe(h2, gates, wg, wu, wd, xmid, mod, g_final, *, tm, tiles_per_mod, final_norm):
    t = h2.shape[0]
    rmod = mod.shape[1]
    tok = pl.BlockSpec((tm, D_MODEL), lambda i, e: (i, 0))
    return pl.pallas_call(
        functools.partial(_moe_kernel, final_norm=final_norm),
        grid=(t // tm, N_EXPERTS),
        in_specs=[
            tok,
            pl.BlockSpec((tm, N_EXPERTS), lambda i, e: (i, 0)),
            pl.BlockSpec((1, D_MODEL, D_EXPERT), lambda i, e: (e, 0, 0)),
            pl.BlockSpec((1, D_MODEL, D_EXPERT), lambda i, e: (e, 0, 0)),
            pl.BlockSpec((1, D_EXPERT, D_MODEL), lambda i, e: (e, 0, 0)),
            tok,
            pl.BlockSpec((1, rmod, D_MODEL), lambda i, e: (i // tiles_per_mod, 0, 0)),
            pl.BlockSpec((1, D_MODEL), lambda i, e: (0, 0)),
        ],
        out_specs=tok,
        out_shape=jax.ShapeDtypeStruct((t, D_MODEL), F32),
        scratch_shapes=[pltpu.VMEM((tm, D_MODEL), F32)],
        compiler_params=_cparams("parallel", "arbitrary"),
        name="moe",
    )(h2, gates, wg, wu, wd, xmid, mod, g_final)


def _layer(x2d, mod, lw, h0_re, h0_im, *, seq_len, nseq, sample, final_norm):
    t = x2d.shape[0]
    if sample:
        tm, tiles_per_mod, tm_moe = t, 1, t
        s5_rows, s5_blocks = nseq, 1
    else:
        tm = min(512, seq_len)
        tiles_per_mod = seq_len // tm
        tm_moe = min(512, seq_len)
        s5_rows, s5_blocks = seq_len // S5_CHUNK, nseq
    outs = _mixer_in(x2d, mod[..., :2 * D_MODEL], lw["g_mix"], lw["w_in"], lw["v_gain"],
                     lw["ws_sample"] if sample else lw["ws"], lw["bs_sample"] if sample else lw["bs"],
                     tm=tm, tiles_per_mod=tiles_per_mod, want_v=sample)
    a_out, u4 = outs[0], outs[1]
    v_rows = outs[2] if sample else None
    y4, hn_re, hn_im = _s5(u4, lw["prep"], lw["d_tiled"],
                           h0_re, h0_im, rows=s5_rows, nblk=s5_blocks, scan=not sample)
    mo = _mixer_out(x2d, a_out, y4, mod[..., 2 * D_MODEL:5 * D_MODEL],
                    lw["g_ffn"], lw["w_glu"], lw["b_glu"], lw["w_out"], lw["wr_hi"], lw["wr_lo"], lw["b_r"],
                    tm=tm, tiles_per_mod=tiles_per_mod, route=not sample)
    if sample:
        xmid, h2, lo, hi, glo, ghi = mo
        eids = jnp.arange(N_EXPERTS, dtype=jnp.int32)[None, :]
        gates = (jnp.where(lo[0][:, None] == eids, glo[0][:, None], 0.0)
                 + jnp.where(hi[0][:, None] == eids, ghi[0][:, None], 0.0))
        x_new = _moe(h2, gates, lw["wg"], lw["wu"], lw["wd"], xmid, mod[..., 5 * D_MODEL:],
                     lw["g_final"], tm=tm_moe, tiles_per_mod=1, final_norm=final_norm)
    else:
        xext, cls, rank, counts = mo
        x_new = _moe_routed(xext, cls, rank, counts, lw["wg"], lw["wu"], lw["wd"],
                            mod[:, 0, 3 * D_MODEL:], lw["g_ffn"], lw["g_final"], final_norm=final_norm)
    return x_new, hn_re, hn_im, v_rows


def _state_out(h, nseq):
    return h.reshape(GROUP_BLOCKS, nseq, GROUPS_PER_BLOCK, SSM_STATE).transpose(1, 0, 2, 3).reshape(
        nseq, SSM_GROUPS, SSM_STATE)


def _state_in(h, nblk, rh):
    nseq = h.shape[0]
    return h.reshape(nseq, GROUP_BLOCKS, STATE_BLOCK).transpose(1, 0, 2).reshape(
        GROUP_BLOCKS, nblk, rh, STATE_BLOCK)


def kernel(x_prompt, x_sample, c_prompt, c_sample, state_s5_re, state_s5_im, w_ada, b_ada, g_norm_mix, g_norm_ffn, w_in, gmlp_v_gain, gmlp_w_spatial, gmlp_b_spatial, s5_a_re, s5_a_im, s5_log_dt, s5_b_re, s5_b_im, s5_c_re, s5_c_im, s5_d, s5_w_glu, s5_b_glu, w_out, w_router, b_router, w_gate, w_up, w_down, g_final):
    nb, seq_len, _ = x_prompt.shape
    ns, dec_len, _ = x_sample.shape
    assert dec_len == S5_CHUNK and ns * dec_len == GMLP_CHUNK and nb + ns <= ADA_ROWS
    assert seq_len % GMLP_CHUNK == 0

    c_all = jnp.concatenate([c_prompt, c_sample, jnp.zeros((ADA_ROWS - nb - ns, D_MODEL), F32)], axis=0)
    mod_all = _ada(c_all, w_ada, b_ada)

    pos = jnp.arange(GMLP_CHUNK)
    causal = (pos[None, :] // CHUNK) <= (pos[:, None] // CHUNK)
    wr_pad = jnp.pad(w_router, ((0, 0), (0, LANES - N_EXPERTS)))
    wr_hi = wr_pad.astype(BF16)
    wr_lo = (wr_pad - wr_hi.astype(F32)).astype(BF16)
    b_r = b_router.reshape(N_EXPERTS, 1)
    g_fin = g_final.reshape(1, D_MODEL)
    eye_s = jnp.eye(ns, dtype=F32)

    xp = x_prompt.reshape(nb * seq_len, D_MODEL)
    xs = x_sample.reshape(ns * dec_len, D_MODEL)
    zeros_p = jnp.zeros((GROUP_BLOCKS, nb, 1, STATE_BLOCK), F32)
    sp_re, sp_im, ss_re, ss_im, v_new = [], [], [], [], []
    for l in range(DEPTH):
        ws = jnp.where(causal[None], gmlp_w_spatial[l], 0.0)
        ws_sample = jnp.einsum("ab,hij->haibj", eye_s, ws[:, :dec_len, :dec_len]).reshape(
            GMLP_HEADS, GMLP_CHUNK, GMLP_CHUNK)
        bs = jnp.repeat(gmlp_b_spatial[l].T, GMLP_HEAD_DIM, axis=1)
        lw = dict(
            g_mix=g_norm_mix[l].reshape(1, D_MODEL), g_ffn=g_norm_ffn[l].reshape(1, D_MODEL),
            w_in=w_in[l].astype(BF16), v_gain=gmlp_v_gain[l].reshape(1, GMLP_WIDTH),
            ws=ws.astype(BF16), ws_sample=ws_sample.astype(BF16),
            bs=bs, bs_sample=jnp.tile(bs[:dec_len], (ns, 1)),
            prep=_s5_prep(s5_a_re[l], s5_a_im[l], s5_log_dt[l], s5_b_re[l], s5_b_im[l],
                          s5_c_re[l], s5_c_im[l]),
            d_tiled=jnp.tile(s5_d[l].reshape(GROUP_BLOCKS, 1, LANES), (1, 1, S5_CHUNK)),
            w_glu=s5_w_glu[l].astype(BF16), b_glu=s5_b_glu[l].reshape(1, SSM_WIDTH),
            w_out=w_out[l].astype(BF16), wr_hi=wr_hi, wr_lo=wr_lo, b_r=b_r,
            wg=w_gate[l].astype(BF16), wu=w_up[l].astype(BF16), wd=w_down[l].astype(BF16),
            g_final=g_fin,
        )
        last = l == DEPTH - 1
        mod_p = mod_all[l, :nb].reshape(nb, 1, 6 * D_MODEL)
        mod_s = jnp.repeat(mod_all[l, nb:nb + ns], dec_len, axis=0).reshape(1, ns * dec_len, 6 * D_MODEL)
        xp, hp_re, hp_im, _ = _layer(xp, mod_p, lw, zeros_p, zeros_p, seq_len=seq_len, nseq=nb,
                                     sample=False, final_norm=last)
        xs, hs_re, hs_im, vs = _layer(xs, mod_s, lw, _state_in(state_s5_re[l], 1, ns),
                                      _state_in(state_s5_im[l], 1, ns), seq_len=dec_len, nseq=ns,
                                      sample=True, final_norm=last)
        sp_re.append(_state_out(hp_re, nb))
        sp_im.append(_state_out(hp_im, nb))
        ss_re.append(_state_out(hs_re, ns))
        ss_im.append(_state_out(hs_im, ns))
        v_new.append(vs.reshape(ns, dec_len, GMLP_WIDTH))
    return (xp.reshape(nb, seq_len, D_MODEL), xs.reshape(ns, dec_len, D_MODEL),
            jnp.stack(sp_re), jnp.stack(sp_im), jnp.stack(ss_re), jnp.stack(ss_im), jnp.stack(v_new))
```

```python
import functools

import jax
import jax.numpy as jnp
from jax import lax
from jax.experimental import pallas as pl
from jax.experimental.pallas import tpu as pltpu

F32 = jnp.float32
BF16 = jnp.bfloat16

D_MODEL = 1024
DEPTH = 2
CHUNK = 64
GMLP_CHUNK = 128
GMLP_WIDTH = 512
GMLP_HEADS = 4
GMLP_HEAD_DIM = 128
SSM_WIDTH = 512
SSM_GROUP = 16
SSM_GROUPS = 32
SSM_STATE = 64
IN_WIDTH = 1536
N_EXPERTS = 16
EXPERTS_PER_GROUP = 4
N_EXPERT_GROUPS = 4
D_EXPERT = 512
EPS = 1e-6

LANES = 128
S5_CHUNK = 16
GROUP_BLOCKS = 4
GROUPS_PER_BLOCK = SSM_GROUPS // GROUP_BLOCKS
STATE_BLOCK = GROUPS_PER_BLOCK * SSM_STATE
S5_ROW = S5_CHUNK * LANES
ADA_ROWS = 16
PAIRS_PER_GROUP = 6
N_CLASSES = N_EXPERT_GROUPS * PAIRS_PER_GROUP
CLASS_ROWS = 32
_PAIRS = [(a, b) for a in range(EXPERTS_PER_GROUP) for b in range(a + 1, EXPERTS_PER_GROUP)]
CLASS_LO = [g * EXPERTS_PER_GROUP + a for g in range(N_EXPERT_GROUPS) for a, _ in _PAIRS]
CLASS_HI = [g * EXPERTS_PER_GROUP + b for g in range(N_EXPERT_GROUPS) for _, b in _PAIRS]
MOE_TILE = 256
XEXT_WIDTH = D_MODEL + LANES
VMEM_LIMIT = 56 * 1024 * 1024


def _cparams(*sem):
    return pltpu.CompilerParams(dimension_semantics=sem, vmem_limit_bytes=VMEM_LIMIT)


def _ada_kernel(c_ref, w_ref, b_ref, o_ref):
    c = c_ref[...]
    s = (c * jax.nn.sigmoid(c)).astype(BF16)
    o_ref[0] = jnp.dot(s, w_ref[0].astype(BF16), preferred_element_type=F32) + b_ref[0]


def _ada(c_all, w_ada, b_ada):
    nblk = 6
    return pl.pallas_call(
        _ada_kernel,
        grid=(DEPTH, nblk),
        in_specs=[
            pl.BlockSpec((ADA_ROWS, D_MODEL), lambda l, j: (0, 0)),
            pl.BlockSpec((1, D_MODEL, D_MODEL), lambda l, j: (l, 0, j)),
            pl.BlockSpec((1, 1, D_MODEL), lambda l, j: (l, 0, j)),
        ],
        out_specs=pl.BlockSpec((1, ADA_ROWS, D_MODEL), lambda l, j: (l, 0, j)),
        out_shape=jax.ShapeDtypeStruct((DEPTH, ADA_ROWS, 6 * D_MODEL), F32),
        compiler_params=_cparams("parallel", "parallel"),
        name="ada",
    )(c_all, w_ada, b_ada.reshape(DEPTH, 1, 6 * D_MODEL))


def _prep_kernel(are_ref, aim_ref, ldt_ref, bre_ref, bim_ref, cre_ref, cim_ref,
                 msre_ref, msim_ref, nt_ref, wrev_ref, a16re_ref, a16im_ref):
    a_re = are_ref[0]
    a_im = aim_ref[0]
    dt = jnp.exp(ldt_ref[0])
    rho = a_re * dt
    th = a_im * dt
    kk = jnp.minimum(lax.broadcasted_iota(jnp.int32, (24, STATE_BLOCK), 0), S5_CHUNK).astype(F32)
    mag = jnp.exp(kk * rho)
    pw_re = mag * jnp.cos(kk * th)
    pw_im = mag * jnp.sin(kk * th)

    lb_re = pw_re[1:2]
    lb_im = pw_im[1:2]
    num_re = lb_re - 1.0
    num_im = lb_im
    den = a_re * a_re + a_im * a_im
    coef_re = (num_re * a_re + num_im * a_im) / den
    coef_im = (num_im * a_re - num_re * a_im) / den
    b_re = bre_ref[0]
    b_im = bim_ref[0]
    bb_re = coef_re * b_re - coef_im * b_im
    bb_im = coef_re * b_im + coef_im * b_re

    rows = lax.broadcasted_iota(jnp.int32, (LANES, STATE_BLOCK), 0)
    cols = lax.broadcasted_iota(jnp.int32, (LANES, STATE_BLOCK), 1)
    same_group = (rows >> 4) == (cols >> 6)

    def blockdiag(x16):
        return jnp.where(same_group, jnp.concatenate([x16] * GROUPS_PER_BLOCK, axis=0), 0.0)

    for s in range(S5_CHUNK):
        k = S5_CHUNK - 1 - s
        p_re = pw_re[k:k + 1]
        p_im = pw_im[k:k + 1]
        msre_ref[0, s * LANES:(s + 1) * LANES, :] = blockdiag(p_re * bb_re - p_im * bb_im).astype(BF16)
        msim_ref[0, s * LANES:(s + 1) * LANES, :] = blockdiag(p_re * bb_im + p_im * bb_re).astype(BF16)

    bcat = jnp.concatenate([blockdiag(bb_re), blockdiag(bb_im)], axis=1)
    c_re = cre_ref[0]
    c_im = cim_ref[0]
    for k in range(S5_CHUNK + 1):
        p_re = pw_re[k:k + 1]
        p_im = pw_im[k:k + 1]
        cl = jnp.concatenate([blockdiag(c_re * p_re - c_im * p_im),
                              -blockdiag(c_re * p_im + c_im * p_re)], axis=1)
        if k >= 1:
            nt_ref[0, (k - 1) * LANES:k * LANES, :] = cl.astype(BF16)
        if k < S5_CHUNK:
            wl = lax.dot_general(bcat, cl, (((1,), (1,)), ((), ())),
                                 precision=lax.Precision.HIGHEST, preferred_element_type=F32)
            j = S5_CHUNK - 1 - k
            wrev_ref[0, j * LANES:(j + 1) * LANES, :] = wl.astype(BF16)

    a16re_ref[0] = pw_re[S5_CHUNK:S5_CHUNK + 1]
    a16im_ref[0] = pw_im[S5_CHUNK:S5_CHUNK + 1]


def _s5_prep(a_re, a_im, log_dt, b_re, b_im, c_re, c_im):
    nstate = SSM_GROUPS * SSM_STATE

    def lane_row(v):
        return v.reshape(GROUP_BLOCKS, 1, STATE_BLOCK)

    def rows16(v):
        return v.reshape(SSM_GROUP, GROUP_BLOCKS, STATE_BLOCK).transpose(1, 0, 2)

    ldt = jnp.repeat(log_dt, SSM_STATE).reshape(SSM_GROUPS, SSM_STATE)
    bt_re = rows16(b_re.transpose(2, 0, 1).reshape(SSM_GROUP, nstate))
    bt_im = rows16(b_im.transpose(2, 0, 1).reshape(SSM_GROUP, nstate))
    ct_re = rows16(c_re.transpose(1, 0, 2).reshape(SSM_GROUP, nstate))
    ct_im = rows16(c_im.transpose(1, 0, 2).reshape(SSM_GROUP, nstate))
    row_spec = pl.BlockSpec((1, 1, STATE_BLOCK), lambda g: (g, 0, 0))
    r16_spec = pl.BlockSpec((1, SSM_GROUP, STATE_BLOCK), lambda g: (g, 0, 0))
    return pl.pallas_call(
        _prep_kernel,
        grid=(GROUP_BLOCKS,),
        in_specs=[row_spec, row_spec, row_spec, r16_spec, r16_spec, r16_spec, r16_spec],
        out_specs=[
            pl.BlockSpec((1, S5_ROW, STATE_BLOCK), lambda g: (g, 0, 0)),
            pl.BlockSpec((1, S5_ROW, STATE_BLOCK), lambda g: (g, 0, 0)),
            pl.BlockSpec((1, S5_ROW, 2 * STATE_BLOCK), lambda g: (g, 0, 0)),
            pl.BlockSpec((1, S5_ROW, LANES), lambda g: (g, 0, 0)),
            row_spec, row_spec,
        ],
        out_shape=[
            jax.ShapeDtypeStruct((GROUP_BLOCKS, S5_ROW, STATE_BLOCK), BF16),
            jax.ShapeDtypeStruct((GROUP_BLOCKS, S5_ROW, STATE_BLOCK), BF16),
            jax.ShapeDtypeStruct((GROUP_BLOCKS, S5_ROW, 2 * STATE_BLOCK), BF16),
            jax.ShapeDtypeStruct((GROUP_BLOCKS, S5_ROW, LANES), BF16),
            jax.ShapeDtypeStruct((GROUP_BLOCKS, 1, STATE_BLOCK), F32),
            jax.ShapeDtypeStruct((GROUP_BLOCKS, 1, STATE_BLOCK), F32),
        ],
        compiler_params=_cparams("parallel"),
        name="s5_prep",
    )(lane_row(a_re), lane_row(a_im), lane_row(ldt), bt_re, bt_im, ct_re, ct_im)


def _rms(x, g):
    return x * lax.rsqrt(jnp.mean(x * x, axis=-1, keepdims=True) + EPS) * g


def _mixer_in_kernel(x_ref, mod_ref, g_ref, win_ref, vg_ref, ws_ref, bs_ref,
                     a_ref, u_ref, *v_ref, tm):
    x = x_ref[...]
    mod = mod_ref[0]
    shift = mod[:, :D_MODEL]
    scale = mod[:, D_MODEL:]
    h = _rms(x, g_ref[...]) * (1.0 + scale) + shift
    proj = jnp.dot(h.astype(BF16), win_ref[...], preferred_element_type=F32)
    z = jax.nn.gelu(proj[:, :2 * GMLP_WIDTH])
    u = z[:, :GMLP_WIDTH]
    v = z[:, GMLP_WIDTH:]
    vc = v - jnp.mean(v, axis=-1, keepdims=True)
    vn = vc * lax.rsqrt(jnp.mean(vc * vc, axis=-1, keepdims=True) + EPS) * vg_ref[...]
    if v_ref:
        v_ref[0][...] = vn
    vb = vn.astype(BF16)
    bias = bs_ref[...]
    for c in range(tm // GMLP_CHUNK):
        r0 = c * GMLP_CHUNK
        for hh in range(GMLP_HEADS):
            l0 = hh * GMLP_HEAD_DIM
            mixed = jnp.dot(ws_ref[hh], vb[r0:r0 + GMLP_CHUNK, l0:l0 + GMLP_HEAD_DIM],
                            preferred_element_type=F32) + bias[:, l0:l0 + GMLP_HEAD_DIM]
            a_ref[r0:r0 + GMLP_CHUNK, l0:l0 + GMLP_HEAD_DIM] = (
                u[r0:r0 + GMLP_CHUNK, l0:l0 + GMLP_HEAD_DIM] * mixed).astype(BF16)
    for gb in range(GROUP_BLOCKS):
        l0 = 2 * GMLP_WIDTH + gb * LANES
        u_ref[gb] = proj[:, l0:l0 + LANES]


def _mixer_in(x2d, mod, g_mix, w_in, v_gain, ws, bs, *, tm, tiles_per_mod, want_v):
    t = x2d.shape[0]
    rmod = mod.shape[1]
    const2 = lambda i: (0, 0)
    out_shape = [jax.ShapeDtypeStruct((t, GMLP_WIDTH), BF16),
                 jax.ShapeDtypeStruct((GROUP_BLOCKS, t, LANES), F32)]
    out_specs = [pl.BlockSpec((tm, GMLP_WIDTH), lambda i: (i, 0)),
                 pl.BlockSpec((GROUP_BLOCKS, tm, LANES), lambda i: (0, i, 0))]
    if want_v:
        out_shape.append(jax.ShapeDtypeStruct((t, GMLP_WIDTH), F32))
        out_specs.append(pl.BlockSpec((tm, GMLP_WIDTH), lambda i: (i, 0)))
    return pl.pallas_call(
        functools.partial(_mixer_in_kernel, tm=tm),
        grid=(t // tm,),
        in_specs=[
            pl.BlockSpec((tm, D_MODEL), lambda i: (i, 0)),
            pl.BlockSpec((1, rmod, 2 * D_MODEL), lambda i: (i // tiles_per_mod, 0, 0)),
            pl.BlockSpec((1, D_MODEL), const2),
            pl.BlockSpec((D_MODEL, IN_WIDTH), const2),
            pl.BlockSpec((1, GMLP_WIDTH), const2),
            pl.BlockSpec((GMLP_HEADS, GMLP_CHUNK, GMLP_CHUNK), lambda i: (0, 0, 0)),
            pl.BlockSpec((GMLP_CHUNK, GMLP_WIDTH), const2),
        ],
        out_specs=out_specs,
        out_shape=out_shape,
        compiler_params=_cparams("parallel"),
        name="mixer_in",
    )(x2d, mod, g_mix, w_in, v_gain, ws, bs)


def _s5_kernel(u_ref, msre_ref, msim_ref, nt_ref, wrev_ref, a16re_ref, a16im_ref, d_ref,
               h0re_ref, h0im_ref, y_ref, hnre_ref, hnim_ref, *scratch, scan):
    rows = u_ref.shape[1] // S5_CHUNK
    u = jnp.concatenate([u_ref[0, pl.ds(s, rows, stride=S5_CHUNK), :] for s in range(S5_CHUNK)], axis=1)
    ub = u.astype(BF16)
    s_re = jnp.dot(ub, msre_ref[0], preferred_element_type=F32)
    s_im = jnp.dot(ub, msim_ref[0], preferred_element_type=F32)
    a_re = a16re_ref[0]
    a_im = a16im_ref[0]
    h0_re = h0re_ref[0, 0]
    h0_im = h0im_ref[0, 0]
    if scan:
        sre_scr, sim_scr, hre_scr, him_scr = scratch
        sre_scr[...] = s_re
        sim_scr[...] = s_im

        def body(r, carry):
            hr, hi = carry
            hre_scr[pl.ds(r, 1), :] = hr
            him_scr[pl.ds(r, 1), :] = hi
            sr = sre_scr[pl.ds(r, 1), :]
            si = sim_scr[pl.ds(r, 1), :]
            return (a_re * hr - a_im * hi + sr, a_re * hi + a_im * hr + si)

        hn_re, hn_im = lax.fori_loop(0, rows, body, (h0_re, h0_im))
        h_re = hre_scr[...]
        h_im = him_scr[...]
    else:
        h_re = h0_re
        h_im = h0_im
        hn_re = a_re * h_re - a_im * h_im + s_re
        hn_im = a_re * h_im + a_im * h_re + s_im
    hnre_ref[0, 0] = hn_re
    hnim_ref[0, 0] = hn_im
    hcat = jnp.concatenate([h_re, h_im], axis=1).astype(BF16)
    inter = lax.dot_general(hcat, nt_ref[0], (((1,), (1,)), ((), ())),
                            preferred_element_type=F32)
    d = d_ref[0]
    for t in range(S5_CHUNK):
        k0 = (S5_CHUNK - 1 - t) * LANES
        intra = jnp.dot(ub[:, :(t + 1) * LANES], wrev_ref[0, k0:, :], preferred_element_type=F32)
        sl = slice(t * LANES, (t + 1) * LANES)
        y_ref[0, pl.ds(t, rows, stride=S5_CHUNK), :] = intra + inter[:, sl] + d[:, sl] * u[:, sl]


def _s5(u4, prep, d_tiled, h0_re, h0_im, *, rows, nblk, scan):
    ms_re, ms_im, nt, wrev, a16_re, a16_im = prep
    rh = h0_re.shape[2]
    wspec = lambda shape: pl.BlockSpec((1,) + shape, lambda g, b: (g, 0, 0))
    hspec = pl.BlockSpec((1, 1, rh, STATE_BLOCK), lambda g, b: (g, b, 0, 0))
    scratch = [pltpu.VMEM((rows, STATE_BLOCK), F32)] * 4 if scan else []
    return pl.pallas_call(
        functools.partial(_s5_kernel, scan=scan),
        grid=(GROUP_BLOCKS, nblk),
        in_specs=[
            pl.BlockSpec((1, rows * S5_CHUNK, LANES), lambda g, b: (g, b, 0)),
            wspec((S5_ROW, STATE_BLOCK)), wspec((S5_ROW, STATE_BLOCK)),
            wspec((S5_ROW, 2 * STATE_BLOCK)), wspec((S5_ROW, LANES)),
            wspec((1, STATE_BLOCK)), wspec((1, STATE_BLOCK)), wspec((1, S5_ROW)),
            hspec, hspec,
        ],
        out_specs=[pl.BlockSpec((1, rows * S5_CHUNK, LANES), lambda g, b: (g, b, 0)), hspec, hspec],
        out_shape=[
            jax.ShapeDtypeStruct(u4.shape, F32),
            jax.ShapeDtypeStruct(h0_re.shape, F32),
            jax.ShapeDtypeStruct(h0_re.shape, F32),
        ],
        scratch_shapes=scratch,
        compiler_params=_cparams("parallel", "parallel"),
        name="s5",
    )(u4, ms_re, ms_im, nt, wrev, a16_re, a16_im, d_tiled, h0_re, h0_im)


def _split_bf16(x):
    hi = x.astype(BF16)
    return hi, (x - hi.astype(F32)).astype(BF16)


def _top2_of4(a):
    m1 = jnp.maximum(jnp.maximum(a[0], a[1]), jnp.maximum(a[2], a[3]))
    i1 = jnp.where(a[0] == m1, 0, jnp.where(a[1] == m1, 1, jnp.where(a[2] == m1, 2, 3)))
    b = [jnp.where(i1 == j, -jnp.inf, a[j]) for j in range(4)]
    m2 = jnp.maximum(jnp.maximum(b[0], b[1]), jnp.maximum(b[2], b[3]))
    i2 = jnp.where(b[0] == m2, 0, jnp.where(b[1] == m2, 1, jnp.where(b[2] == m2, 2, 3)))
    return m1, i1, m2, i2


def _route_rows(h2, wrh_ref, wrl_ref, br_ref):
    h_hi, h_lo = _split_bf16(h2)
    w_hi = wrh_ref[...]
    logits = (jnp.dot(h_hi, w_hi, preferred_element_type=F32)
              + jnp.dot(h_lo, w_hi, preferred_element_type=F32)
              + jnp.dot(h_hi, wrl_ref[...], preferred_element_type=F32))
    lt = logits.T[:N_EXPERTS] + br_ref[...]
    rows = [lt[e:e + 1] for e in range(N_EXPERTS)]
    mx = functools.reduce(jnp.maximum, rows)
    ex = [jnp.exp(r - mx) for r in rows]
    tot = functools.reduce(lambda p, q: p + q, ex)
    scores = [e / tot for e in ex]
    best = None
    for g in range(N_EXPERT_GROUPS):
        m1, i1, m2, i2 = _top2_of4(scores[g * EXPERTS_PER_GROUP:(g + 1) * EXPERTS_PER_GROUP])
        cand = (m1 + m2, m1, i1 + g * EXPERTS_PER_GROUP, m2, i2 + g * EXPERTS_PER_GROUP)
        if best is None:
            best = cand
        else:
            better = cand[0] > best[0]
            best = tuple(jnp.where(better, c, b) for c, b in zip(cand, best))
    _, v1, e1, v2, e2 = best
    den = v1 + v2
    w1 = v1 / den
    w2 = v2 / den
    first_lo = e1 < e2
    return (jnp.where(first_lo, e1, e2), jnp.where(first_lo, e2, e1),
            jnp.where(first_lo, w1, w2), jnp.where(first_lo, w2, w1))


def _mixer_out_kernel(x_ref, a_ref, y_ref, mod_ref, g_ref, wglu_ref, bglu_ref, wout_ref,
                      wrh_ref, wrl_ref, br_ref, *rest, route, tiles_per_mod):
    x = x_ref[...]
    mod = mod_ref[0]
    gate_m = mod[:, :D_MODEL]
    shift_f = mod[:, D_MODEL:2 * D_MODEL]
    scale_f = mod[:, 2 * D_MODEL:]
    ys = jax.nn.gelu(jnp.concatenate([y_ref[gb] for gb in range(GROUP_BLOCKS)], axis=1))
    glu = jnp.dot(ys.astype(BF16), wglu_ref[...], preferred_element_type=F32) + bglu_ref[...]
    b_out = ys * jax.nn.sigmoid(glu)
    mixed = jnp.concatenate([a_ref[...], b_out.astype(BF16)], axis=1)
    xmid = x + gate_m * jnp.dot(mixed, wout_ref[...], preferred_element_type=F32)
    h2 = _rms(xmid, g_ref[...]) * (1.0 + scale_f) + shift_f
    lo, hi, glo, ghi = _route_rows(h2, wrh_ref, wrl_ref, br_ref)
    if not route:
        xmid_ref, h2_ref, lo_ref, hi_ref, glo_ref, ghi_ref = rest
        xmid_ref[...] = xmid
        h2_ref[...] = h2
        lo_ref[...] = lo
        hi_ref[...] = hi
        glo_ref[...] = glo
        ghi_ref[...] = ghi
        return

    tri_ref, xext_ref, cls_ref, rank_ref, cnt_ref, run_ref = rest
    i = pl.program_id(0)
    tm = x.shape[0]

    @pl.when(i == 0)
    def _():
        run_ref[...] = jnp.zeros_like(run_ref)

    a = lo & (EXPERTS_PER_GROUP - 1)
    b = hi & (EXPERTS_PER_GROUP - 1)
    pair = jnp.where(a == 0, 0, jnp.where(a == 1, 3, 5)) + (b - a - 1)
    cls = (lo >> 2) * PAIRS_PER_GROUP + pair
    cls_ref[...] = cls
    onehot = lax.broadcasted_iota(jnp.int32, (CLASS_ROWS, tm), 0) == cls
    prefix = jnp.dot(jnp.where(onehot, 1.0, 0.0).astype(BF16), tri_ref[...],
                     preferred_element_type=F32)
    run = run_ref[...]
    rank = jnp.sum(jnp.where(onehot, prefix - 1.0 + run[:, :1], 0.0), axis=0, keepdims=True)
    rank_ref[...] = rank.astype(jnp.int32)
    run = run + prefix[:, tm - 1:tm]
    run_ref[...] = run
    cnt_ref[...] = run
    bid = jnp.full((1, tm), i // tiles_per_mod, jnp.int32).astype(F32)
    ext = jnp.concatenate([glo, ghi, bid, jnp.zeros((LANES - 3, tm), F32)], axis=0).T
    xext_ref[:, :D_MODEL] = xmid
    xext_ref[:, D_MODEL:] = ext


def _mixer_out(x2d, a_out, y4, mod, g_ffn, w_glu, b_glu, w_out, wr_hi, wr_lo, b_r,
               *, tm, tiles_per_mod, route):
    t = x2d.shape[0]
    rmod = mod.shape[1]
    const2 = lambda i: (0, 0)
    tok = pl.BlockSpec((tm, D_MODEL), lambda i: (i, 0))
    row = pl.BlockSpec((1, tm), lambda i: (0, i))
    in_specs = [
        tok,
        pl.BlockSpec((tm, GMLP_WIDTH), lambda i: (i, 0)),
        pl.BlockSpec((GROUP_BLOCKS, tm, LANES), lambda i: (0, i, 0)),
        pl.BlockSpec((1, rmod, 3 * D_MODEL), lambda i: (i // tiles_per_mod, 0, 0)),
        pl.BlockSpec((1, D_MODEL), const2),
        pl.BlockSpec((SSM_WIDTH, SSM_WIDTH), const2),
        pl.BlockSpec((1, SSM_WIDTH), const2),
        pl.BlockSpec((D_MODEL, D_MODEL), const2),
        pl.BlockSpec((D_MODEL, LANES), const2),
        pl.BlockSpec((D_MODEL, LANES), const2),
        pl.BlockSpec((N_EXPERTS, 1), const2),
    ]
    args = [x2d, a_out, y4, mod, g_ffn, w_glu, b_glu, w_out, wr_hi, wr_lo, b_r]
    if route:
        ids = jnp.arange(tm)
        args.append((ids[:, None] <= ids[None, :]).astype(BF16))
        in_specs.append(pl.BlockSpec((tm, tm), const2))
        out_specs = [pl.BlockSpec((tm, XEXT_WIDTH), lambda i: (i, 0)), row, row,
                     pl.BlockSpec((CLASS_ROWS, LANES), const2)]
        out_shape = [jax.ShapeDtypeStruct((t, XEXT_WIDTH), F32),
                     jax.ShapeDtypeStruct((1, t), jnp.int32),
                     jax.ShapeDtypeStruct((1, t), jnp.int32),
                     jax.ShapeDtypeStruct((CLASS_ROWS, LANES), F32)]
        scratch = [pltpu.VMEM((CLASS_ROWS, LANES), F32)]
    else:
        out_specs = [tok, tok, row, row, row, row]
        out_shape = [jax.ShapeDtypeStruct((t, D_MODEL), F32),
                     jax.ShapeDtypeStruct((t, D_MODEL), F32),
                     jax.ShapeDtypeStruct((1, t), jnp.int32),
                     jax.ShapeDtypeStruct((1, t), jnp.int32),
                     jax.ShapeDtypeStruct((1, t), F32),
                     jax.ShapeDtypeStruct((1, t), F32)]
        scratch = []
    return pl.pallas_call(
        functools.partial(_mixer_out_kernel, route=route, tiles_per_mod=tiles_per_mod),
        grid=(t // tm,),
        in_specs=in_specs,
        out_specs=out_specs,
        out_shape=out_shape,
        scratch_shapes=scratch,
        compiler_params=_cparams("arbitrary" if route else "parallel"),
        name="mixer_out",
    )(*args)


def _plan_kernel(pos_ref, zeros_hbm, src_ref, sem):
    fill = pltpu.make_async_copy(zeros_hbm, src_ref, sem)
    fill.start()
    fill.wait()

    def body(t, carry):
        src_ref[pos_ref[t]] = t
        return carry

    lax.fori_loop(0, pos_ref.shape[0], body, 0, unroll=8)


def _plan(pos, nslots):
    return pl.pallas_call(
        _plan_kernel,
        in_specs=[pl.BlockSpec(memory_space=pltpu.SMEM), pl.BlockSpec(memory_space=pl.ANY)],
        out_specs=pl.BlockSpec(memory_space=pltpu.SMEM),
        out_shape=jax.ShapeDtypeStruct((nslots,), jnp.int32),
        scratch_shapes=[pltpu.SemaphoreType.DMA(())],
        name="moe_plan",
    )(pos, jnp.zeros((nslots,), jnp.int32))


def _moe_routed_kernel(tlo_ref, thi_ref, tval_ref, src_ref,
                       xext_hbm, wgl_ref, wul_ref, wdl_ref, wgh_ref, wuh_ref, wdh_ref,
                       mod_ref, g_ref, gfin_ref, out_hbm, buf_ref, obuf_ref, sem_in, sem_out,
                       *, nseq, final_norm):
    i = pl.program_id(0)
    nt = pl.num_programs(0)
    nv = tval_ref[i]
    nv_next = jnp.where(i + 1 < nt, tval_ref[jnp.minimum(i + 1, nt - 1)], 0)
    nv_prev = jnp.where(i >= 1, tval_ref[jnp.maximum(i - 1, 0)], 0)
    nv_prev2 = jnp.where(i >= 2, tval_ref[jnp.maximum(i - 2, 0)], 0)

    def row_in(tile, sl, j):
        tok = src_ref[tile * MOE_TILE + j]
        return pltpu.make_async_copy(xext_hbm.at[pl.ds(tok, 1), :], buf_ref.at[sl, pl.ds(j, 1), :],
                                     sem_in.at[sl])

    def row_out(tile, sl, j):
        tok = src_ref[tile * MOE_TILE + j]
        return pltpu.make_async_copy(obuf_ref.at[sl, pl.ds(j, 1), :], out_hbm.at[pl.ds(tok, 1), :],
                                     sem_out.at[sl])

    def gather_start(tile, sl):
        for j in range(MOE_TILE):
            row_in(tile, sl, j).start(priority=j % 2)

    def gather_wait(sl):
        pltpu.make_async_copy(xext_hbm.at[pl.ds(0, MOE_TILE), :], buf_ref.at[sl], sem_in.at[sl]).wait()

    def scatter_start(tile, sl, n):
        for g in range(MOE_TILE // 8):
            @pl.when((g + 1) * 8 <= n)
            def _():
                for j in range(g * 8, (g + 1) * 8):
                    row_out(tile, sl, j).start(priority=j % 2)

        def one(j, c):
            row_out(tile, sl, j).start()
            return c

        lax.fori_loop((n >> 3) << 3, n, one, 0)

    def scatter_wait(sl, n):
        n8 = pl.multiple_of((n >> 3) << 3, 8)

        @pl.when(n8 > 0)
        def _():
            pltpu.make_async_copy(obuf_ref.at[sl, pl.ds(0, n8), :], out_hbm.at[pl.ds(0, n8), :],
                                  sem_out.at[sl]).wait()

        def one(j, c):
            pltpu.make_async_copy(obuf_ref.at[sl, pl.ds(0, 1), :], out_hbm.at[pl.ds(0, 1), :],
                                  sem_out.at[sl]).wait()
            return c

        lax.fori_loop(n8, n, one, 0)

    def experts(sl):
        buf = buf_ref[sl]
        xm = buf[:, :D_MODEL]
        glo = buf[:, D_MODEL:D_MODEL + 1]
        ghi = buf[:, D_MODEL + 1:D_MODEL + 2]
        bid = buf[:, D_MODEL + 2:D_MODEL + 3]
        mod = mod_ref[...]

        def per_row(k):
            out = mod[nseq - 1:nseq, k * D_MODEL:(k + 1) * D_MODEL]
            for s in range(nseq - 2, -1, -1):
                out = jnp.where(bid == float(s), mod[s:s + 1, k * D_MODEL:(k + 1) * D_MODEL], out)
            return out

        shift_f, scale_f, gate_f = per_row(0), per_row(1), per_row(2)
        h = (_rms(xm, g_ref[...]) * (1.0 + scale_f) + shift_f).astype(BF16)
        ffn = None
        for wg, wu, wd, gate in ((wgl_ref, wul_ref, wdl_ref, glo), (wgh_ref, wuh_ref, wdh_ref, ghi)):
            he = (jax.nn.silu(jnp.dot(h, wg[0], preferred_element_type=F32))
                  * jnp.dot(h, wu[0], preferred_element_type=F32))
            y = gate * jnp.dot(he.astype(BF16), wd[0], preferred_element_type=F32)
            ffn = y if ffn is None else ffn + y
        x_new = xm + gate_f * ffn
        if final_norm:
            x_new = _rms(x_new, gfin_ref[...])
        obuf_ref[sl] = x_new

    def step(sl):
        other = 1 - sl

        @pl.when(nv_next > 0)
        def _():
            gather_start(i + 1, other)

        @pl.when(nv_prev2 > 0)
        def _():
            scatter_wait(sl, nv_prev2)

        @pl.when(nv > 0)
        def _():
            gather_wait(sl)
            experts(sl)
            scatter_start(i, sl, nv)

        @pl.when(i == nt - 1)
        def _():
            @pl.when(nv_prev > 0)
            def _():
                scatter_wait(other, nv_prev)

            @pl.when(nv > 0)
            def _():
                scatter_wait(sl, nv)

    @pl.when(i == 0)
    def _():
        gather_start(0, 0)

    for sl in range(2):
        pl.when(i % 2 == sl)(functools.partial(step, sl))


def _moe_routed(xext, cls, rank, counts, wg, wu, wd, mod, g_ffn, g_final, *, final_norm):
    t = xext.shape[0]
    nseq = mod.shape[0]
    ntiles = t // MOE_TILE + N_CLASSES
    nslots = ntiles * MOE_TILE
    counts = counts[:N_CLASSES, 0].astype(jnp.int32)
    tiles_c = (counts + MOE_TILE - 1) // MOE_TILE
    tile_end = jnp.cumsum(tiles_c)
    tile_start = tile_end - tiles_c
    used = tile_end[-1]
    pos = (tile_start * MOE_TILE)[cls[0]] + rank[0]
    src = _plan(pos, nslots)
    tid = jnp.arange(ntiles, dtype=jnp.int32)
    tcls = jnp.sum(tile_end[None, :] <= jnp.minimum(tid, used - 1)[:, None], axis=1).astype(jnp.int32)
    tval = jnp.where(tid < used,
                     jnp.clip(counts[tcls] - (tid - tile_start[tcls]) * MOE_TILE, 0, MOE_TILE), 0)
    tlo = jnp.asarray(CLASS_LO, jnp.int32)[tcls]
    thi = jnp.asarray(CLASS_HI, jnp.int32)[tcls]
    wspec_lo = lambda shape: pl.BlockSpec(shape, lambda i, tlo, thi, tval, src: (tlo[i], 0, 0))
    wspec_hi = lambda shape: pl.BlockSpec(shape, lambda i, tlo, thi, tval, src: (thi[i], 0, 0))
    const2 = lambda i, tlo, thi, tval, src: (0, 0)
    up = (1, D_MODEL, D_EXPERT)
    down = (1, D_EXPERT, D_MODEL)
    return pl.pallas_call(
        functools.partial(_moe_routed_kernel, nseq=nseq, final_norm=final_norm),
        grid_spec=pltpu.PrefetchScalarGridSpec(
            num_scalar_prefetch=4,
            grid=(ntiles,),
            in_specs=[
                pl.BlockSpec(memory_space=pl.ANY),
                wspec_lo(up), wspec_lo(up), wspec_lo(down),
                wspec_hi(up), wspec_hi(up), wspec_hi(down),
                pl.BlockSpec((nseq, 3 * D_MODEL), const2),
                pl.BlockSpec((1, D_MODEL), const2),
                pl.BlockSpec((1, D_MODEL), const2),
            ],
            out_specs=pl.BlockSpec(memory_space=pl.ANY),
            scratch_shapes=[
                pltpu.VMEM((2, MOE_TILE, XEXT_WIDTH), F32),
                pltpu.VMEM((2, MOE_TILE, D_MODEL), F32),
                pltpu.SemaphoreType.DMA((2,)),
                pltpu.SemaphoreType.DMA((2,)),
            ],
        ),
        out_shape=jax.ShapeDtypeStruct((t, D_MODEL), F32),
        compiler_params=_cparams("arbitrary"),
        name="moe_routed",
    )(tlo, thi, tval, src, xext, wg, wu, wd, wg, wu, wd, mod, g_ffn, g_final)


def _moe_kernel(h2_ref, gates_ref, wg_ref, wu_ref, wd_ref, xmid_ref, mod_ref, gfin_ref,
                o_ref, acc_ref, *, final_norm):
    e = pl.program_id(1)

    @pl.when(e == 0)
    def _():
        acc_ref[...] = jnp.zeros_like(acc_ref)

    h = h2_ref[...].astype(BF16)
    he = (jax.nn.silu(jnp.dot(h, wg_ref[0], preferred_element_type=F32))
          * jnp.dot(h, wu_ref[0], preferred_element_type=F32))
    y = jnp.dot(he.astype(BF16), wd_ref[0], preferred_element_type=F32)
    gates = gates_ref[...]
    lane = lax.broadcasted_iota(jnp.int32, gates.shape, 1)
    gcol = jnp.sum(jnp.where(lane == e, gates, 0.0), axis=1, keepdims=True)
    acc_ref[...] += gcol * y

    @pl.when(e == N_EXPERTS - 1)
    def _():
        x = xmid_ref[...] + mod_ref[0] * acc_ref[...]
        if final_norm:
            x = _rms(x, gfin_ref[...])
        o_ref[...] = x


def _moe(h2, gates, wg, wu, wd, xmid, mod, g_final, *, tm, tiles_per_mod, final_norm):
    t = h2.shape[0]
    rmod = mod.shape[1]
    tok = pl.BlockSpec((tm, D_MODEL), lambda i, e: (i, 0))
    return pl.pallas_call(
        functools.partial(_moe_kernel, final_norm=final_norm),
        grid=(t // tm, N_EXPERTS),
        in_specs=[
            tok,
            pl.BlockSpec((tm, N_EXPERTS), lambda i, e: (i, 0)),
            pl.BlockSpec((1, D_MODEL, D_EXPERT), lambda i, e: (e, 0, 0)),
            pl.BlockSpec((1, D_MODEL, D_EXPERT), lambda i, e: (e, 0, 0)),
            pl.BlockSpec((1, D_EXPERT, D_MODEL), lambda i, e: (e, 0, 0)),
            tok,
            pl.BlockSpec((1, rmod, D_MODEL), lambda i, e: (i // tiles_per_mod, 0, 0)),
            pl.BlockSpec((1, D_MODEL), lambda i, e: (0, 0)),
        ],
        out_specs=tok,
        out_shape=jax.ShapeDtypeStruct((t, D_MODEL), F32),
        scratch_shapes=[pltpu.VMEM((tm, D_MODEL), F32)],
        compiler_params=_cparams("parallel", "arbitrary"),
        name="moe",
    )(h2, gates, wg, wu, wd, xmid, mod, g_final)


def _layer(x2d, mod, lw, h0_re, h0_im, *, seq_len, nseq, sample, final_norm):
    t = x2d.shape[0]
    if sample:
        tm, tiles_per_mod, tm_moe = t, 1, t
        s5_rows, s5_blocks = nseq, 1
    else:
        tm = min(512, seq_len)
        tiles_per_mod = seq_len // tm
        tm_moe = min(512, seq_len)
        s5_rows, s5_blocks = seq_len // S5_CHUNK, nseq
    outs = _mixer_in(x2d, mod[..., :2 * D_MODEL], lw["g_mix"], lw["w_in"], lw["v_gain"],
                     lw["ws_sample"] if sample else lw["ws"], lw["bs_sample"] if sample else lw["bs"],
                     tm=tm, tiles_per_mod=tiles_per_mod, want_v=sample)
    a_out, u4 = outs[0], outs[1]
    v_rows = outs[2] if sample else None
    y4, hn_re, hn_im = _s5(u4, lw["prep"], lw["d_tiled"],
                           h0_re, h0_im, rows=s5_rows, nblk=s5_blocks, scan=not sample)
    mo = _mixer_out(x2d, a_out, y4, mod[..., 2 * D_MODEL:5 * D_MODEL],
                    lw["g_ffn"], lw["w_glu"], lw["b_glu"], lw["w_out"], lw["wr_hi"], lw["wr_lo"], lw["b_r"],
                    tm=tm, tiles_per_mod=tiles_per_mod, route=not sample)
    if sample:
        xmid, h2, lo, hi, glo, ghi = mo
        eids = jnp.arange(N_EXPERTS, dtype=jnp.int32)[None, :]
        gates = (jnp.where(lo[0][:, None] == eids, glo[0][:, None], 0.0)
                 + jnp.where(hi[0][:, None] == eids, ghi[0][:, None], 0.0))
        x_new = _moe(h2, gates, lw["wg"], lw["wu"], lw["wd"], xmid, mod[..., 5 * D_MODEL:],
                     lw["g_final"], tm=tm_moe, tiles_per_mod=1, final_norm=final_norm)
    else:
        xext, cls, rank, counts = mo
        x_new = _moe_routed(xext, cls, rank, counts, lw["wg"], lw["wu"], lw["wd"],
                            mod[:, 0, 3 * D_MODEL:], lw["g_ffn"], lw["g_final"], final_norm=final_norm)
    return x_new, hn_re, hn_im, v_rows


def _state_out(h, nseq):
    return h.reshape(GROUP_BLOCKS, nseq, GROUPS_PER_BLOCK, SSM_STATE).transpose(1, 0, 2, 3).reshape(
        nseq, SSM_GROUPS, SSM_STATE)


def _state_in(h, nblk, rh):
    nseq = h.shape[0]
    return h.reshape(nseq, GROUP_BLOCKS, STATE_BLOCK).transpose(1, 0, 2).reshape(
        GROUP_BLOCKS, nblk, rh, STATE_BLOCK)


def kernel(x_prompt, x_sample, c_prompt, c_sample, state_s5_re, state_s5_im, w_ada, b_ada, g_norm_mix, g_norm_ffn, w_in, gmlp_v_gain, gmlp_w_spatial, gmlp_b_spatial, s5_a_re, s5_a_im, s5_log_dt, s5_b_re, s5_b_im, s5_c_re, s5_c_im, s5_d, s5_w_glu, s5_b_glu, w_out, w_router, b_router, w_gate, w_up, w_down, g_final):
    nb, seq_len, _ = x_prompt.shape
    ns, dec_len, _ = x_sample.shape
    assert dec_len == S5_CHUNK and ns * dec_len == GMLP_CHUNK and nb + ns <= ADA_ROWS
    assert seq_len % GMLP_CHUNK == 0

    c_all = jnp.concatenate([c_prompt, c_sample, jnp.zeros((ADA_ROWS - nb - ns, D_MODEL), F32)], axis=0)
    mod_all = _ada(c_all, w_ada, b_ada)

    pos = jnp.arange(GMLP_CHUNK)
    causal = (pos[None, :] // CHUNK) <= (pos[:, None] // CHUNK)
    wr_pad = jnp.pad(w_router, ((0, 0), (0, LANES - N_EXPERTS)))
    wr_hi = wr_pad.astype(BF16)
    wr_lo = (wr_pad - wr_hi.astype(F32)).astype(BF16)
    b_r = b_router.reshape(N_EXPERTS, 1)
    g_fin = g_final.reshape(1, D_MODEL)
    eye_s = jnp.eye(ns, dtype=F32)

    xp = x_prompt.reshape(nb * seq_len, D_MODEL)
    xs = x_sample.reshape(ns * dec_len, D_MODEL)
    zeros_p = jnp.zeros((GROUP_BLOCKS, nb, 1, STATE_BLOCK), F32)
    sp_re, sp_im, ss_re, ss_im, v_new = [], [], [], [], []
    for l in range(DEPTH):
        ws = jnp.where(causal[None], gmlp_w_spatial[l], 0.0)
        ws_sample = jnp.einsum("ab,hij->haibj", eye_s, ws[:, :dec_len, :dec_len]).reshape(
            GMLP_HEADS, GMLP_CHUNK, GMLP_CHUNK)
        bs = jnp.repeat(gmlp_b_spatial[l].T, GMLP_HEAD_DIM, axis=1)
        lw = dict(
            g_mix=g_norm_mix[l].reshape(1, D_MODEL), g_ffn=g_norm_ffn[l].reshape(1, D_MODEL),
            w_in=w_in[l].astype(BF16), v_gain=gmlp_v_gain[l].reshape(1, GMLP_WIDTH),
            ws=ws.astype(BF16), ws_sample=ws_sample.astype(BF16),
            bs=bs, bs_sample=jnp.tile(bs[:dec_len], (ns, 1)),
            prep=_s5_prep(s5_a_re[l], s5_a_im[l], s5_log_dt[l], s5_b_re[l], s5_b_im[l],
                          s5_c_re[l], s5_c_im[l]),
            d_tiled=jnp.tile(s5_d[l].reshape(GROUP_BLOCKS, 1, LANES), (1, 1, S5_CHUNK)),
            w_glu=s5_w_glu[l].astype(BF16), b_glu=s5_b_glu[l].reshape(1, SSM_WIDTH),
            w_out=w_out[l].astype(BF16), wr_hi=wr_hi, wr_lo=wr_lo, b_r=b_r,
            wg=w_gate[l].astype(BF16), wu=w_up[l].astype(BF16), wd=w_down[l].astype(BF16),
            g_final=g_fin,
        )
        last = l == DEPTH - 1
        mod_p = mod_all[l, :nb].reshape(nb, 1, 6 * D_MODEL)
        mod_s = jnp.repeat(mod_all[l, nb:nb + ns], dec_len, axis=0).reshape(1, ns * dec_len, 6 * D_MODEL)
        xp, hp_re, hp_im, _ = _layer(xp, mod_p, lw, zeros_p, zeros_p, seq_len=seq_len, nseq=nb,
                                     sample=False, final_norm=last)
        xs, hs_re, hs_im, vs = _layer(xs, mod_s, lw, _state_in(state_s5_re[l], 1, ns),
                                      _state_in(state_s5_im[l], 1, ns), seq_len=dec_len, nseq=ns,
                                      sample=True, final_norm=last)
        sp_re.append(_state_out(hp_re, nb))
        sp_im.append(_state_out(hp_im, nb))
        ss_re.append(_state_out(hs_re, ns))
        ss_im.append(_state_out(hs_im, ns))
        v_new.append(vs.reshape(ns, dec_len, GMLP_WIDTH))
    return (xp.reshape(nb, seq_len, D_MODEL), xs.reshape(ns, dec_len, D_MODEL),
            jnp.stack(sp_re), jnp.stack(sp_im), jnp.stack(ss_re), jnp.stack(ss_im), jnp.stack(v_new))
```

```python
import functools

import jax
import jax.numpy as jnp
from jax import lax
from jax.experimental import pallas as pl
from jax.experimental.pallas import tpu as pltpu

F32 = jnp.float32
BF16 = jnp.bfloat16

D_MODEL = 1024
DEPTH = 2
CHUNK = 64
GMLP_CHUNK = 128
GMLP_WIDTH = 512
GMLP_HEADS = 4
GMLP_HEAD_DIM = 128
SSM_WIDTH = 512
SSM_GROUP = 16
SSM_GROUPS = 32
SSM_STATE = 64
IN_WIDTH = 1536
N_EXPERTS = 16
EXPERTS_PER_GROUP = 4
N_EXPERT_GROUPS = 4
D_EXPERT = 512
EPS = 1e-6

LANES = 128
S5_CHUNK = 16
GROUP_BLOCKS = 4
GROUPS_PER_BLOCK = SSM_GROUPS // GROUP_BLOCKS
STATE_BLOCK = GROUPS_PER_BLOCK * SSM_STATE
S5_ROW = S5_CHUNK * LANES
ADA_ROWS = 16
PAIRS_PER_GROUP = 6
N_CLASSES = N_EXPERT_GROUPS * PAIRS_PER_GROUP
CLASS_ROWS = 32
_PAIRS = [(a, b) for a in range(EXPERTS_PER_GROUP) for b in range(a + 1, EXPERTS_PER_GROUP)]
CLASS_LO = [g * EXPERTS_PER_GROUP + a for g in range(N_EXPERT_GROUPS) for a, _ in _PAIRS]
CLASS_HI = [g * EXPERTS_PER_GROUP + b for g in range(N_EXPERT_GROUPS) for _, b in _PAIRS]
MOE_TILE = 256
XEXT_WIDTH = D_MODEL + LANES
VMEM_LIMIT = 56 * 1024 * 1024


def _cparams(*sem):
    return pltpu.CompilerParams(dimension_semantics=sem, vmem_limit_bytes=VMEM_LIMIT)


def _ada_kernel(c_ref, w_ref, b_ref, o_ref):
    c = c_ref[...]
    s = (c * jax.nn.sigmoid(c)).astype(BF16)
    o_ref[0] = jnp.dot(s, w_ref[0].astype(BF16), preferred_element_type=F32) + b_ref[0]


def _ada(c_all, w_ada, b_ada):
    nblk = 6
    return pl.pallas_call(
        _ada_kernel,
        grid=(DEPTH, nblk),
        in_specs=[
            pl.BlockSpec((ADA_ROWS, D_MODEL), lambda l, j: (0, 0)),
            pl.BlockSpec((1, D_MODEL, D_MODEL), lambda l, j: (l, 0, j)),
            pl.BlockSpec((1, 1, D_MODEL), lambda l, j: (l, 0, j)),
        ],
        out_specs=pl.BlockSpec((1, ADA_ROWS, D_MODEL), lambda l, j: (l, 0, j)),
        out_shape=jax.ShapeDtypeStruct((DEPTH, ADA_ROWS, 6 * D_MODEL), F32),
        compiler_params=_cparams("parallel", "parallel"),
        name="ada",
    )(c_all, w_ada, b_ada.reshape(DEPTH, 1, 6 * D_MODEL))


def _prep_kernel(are_ref, aim_ref, ldt_ref, bre_ref, bim_ref, cre_ref, cim_ref,
                 msre_ref, msim_ref, nt_ref, wrev_ref, a16re_ref, a16im_ref):
    a_re = are_ref[0]
    a_im = aim_ref[0]
    dt = jnp.exp(ldt_ref[0])
    rho = a_re * dt
    th = a_im * dt
    kk = jnp.minimum(lax.broadcasted_iota(jnp.int32, (24, STATE_BLOCK), 0), S5_CHUNK).astype(F32)
    mag = jnp.exp(kk * rho)
    pw_re = mag * jnp.cos(kk * th)
    pw_im = mag * jnp.sin(kk * th)

    lb_re = pw_re[1:2]
    lb_im = pw_im[1:2]
    num_re = lb_re - 1.0
    num_im = lb_im
    den = a_re * a_re + a_im * a_im
    coef_re = (num_re * a_re + num_im * a_im) / den
    coef_im = (num_im * a_re - num_re * a_im) / den
    b_re = bre_ref[0]
    b_im = bim_ref[0]
    bb_re = coef_re * b_re - coef_im * b_im
    bb_im = coef_re * b_im + coef_im * b_re

    rows = lax.broadcasted_iota(jnp.int32, (LANES, STATE_BLOCK), 0)
    cols = lax.broadcasted_iota(jnp.int32, (LANES, STATE_BLOCK), 1)
    same_group = (rows >> 4) == (cols >> 6)

    def blockdiag(x16):
        return jnp.where(same_group, jnp.concatenate([x16] * GROUPS_PER_BLOCK, axis=0), 0.0)

    for s in range(S5_CHUNK):
        k = S5_CHUNK - 1 - s
        p_re = pw_re[k:k + 1]
        p_im = pw_im[k:k + 1]
        msre_ref[0, s * LANES:(s + 1) * LANES, :] = blockdiag(p_re * bb_re - p_im * bb_im).astype(BF16)
        msim_ref[0, s * LANES:(s + 1) * LANES, :] = blockdiag(p_re * bb_im + p_im * bb_re).astype(BF16)

    bcat = jnp.concatenate([blockdiag(bb_re), blockdiag(bb_im)], axis=1)
    c_re = cre_ref[0]
    c_im = cim_ref[0]
    for k in range(S5_CHUNK + 1):
        p_re = pw_re[k:k + 1]
        p_im = pw_im[k:k + 1]
        cl = jnp.concatenate([blockdiag(c_re * p_re - c_im * p_im),
                              -blockdiag(c_re * p_im + c_im * p_re)], axis=1)
        if k >= 1:
            nt_ref[0, (k - 1) * LANES:k * LANES, :] = cl.astype(BF16)
        if k < S5_CHUNK:
            wl = lax.dot_general(bcat, cl, (((1,), (1,)), ((), ())),
                                 precision=lax.Precision.HIGHEST, preferred_element_type=F32)
            j = S5_CHUNK - 1 - k
            wrev_ref[0, j * LANES:(j + 1) * LANES, :] = wl.astype(BF16)

    a16re_ref[0] = pw_re[S5_CHUNK:S5_CHUNK + 1]
    a16im_ref[0] = pw_im[S5_CHUNK:S5_CHUNK + 1]


def _s5_prep(a_re, a_im, log_dt, b_re, b_im, c_re, c_im):
    nstate = SSM_GROUPS * SSM_STATE

    def lane_row(v):
        return v.reshape(GROUP_BLOCKS, 1, STATE_BLOCK)

    def rows16(v):
        return v.reshape(SSM_GROUP, GROUP_BLOCKS, STATE_BLOCK).transpose(1, 0, 2)

    ldt = jnp.repeat(log_dt, SSM_STATE).reshape(SSM_GROUPS, SSM_STATE)
    bt_re = rows16(b_re.transpose(2, 0, 1).reshape(SSM_GROUP, nstate))
    bt_im = rows16(b_im.transpose(2, 0, 1).reshape(SSM_GROUP, nstate))
    ct_re = rows16(c_re.transpose(1, 0, 2).reshape(SSM_GROUP, nstate))
    ct_im = rows16(c_im.transpose(1, 0, 2).reshape(SSM_GROUP, nstate))
    row_spec = pl.BlockSpec((1, 1, STATE_BLOCK), lambda g: (g, 0, 0))
    r16_spec = pl.BlockSpec((1, SSM_GROUP, STATE_BLOCK), lambda g: (g, 0, 0))
    return pl.pallas_call(
        _prep_kernel,
        grid=(GROUP_BLOCKS,),
        in_specs=[row_spec, row_spec, row_spec, r16_spec, r16_spec, r16_spec, r16_spec],
        out_specs=[
            pl.BlockSpec((1, S5_ROW, STATE_BLOCK), lambda g: (g, 0, 0)),
            pl.BlockSpec((1, S5_ROW, STATE_BLOCK), lambda g: (g, 0, 0)),
            pl.BlockSpec((1, S5_ROW, 2 * STATE_BLOCK), lambda g: (g, 0, 0)),
            pl.BlockSpec((1, S5_ROW, LANES), lambda g: (g, 0, 0)),
            row_spec, row_spec,
        ],
        out_shape=[
            jax.ShapeDtypeStruct((GROUP_BLOCKS, S5_ROW, STATE_BLOCK), BF16),
            jax.ShapeDtypeStruct((GROUP_BLOCKS, S5_ROW, STATE_BLOCK), BF16),
            jax.ShapeDtypeStruct((GROUP_BLOCKS, S5_ROW, 2 * STATE_BLOCK), BF16),
            jax.ShapeDtypeStruct((GROUP_BLOCKS, S5_ROW, LANES), BF16),
            jax.ShapeDtypeStruct((GROUP_BLOCKS, 1, STATE_BLOCK), F32),
            jax.ShapeDtypeStruct((GROUP_BLOCKS, 1, STATE_BLOCK), F32),
        ],
        compiler_params=_cparams("parallel"),
        name="s5_prep",
    )(lane_row(a_re), lane_row(a_im), lane_row(ldt), bt_re, bt_im, ct_re, ct_im)


def _rms(x, g):
    return x * lax.rsqrt(jnp.mean(x * x, axis=-1, keepdims=True) + EPS) * g


def _mixer_in_kernel(x_ref, mod_ref, g_ref, win_ref, vg_ref, ws_ref, bs_ref,
                     a_ref, u_ref, *v_ref, tm):
    x = x_ref[...]
    mod = mod_ref[0]
    shift = mod[:, :D_MODEL]
    scale = mod[:, D_MODEL:]
    h = _rms(x, g_ref[...]) * (1.0 + scale) + shift
    proj = jnp.dot(h.astype(BF16), win_ref[...], preferred_element_type=F32)
    z = jax.nn.gelu(proj[:, :2 * GMLP_WIDTH])
    u = z[:, :GMLP_WIDTH]
    v = z[:, GMLP_WIDTH:]
    vc = v - jnp.mean(v, axis=-1, keepdims=True)
    vn = vc * lax.rsqrt(jnp.mean(vc * vc, axis=-1, keepdims=True) + EPS) * vg_ref[...]
    if v_ref:
        v_ref[0][...] = vn
    vb = vn.astype(BF16)
    bias = bs_ref[...]
    for c in range(tm // GMLP_CHUNK):
        r0 = c * GMLP_CHUNK
        for hh in range(GMLP_HEADS):
            l0 = hh * GMLP_HEAD_DIM
            mixed = jnp.dot(ws_ref[hh], vb[r0:r0 + GMLP_CHUNK, l0:l0 + GMLP_HEAD_DIM],
                            preferred_element_type=F32) + bias[:, l0:l0 + GMLP_HEAD_DIM]
            a_ref[r0:r0 + GMLP_CHUNK, l0:l0 + GMLP_HEAD_DIM] = (
                u[r0:r0 + GMLP_CHUNK, l0:l0 + GMLP_HEAD_DIM] * mixed).astype(BF16)
    for gb in range(GROUP_BLOCKS):
        l0 = 2 * GMLP_WIDTH + gb * LANES
        u_ref[gb] = proj[:, l0:l0 + LANES]


def _mixer_in(x2d, mod, g_mix, w_in, v_gain, ws, bs, *, tm, tiles_per_mod, want_v):
    t = x2d.shape[0]
    rmod = mod.shape[1]
    const2 = lambda i: (0, 0)
    out_shape = [jax.ShapeDtypeStruct((t, GMLP_WIDTH), BF16),
                 jax.ShapeDtypeStruct((GROUP_BLOCKS, t, LANES), F32)]
    out_specs = [pl.BlockSpec((tm, GMLP_WIDTH), lambda i: (i, 0)),
                 pl.BlockSpec((GROUP_BLOCKS, tm, LANES), lambda i: (0, i, 0))]
    if want_v:
        out_shape.append(jax.ShapeDtypeStruct((t, GMLP_WIDTH), F32))
        out_specs.append(pl.BlockSpec((tm, GMLP_WIDTH), lambda i: (i, 0)))
    return pl.pallas_call(
        functools.partial(_mixer_in_kernel, tm=tm),
        grid=(t // tm,),
        in_specs=[
            pl.BlockSpec((tm, D_MODEL), lambda i: (i, 0)),
            pl.BlockSpec((1, rmod, 2 * D_MODEL), lambda i: (i // tiles_per_mod, 0, 0)),
            pl.BlockSpec((1, D_MODEL), const2),
            pl.BlockSpec((D_MODEL, IN_WIDTH), const2),
            pl.BlockSpec((1, GMLP_WIDTH), const2),
            pl.BlockSpec((GMLP_HEADS, GMLP_CHUNK, GMLP_CHUNK), lambda i: (0, 0, 0)),
            pl.BlockSpec((GMLP_CHUNK, GMLP_WIDTH), const2),
        ],
        out_specs=out_specs,
        out_shape=out_shape,
        compiler_params=_cparams("parallel"),
        name="mixer_in",
    )(x2d, mod, g_mix, w_in, v_gain, ws, bs)


def _s5_kernel(u_ref, msre_ref, msim_ref, nt_ref, wrev_ref, a16re_ref, a16im_ref, d_ref,
               h0re_ref, h0im_ref, y_ref, hnre_ref, hnim_ref, *scratch, scan):
    rows = u_ref.shape[1] // S5_CHUNK
    u = jnp.concatenate([u_ref[0, pl.ds(s, rows, stride=S5_CHUNK), :] for s in range(S5_CHUNK)], axis=1)
    ub = u.astype(BF16)
    s_re = jnp.dot(ub, msre_ref[0], preferred_element_type=F32)
    s_im = jnp.dot(ub, msim_ref[0], preferred_element_type=F32)
    a_re = a16re_ref[0]
    a_im = a16im_ref[0]
    h0_re = h0re_ref[0, 0]
    h0_im = h0im_ref[0, 0]
    if scan:
        sre_scr, sim_scr, hre_scr, him_scr = scratch
        sre_scr[...] = s_re
        sim_scr[...] = s_im

        def body(r, carry):
            hr, hi = carry
            hre_scr[pl.ds(r, 1), :] = hr
            him_scr[pl.ds(r, 1), :] = hi
            sr = sre_scr[pl.ds(r, 1), :]
            si = sim_scr[pl.ds(r, 1), :]
            return (a_re * hr - a_im * hi + sr, a_re * hi + a_im * hr + si)

        hn_re, hn_im = lax.fori_loop(0, rows, body, (h0_re, h0_im))
        h_re = hre_scr[...]
        h_im = him_scr[...]
    else:
        h_re = h0_re
        h_im = h0_im
        hn_re = a_re * h_re - a_im * h_im + s_re
        hn_im = a_re * h_im + a_im * h_re + s_im
    hnre_ref[0, 0] = hn_re
    hnim_ref[0, 0] = hn_im
    hcat = jnp.concatenate([h_re, h_im], axis=1).astype(BF16)
    inter = lax.dot_general(hcat, nt_ref[0], (((1,), (1,)), ((), ())),
                            preferred_element_type=F32)
    d = d_ref[0]
    for t in range(S5_CHUNK):
        k0 = (S5_CHUNK - 1 - t) * LANES
        intra = jnp.dot(ub[:, :(t + 1) * LANES], wrev_ref[0, k0:, :], preferred_element_type=F32)
        sl = slice(t * LANES, (t + 1) * LANES)
        y_ref[0, pl.ds(t, rows, stride=S5_CHUNK), :] = intra + inter[:, sl] + d[:, sl] * u[:, sl]


def _s5(u4, prep, d_tiled, h0_re, h0_im, *, rows, nblk, scan):
    ms_re, ms_im, nt, wrev, a16_re, a16_im = prep
    rh = h0_re.shape[2]
    wspec = lambda shape: pl.BlockSpec((1,) + shape, lambda g, b: (g, 0, 0))
    hspec = pl.BlockSpec((1, 1, rh, STATE_BLOCK), lambda g, b: (g, b, 0, 0))
    scratch = [pltpu.VMEM((rows, STATE_BLOCK), F32)] * 4 if scan else []
    return pl.pallas_call(
        functools.partial(_s5_kernel, scan=scan),
        grid=(GROUP_BLOCKS, nblk),
        in_specs=[
            pl.BlockSpec((1, rows * S5_CHUNK, LANES), lambda g, b: (g, b, 0)),
            wspec((S5_ROW, STATE_BLOCK)), wspec((S5_ROW, STATE_BLOCK)),
            wspec((S5_ROW, 2 * STATE_BLOCK)), wspec((S5_ROW, LANES)),
            wspec((1, STATE_BLOCK)), wspec((1, STATE_BLOCK)), wspec((1, S5_ROW)),
            hspec, hspec,
        ],
        out_specs=[pl.BlockSpec((1, rows * S5_CHUNK, LANES), lambda g, b: (g, b, 0)), hspec, hspec],
        out_shape=[
            jax.ShapeDtypeStruct(u4.shape, F32),
            jax.ShapeDtypeStruct(h0_re.shape, F32),
            jax.ShapeDtypeStruct(h0_re.shape, F32),
        ],
        scratch_shapes=scratch,
        compiler_params=_cparams("parallel", "parallel"),
        name="s5",
    )(u4, ms_re, ms_im, nt, wrev, a16_re, a16_im, d_tiled, h0_re, h0_im)


def _split_bf16(x):
    hi = x.astype(BF16)
    return hi, (x - hi.astype(F32)).astype(BF16)


def _top2_of4(a):
    m1 = jnp.maximum(jnp.maximum(a[0], a[1]), jnp.maximum(a[2], a[3]))
    i1 = jnp.where(a[0] == m1, 0, jnp.where(a[1] == m1, 1, jnp.where(a[2] == m1, 2, 3)))
    b = [jnp.where(i1 == j, -jnp.inf, a[j]) for j in range(4)]
    m2 = jnp.maximum(jnp.maximum(b[0], b[1]), jnp.maximum(b[2], b[3]))
    i2 = jnp.where(b[0] == m2, 0, jnp.where(b[1] == m2, 1, jnp.where(b[2] == m2, 2, 3)))
    return m1, i1, m2, i2


def _route_rows(h2, wrh_ref, wrl_ref, br_ref):
    h_hi, h_lo = _split_bf16(h2)
    w_hi = wrh_ref[...]
    logits = (jnp.dot(h_hi, w_hi, preferred_element_type=F32)
              + jnp.dot(h_lo, w_hi, preferred_element_type=F32)
              + jnp.dot(h_hi, wrl_ref[...], preferred_element_type=F32))
    lt = logits.T[:N_EXPERTS] + br_ref[...]
    rows = [lt[e:e + 1] for e in range(N_EXPERTS)]
    mx = functools.reduce(jnp.maximum, rows)
    ex = [jnp.exp(r - mx) for r in rows]
    tot = functools.reduce(lambda p, q: p + q, ex)
    scores = [e / tot for e in ex]
    best = None
    for g in range(N_EXPERT_GROUPS):
        m1, i1, m2, i2 = _top2_of4(scores[g * EXPERTS_PER_GROUP:(g + 1) * EXPERTS_PER_GROUP])
        cand = (m1 + m2, m1, i1 + g * EXPERTS_PER_GROUP, m2, i2 + g * EXPERTS_PER_GROUP)
        if best is None:
            best = cand
        else:
            better = cand[0] > best[0]
            best = tuple(jnp.where(better, c, b) for c, b in zip(cand, best))
    _, v1, e1, v2, e2 = best
    den = v1 + v2
    w1 = v1 / den
    w2 = v2 / den
    first_lo = e1 < e2
    return (jnp.where(first_lo, e1, e2), jnp.where(first_lo, e2, e1),
            jnp.where(first_lo, w1, w2), jnp.where(first_lo, w2, w1))


def _mixer_out_kernel(x_ref, a_ref, y_ref, mod_ref, g_ref, wglu_ref, bglu_ref, wout_ref,
                      wrh_ref, wrl_ref, br_ref, *rest, route, tiles_per_mod):
    x = x_ref[...]
    mod = mod_ref[0]
    gate_m = mod[:, :D_MODEL]
    shift_f = mod[:, D_MODEL:2 * D_MODEL]
    scale_f = mod[:, 2 * D_MODEL:]
    ys = jax.nn.gelu(jnp.concatenate([y_ref[gb] for gb in range(GROUP_BLOCKS)], axis=1))
    glu = jnp.dot(ys.astype(BF16), wglu_ref[...], preferred_element_type=F32) + bglu_ref[...]
    b_out = ys * jax.nn.sigmoid(glu)
    mixed = jnp.concatenate([a_ref[...], b_out.astype(BF16)], axis=1)
    xmid = x + gate_m * jnp.dot(mixed, wout_ref[...], preferred_element_type=F32)
    h2 = _rms(xmid, g_ref[...]) * (1.0 + scale_f) + shift_f
    lo, hi, glo, ghi = _route_rows(h2, wrh_ref, wrl_ref, br_ref)
    if not route:
        xmid_ref, h2_ref, lo_ref, hi_ref, glo_ref, ghi_ref = rest
        xmid_ref[...] = xmid
        h2_ref[...] = h2
        lo_ref[...] = lo
        hi_ref[...] = hi
        glo_ref[...] = glo
        ghi_ref[...] = ghi
        return

    tri_ref, xext_ref, cls_ref, rank_ref, cnt_ref, run_ref = rest
    i = pl.program_id(0)
    tm = x.shape[0]

    @pl.when(i == 0)
    def _():
        run_ref[...] = jnp.zeros_like(run_ref)

    a = lo & (EXPERTS_PER_GROUP - 1)
    b = hi & (EXPERTS_PER_GROUP - 1)
    pair = jnp.where(a == 0, 0, jnp.where(a == 1, 3, 5)) + (b - a - 1)
    cls = (lo >> 2) * PAIRS_PER_GROUP + pair
    cls_ref[...] = cls
    onehot = lax.broadcasted_iota(jnp.int32, (CLASS_ROWS, tm), 0) == cls
    prefix = jnp.dot(jnp.where(onehot, 1.0, 0.0).astype(BF16), tri_ref[...],
                     preferred_element_type=F32)
    run = run_ref[...]
    rank = jnp.sum(jnp.where(onehot, prefix - 1.0 + run[:, :1], 0.0), axis=0, keepdims=True)
    rank_ref[...] = rank.astype(jnp.int32)
    run = run + prefix[:, tm - 1:tm]
    run_ref[...] = run
    cnt_ref[...] = run
    bid = jnp.full((1, tm), i // tiles_per_mod, jnp.int32).astype(F32)
    ext = jnp.concatenate([glo, ghi, bid, jnp.zeros((LANES - 3, tm), F32)], axis=0).T
    xext_ref[:, :D_MODEL] = xmid
    xext_ref[:, D_MODEL:] = ext


def _mixer_out(x2d, a_out, y4, mod, g_ffn, w_glu, b_glu, w_out, wr_hi, wr_lo, b_r,
               *, tm, tiles_per_mod, route):
    t = x2d.shape[0]
    rmod = mod.shape[1]
    const2 = lambda i: (0, 0)
    tok = pl.BlockSpec((tm, D_MODEL), lambda i: (i, 0))
    row = pl.BlockSpec((1, tm), lambda i: (0, i))
    in_specs = [
        tok,
        pl.BlockSpec((tm, GMLP_WIDTH), lambda i: (i, 0)),
        pl.BlockSpec((GROUP_BLOCKS, tm, LANES), lambda i: (0, i, 0)),
        pl.BlockSpec((1, rmod, 3 * D_MODEL), lambda i: (i // tiles_per_mod, 0, 0)),
        pl.BlockSpec((1, D_MODEL), const2),
        pl.BlockSpec((SSM_WIDTH, SSM_WIDTH), const2),
        pl.BlockSpec((1, SSM_WIDTH), const2),
        pl.BlockSpec((D_MODEL, D_MODEL), const2),
        pl.BlockSpec((D_MODEL, LANES), const2),
        pl.BlockSpec((D_MODEL, LANES), const2),
        pl.BlockSpec((N_EXPERTS, 1), const2),
    ]
    args = [x2d, a_out, y4, mod, g_ffn, w_glu, b_glu, w_out, wr_hi, wr_lo, b_r]
    if route:
        ids = jnp.arange(tm)
        args.append((ids[:, None] <= ids[None, :]).astype(BF16))
        in_specs.append(pl.BlockSpec((tm, tm), const2))
        out_specs = [pl.BlockSpec((tm, XEXT_WIDTH), lambda i: (i, 0)), row, row,
                     pl.BlockSpec((CLASS_ROWS, LANES), const2)]
        out_shape = [jax.ShapeDtypeStruct((t, XEXT_WIDTH), F32),
                     jax.ShapeDtypeStruct((1, t), jnp.int32),
                     jax.ShapeDtypeStruct((1, t), jnp.int32),
                     jax.ShapeDtypeStruct((CLASS_ROWS, LANES), F32)]
        scratch = [pltpu.VMEM((CLASS_ROWS, LANES), F32)]
    else:
        out_specs = [tok, tok, row, row, row, row]
        out_shape = [jax.ShapeDtypeStruct((t, D_MODEL), F32),
                     jax.ShapeDtypeStruct((t, D_MODEL), F32),
                     jax.ShapeDtypeStruct((1, t), jnp.int32),
                     jax.ShapeDtypeStruct((1, t), jnp.int32),
                     jax.ShapeDtypeStruct((1, t), F32),
                     jax.ShapeDtypeStruct((1, t), F32)]
        scratch = []
    return pl.pallas_call(
        functools.partial(_mixer_out_kernel, route=route, tiles_per_mod=tiles_per_mod),
        grid=(t // tm,),
        in_specs=in_specs,
        out_specs=out_specs,
        out_shape=out_shape,
        scratch_shapes=scratch,
        compiler_params=_cparams("arbitrary" if route else "parallel"),
        name="mixer_out",
    )(*args)


def _plan_kernel(pos_ref, zeros_hbm, src_ref, sem):
    fill = pltpu.make_async_copy(zeros_hbm, src_ref, sem)
    fill.start()
    fill.wait()

    def body(t, carry):
        src_ref[pos_ref[t]] = t
        return carry

    lax.fori_loop(0, pos_ref.shape[0], body, 0, unroll=8)


def _plan(pos, nslots):
    return pl.pallas_call(
        _plan_kernel,
        in_specs=[pl.BlockSpec(memory_space=pltpu.SMEM), pl.BlockSpec(memory_space=pl.ANY)],
        out_specs=pl.BlockSpec(memory_space=pltpu.SMEM),
        out_shape=jax.ShapeDtypeStruct((nslots,), jnp.int32),
        scratch_shapes=[pltpu.SemaphoreType.DMA(())],
        name="moe_plan",
    )(pos, jnp.zeros((nslots,), jnp.int32))


def _moe_routed_kernel(tlo_ref, thi_ref, tval_ref, src_ref,
                       xext_hbm, wgl_ref, wul_ref, wdl_ref, wgh_ref, wuh_ref, wdh_ref,
                       mod_ref, g_ref, gfin_ref, out_hbm, buf_ref, obuf_ref, sem_in, sem_out,
                       *, nseq, final_norm):
    i = pl.program_id(0)
    nt = pl.num_programs(0)
    nv = tval_ref[i]
    nv_next = jnp.where(i + 1 < nt, tval_ref[jnp.minimum(i + 1, nt - 1)], 0)
    nv_prev = jnp.where(i >= 1, tval_ref[jnp.maximum(i - 1, 0)], 0)
    nv_prev2 = jnp.where(i >= 2, tval_ref[jnp.maximum(i - 2, 0)], 0)

    def row_in(tile, sl, j):
        tok = src_ref[tile * MOE_TILE + j]
        return pltpu.make_async_copy(xext_hbm.at[pl.ds(tok, 1), :], buf_ref.at[sl, pl.ds(j, 1), :],
                                     sem_in.at[sl])

    def row_out(tile, sl, j):
        tok = src_ref[tile * MOE_TILE + j]
        return pltpu.make_async_copy(obuf_ref.at[sl, pl.ds(j, 1), :], out_hbm.at[pl.ds(tok, 1), :],
                                     sem_out.at[sl])

    ngroups = MOE_TILE // 8

    def gather_group(tile, sl, g):
        for j in range(g * 8, (g + 1) * 8):
            row_in(tile, sl, j).start(priority=j % 2)

    def gather_wait(sl):
        pltpu.make_async_copy(xext_hbm.at[pl.ds(0, MOE_TILE), :], buf_ref.at[sl], sem_in.at[sl]).wait()

    def scatter_group(tile, sl, n, g):
        @pl.when((g + 1) * 8 <= n)
        def _():
            for j in range(g * 8, (g + 1) * 8):
                row_out(tile, sl, j).start(priority=j % 2)

    def scatter_tail(tile, sl, n):
        def one(j, c):
            row_out(tile, sl, j).start()
            return c

        lax.fori_loop((n >> 3) << 3, n, one, 0)

    def scatter_wait(sl, n):
        n8 = pl.multiple_of((n >> 3) << 3, 8)

        @pl.when(n8 > 0)
        def _():
            pltpu.make_async_copy(obuf_ref.at[sl, pl.ds(0, n8), :], out_hbm.at[pl.ds(0, n8), :],
                                  sem_out.at[sl]).wait()

        def one(j, c):
            pltpu.make_async_copy(obuf_ref.at[sl, pl.ds(0, 1), :], out_hbm.at[pl.ds(0, 1), :],
                                  sem_out.at[sl]).wait()
            return c

        lax.fori_loop(n8, n, one, 0)

    def experts(sl, between):
        buf = buf_ref[sl]
        xm = buf[:, :D_MODEL]
        glo = buf[:, D_MODEL:D_MODEL + 1]
        ghi = buf[:, D_MODEL + 1:D_MODEL + 2]
        bid = buf[:, D_MODEL + 2:D_MODEL + 3]
        mod = mod_ref[...]

        def per_row(k):
            out = mod[nseq - 1:nseq, k * D_MODEL:(k + 1) * D_MODEL]
            for s in range(nseq - 2, -1, -1):
                out = jnp.where(bid == float(s), mod[s:s + 1, k * D_MODEL:(k + 1) * D_MODEL], out)
            return out

        def mm(lhs, w_ref, cols):
            out = jnp.dot(lhs, w_ref[0, :, cols], preferred_element_type=F32)
            between()
            return out

        shift_f, scale_f, gate_f = per_row(0), per_row(1), per_row(2)
        h = (_rms(xm, g_ref[...]) * (1.0 + scale_f) + shift_f).astype(BF16)
        halves = [slice(c * (D_EXPERT // 2), (c + 1) * (D_EXPERT // 2)) for c in range(2)]
        out_halves = [slice(c * (D_MODEL // 2), (c + 1) * (D_MODEL // 2)) for c in range(2)]
        ffn = [None, None]
        for wg, wu, wd, gate in ((wgl_ref, wul_ref, wdl_ref, glo), (wgh_ref, wuh_ref, wdh_ref, ghi)):
            he = jnp.concatenate([jax.nn.silu(mm(h, wg, c)) * mm(h, wu, c) for c in halves],
                                 axis=1).astype(BF16)
            for k, c in enumerate(out_halves):
                y = gate * mm(he, wd, c)
                ffn[k] = y if ffn[k] is None else ffn[k] + y
        x_new = xm + gate_f * jnp.concatenate(ffn, axis=1)
        if final_norm:
            x_new = _rms(x_new, gfin_ref[...])
        obuf_ref[sl] = x_new

    def step(sl):
        other = 1 - sl

        @pl.when(nv_prev2 > 0)
        def _():
            scatter_wait(sl, nv_prev2)

        @pl.when(nv > 0)
        def _():
            gather_wait(sl)
            todo = iter(range(ngroups))
            per_call = -(-ngroups // 12)

            def between():
                for _ in range(per_call):
                    g = next(todo, None)
                    if g is None:
                        return

                    @pl.when(nv_next > 0)
                    def _():
                        gather_group(i + 1, other, g)

                    scatter_group(i - 1, other, nv_prev, g)

            experts(sl, between)
            assert next(todo, None) is None
            scatter_tail(i - 1, other, nv_prev)

        @pl.when(jnp.logical_and(nv == 0, nv_prev > 0))
        def _():
            for g in range(ngroups):
                scatter_group(i - 1, other, nv_prev, g)
            scatter_tail(i - 1, other, nv_prev)

        @pl.when(i == nt - 1)
        def _():
            @pl.when(nv > 0)
            def _():
                for g in range(ngroups):
                    scatter_group(i, sl, nv, g)
                scatter_tail(i, sl, nv)

            @pl.when(nv_prev > 0)
            def _():
                scatter_wait(other, nv_prev)

            @pl.when(nv > 0)
            def _():
                scatter_wait(sl, nv)

    @pl.when(i == 0)
    def _():
        for g in range(ngroups):
            gather_group(0, 0, g)

    for sl in range(2):
        pl.when(i % 2 == sl)(functools.partial(step, sl))


def _moe_routed(xext, cls, rank, counts, wg, wu, wd, mod, g_ffn, g_final, *, final_norm):
    t = xext.shape[0]
    nseq = mod.shape[0]
    ntiles = t // MOE_TILE + N_CLASSES
    nslots = ntiles * MOE_TILE
    counts = counts[:N_CLASSES, 0].astype(jnp.int32)
    tiles_c = (counts + MOE_TILE - 1) // MOE_TILE
    tile_end = jnp.cumsum(tiles_c)
    tile_start = tile_end - tiles_c
    used = tile_end[-1]
    pos = (tile_start * MOE_TILE)[cls[0]] + rank[0]
    src = _plan(pos, nslots)
    tid = jnp.arange(ntiles, dtype=jnp.int32)
    tcls = jnp.sum(tile_end[None, :] <= jnp.minimum(tid, used - 1)[:, None], axis=1).astype(jnp.int32)
    tval = jnp.where(tid < used,
                     jnp.clip(counts[tcls] - (tid - tile_start[tcls]) * MOE_TILE, 0, MOE_TILE), 0)
    tlo = jnp.asarray(CLASS_LO, jnp.int32)[tcls]
    thi = jnp.asarray(CLASS_HI, jnp.int32)[tcls]
    wspec_lo = lambda shape: pl.BlockSpec(shape, lambda i, tlo, thi, tval, src: (tlo[i], 0, 0))
    wspec_hi = lambda shape: pl.BlockSpec(shape, lambda i, tlo, thi, tval, src: (thi[i], 0, 0))
    const2 = lambda i, tlo, thi, tval, src: (0, 0)
    up = (1, D_MODEL, D_EXPERT)
    down = (1, D_EXPERT, D_MODEL)
    return pl.pallas_call(
        functools.partial(_moe_routed_kernel, nseq=nseq, final_norm=final_norm),
        grid_spec=pltpu.PrefetchScalarGridSpec(
            num_scalar_prefetch=4,
            grid=(ntiles,),
            in_specs=[
                pl.BlockSpec(memory_space=pl.ANY),
                wspec_lo(up), wspec_lo(up), wspec_lo(down),
                wspec_hi(up), wspec_hi(up), wspec_hi(down),
                pl.BlockSpec((nseq, 3 * D_MODEL), const2),
                pl.BlockSpec((1, D_MODEL), const2),
                pl.BlockSpec((1, D_MODEL), const2),
            ],
            out_specs=pl.BlockSpec(memory_space=pl.ANY),
            scratch_shapes=[
                pltpu.VMEM((2, MOE_TILE, XEXT_WIDTH), F32),
                pltpu.VMEM((2, MOE_TILE, D_MODEL), F32),
                pltpu.SemaphoreType.DMA((2,)),
                pltpu.SemaphoreType.DMA((2,)),
            ],
        ),
        out_shape=jax.ShapeDtypeStruct((t, D_MODEL), F32),
        compiler_params=_cparams("arbitrary"),
        name="moe_routed",
    )(tlo, thi, tval, src, xext, wg, wu, wd, wg, wu, wd, mod, g_ffn, g_final)


def _moe_kernel(h2_ref, gates_ref, wg_ref, wu_ref, wd_ref, xmid_ref, mod_ref, gfin_ref,
                o_ref, acc_ref, *, final_norm):
    e = pl.program_id(1)

    @pl.when(e == 0)
    def _():
        acc_ref[...] = jnp.zeros_like(acc_ref)

    h = h2_ref[...].astype(BF16)
    he = (jax.nn.silu(jnp.dot(h, wg_ref[0], preferred_element_type=F32))
          * jnp.dot(h, wu_ref[0], preferred_element_type=F32))
    y = jnp.dot(he.astype(BF16), wd_ref[0], preferred_element_type=F32)
    gates = gates_ref[...]
    lane = lax.broadcasted_iota(jnp.int32, gates.shape, 1)
    gcol = jnp.sum(jnp.where(lane == e, gates, 0.0), axis=1, keepdims=True)
    acc_ref[...] += gcol * y

    @pl.when(e == N_EXPERTS - 1)
    def _():
        x = xmid_ref[...] + mod_ref[0] * acc_ref[...]
        if final_norm:
            x = _rms(x, gfin_ref[...])
        o_ref[...] = x


def _moe(h2, gates, wg, wu, wd, xmid, mod, g_final, *, tm, tiles_per_mod, final_norm):
    t = h2.shape[0]
    rmod = mod.shape[1]
    tok = pl.BlockSpec((tm, D_MODEL), lambda i, e: (i, 0))
    return pl.pallas_call(
        functools.partial(_moe_kernel, final_norm=final_norm),
        grid=(t // tm, N_EXPERTS),
        in_specs=[
            tok,
            pl.BlockSpec((tm, N_EXPERTS), lambda i, e: (i, 0)),
            pl.BlockSpec((1, D_MODEL, D_EXPERT), lambda i, e: (e, 0, 0)),
            pl.BlockSpec((1, D_MODEL, D_EXPERT), lambda i, e: (e, 0, 0)),
            pl.BlockSpec((1, D_EXPERT, D_MODEL), lambda i, e: (e, 0, 0)),
            tok,
            pl.BlockSpec((1, rmod, D_MODEL), lambda i, e: (i // tiles_per_mod, 0, 0)),
            pl.BlockSpec((1, D_MODEL), lambda i, e: (0, 0)),
        ],
        out_specs=tok,
        out_shape=jax.ShapeDtypeStruct((t, D_MODEL), F32),
        scratch_shapes=[pltpu.VMEM((tm, D_MODEL), F32)],
        compiler_params=_cparams("parallel", "arbitrary"),
        name="moe",
    )(h2, gates, wg, wu, wd, xmid, mod, g_final)


def _layer(x2d, mod, lw, h0_re, h0_im, *, seq_len, nseq, sample, final_norm):
    t = x2d.shape[0]
    if sample:
        tm, tiles_per_mod, tm_moe = t, 1, t
        s5_rows, s5_blocks = nseq, 1
    else:
        tm = min(512, seq_len)
        tiles_per_mod = seq_len // tm
        tm_moe = min(512, seq_len)
        s5_rows, s5_blocks = seq_len // S5_CHUNK, nseq
    outs = _mixer_in(x2d, mod[..., :2 * D_MODEL], lw["g_mix"], lw["w_in"], lw["v_gain"],
                     lw["ws_sample"] if sample else lw["ws"], lw["bs_sample"] if sample else lw["bs"],
                     tm=tm, tiles_per_mod=tiles_per_mod, want_v=sample)
    a_out, u4 = outs[0], outs[1]
    v_rows = outs[2] if sample else None
    y4, hn_re, hn_im = _s5(u4, lw["prep"], lw["d_tiled"],
                           h0_re, h0_im, rows=s5_rows, nblk=s5_blocks, scan=not sample)
    mo = _mixer_out(x2d, a_out, y4, mod[..., 2 * D_MODEL:5 * D_MODEL],
                    lw["g_ffn"], lw["w_glu"], lw["b_glu"], lw["w_out"], lw["wr_hi"], lw["wr_lo"], lw["b_r"],
                    tm=tm, tiles_per_mod=tiles_per_mod, route=not sample)
    if sample:
        xmid, h2, lo, hi, glo, ghi = mo
        eids = jnp.arange(N_EXPERTS, dtype=jnp.int32)[None, :]
        gates = (jnp.where(lo[0][:, None] == eids, glo[0][:, None], 0.0)
                 + jnp.where(hi[0][:, None] == eids, ghi[0][:, None], 0.0))
        x_new = _moe(h2, gates, lw["wg"], lw["wu"], lw["wd"], xmid, mod[..., 5 * D_MODEL:],
                     lw["g_final"], tm=tm_moe, tiles_per_mod=1, final_norm=final_norm)
    else:
        xext, cls, rank, counts = mo
        x_new = _moe_routed(xext, cls, rank, counts, lw["wg"], lw["wu"], lw["wd"],
                            mod[:, 0, 3 * D_MODEL:], lw["g_ffn"], lw["g_final"], final_norm=final_norm)
    return x_new, hn_re, hn_im, v_rows


def _state_out(h, nseq):
    return h.reshape(GROUP_BLOCKS, nseq, GROUPS_PER_BLOCK, SSM_STATE).transpose(1, 0, 2, 3).reshape(
        nseq, SSM_GROUPS, SSM_STATE)


def _state_in(h, nblk, rh):
    nseq = h.shape[0]
    return h.reshape(nseq, GROUP_BLOCKS, STATE_BLOCK).transpose(1, 0, 2).reshape(
        GROUP_BLOCKS, nblk, rh, STATE_BLOCK)


def kernel(x_prompt, x_sample, c_prompt, c_sample, state_s5_re, state_s5_im, w_ada, b_ada, g_norm_mix, g_norm_ffn, w_in, gmlp_v_gain, gmlp_w_spatial, gmlp_b_spatial, s5_a_re, s5_a_im, s5_log_dt, s5_b_re, s5_b_im, s5_c_re, s5_c_im, s5_d, s5_w_glu, s5_b_glu, w_out, w_router, b_router, w_gate, w_up, w_down, g_final):
    nb, seq_len, _ = x_prompt.shape
    ns, dec_len, _ = x_sample.shape
    assert dec_len == S5_CHUNK and ns * dec_len == GMLP_CHUNK and nb + ns <= ADA_ROWS
    assert seq_len % GMLP_CHUNK == 0

    c_all = jnp.concatenate([c_prompt, c_sample, jnp.zeros((ADA_ROWS - nb - ns, D_MODEL), F32)], axis=0)
    mod_all = _ada(c_all, w_ada, b_ada)

    pos = jnp.arange(GMLP_CHUNK)
    causal = (pos[None, :] // CHUNK) <= (pos[:, None] // CHUNK)
    wr_pad = jnp.pad(w_router, ((0, 0), (0, LANES - N_EXPERTS)))
    wr_hi = wr_pad.astype(BF16)
    wr_lo = (wr_pad - wr_hi.astype(F32)).astype(BF16)
    b_r = b_router.reshape(N_EXPERTS, 1)
    g_fin = g_final.reshape(1, D_MODEL)
    eye_s = jnp.eye(ns, dtype=F32)

    xp = x_prompt.reshape(nb * seq_len, D_MODEL)
    xs = x_sample.reshape(ns * dec_len, D_MODEL)
    zeros_p = jnp.zeros((GROUP_BLOCKS, nb, 1, STATE_BLOCK), F32)
    sp_re, sp_im, ss_re, ss_im, v_new = [], [], [], [], []
    for l in range(DEPTH):
        ws = jnp.where(causal[None], gmlp_w_spatial[l], 0.0)
        ws_sample = jnp.einsum("ab,hij->haibj", eye_s, ws[:, :dec_len, :dec_len]).reshape(
            GMLP_HEADS, GMLP_CHUNK, GMLP_CHUNK)
        bs = jnp.repeat(gmlp_b_spatial[l].T, GMLP_HEAD_DIM, axis=1)
        lw = dict(
            g_mix=g_norm_mix[l].reshape(1, D_MODEL), g_ffn=g_norm_ffn[l].reshape(1, D_MODEL),
            w_in=w_in[l].astype(BF16), v_gain=gmlp_v_gain[l].reshape(1, GMLP_WIDTH),
            ws=ws.astype(BF16), ws_sample=ws_sample.astype(BF16),
            bs=bs, bs_sample=jnp.tile(bs[:dec_len], (ns, 1)),
            prep=_s5_prep(s5_a_re[l], s5_a_im[l], s5_log_dt[l], s5_b_re[l], s5_b_im[l],
                          s5_c_re[l], s5_c_im[l]),
            d_tiled=jnp.tile(s5_d[l].reshape(GROUP_BLOCKS, 1, LANES), (1, 1, S5_CHUNK)),
            w_glu=s5_w_glu[l].astype(BF16), b_glu=s5_b_glu[l].reshape(1, SSM_WIDTH),
            w_out=w_out[l].astype(BF16), wr_hi=wr_hi, wr_lo=wr_lo, b_r=b_r,
            wg=w_gate[l].astype(BF16), wu=w_up[l].astype(BF16), wd=w_down[l].astype(BF16),
            g_final=g_fin,
        )
        last = l == DEPTH - 1
        mod_p = mod_all[l, :nb].reshape(nb, 1, 6 * D_MODEL)
        mod_s = jnp.repeat(mod_all[l, nb:nb + ns], dec_len, axis=0).reshape(1, ns * dec_len, 6 * D_MODEL)
        xp, hp_re, hp_im, _ = _layer(xp, mod_p, lw, zeros_p, zeros_p, seq_len=seq_len, nseq=nb,
                                     sample=False, final_norm=last)
        xs, hs_re, hs_im, vs = _layer(xs, mod_s, lw, _state_in(state_s5_re[l], 1, ns),
                                      _state_in(state_s5_im[l], 1, ns), seq_len=dec_len, nseq=ns,
                                      sample=True, final_norm=last)
        sp_re.append(_state_out(hp_re, nb))
        sp_im.append(_state_out(hp_im, nb))
        ss_re.append(_state_out(hs_re, ns))
        ss_im.append(_state_out(hs_im, ns))
        v_new.append(vs.reshape(ns, dec_len, GMLP_WIDTH))
    return (xp.reshape(nb, seq_len, D_MODEL), xs.reshape(ns, dec_len, D_MODEL),
            jnp.stack(sp_re), jnp.stack(sp_im), jnp.stack(ss_re), jnp.stack(ss_im), jnp.stack(v_new))
```

```python
import functools

import jax
import jax.numpy as jnp
from jax import lax
from jax.experimental import pallas as pl
from jax.experimental.pallas import tpu as pltpu

F32 = jnp.float32
BF16 = jnp.bfloat16

D_MODEL = 1024
DEPTH = 2
CHUNK = 64
GMLP_CHUNK = 128
GMLP_WIDTH = 512
GMLP_HEADS = 4
GMLP_HEAD_DIM = 128
SSM_WIDTH = 512
SSM_GROUP = 16
SSM_GROUPS = 32
SSM_STATE = 64
IN_WIDTH = 1536
N_EXPERTS = 16
EXPERTS_PER_GROUP = 4
N_EXPERT_GROUPS = 4
D_EXPERT = 512
EPS = 1e-6

LANES = 128
S5_CHUNK = 16
GROUP_BLOCKS = 4
GROUPS_PER_BLOCK = SSM_GROUPS // GROUP_BLOCKS
STATE_BLOCK = GROUPS_PER_BLOCK * SSM_STATE
S5_ROW = S5_CHUNK * LANES
ADA_ROWS = 16
PAIRS_PER_GROUP = 6
N_CLASSES = N_EXPERT_GROUPS * PAIRS_PER_GROUP
CLASS_ROWS = 32
_PAIRS = [(a, b) for a in range(EXPERTS_PER_GROUP) for b in range(a + 1, EXPERTS_PER_GROUP)]
CLASS_LO = [g * EXPERTS_PER_GROUP + a for g in range(N_EXPERT_GROUPS) for a, _ in _PAIRS]
CLASS_HI = [g * EXPERTS_PER_GROUP + b for g in range(N_EXPERT_GROUPS) for _, b in _PAIRS]
MOE_TILE = 256
XEXT_WIDTH = D_MODEL + LANES
VMEM_LIMIT = 56 * 1024 * 1024


def _cparams(*sem):
    return pltpu.CompilerParams(dimension_semantics=sem, vmem_limit_bytes=VMEM_LIMIT)


def _ada_kernel(c_ref, w_ref, b_ref, o_ref):
    c = c_ref[...]
    s = (c * jax.nn.sigmoid(c)).astype(BF16)
    o_ref[0] = jnp.dot(s, w_ref[0].astype(BF16), preferred_element_type=F32) + b_ref[0]


def _ada(c_all, w_ada, b_ada):
    nblk = 6
    return pl.pallas_call(
        _ada_kernel,
        grid=(DEPTH, nblk),
        in_specs=[
            pl.BlockSpec((ADA_ROWS, D_MODEL), lambda l, j: (0, 0)),
            pl.BlockSpec((1, D_MODEL, D_MODEL), lambda l, j: (l, 0, j)),
            pl.BlockSpec((1, 1, D_MODEL), lambda l, j: (l, 0, j)),
        ],
        out_specs=pl.BlockSpec((1, ADA_ROWS, D_MODEL), lambda l, j: (l, 0, j)),
        out_shape=jax.ShapeDtypeStruct((DEPTH, ADA_ROWS, 6 * D_MODEL), F32),
        compiler_params=_cparams("parallel", "parallel"),
        name="ada",
    )(c_all, w_ada, b_ada.reshape(DEPTH, 1, 6 * D_MODEL))


def _prep_kernel(are_ref, aim_ref, ldt_ref, bre_ref, bim_ref, cre_ref, cim_ref,
                 msre_ref, msim_ref, nt_ref, wrev_ref, a16re_ref, a16im_ref):
    a_re = are_ref[0]
    a_im = aim_ref[0]
    dt = jnp.exp(ldt_ref[0])
    rho = a_re * dt
    th = a_im * dt
    kk = jnp.minimum(lax.broadcasted_iota(jnp.int32, (24, STATE_BLOCK), 0), S5_CHUNK).astype(F32)
    mag = jnp.exp(kk * rho)
    pw_re = mag * jnp.cos(kk * th)
    pw_im = mag * jnp.sin(kk * th)

    lb_re = pw_re[1:2]
    lb_im = pw_im[1:2]
    num_re = lb_re - 1.0
    num_im = lb_im
    den = a_re * a_re + a_im * a_im
    coef_re = (num_re * a_re + num_im * a_im) / den
    coef_im = (num_im * a_re - num_re * a_im) / den
    b_re = bre_ref[0]
    b_im = bim_ref[0]
    bb_re = coef_re * b_re - coef_im * b_im
    bb_im = coef_re * b_im + coef_im * b_re

    rows = lax.broadcasted_iota(jnp.int32, (LANES, STATE_BLOCK), 0)
    cols = lax.broadcasted_iota(jnp.int32, (LANES, STATE_BLOCK), 1)
    same_group = (rows >> 4) == (cols >> 6)

    def blockdiag(x16):
        return jnp.where(same_group, jnp.concatenate([x16] * GROUPS_PER_BLOCK, axis=0), 0.0)

    for s in range(S5_CHUNK):
        k = S5_CHUNK - 1 - s
        p_re = pw_re[k:k + 1]
        p_im = pw_im[k:k + 1]
        msre_ref[0, s * LANES:(s + 1) * LANES, :] = blockdiag(p_re * bb_re - p_im * bb_im).astype(BF16)
        msim_ref[0, s * LANES:(s + 1) * LANES, :] = blockdiag(p_re * bb_im + p_im * bb_re).astype(BF16)

    bcat = jnp.concatenate([blockdiag(bb_re), blockdiag(bb_im)], axis=1)
    c_re = cre_ref[0]
    c_im = cim_ref[0]
    for k in range(S5_CHUNK + 1):
        p_re = pw_re[k:k + 1]
        p_im = pw_im[k:k + 1]
        cl = jnp.concatenate([blockdiag(c_re * p_re - c_im * p_im),
                              -blockdiag(c_re * p_im + c_im * p_re)], axis=1)
        if k >= 1:
            nt_ref[0, (k - 1) * LANES:k * LANES, :] = cl.astype(BF16)
        if k < S5_CHUNK:
            wl = lax.dot_general(bcat, cl, (((1,), (1,)), ((), ())),
                                 precision=lax.Precision.HIGHEST, preferred_element_type=F32)
            j = S5_CHUNK - 1 - k
            wrev_ref[0, j * LANES:(j + 1) * LANES, :LANES] = wl.astype(BF16)
            if j >= 1:
                wrev_ref[0, (j - 1) * LANES:j * LANES, LANES:] = wl.astype(BF16)
    wrev_ref[0, (S5_CHUNK - 1) * LANES:, LANES:] = jnp.zeros((LANES, LANES), BF16)

    a16re_ref[0] = pw_re[S5_CHUNK:S5_CHUNK + 1]
    a16im_ref[0] = pw_im[S5_CHUNK:S5_CHUNK + 1]


def _s5_prep(a_re, a_im, log_dt, b_re, b_im, c_re, c_im):
    nstate = SSM_GROUPS * SSM_STATE

    def lane_row(v):
        return v.reshape(GROUP_BLOCKS, 1, STATE_BLOCK)

    def rows16(v):
        return v.reshape(SSM_GROUP, GROUP_BLOCKS, STATE_BLOCK).transpose(1, 0, 2)

    ldt = jnp.repeat(log_dt, SSM_STATE).reshape(SSM_GROUPS, SSM_STATE)
    bt_re = rows16(b_re.transpose(2, 0, 1).reshape(SSM_GROUP, nstate))
    bt_im = rows16(b_im.transpose(2, 0, 1).reshape(SSM_GROUP, nstate))
    ct_re = rows16(c_re.transpose(1, 0, 2).reshape(SSM_GROUP, nstate))
    ct_im = rows16(c_im.transpose(1, 0, 2).reshape(SSM_GROUP, nstate))
    row_spec = pl.BlockSpec((1, 1, STATE_BLOCK), lambda g: (g, 0, 0))
    r16_spec = pl.BlockSpec((1, SSM_GROUP, STATE_BLOCK), lambda g: (g, 0, 0))
    return pl.pallas_call(
        _prep_kernel,
        grid=(GROUP_BLOCKS,),
        in_specs=[row_spec, row_spec, row_spec, r16_spec, r16_spec, r16_spec, r16_spec],
        out_specs=[
            pl.BlockSpec((1, S5_ROW, STATE_BLOCK), lambda g: (g, 0, 0)),
            pl.BlockSpec((1, S5_ROW, STATE_BLOCK), lambda g: (g, 0, 0)),
            pl.BlockSpec((1, S5_ROW, 2 * STATE_BLOCK), lambda g: (g, 0, 0)),
            pl.BlockSpec((1, S5_ROW, 2 * LANES), lambda g: (g, 0, 0)),
            row_spec, row_spec,
        ],
        out_shape=[
            jax.ShapeDtypeStruct((GROUP_BLOCKS, S5_ROW, STATE_BLOCK), BF16),
            jax.ShapeDtypeStruct((GROUP_BLOCKS, S5_ROW, STATE_BLOCK), BF16),
            jax.ShapeDtypeStruct((GROUP_BLOCKS, S5_ROW, 2 * STATE_BLOCK), BF16),
            jax.ShapeDtypeStruct((GROUP_BLOCKS, S5_ROW, 2 * LANES), BF16),
            jax.ShapeDtypeStruct((GROUP_BLOCKS, 1, STATE_BLOCK), F32),
            jax.ShapeDtypeStruct((GROUP_BLOCKS, 1, STATE_BLOCK), F32),
        ],
        compiler_params=_cparams("parallel"),
        name="s5_prep",
    )(lane_row(a_re), lane_row(a_im), lane_row(ldt), bt_re, bt_im, ct_re, ct_im)


def _rms(x, g):
    return x * lax.rsqrt(jnp.mean(x * x, axis=-1, keepdims=True) + EPS) * g


def _mixer_in_kernel(x_ref, mod_ref, g_ref, win_ref, vg_ref, ws_ref, bs_ref,
                     a_ref, u_ref, *v_ref, tm):
    x = x_ref[...]
    mod = mod_ref[0]
    shift = mod[:, :D_MODEL]
    scale = mod[:, D_MODEL:]
    h = _rms(x, g_ref[...]) * (1.0 + scale) + shift
    proj = jnp.dot(h.astype(BF16), win_ref[...], preferred_element_type=F32)
    z = jax.nn.gelu(proj[:, :2 * GMLP_WIDTH])
    u = z[:, :GMLP_WIDTH]
    v = z[:, GMLP_WIDTH:]
    vc = v - jnp.mean(v, axis=-1, keepdims=True)
    vn = vc * lax.rsqrt(jnp.mean(vc * vc, axis=-1, keepdims=True) + EPS) * vg_ref[...]
    if v_ref:
        v_ref[0][...] = vn
    vb = vn.astype(BF16)
    bias = bs_ref[...]
    for c in range(tm // GMLP_CHUNK):
        r0 = c * GMLP_CHUNK
        for hh in range(GMLP_HEADS):
            l0 = hh * GMLP_HEAD_DIM
            mixed = jnp.dot(ws_ref[hh], vb[r0:r0 + GMLP_CHUNK, l0:l0 + GMLP_HEAD_DIM],
                            preferred_element_type=F32) + bias[:, l0:l0 + GMLP_HEAD_DIM]
            a_ref[r0:r0 + GMLP_CHUNK, l0:l0 + GMLP_HEAD_DIM] = (
                u[r0:r0 + GMLP_CHUNK, l0:l0 + GMLP_HEAD_DIM] * mixed).astype(BF16)
    for gb in range(GROUP_BLOCKS):
        l0 = 2 * GMLP_WIDTH + gb * LANES
        u_ref[gb] = proj[:, l0:l0 + LANES]


def _mixer_in(x2d, mod, g_mix, w_in, v_gain, ws, bs, *, tm, tiles_per_mod, want_v):
    t = x2d.shape[0]
    rmod = mod.shape[1]
    const2 = lambda i: (0, 0)
    out_shape = [jax.ShapeDtypeStruct((t, GMLP_WIDTH), BF16),
                 jax.ShapeDtypeStruct((GROUP_BLOCKS, t, LANES), F32)]
    out_specs = [pl.BlockSpec((tm, GMLP_WIDTH), lambda i: (i, 0)),
                 pl.BlockSpec((GROUP_BLOCKS, tm, LANES), lambda i: (0, i, 0))]
    if want_v:
        out_shape.append(jax.ShapeDtypeStruct((t, GMLP_WIDTH), F32))
        out_specs.append(pl.BlockSpec((tm, GMLP_WIDTH), lambda i: (i, 0)))
    return pl.pallas_call(
        functools.partial(_mixer_in_kernel, tm=tm),
        grid=(t // tm,),
        in_specs=[
            pl.BlockSpec((tm, D_MODEL), lambda i: (i, 0)),
            pl.BlockSpec((1, rmod, 2 * D_MODEL), lambda i: (i // tiles_per_mod, 0, 0)),
            pl.BlockSpec((1, D_MODEL), const2),
            pl.BlockSpec((D_MODEL, IN_WIDTH), const2),
            pl.BlockSpec((1, GMLP_WIDTH), const2),
            pl.BlockSpec((GMLP_HEADS, GMLP_CHUNK, GMLP_CHUNK), lambda i: (0, 0, 0)),
            pl.BlockSpec((GMLP_CHUNK, GMLP_WIDTH), const2),
        ],
        out_specs=out_specs,
        out_shape=out_shape,
        compiler_params=_cparams("parallel"),
        name="mixer_in",
    )(x2d, mod, g_mix, w_in, v_gain, ws, bs)


def _s5_kernel(u_ref, msre_ref, msim_ref, nt_ref, wrev_ref, a16re_ref, a16im_ref, d_ref,
               h0re_ref, h0im_ref, y_ref, hnre_ref, hnim_ref, *scratch, scan):
    rows = u_ref.shape[1] // S5_CHUNK
    u = jnp.concatenate([u_ref[0, pl.ds(s, rows, stride=S5_CHUNK), :] for s in range(S5_CHUNK)], axis=1)
    ub = u.astype(BF16)
    s_re = jnp.dot(ub, msre_ref[0], preferred_element_type=F32)
    s_im = jnp.dot(ub, msim_ref[0], preferred_element_type=F32)
    a_re = a16re_ref[0]
    a_im = a16im_ref[0]
    h0_re = h0re_ref[0, 0]
    h0_im = h0im_ref[0, 0]
    if scan:
        sre_scr, sim_scr, hre_scr, him_scr = scratch
        sre_scr[...] = s_re
        sim_scr[...] = s_im

        def body(r, carry):
            hr, hi = carry
            hre_scr[pl.ds(r, 1), :] = hr
            him_scr[pl.ds(r, 1), :] = hi
            sr = sre_scr[pl.ds(r, 1), :]
            si = sim_scr[pl.ds(r, 1), :]
            return (a_re * hr - a_im * hi + sr, a_re * hi + a_im * hr + si)

        hn_re, hn_im = lax.fori_loop(0, rows, body, (h0_re, h0_im), unroll=8)
        h_re = hre_scr[...]
        h_im = him_scr[...]
    else:
        h_re = h0_re
        h_im = h0_im
        hn_re = a_re * h_re - a_im * h_im + s_re
        hn_im = a_re * h_im + a_im * h_re + s_im
    hnre_ref[0, 0] = hn_re
    hnim_ref[0, 0] = hn_im
    hcat = jnp.concatenate([h_re, h_im], axis=1).astype(BF16)
    inter = lax.dot_general(hcat, nt_ref[0], (((1,), (1,)), ((), ())),
                            preferred_element_type=F32)
    d = d_ref[0]
    for t in range(0, S5_CHUNK, 2):
        k0 = (S5_CHUNK - 2 - t) * LANES
        pair = jnp.dot(ub[:, :(t + 2) * LANES], wrev_ref[0, k0:, :], preferred_element_type=F32)
        for step, intra in ((t + 1, pair[:, :LANES]), (t, pair[:, LANES:])):
            sl = slice(step * LANES, (step + 1) * LANES)
            y_ref[0, pl.ds(step, rows, stride=S5_CHUNK), :] = intra + inter[:, sl] + d[:, sl] * u[:, sl]


def _s5(u4, prep, d_tiled, h0_re, h0_im, *, rows, nblk, scan):
    ms_re, ms_im, nt, wrev, a16_re, a16_im = prep
    rh = h0_re.shape[2]
    wspec = lambda shape: pl.BlockSpec((1,) + shape, lambda g, b: (g, 0, 0))
    hspec = pl.BlockSpec((1, 1, rh, STATE_BLOCK), lambda g, b: (g, b, 0, 0))
    scratch = [pltpu.VMEM((rows, STATE_BLOCK), F32)] * 4 if scan else []
    return pl.pallas_call(
        functools.partial(_s5_kernel, scan=scan),
        grid=(GROUP_BLOCKS, nblk),
        in_specs=[
            pl.BlockSpec((1, rows * S5_CHUNK, LANES), lambda g, b: (g, b, 0)),
            wspec((S5_ROW, STATE_BLOCK)), wspec((S5_ROW, STATE_BLOCK)),
            wspec((S5_ROW, 2 * STATE_BLOCK)), wspec((S5_ROW, 2 * LANES)),
            wspec((1, STATE_BLOCK)), wspec((1, STATE_BLOCK)), wspec((1, S5_ROW)),
            hspec, hspec,
        ],
        out_specs=[pl.BlockSpec((1, rows * S5_CHUNK, LANES), lambda g, b: (g, b, 0)), hspec, hspec],
        out_shape=[
            jax.ShapeDtypeStruct(u4.shape, F32),
            jax.ShapeDtypeStruct(h0_re.shape, F32),
            jax.ShapeDtypeStruct(h0_re.shape, F32),
        ],
        scratch_shapes=scratch,
        compiler_params=_cparams("parallel", "parallel"),
        name="s5",
    )(u4, ms_re, ms_im, nt, wrev, a16_re, a16_im, d_tiled, h0_re, h0_im)


def _split_bf16(x):
    hi = x.astype(BF16)
    return hi, (x - hi.astype(F32)).astype(BF16)


def _top2_of4(a):
    m1 = jnp.maximum(jnp.maximum(a[0], a[1]), jnp.maximum(a[2], a[3]))
    i1 = jnp.where(a[0] == m1, 0, jnp.where(a[1] == m1, 1, jnp.where(a[2] == m1, 2, 3)))
    b = [jnp.where(i1 == j, -jnp.inf, a[j]) for j in range(4)]
    m2 = jnp.maximum(jnp.maximum(b[0], b[1]), jnp.maximum(b[2], b[3]))
    i2 = jnp.where(b[0] == m2, 0, jnp.where(b[1] == m2, 1, jnp.where(b[2] == m2, 2, 3)))
    return m1, i1, m2, i2


def _route_rows(h2, wr_ref, br_ref):
    h_hi, h_lo = _split_bf16(h2)
    parts = jnp.dot(jnp.concatenate([h_hi, h_lo], axis=1), wr_ref[...], preferred_element_type=F32)
    pt = parts.T
    lt = pt[:N_EXPERTS] + pt[N_EXPERTS:2 * N_EXPERTS] + br_ref[...]
    rows = [lt[e:e + 1] for e in range(N_EXPERTS)]
    mx = functools.reduce(jnp.maximum, rows)
    ex = [jnp.exp(r - mx) for r in rows]
    tot = functools.reduce(lambda p, q: p + q, ex)
    scores = [e / tot for e in ex]
    best = None
    for g in range(N_EXPERT_GROUPS):
        m1, i1, m2, i2 = _top2_of4(scores[g * EXPERTS_PER_GROUP:(g + 1) * EXPERTS_PER_GROUP])
        cand = (m1 + m2, m1, i1 + g * EXPERTS_PER_GROUP, m2, i2 + g * EXPERTS_PER_GROUP)
        if best is None:
            best = cand
        else:
            better = cand[0] > best[0]
            best = tuple(jnp.where(better, c, b) for c, b in zip(cand, best))
    _, v1, e1, v2, e2 = best
    den = v1 + v2
    w1 = v1 / den
    w2 = v2 / den
    first_lo = e1 < e2
    return (jnp.where(first_lo, e1, e2), jnp.where(first_lo, e2, e1),
            jnp.where(first_lo, w1, w2), jnp.where(first_lo, w2, w1))


def _mixer_out_kernel(x_ref, a_ref, y_ref, mod_ref, g_ref, wglu_ref, bglu_ref, wout_ref,
                      wr_ref, br_ref, *rest, route, tiles_per_mod):
    x = x_ref[...]
    mod = mod_ref[0]
    gate_m = mod[:, :D_MODEL]
    shift_f = mod[:, D_MODEL:2 * D_MODEL]
    scale_f = mod[:, 2 * D_MODEL:]
    ys = jax.nn.gelu(jnp.concatenate([y_ref[gb] for gb in range(GROUP_BLOCKS)], axis=1))
    glu = jnp.dot(ys.astype(BF16), wglu_ref[...], preferred_element_type=F32) + bglu_ref[...]
    b_out = ys * jax.nn.sigmoid(glu)
    mixed = jnp.concatenate([a_ref[...], b_out.astype(BF16)], axis=1)
    xmid = x + gate_m * jnp.dot(mixed, wout_ref[...], preferred_element_type=F32)
    h2 = _rms(xmid, g_ref[...]) * (1.0 + scale_f) + shift_f
    lo, hi, glo, ghi = _route_rows(h2, wr_ref, br_ref)
    if not route:
        xmid_ref, h2_ref, lo_ref, hi_ref, glo_ref, ghi_ref = rest
        xmid_ref[...] = xmid
        h2_ref[...] = h2
        lo_ref[...] = lo
        hi_ref[...] = hi
        glo_ref[...] = glo
        ghi_ref[...] = ghi
        return

    tri_ref, xext_ref, cls_ref, rank_ref, cnt_ref, run_ref = rest
    i = pl.program_id(0)
    tm = x.shape[0]

    @pl.when(i == 0)
    def _():
        run_ref[...] = jnp.zeros_like(run_ref)

    a = lo & (EXPERTS_PER_GROUP - 1)
    b = hi & (EXPERTS_PER_GROUP - 1)
    pair = jnp.where(a == 0, 0, jnp.where(a == 1, 3, 5)) + (b - a - 1)
    cls = (lo >> 2) * PAIRS_PER_GROUP + pair
    cls_ref[...] = cls
    onehot = lax.broadcasted_iota(jnp.int32, (CLASS_ROWS, tm), 0) == cls
    prefix = jnp.dot(jnp.where(onehot, 1.0, 0.0).astype(BF16), tri_ref[...],
                     preferred_element_type=F32)
    run = run_ref[...]
    rank = jnp.sum(jnp.where(onehot, prefix - 1.0 + run[:, :1], 0.0), axis=0, keepdims=True)
    rank_ref[...] = rank.astype(jnp.int32)
    run = run + prefix[:, tm - 1:tm]
    run_ref[...] = run
    cnt_ref[...] = run
    bid = jnp.full((1, tm), i // tiles_per_mod, jnp.int32).astype(F32)
    ext = jnp.concatenate([glo, ghi, bid, jnp.zeros((LANES - 3, tm), F32)], axis=0).T
    xext_ref[:, :D_MODEL] = xmid
    xext_ref[:, D_MODEL:] = ext


def _mixer_out(x2d, a_out, y4, mod, g_ffn, w_glu, b_glu, w_out, wr, b_r,
               *, tm, tiles_per_mod, route):
    t = x2d.shape[0]
    rmod = mod.shape[1]
    const2 = lambda i: (0, 0)
    tok = pl.BlockSpec((tm, D_MODEL), lambda i: (i, 0))
    row = pl.BlockSpec((1, tm), lambda i: (0, i))
    in_specs = [
        tok,
        pl.BlockSpec((tm, GMLP_WIDTH), lambda i: (i, 0)),
        pl.BlockSpec((GROUP_BLOCKS, tm, LANES), lambda i: (0, i, 0)),
        pl.BlockSpec((1, rmod, 3 * D_MODEL), lambda i: (i // tiles_per_mod, 0, 0)),
        pl.BlockSpec((1, D_MODEL), const2),
        pl.BlockSpec((SSM_WIDTH, SSM_WIDTH), const2),
        pl.BlockSpec((1, SSM_WIDTH), const2),
        pl.BlockSpec((D_MODEL, D_MODEL), const2),
        pl.BlockSpec((2 * D_MODEL, LANES), const2),
        pl.BlockSpec((N_EXPERTS, 1), const2),
    ]
    args = [x2d, a_out, y4, mod, g_ffn, w_glu, b_glu, w_out, wr, b_r]
    if route:
        ids = jnp.arange(tm)
        args.append((ids[:, None] <= ids[None, :]).astype(BF16))
        in_specs.append(pl.BlockSpec((tm, tm), const2))
        out_specs = [pl.BlockSpec((tm, XEXT_WIDTH), lambda i: (i, 0)), row, row,
                     pl.BlockSpec((CLASS_ROWS, LANES), const2)]
        out_shape = [jax.ShapeDtypeStruct((t, XEXT_WIDTH), F32),
                     jax.ShapeDtypeStruct((1, t), jnp.int32),
                     jax.ShapeDtypeStruct((1, t), jnp.int32),
                     jax.ShapeDtypeStruct((CLASS_ROWS, LANES), F32)]
        scratch = [pltpu.VMEM((CLASS_ROWS, LANES), F32)]
    else:
        out_specs = [tok, tok, row, row, row, row]
        out_shape = [jax.ShapeDtypeStruct((t, D_MODEL), F32),
                     jax.ShapeDtypeStruct((t, D_MODEL), F32),
                     jax.ShapeDtypeStruct((1, t), jnp.int32),
                     jax.ShapeDtypeStruct((1, t), jnp.int32),
                     jax.ShapeDtypeStruct((1, t), F32),
                     jax.ShapeDtypeStruct((1, t), F32)]
        scratch = []
    return pl.pallas_call(
        functools.partial(_mixer_out_kernel, route=route, tiles_per_mod=tiles_per_mod),
        grid=(t // tm,),
        in_specs=in_specs,
        out_specs=out_specs,
        out_shape=out_shape,
        scratch_shapes=scratch,
        compiler_params=_cparams("arbitrary" if route else "parallel"),
        name="mixer_out",
    )(*args)


def _plan_kernel(pos_ref, zeros_hbm, src_ref, sem):
    fill = pltpu.make_async_copy(zeros_hbm, src_ref, sem)
    fill.start()
    fill.wait()

    def body(t, carry):
        src_ref[pos_ref[t]] = t
        return carry

    lax.fori_loop(0, pos_ref.shape[0], body, 0, unroll=8)


def _plan(pos, nslots):
    return pl.pallas_call(
        _plan_kernel,
        in_specs=[pl.BlockSpec(memory_space=pltpu.SMEM), pl.BlockSpec(memory_space=pl.ANY)],
        out_specs=pl.BlockSpec(memory_space=pltpu.SMEM),
        out_shape=jax.ShapeDtypeStruct((nslots,), jnp.int32),
        scratch_shapes=[pltpu.SemaphoreType.DMA(())],
        name="moe_plan",
    )(pos, jnp.zeros((nslots,), jnp.int32))


def _moe_routed_kernel(tlo_ref, thi_ref, tval_ref, src_ref,
                       xext_hbm, wgl_ref, wul_ref, wdl_ref, wgh_ref, wuh_ref, wdh_ref,
                       mod_ref, g_ref, gfin_ref, out_hbm, buf_ref, obuf_ref, sem_in, sem_out,
                       *, nseq, final_norm):
    i = pl.program_id(0)
    nt = pl.num_programs(0)
    nv = tval_ref[i]
    nv_next = jnp.where(i + 1 < nt, tval_ref[jnp.minimum(i + 1, nt - 1)], 0)
    nv_prev = jnp.where(i >= 1, tval_ref[jnp.maximum(i - 1, 0)], 0)
    nv_prev2 = jnp.where(i >= 2, tval_ref[jnp.maximum(i - 2, 0)], 0)

    def row_in(tile, sl, j):
        tok = src_ref[tile * MOE_TILE + j]
        return pltpu.make_async_copy(xext_hbm.at[pl.ds(tok, 1), :], buf_ref.at[sl, pl.ds(j, 1), :],
                                     sem_in.at[sl])

    def row_out(tile, sl, j):
        tok = src_ref[tile * MOE_TILE + j]
        return pltpu.make_async_copy(obuf_ref.at[sl, pl.ds(j, 1), :], out_hbm.at[pl.ds(tok, 1), :],
                                     sem_out.at[sl])

    def gather_start(tile, sl):
        for j in range(MOE_TILE):
            row_in(tile, sl, j).start(priority=j % 2)

    def gather_wait(sl):
        pltpu.make_async_copy(xext_hbm.at[pl.ds(0, MOE_TILE), :], buf_ref.at[sl], sem_in.at[sl]).wait()

    def scatter_start(tile, sl, n):
        for g in range(MOE_TILE // 8):
            @pl.when((g + 1) * 8 <= n)
            def _():
                for j in range(g * 8, (g + 1) * 8):
                    row_out(tile, sl, j).start(priority=j % 2)

        def one(j, c):
            row_out(tile, sl, j).start()
            return c

        lax.fori_loop((n >> 3) << 3, n, one, 0)

    def scatter_wait(sl, n):
        n8 = pl.multiple_of((n >> 3) << 3, 8)

        @pl.when(n8 > 0)
        def _():
            pltpu.make_async_copy(obuf_ref.at[sl, pl.ds(0, n8), :], out_hbm.at[pl.ds(0, n8), :],
                                  sem_out.at[sl]).wait()

        def one(j, c):
            pltpu.make_async_copy(obuf_ref.at[sl, pl.ds(0, 1), :], out_hbm.at[pl.ds(0, 1), :],
                                  sem_out.at[sl]).wait()
            return c

        lax.fori_loop(n8, n, one, 0)

    def experts(sl):
        buf = buf_ref[sl]
        xm = buf[:, :D_MODEL]
        glo = buf[:, D_MODEL:D_MODEL + 1]
        ghi = buf[:, D_MODEL + 1:D_MODEL + 2]
        bid = buf[:, D_MODEL + 2:D_MODEL + 3]
        mod = mod_ref[...]

        def per_row(k):
            out = mod[nseq - 1:nseq, k * D_MODEL:(k + 1) * D_MODEL]
            for s in range(nseq - 2, -1, -1):
                out = jnp.where(bid == float(s), mod[s:s + 1, k * D_MODEL:(k + 1) * D_MODEL], out)
            return out

        shift_f, scale_f, gate_f = per_row(0), per_row(1), per_row(2)
        h = (_rms(xm, g_ref[...]) * (1.0 + scale_f) + shift_f).astype(BF16)
        ffn = None
        for wg, wu, wd, gate in ((wgl_ref, wul_ref, wdl_ref, glo), (wgh_ref, wuh_ref, wdh_ref, ghi)):
            he = (jax.nn.silu(jnp.dot(h, wg[0], preferred_element_type=F32))
                  * jnp.dot(h, wu[0], preferred_element_type=F32))
            y = gate * jnp.dot(he.astype(BF16), wd[0], preferred_element_type=F32)
            ffn = y if ffn is None else ffn + y
        x_new = xm + gate_f * ffn
        if final_norm:
            x_new = _rms(x_new, gfin_ref[...])
        obuf_ref[sl] = x_new

    def step(sl):
        other = 1 - sl

        @pl.when(nv_next > 0)
        def _():
            gather_start(i + 1, other)

        @pl.when(nv_prev2 > 0)
        def _():
            scatter_wait(sl, nv_prev2)

        @pl.when(nv > 0)
        def _():
            gather_wait(sl)
            experts(sl)
            scatter_start(i, sl, nv)

        @pl.when(i == nt - 1)
        def _():
            @pl.when(nv_prev > 0)
            def _():
                scatter_wait(other, nv_prev)

            @pl.when(nv > 0)
            def _():
                scatter_wait(sl, nv)

    @pl.when(i == 0)
    def _():
        gather_start(0, 0)

    for sl in range(2):
        pl.when(i % 2 == sl)(functools.partial(step, sl))


def _moe_routed(xext, cls, rank, counts, wg, wu, wd, mod, g_ffn, g_final, *, expert0, final_norm):
    t = xext.shape[0]
    nseq = mod.shape[0]
    ntiles = t // MOE_TILE + N_CLASSES
    nslots = ntiles * MOE_TILE
    counts = counts[:N_CLASSES, 0].astype(jnp.int32)
    tiles_c = (counts + MOE_TILE - 1) // MOE_TILE
    tile_end = jnp.cumsum(tiles_c)
    tile_start = tile_end - tiles_c
    used = tile_end[-1]
    pos = (tile_start * MOE_TILE)[cls[0]] + rank[0]
    src = _plan(pos, nslots)
    tid = jnp.arange(ntiles, dtype=jnp.int32)
    tcls = jnp.sum(tile_end[None, :] <= jnp.minimum(tid, used - 1)[:, None], axis=1).astype(jnp.int32)
    tval = jnp.where(tid < used,
                     jnp.clip(counts[tcls] - (tid - tile_start[tcls]) * MOE_TILE, 0, MOE_TILE), 0)
    tlo = jnp.asarray(CLASS_LO, jnp.int32)[tcls]
    thi = jnp.asarray(CLASS_HI, jnp.int32)[tcls]
    wspec_lo = lambda shape: pl.BlockSpec(shape, lambda i, tlo, thi, tval, src: (expert0 + tlo[i], 0, 0))
    wspec_hi = lambda shape: pl.BlockSpec(shape, lambda i, tlo, thi, tval, src: (expert0 + thi[i], 0, 0))
    const2 = lambda i, tlo, thi, tval, src: (0, 0)
    up = (1, D_MODEL, D_EXPERT)
    down = (1, D_EXPERT, D_MODEL)
    return pl.pallas_call(
        functools.partial(_moe_routed_kernel, nseq=nseq, final_norm=final_norm),
        grid_spec=pltpu.PrefetchScalarGridSpec(
            num_scalar_prefetch=4,
            grid=(ntiles,),
            in_specs=[
                pl.BlockSpec(memory_space=pl.ANY),
                wspec_lo(up), wspec_lo(up), wspec_lo(down),
                wspec_hi(up), wspec_hi(up), wspec_hi(down),
                pl.BlockSpec((nseq, 3 * D_MODEL), const2),
                pl.BlockSpec((1, D_MODEL), const2),
                pl.BlockSpec((1, D_MODEL), const2),
            ],
            out_specs=pl.BlockSpec(memory_space=pl.ANY),
            scratch_shapes=[
                pltpu.VMEM((2, MOE_TILE, XEXT_WIDTH), F32),
                pltpu.VMEM((2, MOE_TILE, D_MODEL), F32),
                pltpu.SemaphoreType.DMA((2,)),
                pltpu.SemaphoreType.DMA((2,)),
            ],
        ),
        out_shape=jax.ShapeDtypeStruct((t, D_MODEL), F32),
        compiler_params=_cparams("arbitrary"),
        name="moe_routed",
    )(tlo, thi, tval, src, xext, wg, wu, wd, wg, wu, wd, mod, g_ffn, g_final)


def _moe_kernel(h2_ref, gates_ref, wg_ref, wu_ref, wd_ref, xmid_ref, mod_ref, gfin_ref,
                o_ref, acc_ref, *, final_norm):
    e = pl.program_id(1)

    @pl.when(e == 0)
    def _():
        acc_ref[...] = jnp.zeros_like(acc_ref)

    h = h2_ref[...].astype(BF16)
    he = (jax.nn.silu(jnp.dot(h, wg_ref[0], preferred_element_type=F32))
          * jnp.dot(h, wu_ref[0], preferred_element_type=F32))
    y = jnp.dot(he.astype(BF16), wd_ref[0], preferred_element_type=F32)
    gates = gates_ref[...]
    lane = lax.broadcasted_iota(jnp.int32, gates.shape, 1)
    gcol = jnp.sum(jnp.where(lane == e, gates, 0.0), axis=1, keepdims=True)
    acc_ref[...] += gcol * y

    @pl.when(e == N_EXPERTS - 1)
    def _():
        x = xmid_ref[...] + mod_ref[0] * acc_ref[...]
        if final_norm:
            x = _rms(x, gfin_ref[...])
        o_ref[...] = x


def _moe(h2, gates, wg, wu, wd, xmid, mod, g_final, *, tm, tiles_per_mod, expert0, final_norm):
    t = h2.shape[0]
    rmod = mod.shape[1]
    tok = pl.BlockSpec((tm, D_MODEL), lambda i, e: (i, 0))
    return pl.pallas_call(
        functools.partial(_moe_kernel, final_norm=final_norm),
        grid=(t // tm, N_EXPERTS),
        in_specs=[
            tok,
            pl.BlockSpec((tm, N_EXPERTS), lambda i, e: (i, 0)),
            pl.BlockSpec((1, D_MODEL, D_EXPERT), lambda i, e: (expert0 + e, 0, 0)),
            pl.BlockSpec((1, D_MODEL, D_EXPERT), lambda i, e: (expert0 + e, 0, 0)),
            pl.BlockSpec((1, D_EXPERT, D_MODEL), lambda i, e: (expert0 + e, 0, 0)),
            tok,
            pl.BlockSpec((1, rmod, D_MODEL), lambda i, e: (i // tiles_per_mod, 0, 0)),
            pl.BlockSpec((1, D_MODEL), lambda i, e: (0, 0)),
        ],
        out_specs=tok,
        out_shape=jax.ShapeDtypeStruct((t, D_MODEL), F32),
        scratch_shapes=[pltpu.VMEM((tm, D_MODEL), F32)],
        compiler_params=_cparams("parallel", "arbitrary"),
        name="moe",
    )(h2, gates, wg, wu, wd, xmid, mod, g_final)


def _layer(x2d, mod, lw, h0_re, h0_im, *, seq_len, nseq, sample, final_norm):
    t = x2d.shape[0]
    if sample:
        tm, tiles_per_mod, tm_moe = t, 1, t
        s5_rows, s5_blocks = nseq, 1
    else:
        tm = min(512, seq_len)
        tiles_per_mod = seq_len // tm
        tm_moe = min(512, seq_len)
        s5_rows, s5_blocks = seq_len // S5_CHUNK, nseq
    outs = _mixer_in(x2d, mod[..., :2 * D_MODEL], lw["g_mix"], lw["w_in"], lw["v_gain"],
                     lw["ws_sample"] if sample else lw["ws"], lw["bs_sample"] if sample else lw["bs"],
                     tm=tm, tiles_per_mod=tiles_per_mod, want_v=sample)
    a_out, u4 = outs[0], outs[1]
    v_rows = outs[2] if sample else None
    y4, hn_re, hn_im = _s5(u4, lw["prep"], lw["d_tiled"],
                           h0_re, h0_im, rows=s5_rows, nblk=s5_blocks, scan=not sample)
    mo = _mixer_out(x2d, a_out, y4, mod[..., 2 * D_MODEL:5 * D_MODEL],
                    lw["g_ffn"], lw["w_glu"], lw["b_glu"], lw["w_out"], lw["wr"], lw["b_r"],
                    tm=tm, tiles_per_mod=tiles_per_mod, route=not sample)
    if sample:
        xmid, h2, lo, hi, glo, ghi = mo
        eids = jnp.arange(N_EXPERTS, dtype=jnp.int32)[None, :]
        gates = (jnp.where(lo[0][:, None] == eids, glo[0][:, None], 0.0)
                 + jnp.where(hi[0][:, None] == eids, ghi[0][:, None], 0.0))
        x_new = _moe(h2, gates, lw["wg"], lw["wu"], lw["wd"], xmid, mod[..., 5 * D_MODEL:],
                     lw["g_final"], tm=tm_moe, tiles_per_mod=1, expert0=lw["expert0"],
                     final_norm=final_norm)
    else:
        xext, cls, rank, counts = mo
        x_new = _moe_routed(xext, cls, rank, counts, lw["wg"], lw["wu"], lw["wd"],
                            mod[:, 0, 3 * D_MODEL:], lw["g_ffn"], lw["g_final"], expert0=lw["expert0"],
                            final_norm=final_norm)
    return x_new, hn_re, hn_im, v_rows


def _state_out(h, nseq):
    return h.reshape(GROUP_BLOCKS, nseq, GROUPS_PER_BLOCK, SSM_STATE).transpose(1, 0, 2, 3).reshape(
        nseq, SSM_GROUPS, SSM_STATE)


def _state_in(h, nblk, rh):
    nseq = h.shape[0]
    return h.reshape(nseq, GROUP_BLOCKS, STATE_BLOCK).transpose(1, 0, 2).reshape(
        GROUP_BLOCKS, nblk, rh, STATE_BLOCK)


def kernel(x_prompt, x_sample, c_prompt, c_sample, state_s5_re, state_s5_im, w_ada, b_ada, g_norm_mix, g_norm_ffn, w_in, gmlp_v_gain, gmlp_w_spatial, gmlp_b_spatial, s5_a_re, s5_a_im, s5_log_dt, s5_b_re, s5_b_im, s5_c_re, s5_c_im, s5_d, s5_w_glu, s5_b_glu, w_out, w_router, b_router, w_gate, w_up, w_down, g_final):
    nb, seq_len, _ = x_prompt.shape
    ns, dec_len, _ = x_sample.shape
    assert dec_len == S5_CHUNK and ns * dec_len == GMLP_CHUNK and nb + ns <= ADA_ROWS
    assert seq_len % GMLP_CHUNK == 0

    c_all = jnp.concatenate([c_prompt, c_sample, jnp.zeros((ADA_ROWS - nb - ns, D_MODEL), F32)], axis=0)
    mod_all = _ada(c_all, w_ada, b_ada)

    pos = jnp.arange(GMLP_CHUNK)
    causal = (pos[None, :] // CHUNK) <= (pos[:, None] // CHUNK)
    wr_hi = w_router.astype(BF16)
    wr_lo = (w_router - wr_hi.astype(F32)).astype(BF16)
    wr = jnp.pad(jnp.concatenate([wr_hi, wr_lo], axis=1), ((0, 0), (0, LANES - 2 * N_EXPERTS)))
    wr = jnp.concatenate([wr, wr], axis=0)
    b_r = b_router.reshape(N_EXPERTS, 1)
    g_fin = g_final.reshape(1, D_MODEL)
    eye_s = jnp.eye(ns, dtype=F32)

    xp = x_prompt.reshape(nb * seq_len, D_MODEL)
    xs = x_sample.reshape(ns * dec_len, D_MODEL)
    zeros_p = jnp.zeros((GROUP_BLOCKS, nb, 1, STATE_BLOCK), F32)
    wg_all = w_gate.astype(BF16).reshape(DEPTH * N_EXPERTS, D_MODEL, D_EXPERT)
    wu_all = w_up.astype(BF16).reshape(DEPTH * N_EXPERTS, D_MODEL, D_EXPERT)
    wd_all = w_down.astype(BF16).reshape(DEPTH * N_EXPERTS, D_EXPERT, D_MODEL)
    sp_re, sp_im, ss_re, ss_im, v_new = [], [], [], [], []
    for l in range(DEPTH):
        ws = jnp.where(causal[None], gmlp_w_spatial[l], 0.0)
        ws_sample = jnp.einsum("ab,hij->haibj", eye_s, ws[:, :dec_len, :dec_len]).reshape(
            GMLP_HEADS, GMLP_CHUNK, GMLP_CHUNK)
        bs = jnp.repeat(gmlp_b_spatial[l].T, GMLP_HEAD_DIM, axis=1)
        lw = dict(
            g_mix=g_norm_mix[l].reshape(1, D_MODEL), g_ffn=g_norm_ffn[l].reshape(1, D_MODEL),
            w_in=w_in[l].astype(BF16), v_gain=gmlp_v_gain[l].reshape(1, GMLP_WIDTH),
            ws=ws.astype(BF16), ws_sample=ws_sample.astype(BF16),
            bs=bs, bs_sample=jnp.tile(bs[:dec_len], (ns, 1)),
            prep=_s5_prep(s5_a_re[l], s5_a_im[l], s5_log_dt[l], s5_b_re[l], s5_b_im[l],
                          s5_c_re[l], s5_c_im[l]),
            d_tiled=jnp.tile(s5_d[l].reshape(GROUP_BLOCKS, 1, LANES), (1, 1, S5_CHUNK)),
            w_glu=s5_w_glu[l].astype(BF16), b_glu=s5_b_glu[l].reshape(1, SSM_WIDTH),
            w_out=w_out[l].astype(BF16), wr=wr, b_r=b_r,
            wg=wg_all, wu=wu_all, wd=wd_all, expert0=l * N_EXPERTS,
            g_final=g_fin,
        )
        last = l == DEPTH - 1
        mod_p = mod_all[l, :nb].reshape(nb, 1, 6 * D_MODEL)
        mod_s = jnp.repeat(mod_all[l, nb:nb + ns], dec_len, axis=0).reshape(1, ns * dec_len, 6 * D_MODEL)
        xp, hp_re, hp_im, _ = _layer(xp, mod_p, lw, zeros_p, zeros_p, seq_len=seq_len, nseq=nb,
                                     sample=False, final_norm=last)
        xs, hs_re, hs_im, vs = _layer(xs, mod_s, lw, _state_in(state_s5_re[l], 1, ns),
                                      _state_in(state_s5_im[l], 1, ns), seq_len=dec_len, nseq=ns,
                                      sample=True, final_norm=last)
        sp_re.append(_state_out(hp_re, nb))
        sp_im.append(_state_out(hp_im, nb))
        ss_re.append(_state_out(hs_re, ns))
        ss_im.append(_state_out(hs_im, ns))
        v_new.append(vs.reshape(ns, dec_len, GMLP_WIDTH))
    return (xp.reshape(nb, seq_len, D_MODEL), xs.reshape(ns, dec_len, D_MODEL),
            jnp.stack(sp_re), jnp.stack(sp_im), jnp.stack(ss_re), jnp.stack(ss_im), jnp.stack(v_new))
```

```python
import functools

import jax
import jax.numpy as jnp
from jax import lax
from jax.experimental import pallas as pl
from jax.experimental.pallas import tpu as pltpu

F32 = jnp.float32
BF16 = jnp.bfloat16

D_MODEL = 1024
DEPTH = 2
CHUNK = 64
GMLP_CHUNK = 128
GMLP_WIDTH = 512
GMLP_HEADS = 4
GMLP_HEAD_DIM = 128
SSM_WIDTH = 512
SSM_GROUP = 16
SSM_GROUPS = 32
SSM_STATE = 64
IN_WIDTH = 1536
N_EXPERTS = 16
EXPERTS_PER_GROUP = 4
N_EXPERT_GROUPS = 4
D_EXPERT = 512
EPS = 1e-6

LANES = 128
S5_CHUNK = 16
GROUP_BLOCKS = 4
GROUPS_PER_BLOCK = SSM_GROUPS // GROUP_BLOCKS
STATE_BLOCK = GROUPS_PER_BLOCK * SSM_STATE
S5_ROW = S5_CHUNK * LANES
ADA_ROWS = 16
PAIRS_PER_GROUP = 6
N_CLASSES = N_EXPERT_GROUPS * PAIRS_PER_GROUP
CLASS_ROWS = 32
_PAIRS = [(a, b) for a in range(EXPERTS_PER_GROUP) for b in range(a + 1, EXPERTS_PER_GROUP)]
CLASS_LO = [g * EXPERTS_PER_GROUP + a for g in range(N_EXPERT_GROUPS) for a, _ in _PAIRS]
CLASS_HI = [g * EXPERTS_PER_GROUP + b for g in range(N_EXPERT_GROUPS) for _, b in _PAIRS]
MOE_TILE = 256
MIX_TILE = 512
RUN_ALIGN = 8
SORT_ROWS = 768
XS_WIDTH = D_MODEL + LANES
VMEM_LIMIT = 56 * 1024 * 1024


def _cparams(*sem):
    return pltpu.CompilerParams(dimension_semantics=sem, vmem_limit_bytes=VMEM_LIMIT)


def _ada_kernel(c_ref, w_ref, b_ref, o_ref):
    c = c_ref[...]
    s = (c * jax.nn.sigmoid(c)).astype(BF16)
    o_ref[0] = jnp.dot(s, w_ref[0].astype(BF16), preferred_element_type=F32) + b_ref[0]


def _ada(c_all, w_ada, b_ada):
    nblk = 6
    return pl.pallas_call(
        _ada_kernel,
        grid=(DEPTH, nblk),
        in_specs=[
            pl.BlockSpec((ADA_ROWS, D_MODEL), lambda l, j: (0, 0)),
            pl.BlockSpec((1, D_MODEL, D_MODEL), lambda l, j: (l, 0, j)),
            pl.BlockSpec((1, 1, D_MODEL), lambda l, j: (l, 0, j)),
        ],
        out_specs=pl.BlockSpec((1, ADA_ROWS, D_MODEL), lambda l, j: (l, 0, j)),
        out_shape=jax.ShapeDtypeStruct((DEPTH, ADA_ROWS, 6 * D_MODEL), F32),
        compiler_params=_cparams("parallel", "parallel"),
        name="ada",
    )(c_all, w_ada, b_ada.reshape(DEPTH, 1, 6 * D_MODEL))


def _prep_kernel(are_ref, aim_ref, ldt_ref, bre_ref, bim_ref, cre_ref, cim_ref,
                 msre_ref, msim_ref, nt_ref, wrev_ref, a16re_ref, a16im_ref):
    a_re = are_ref[0]
    a_im = aim_ref[0]
    dt = jnp.exp(ldt_ref[0])
    rho = a_re * dt
    th = a_im * dt
    kk = jnp.minimum(lax.broadcasted_iota(jnp.int32, (24, STATE_BLOCK), 0), S5_CHUNK).astype(F32)
    mag = jnp.exp(kk * rho)
    pw_re = mag * jnp.cos(kk * th)
    pw_im = mag * jnp.sin(kk * th)

    lb_re = pw_re[1:2]
    lb_im = pw_im[1:2]
    num_re = lb_re - 1.0
    num_im = lb_im
    den = a_re * a_re + a_im * a_im
    coef_re = (num_re * a_re + num_im * a_im) / den
    coef_im = (num_im * a_re - num_re * a_im) / den
    b_re = bre_ref[0]
    b_im = bim_ref[0]
    bb_re = coef_re * b_re - coef_im * b_im
    bb_im = coef_re * b_im + coef_im * b_re

    rows = lax.broadcasted_iota(jnp.int32, (LANES, STATE_BLOCK), 0)
    cols = lax.broadcasted_iota(jnp.int32, (LANES, STATE_BLOCK), 1)
    same_group = (rows >> 4) == (cols >> 6)

    def blockdiag(x16):
        return jnp.where(same_group, jnp.concatenate([x16] * GROUPS_PER_BLOCK, axis=0), 0.0)

    for s in range(S5_CHUNK):
        k = S5_CHUNK - 1 - s
        p_re = pw_re[k:k + 1]
        p_im = pw_im[k:k + 1]
        msre_ref[0, s * LANES:(s + 1) * LANES, :] = blockdiag(p_re * bb_re - p_im * bb_im).astype(BF16)
        msim_ref[0, s * LANES:(s + 1) * LANES, :] = blockdiag(p_re * bb_im + p_im * bb_re).astype(BF16)

    bcat = jnp.concatenate([blockdiag(bb_re), blockdiag(bb_im)], axis=1)
    c_re = cre_ref[0]
    c_im = cim_ref[0]
    for k in range(S5_CHUNK + 1):
        p_re = pw_re[k:k + 1]
        p_im = pw_im[k:k + 1]
        cl = jnp.concatenate([blockdiag(c_re * p_re - c_im * p_im),
                              -blockdiag(c_re * p_im + c_im * p_re)], axis=1)
        if k >= 1:
            nt_ref[0, (k - 1) * LANES:k * LANES, :] = cl.astype(BF16)
        if k < S5_CHUNK:
            wl = lax.dot_general(bcat, cl, (((1,), (1,)), ((), ())),
                                 precision=lax.Precision.HIGHEST, preferred_element_type=F32)
            j = S5_CHUNK - 1 - k
            wrev_ref[0, j * LANES:(j + 1) * LANES, :LANES] = wl.astype(BF16)
            if j >= 1:
                wrev_ref[0, (j - 1) * LANES:j * LANES, LANES:] = wl.astype(BF16)
    wrev_ref[0, (S5_CHUNK - 1) * LANES:, LANES:] = jnp.zeros((LANES, LANES), BF16)

    a16re_ref[0] = pw_re[S5_CHUNK:S5_CHUNK + 1]
    a16im_ref[0] = pw_im[S5_CHUNK:S5_CHUNK + 1]


def _s5_prep(a_re, a_im, log_dt, b_re, b_im, c_re, c_im):
    nstate = SSM_GROUPS * SSM_STATE

    def lane_row(v):
        return v.reshape(GROUP_BLOCKS, 1, STATE_BLOCK)

    def rows16(v):
        return v.reshape(SSM_GROUP, GROUP_BLOCKS, STATE_BLOCK).transpose(1, 0, 2)

    ldt = jnp.repeat(log_dt, SSM_STATE).reshape(SSM_GROUPS, SSM_STATE)
    bt_re = rows16(b_re.transpose(2, 0, 1).reshape(SSM_GROUP, nstate))
    bt_im = rows16(b_im.transpose(2, 0, 1).reshape(SSM_GROUP, nstate))
    ct_re = rows16(c_re.transpose(1, 0, 2).reshape(SSM_GROUP, nstate))
    ct_im = rows16(c_im.transpose(1, 0, 2).reshape(SSM_GROUP, nstate))
    row_spec = pl.BlockSpec((1, 1, STATE_BLOCK), lambda g: (g, 0, 0))
    r16_spec = pl.BlockSpec((1, SSM_GROUP, STATE_BLOCK), lambda g: (g, 0, 0))
    return pl.pallas_call(
        _prep_kernel,
        grid=(GROUP_BLOCKS,),
        in_specs=[row_spec, row_spec, row_spec, r16_spec, r16_spec, r16_spec, r16_spec],
        out_specs=[
            pl.BlockSpec((1, S5_ROW, STATE_BLOCK), lambda g: (g, 0, 0)),
            pl.BlockSpec((1, S5_ROW, STATE_BLOCK), lambda g: (g, 0, 0)),
            pl.BlockSpec((1, S5_ROW, 2 * STATE_BLOCK), lambda g: (g, 0, 0)),
            pl.BlockSpec((1, S5_ROW, 2 * LANES), lambda g: (g, 0, 0)),
            row_spec, row_spec,
        ],
        out_shape=[
            jax.ShapeDtypeStruct((GROUP_BLOCKS, S5_ROW, STATE_BLOCK), BF16),
            jax.ShapeDtypeStruct((GROUP_BLOCKS, S5_ROW, STATE_BLOCK), BF16),
            jax.ShapeDtypeStruct((GROUP_BLOCKS, S5_ROW, 2 * STATE_BLOCK), BF16),
            jax.ShapeDtypeStruct((GROUP_BLOCKS, S5_ROW, 2 * LANES), BF16),
            jax.ShapeDtypeStruct((GROUP_BLOCKS, 1, STATE_BLOCK), F32),
            jax.ShapeDtypeStruct((GROUP_BLOCKS, 1, STATE_BLOCK), F32),
        ],
        compiler_params=_cparams("parallel"),
        name="s5_prep",
    )(lane_row(a_re), lane_row(a_im), lane_row(ldt), bt_re, bt_im, ct_re, ct_im)


def _rms(x, g):
    return x * lax.rsqrt(jnp.mean(x * x, axis=-1, keepdims=True) + EPS) * g


def _unsorted_ffn(zs_ref, prow_ref):
    tm = prow_ref.shape[1]
    zs = zs_ref[...].astype(BF16)
    pcol = jnp.concatenate([prow_ref[...].astype(F32), jnp.zeros((LANES - 1, tm), F32)], axis=0).T[:, :1]
    lanes = lax.broadcasted_iota(jnp.int32, (tm, SORT_ROWS), 1)
    pick = jnp.where(lanes == pcol.astype(jnp.int32), 1.0, 0.0).astype(BF16)
    return jnp.dot(pick, zs, preferred_element_type=F32)


def _mixer_in_kernel(x_ref, *refs, tm, fused, want_v):
    if fused:
        zs_ref, prow_ref, gf_ref = refs[:3]
        refs = refs[3:]
    mod_ref, g_ref, win_ref, vg_ref, ws_ref, bs_ref, a_ref, u_ref = refs[:8]
    refs = refs[8:]
    x = x_ref[...]
    if fused:
        x = x + gf_ref[0] * _unsorted_ffn(zs_ref, prow_ref)
        refs[0][...] = x
        refs = refs[1:]
    mod = mod_ref[0]
    shift = mod[:, :D_MODEL]
    scale = mod[:, D_MODEL:]
    h = _rms(x, g_ref[...]) * (1.0 + scale) + shift
    proj = jnp.dot(h.astype(BF16), win_ref[...], preferred_element_type=F32)
    z = jax.nn.gelu(proj[:, :2 * GMLP_WIDTH])
    u = z[:, :GMLP_WIDTH]
    v = z[:, GMLP_WIDTH:]
    vc = v - jnp.mean(v, axis=-1, keepdims=True)
    vn = vc * lax.rsqrt(jnp.mean(vc * vc, axis=-1, keepdims=True) + EPS) * vg_ref[...]
    if want_v:
        refs[0][...] = vn
    vb = vn.astype(BF16)
    bias = bs_ref[...]
    for c in range(tm // GMLP_CHUNK):
        r0 = c * GMLP_CHUNK
        for hh in range(GMLP_HEADS):
            l0 = hh * GMLP_HEAD_DIM
            mixed = jnp.dot(ws_ref[hh], vb[r0:r0 + GMLP_CHUNK, l0:l0 + GMLP_HEAD_DIM],
                            preferred_element_type=F32) + bias[:, l0:l0 + GMLP_HEAD_DIM]
            a_ref[r0:r0 + GMLP_CHUNK, l0:l0 + GMLP_HEAD_DIM] = (
                u[r0:r0 + GMLP_CHUNK, l0:l0 + GMLP_HEAD_DIM] * mixed).astype(BF16)
    for gb in range(GROUP_BLOCKS):
        l0 = 2 * GMLP_WIDTH + gb * LANES
        u_ref[gb] = proj[:, l0:l0 + LANES]


def _mixer_in(x2d, mod, g_mix, w_in, v_gain, ws, bs, *, tm, tiles_per_mod, want_v, ffn=None):
    t = x2d.shape[0]
    rmod = mod.shape[1]
    const2 = lambda i: (0, 0)
    tok = pl.BlockSpec((tm, D_MODEL), lambda i: (i, 0))
    in_specs = [tok]
    args = [x2d]
    if ffn is not None:
        in_specs += _ffn_specs(tm, rmod, tiles_per_mod)
        args += list(ffn)
    in_specs += [
        pl.BlockSpec((1, rmod, 2 * D_MODEL), lambda i: (i // tiles_per_mod, 0, 0)),
        pl.BlockSpec((1, D_MODEL), const2),
        pl.BlockSpec((D_MODEL, IN_WIDTH), const2),
        pl.BlockSpec((1, GMLP_WIDTH), const2),
        pl.BlockSpec((GMLP_HEADS, GMLP_CHUNK, GMLP_CHUNK), lambda i: (0, 0, 0)),
        pl.BlockSpec((GMLP_CHUNK, GMLP_WIDTH), const2),
    ]
    args += [mod, g_mix, w_in, v_gain, ws, bs]
    out_shape = [jax.ShapeDtypeStruct((t, GMLP_WIDTH), BF16),
                 jax.ShapeDtypeStruct((GROUP_BLOCKS, t, LANES), F32)]
    out_specs = [pl.BlockSpec((tm, GMLP_WIDTH), lambda i: (i, 0)),
                 pl.BlockSpec((GROUP_BLOCKS, tm, LANES), lambda i: (0, i, 0))]
    if ffn is not None:
        out_shape.append(jax.ShapeDtypeStruct((t, D_MODEL), F32))
        out_specs.append(tok)
    if want_v:
        out_shape.append(jax.ShapeDtypeStruct((t, GMLP_WIDTH), F32))
        out_specs.append(pl.BlockSpec((tm, GMLP_WIDTH), lambda i: (i, 0)))
    return pl.pallas_call(
        functools.partial(_mixer_in_kernel, tm=tm, fused=ffn is not None, want_v=want_v),
        grid=(t // tm,),
        in_specs=in_specs,
        out_specs=out_specs,
        out_shape=out_shape,
        compiler_params=_cparams("parallel"),
        name="mixer_in",
    )(*args)


def _ffn_specs(tm, rmod, tiles_per_mod):
    return [
        pl.BlockSpec((SORT_ROWS, D_MODEL), lambda i: (i, 0)),
        pl.BlockSpec((1, tm), lambda i: (0, i)),
        pl.BlockSpec((1, rmod, D_MODEL), lambda i: (i // tiles_per_mod, 0, 0)),
    ]


def _final_kernel(x_ref, zs_ref, prow_ref, gf_ref, g_ref, o_ref):
    x = x_ref[...] + gf_ref[0] * _unsorted_ffn(zs_ref, prow_ref)
    o_ref[...] = _rms(x, g_ref[...])


def _final(xmid, ffn, g_final, *, tm, tiles_per_mod):
    t = xmid.shape[0]
    rmod = ffn[2].shape[1]
    tok = pl.BlockSpec((tm, D_MODEL), lambda i: (i, 0))
    return pl.pallas_call(
        _final_kernel,
        grid=(t // tm,),
        in_specs=[tok] + _ffn_specs(tm, rmod, tiles_per_mod) + [pl.BlockSpec((1, D_MODEL), lambda i: (0, 0))],
        out_specs=tok,
        out_shape=jax.ShapeDtypeStruct((t, D_MODEL), F32),
        compiler_params=_cparams("parallel"),
        name="final_norm",
    )(xmid, *ffn, g_final)


def _s5_kernel(u_ref, msre_ref, msim_ref, nt_ref, wrev_ref, a16re_ref, a16im_ref, d_ref,
               h0re_ref, h0im_ref, y_ref, hnre_ref, hnim_ref, *scratch, scan):
    rows = u_ref.shape[1] // S5_CHUNK
    u = jnp.concatenate([u_ref[0, pl.ds(s, rows, stride=S5_CHUNK), :] for s in range(S5_CHUNK)], axis=1)
    ub = u.astype(BF16)
    s_re = jnp.dot(ub, msre_ref[0], preferred_element_type=F32)
    s_im = jnp.dot(ub, msim_ref[0], preferred_element_type=F32)
    a_re = a16re_ref[0]
    a_im = a16im_ref[0]
    h0_re = h0re_ref[0, 0]
    h0_im = h0im_ref[0, 0]
    if scan:
        sre_scr, sim_scr, hre_scr, him_scr = scratch
        sre_scr[...] = s_re
        sim_scr[...] = s_im

        def body(r, carry):
            hr, hi = carry
            hre_scr[pl.ds(r, 1), :] = hr
            him_scr[pl.ds(r, 1), :] = hi
            sr = sre_scr[pl.ds(r, 1), :]
            si = sim_scr[pl.ds(r, 1), :]
            return (a_re * hr - a_im * hi + sr, a_re * hi + a_im * hr + si)

        hn_re, hn_im = lax.fori_loop(0, rows, body, (h0_re, h0_im), unroll=8)
        h_re = hre_scr[...]
        h_im = him_scr[...]
    else:
        h_re = h0_re
        h_im = h0_im
        hn_re = a_re * h_re - a_im * h_im + s_re
        hn_im = a_re * h_im + a_im * h_re + s_im
    hnre_ref[0, 0] = hn_re
    hnim_ref[0, 0] = hn_im
    hcat = jnp.concatenate([h_re, h_im], axis=1).astype(BF16)
    inter = lax.dot_general(hcat, nt_ref[0], (((1,), (1,)), ((), ())),
                            preferred_element_type=F32)
    d = d_ref[0]
    for t in range(0, S5_CHUNK, 2):
        k0 = (S5_CHUNK - 2 - t) * LANES
        pair = jnp.dot(ub[:, :(t + 2) * LANES], wrev_ref[0, k0:, :], preferred_element_type=F32)
        for step, intra in ((t + 1, pair[:, :LANES]), (t, pair[:, LANES:])):
            sl = slice(step * LANES, (step + 1) * LANES)
            y_ref[0, pl.ds(step, rows, stride=S5_CHUNK), :] = intra + inter[:, sl] + d[:, sl] * u[:, sl]


def _s5(u4, prep, d_tiled, h0_re, h0_im, *, rows, nblk, scan):
    ms_re, ms_im, nt, wrev, a16_re, a16_im = prep
    rh = h0_re.shape[2]
    wspec = lambda shape: pl.BlockSpec((1,) + shape, lambda g, b: (g, 0, 0))
    hspec = pl.BlockSpec((1, 1, rh, STATE_BLOCK), lambda g, b: (g, b, 0, 0))
    scratch = [pltpu.VMEM((rows, STATE_BLOCK), F32)] * 4 if scan else []
    return pl.pallas_call(
        functools.partial(_s5_kernel, scan=scan),
        grid=(GROUP_BLOCKS, nblk),
        in_specs=[
            pl.BlockSpec((1, rows * S5_CHUNK, LANES), lambda g, b: (g, b, 0)),
            wspec((S5_ROW, STATE_BLOCK)), wspec((S5_ROW, STATE_BLOCK)),
            wspec((S5_ROW, 2 * STATE_BLOCK)), wspec((S5_ROW, 2 * LANES)),
            wspec((1, STATE_BLOCK)), wspec((1, STATE_BLOCK)), wspec((1, S5_ROW)),
            hspec, hspec,
        ],
        out_specs=[pl.BlockSpec((1, rows * S5_CHUNK, LANES), lambda g, b: (g, b, 0)), hspec, hspec],
        out_shape=[
            jax.ShapeDtypeStruct(u4.shape, F32),
            jax.ShapeDtypeStruct(h0_re.shape, F32),
            jax.ShapeDtypeStruct(h0_re.shape, F32),
        ],
        scratch_shapes=scratch,
        compiler_params=_cparams("parallel", "parallel"),
        name="s5",
    )(u4, ms_re, ms_im, nt, wrev, a16_re, a16_im, d_tiled, h0_re, h0_im)


def _split_bf16(x):
    hi = x.astype(BF16)
    return hi, (x - hi.astype(F32)).astype(BF16)


def _top2_of4(a):
    m1 = jnp.maximum(jnp.maximum(a[0], a[1]), jnp.maximum(a[2], a[3]))
    i1 = jnp.where(a[0] == m1, 0, jnp.where(a[1] == m1, 1, jnp.where(a[2] == m1, 2, 3)))
    b = [jnp.where(i1 == j, -jnp.inf, a[j]) for j in range(4)]
    m2 = jnp.maximum(jnp.maximum(b[0], b[1]), jnp.maximum(b[2], b[3]))
    i2 = jnp.where(b[0] == m2, 0, jnp.where(b[1] == m2, 1, jnp.where(b[2] == m2, 2, 3)))
    return m1, i1, m2, i2


def _route_rows(h2, wr_ref, br_ref):
    h_hi, h_lo = _split_bf16(h2)
    parts = jnp.dot(jnp.concatenate([h_hi, h_lo], axis=1), wr_ref[...], preferred_element_type=F32)
    pt = parts.T
    lt = pt[:N_EXPERTS] + pt[N_EXPERTS:2 * N_EXPERTS] + br_ref[...]
    rows = [lt[e:e + 1] for e in range(N_EXPERTS)]
    mx = functools.reduce(jnp.maximum, rows)
    ex = [jnp.exp(r - mx) for r in rows]
    tot = functools.reduce(lambda p, q: p + q, ex)
    scores = [e / tot for e in ex]
    best = None
    for g in range(N_EXPERT_GROUPS):
        m1, i1, m2, i2 = _top2_of4(scores[g * EXPERTS_PER_GROUP:(g + 1) * EXPERTS_PER_GROUP])
        cand = (m1 + m2, m1, i1 + g * EXPERTS_PER_GROUP, m2, i2 + g * EXPERTS_PER_GROUP)
        if best is None:
            best = cand
        else:
            better = cand[0] > best[0]
            best = tuple(jnp.where(better, c, b) for c, b in zip(cand, best))
    _, v1, e1, v2, e2 = best
    den = v1 + v2
    w1 = v1 / den
    w2 = v2 / den
    first_lo = e1 < e2
    return (jnp.where(first_lo, e1, e2), jnp.where(first_lo, e2, e1),
            jnp.where(first_lo, w1, w2), jnp.where(first_lo, w2, w1))


def _mixer_out_kernel(x_ref, a_ref, y_ref, mod_ref, g_ref, wglu_ref, bglu_ref, wout_ref,
                      wr_ref, br_ref, *rest, route, tiles_per_mod):
    x = x_ref[...]
    mod = mod_ref[0]
    gate_m = mod[:, :D_MODEL]
    shift_f = mod[:, D_MODEL:2 * D_MODEL]
    scale_f = mod[:, 2 * D_MODEL:]
    ys = jax.nn.gelu(jnp.concatenate([y_ref[gb] for gb in range(GROUP_BLOCKS)], axis=1))
    glu = jnp.dot(ys.astype(BF16), wglu_ref[...], preferred_element_type=F32) + bglu_ref[...]
    b_out = ys * jax.nn.sigmoid(glu)
    mixed = jnp.concatenate([a_ref[...], b_out.astype(BF16)], axis=1)
    xmid = x + gate_m * jnp.dot(mixed, wout_ref[...], preferred_element_type=F32)
    h2 = _rms(xmid, g_ref[...]) * (1.0 + scale_f) + shift_f
    lo, hi, glo, ghi = _route_rows(h2, wr_ref, br_ref)
    if not route:
        xmid_ref, h2_ref, lo_ref, hi_ref, glo_ref, ghi_ref = rest
        xmid_ref[...] = xmid
        h2_ref[...] = h2
        lo_ref[...] = lo
        hi_ref[...] = hi
        glo_ref[...] = glo
        ghi_ref[...] = ghi
        return

    tri_ref, ltri_ref, xmid_ref, xs_ref, prow_ref, cnt_ref = rest
    tm = x.shape[0]
    xmid_ref[...] = xmid
    a = lo & (EXPERTS_PER_GROUP - 1)
    b = hi & (EXPERTS_PER_GROUP - 1)
    pair = jnp.where(a == 0, 0, jnp.where(a == 1, 3, 5)) + (b - a - 1)
    cls = (lo >> 2) * PAIRS_PER_GROUP + pair
    onehot = lax.broadcasted_iota(jnp.int32, (CLASS_ROWS, tm), 0) == cls
    prefix = jnp.dot(jnp.where(onehot, 1.0, 0.0).astype(BF16), tri_ref[...],
                     preferred_element_type=F32)
    total = prefix[:, tm - 1:tm]
    cnt_ref[0] = jnp.broadcast_to(total, (CLASS_ROWS, LANES))
    groups = jnp.floor((total + (RUN_ALIGN - 1)) * (1.0 / RUN_ALIGN))
    before = jnp.dot(ltri_ref[...], jnp.broadcast_to(groups, (CLASS_ROWS, LANES)).astype(BF16),
                     preferred_element_type=F32)[:, :1] * RUN_ALIGN
    prow = jnp.sum(jnp.where(onehot, before + prefix - 1.0, 0.0), axis=0, keepdims=True).astype(jnp.int32)
    prow_ref[...] = prow
    pick = jnp.where(lax.broadcasted_iota(jnp.int32, (SORT_ROWS, tm), 0) == prow, 1.0, 0.0).astype(BF16)
    glo_hi, glo_lo = _split_bf16(glo)
    ghi_hi, ghi_lo = _split_bf16(ghi)
    gates = jnp.concatenate([glo_hi.astype(F32), glo_lo.astype(F32), ghi_hi.astype(F32), ghi_lo.astype(F32),
                             jnp.zeros((LANES - 4, tm), F32)], axis=0).T
    payload = jnp.concatenate([h2.astype(BF16), gates.astype(BF16)], axis=1)
    xs_ref[...] = jnp.dot(pick, payload, preferred_element_type=F32)


def _mixer_out(x2d, a_out, y4, mod, g_ffn, w_glu, b_glu, w_out, wr, b_r,
               *, tm, tiles_per_mod, route):
    t = x2d.shape[0]
    rmod = mod.shape[1]
    const2 = lambda i: (0, 0)
    tok = pl.BlockSpec((tm, D_MODEL), lambda i: (i, 0))
    row = pl.BlockSpec((1, tm), lambda i: (0, i))
    in_specs = [
        tok,
        pl.BlockSpec((tm, GMLP_WIDTH), lambda i: (i, 0)),
        pl.BlockSpec((GROUP_BLOCKS, tm, LANES), lambda i: (0, i, 0)),
        pl.BlockSpec((1, rmod, 3 * D_MODEL), lambda i: (i // tiles_per_mod, 0, 0)),
        pl.BlockSpec((1, D_MODEL), const2),
        pl.BlockSpec((SSM_WIDTH, SSM_WIDTH), const2),
        pl.BlockSpec((1, SSM_WIDTH), const2),
        pl.BlockSpec((D_MODEL, D_MODEL), const2),
        pl.BlockSpec((2 * D_MODEL, LANES), const2),
        pl.BlockSpec((N_EXPERTS, 1), const2),
    ]
    args = [x2d, a_out, y4, mod, g_ffn, w_glu, b_glu, w_out, wr, b_r]
    if route:
        ids = jnp.arange(tm)
        cids = jnp.arange(CLASS_ROWS)
        args += [(ids[:, None] <= ids[None, :]).astype(BF16), (cids[None, :] < cids[:, None]).astype(BF16)]
        in_specs += [pl.BlockSpec((tm, tm), const2), pl.BlockSpec((CLASS_ROWS, CLASS_ROWS), const2)]
        out_specs = [tok, pl.BlockSpec((SORT_ROWS, XS_WIDTH), lambda i: (i, 0)), row,
                     pl.BlockSpec((1, CLASS_ROWS, LANES), lambda i: (i, 0, 0))]
        out_shape = [jax.ShapeDtypeStruct((t, D_MODEL), F32),
                     jax.ShapeDtypeStruct((t // tm * SORT_ROWS, XS_WIDTH), F32),
                     jax.ShapeDtypeStruct((1, t), jnp.int32),
                     jax.ShapeDtypeStruct((t // tm, CLASS_ROWS, LANES), F32)]
        scratch = []
    else:
        out_specs = [tok, tok, row, row, row, row]
        out_shape = [jax.ShapeDtypeStruct((t, D_MODEL), F32),
                     jax.ShapeDtypeStruct((t, D_MODEL), F32),
                     jax.ShapeDtypeStruct((1, t), jnp.int32),
                     jax.ShapeDtypeStruct((1, t), jnp.int32),
                     jax.ShapeDtypeStruct((1, t), F32),
                     jax.ShapeDtypeStruct((1, t), F32)]
        scratch = []
    return pl.pallas_call(
        functools.partial(_mixer_out_kernel, route=route, tiles_per_mod=tiles_per_mod),
        grid=(t // tm,),
        in_specs=in_specs,
        out_specs=out_specs,
        out_shape=out_shape,
        scratch_shapes=scratch,
        compiler_params=_cparams("parallel"),
        name="mixer_out",
    )(*args)


def _moe_runs_kernel(tlo_ref, thi_ref, tval_ref, psrc_ref, pdst_ref, plen_ref, used_ref,
                     xs_hbm, wgl_ref, wul_ref, wdl_ref, wgh_ref, wuh_ref, wdh_ref,
                     zs_hbm, buf_ref, obuf_ref, sem_in, sem_out, *, nsrc):
    i = pl.program_id(0)
    nt = pl.num_programs(0)
    slot = i % 2
    other = 1 - slot
    nv = tval_ref[i]
    nv_next = jnp.where(i + 1 < nt, tval_ref[jnp.minimum(i + 1, nt - 1)], 0)
    nv_prev = jnp.where(i >= 1, tval_ref[jnp.maximum(i - 1, 0)], 0)
    nv_prev2 = jnp.where(i >= 2, tval_ref[jnp.maximum(i - 2, 0)], 0)

    def rows8(v):
        return pl.multiple_of(v, RUN_ALIGN)

    def for_pieces(tile, fn):
        def body(k, c):
            n = plen_ref[tile * nsrc + k]

            @pl.when(n > 0)
            def _():
                fn(rows8(psrc_ref[tile * nsrc + k]), rows8(pdst_ref[tile * nsrc + k]), rows8(n))

            return c

        lax.fori_loop(0, nsrc, body, 0)

    def gather_start(tile, sl):
        for_pieces(tile, lambda src, dst, n: pltpu.make_async_copy(
            xs_hbm.at[pl.ds(src, n), :], buf_ref.at[sl, pl.ds(dst, n), :], sem_in.at[sl]).start())

    def gather_wait(sl, n):
        pltpu.make_async_copy(xs_hbm.at[pl.ds(0, rows8(n)), :], buf_ref.at[sl, pl.ds(0, rows8(n)), :],
                              sem_in.at[sl]).wait()

    def scatter_start(tile, sl):
        for_pieces(tile, lambda src, dst, n: pltpu.make_async_copy(
            obuf_ref.at[sl, pl.ds(dst, n), :], zs_hbm.at[pl.ds(src, n), :], sem_out.at[sl]).start())

    def scatter_wait(sl, n):
        pltpu.make_async_copy(obuf_ref.at[sl, pl.ds(0, rows8(n)), :], zs_hbm.at[pl.ds(0, rows8(n)), :],
                              sem_out.at[sl]).wait()

    @pl.when(i == 0)
    def _():
        buf_ref[...] = jnp.zeros_like(buf_ref)
        gather_start(0, 0)
        obuf_ref[1] = jnp.zeros((MOE_TILE, D_MODEL), F32)

        def tail(k, start):
            used = rows8(used_ref[k])
            n = rows8(SORT_ROWS - used)
            copy = pltpu.make_async_copy(obuf_ref.at[1, pl.ds(0, n), :],
                                         zs_hbm.at[pl.ds(rows8(k * SORT_ROWS + used), n), :], sem_out.at[1])

            @pl.when(n > 0)
            def _():
                copy.start() if start else copy.wait()

        for start in (True, False):
            lax.fori_loop(0, nsrc, lambda k, c: (tail(k, start), c)[1], 0)

    @pl.when(nv_next > 0)
    def _():
        gather_start(i + 1, other)

    @pl.when(nv_prev2 > 0)
    def _():
        scatter_wait(slot, nv_prev2)

    @pl.when(nv > 0)
    def _():
        gather_wait(slot, nv)
        buf = buf_ref[slot]
        h = buf[:, :D_MODEL].astype(BF16)
        glo = buf[:, D_MODEL:D_MODEL + 1] + buf[:, D_MODEL + 1:D_MODEL + 2]
        ghi = buf[:, D_MODEL + 2:D_MODEL + 3] + buf[:, D_MODEL + 3:D_MODEL + 4]
        ffn = None
        for wg, wu, wd, gate in ((wgl_ref, wul_ref, wdl_ref, glo), (wgh_ref, wuh_ref, wdh_ref, ghi)):
            he = (jax.nn.silu(jnp.dot(h, wg[0], preferred_element_type=F32))
                  * jnp.dot(h, wu[0], preferred_element_type=F32))
            y = gate * jnp.dot(he.astype(BF16), wd[0], preferred_element_type=F32)
            ffn = y if ffn is None else ffn + y
        obuf_ref[slot] = ffn.astype(BF16).astype(F32)
        scatter_start(i, slot)

    @pl.when(i == nt - 1)
    def _():
        @pl.when(nv_prev > 0)
        def _():
            scatter_wait(other, nv_prev)

        @pl.when(nv > 0)
        def _():
            scatter_wait(slot, nv)


def _moe_runs(xs, counts, wg, wu, wd, *, tm, expert0):
    nmix = counts.shape[0]
    max_rows = nmix * min(SORT_ROWS, tm + N_CLASSES * (RUN_ALIGN - 1))
    ntiles = -(-max_rows // MOE_TILE) + N_CLASSES
    n_kc = counts[:, :N_CLASSES, 0].astype(jnp.int32)
    len_kc = (n_kc + RUN_ALIGN - 1) // RUN_ALIGN * RUN_ALIGN
    off_kc = jnp.cumsum(len_kc, axis=1) - len_kc
    used_k = jnp.sum(len_kc, axis=1)
    start_kc = jnp.cumsum(len_kc, axis=0) - len_kc
    region_c = jnp.sum(len_kc, axis=0)
    tiles_c = (region_c + MOE_TILE - 1) // MOE_TILE
    tile_end = jnp.cumsum(tiles_c)
    tile_start = tile_end - tiles_c
    used = tile_end[-1]
    tid = jnp.arange(ntiles, dtype=jnp.int32)
    tcls = jnp.sum(tile_end[None, :] <= jnp.minimum(tid, used - 1)[:, None], axis=1).astype(jnp.int32)
    lo_row = (tid - tile_start[tcls]) * MOE_TILE
    tval = jnp.where(tid < used, jnp.clip(region_c[tcls] - lo_row, 0, MOE_TILE), 0)
    run_lo = start_kc[:, tcls].T
    run_hi = run_lo + len_kc[:, tcls].T
    piece_lo = jnp.maximum(run_lo, lo_row[:, None])
    piece_hi = jnp.minimum(run_hi, (lo_row + tval)[:, None])
    plen = jnp.maximum(piece_hi - piece_lo, 0)
    psrc = jnp.arange(nmix, dtype=jnp.int32)[None, :] * SORT_ROWS + off_kc[:, tcls].T + (piece_lo - run_lo)
    pdst = piece_lo - lo_row[:, None]
    flat = lambda v: jnp.where(plen > 0, v, 0).reshape(-1).astype(jnp.int32)
    tlo = jnp.asarray(CLASS_LO, jnp.int32)[tcls]
    thi = jnp.asarray(CLASS_HI, jnp.int32)[tcls]
    nprefetch = 7
    wspec_lo = lambda shape: pl.BlockSpec(shape, lambda i, tlo, *_: (expert0 + tlo[i], 0, 0))
    wspec_hi = lambda shape: pl.BlockSpec(shape, lambda i, tlo, thi, *_: (expert0 + thi[i], 0, 0))
    up = (1, D_MODEL, D_EXPERT)
    down = (1, D_EXPERT, D_MODEL)
    zs = pl.pallas_call(
        functools.partial(_moe_runs_kernel, nsrc=nmix),
        grid_spec=pltpu.PrefetchScalarGridSpec(
            num_scalar_prefetch=nprefetch,
            grid=(ntiles,),
            in_specs=[
                pl.BlockSpec(memory_space=pl.ANY),
                wspec_lo(up), wspec_lo(up), wspec_lo(down),
                wspec_hi(up), wspec_hi(up), wspec_hi(down),
            ],
            out_specs=pl.BlockSpec(memory_space=pl.ANY),
            scratch_shapes=[
                pltpu.VMEM((2, MOE_TILE, XS_WIDTH), F32),
                pltpu.VMEM((2, MOE_TILE, D_MODEL), F32),
                pltpu.SemaphoreType.DMA((2,)),
                pltpu.SemaphoreType.DMA((2,)),
            ],
        ),
        out_shape=jax.ShapeDtypeStruct((nmix * SORT_ROWS, D_MODEL), F32),
        compiler_params=_cparams("arbitrary"),
        name="moe_runs",
    )(tlo, thi, tval.astype(jnp.int32), flat(psrc), flat(pdst), flat(plen), used_k.astype(jnp.int32),
      xs, wg, wu, wd, wg, wu, wd)
    return zs


def _moe_kernel(h2_ref, gates_ref, wg_ref, wu_ref, wd_ref, xmid_ref, mod_ref, gfin_ref,
                o_ref, acc_ref, *, final_norm):
    e = pl.program_id(1)

    @pl.when(e == 0)
    def _():
        acc_ref[...] = jnp.zeros_like(acc_ref)

    h = h2_ref[...].astype(BF16)
    he = (jax.nn.silu(jnp.dot(h, wg_ref[0], preferred_element_type=F32))
          * jnp.dot(h, wu_ref[0], preferred_element_type=F32))
    y = jnp.dot(he.astype(BF16), wd_ref[0], preferred_element_type=F32)
    gates = gates_ref[...]
    lane = lax.broadcasted_iota(jnp.int32, gates.shape, 1)
    gcol = jnp.sum(jnp.where(lane == e, gates, 0.0), axis=1, keepdims=True)
    acc_ref[...] += gcol * y

    @pl.when(e == N_EXPERTS - 1)
    def _():
        x = xmid_ref[...] + mod_ref[0] * acc_ref[...]
        if final_norm:
            x = _rms(x, gfin_ref[...])
        o_ref[...] = x


def _moe(h2, gates, wg, wu, wd, xmid, mod, g_final, *, tm, tiles_per_mod, expert0, final_norm):
    t = h2.shape[0]
    rmod = mod.shape[1]
    tok = pl.BlockSpec((tm, D_MODEL), lambda i, e: (i, 0))
    return pl.pallas_call(
        functools.partial(_moe_kernel, final_norm=final_norm),
        grid=(t // tm, N_EXPERTS),
        in_specs=[
            tok,
            pl.BlockSpec((tm, N_EXPERTS), lambda i, e: (i, 0)),
            pl.BlockSpec((1, D_MODEL, D_EXPERT), lambda i, e: (expert0 + e, 0, 0)),
            pl.BlockSpec((1, D_MODEL, D_EXPERT), lambda i, e: (expert0 + e, 0, 0)),
            pl.BlockSpec((1, D_EXPERT, D_MODEL), lambda i, e: (expert0 + e, 0, 0)),
            tok,
            pl.BlockSpec((1, rmod, D_MODEL), lambda i, e: (i // tiles_per_mod, 0, 0)),
            pl.BlockSpec((1, D_MODEL), lambda i, e: (0, 0)),
        ],
        out_specs=tok,
        out_shape=jax.ShapeDtypeStruct((t, D_MODEL), F32),
        scratch_shapes=[pltpu.VMEM((tm, D_MODEL), F32)],
        compiler_params=_cparams("parallel", "arbitrary"),
        name="moe",
    )(h2, gates, wg, wu, wd, xmid, mod, g_final)


def _layer(x2d, mod, lw, h0_re, h0_im, *, seq_len, nseq, sample, final_norm, ffn=None):
    t = x2d.shape[0]
    if sample:
        tm, tiles_per_mod, tm_moe = t, 1, t
        s5_rows, s5_blocks = nseq, 1
    else:
        tm = min(MIX_TILE, seq_len)
        tiles_per_mod = seq_len // tm
        s5_rows, s5_blocks = seq_len // S5_CHUNK, nseq
    outs = _mixer_in(x2d, mod[..., :2 * D_MODEL], lw["g_mix"], lw["w_in"], lw["v_gain"],
                     lw["ws_sample"] if sample else lw["ws"], lw["bs_sample"] if sample else lw["bs"],
                     tm=tm, tiles_per_mod=tiles_per_mod, want_v=sample, ffn=ffn)
    a_out, u4 = outs[0], outs[1]
    if ffn is not None:
        x2d = outs[2]
    v_rows = outs[-1] if sample else None
    y4, hn_re, hn_im = _s5(u4, lw["prep"], lw["d_tiled"],
                           h0_re, h0_im, rows=s5_rows, nblk=s5_blocks, scan=not sample)
    mo = _mixer_out(x2d, a_out, y4, mod[..., 2 * D_MODEL:5 * D_MODEL],
                    lw["g_ffn"], lw["w_glu"], lw["b_glu"], lw["w_out"], lw["wr"], lw["b_r"],
                    tm=tm, tiles_per_mod=tiles_per_mod, route=not sample)
    if sample:
        xmid, h2, lo, hi, glo, ghi = mo
        eids = jnp.arange(N_EXPERTS, dtype=jnp.int32)[None, :]
        gates = (jnp.where(lo[0][:, None] == eids, glo[0][:, None], 0.0)
                 + jnp.where(hi[0][:, None] == eids, ghi[0][:, None], 0.0))
        x_new = _moe(h2, gates, lw["wg"], lw["wu"], lw["wd"], xmid, mod[..., 5 * D_MODEL:],
                     lw["g_final"], tm=tm_moe, tiles_per_mod=1, expert0=lw["expert0"],
                     final_norm=final_norm)
        return x_new, hn_re, hn_im, v_rows
    xmid, xs, prow, counts = mo
    zs = _moe_runs(xs, counts, lw["wg"], lw["wu"], lw["wd"], tm=tm, expert0=lw["expert0"])
    return (xmid, (zs, prow, mod[..., 5 * D_MODEL:])), hn_re, hn_im, None


def _state_out(h, nseq):
    return h.reshape(GROUP_BLOCKS, nseq, GROUPS_PER_BLOCK, SSM_STATE).transpose(1, 0, 2, 3).reshape(
        nseq, SSM_GROUPS, SSM_STATE)


def _state_in(h, nblk, rh):
    nseq = h.shape[0]
    return h.reshape(nseq, GROUP_BLOCKS, STATE_BLOCK).transpose(1, 0, 2).reshape(
        GROUP_BLOCKS, nblk, rh, STATE_BLOCK)


def kernel(x_prompt, x_sample, c_prompt, c_sample, state_s5_re, state_s5_im, w_ada, b_ada, g_norm_mix, g_norm_ffn, w_in, gmlp_v_gain, gmlp_w_spatial, gmlp_b_spatial, s5_a_re, s5_a_im, s5_log_dt, s5_b_re, s5_b_im, s5_c_re, s5_c_im, s5_d, s5_w_glu, s5_b_glu, w_out, w_router, b_router, w_gate, w_up, w_down, g_final):
    nb, seq_len, _ = x_prompt.shape
    ns, dec_len, _ = x_sample.shape
    assert dec_len == S5_CHUNK and ns * dec_len == GMLP_CHUNK and nb + ns <= ADA_ROWS
    assert seq_len % GMLP_CHUNK == 0 and MIX_TILE + N_CLASSES * (RUN_ALIGN - 1) <= SORT_ROWS

    c_all = jnp.concatenate([c_prompt, c_sample, jnp.zeros((ADA_ROWS - nb - ns, D_MODEL), F32)], axis=0)
    mod_all = _ada(c_all, w_ada, b_ada)

    pos = jnp.arange(GMLP_CHUNK)
    causal = (pos[None, :] // CHUNK) <= (pos[:, None] // CHUNK)
    wr_hi = w_router.astype(BF16)
    wr_lo = (w_router - wr_hi.astype(F32)).astype(BF16)
    wr = jnp.pad(jnp.concatenate([wr_hi, wr_lo], axis=1), ((0, 0), (0, LANES - 2 * N_EXPERTS)))
    wr = jnp.concatenate([wr, wr], axis=0)
    b_r = b_router.reshape(N_EXPERTS, 1)
    g_fin = g_final.reshape(1, D_MODEL)
    eye_s = jnp.eye(ns, dtype=F32)

    xp = x_prompt.reshape(nb * seq_len, D_MODEL)
    xs = x_sample.reshape(ns * dec_len, D_MODEL)
    zeros_p = jnp.zeros((GROUP_BLOCKS, nb, 1, STATE_BLOCK), F32)
    wg_all = w_gate.astype(BF16).reshape(DEPTH * N_EXPERTS, D_MODEL, D_EXPERT)
    wu_all = w_up.astype(BF16).reshape(DEPTH * N_EXPERTS, D_MODEL, D_EXPERT)
    wd_all = w_down.astype(BF16).reshape(DEPTH * N_EXPERTS, D_EXPERT, D_MODEL)
    sp_re, sp_im, ss_re, ss_im, v_new = [], [], [], [], []
    ffn_p = None
    for l in range(DEPTH):
        ws = jnp.where(causal[None], gmlp_w_spatial[l], 0.0)
        ws_sample = jnp.einsum("ab,hij->haibj", eye_s, ws[:, :dec_len, :dec_len]).reshape(
            GMLP_HEADS, GMLP_CHUNK, GMLP_CHUNK)
        bs = jnp.repeat(gmlp_b_spatial[l].T, GMLP_HEAD_DIM, axis=1)
        lw = dict(
            g_mix=g_norm_mix[l].reshape(1, D_MODEL), g_ffn=g_norm_ffn[l].reshape(1, D_MODEL),
            w_in=w_in[l].astype(BF16), v_gain=gmlp_v_gain[l].reshape(1, GMLP_WIDTH),
            ws=ws.astype(BF16), ws_sample=ws_sample.astype(BF16),
            bs=bs, bs_sample=jnp.tile(bs[:dec_len], (ns, 1)),
            prep=_s5_prep(s5_a_re[l], s5_a_im[l], s5_log_dt[l], s5_b_re[l], s5_b_im[l],
                          s5_c_re[l], s5_c_im[l]),
            d_tiled=jnp.tile(s5_d[l].reshape(GROUP_BLOCKS, 1, LANES), (1, 1, S5_CHUNK)),
            w_glu=s5_w_glu[l].astype(BF16), b_glu=s5_b_glu[l].reshape(1, SSM_WIDTH),
            w_out=w_out[l].astype(BF16), wr=wr, b_r=b_r,
            wg=wg_all, wu=wu_all, wd=wd_all, expert0=l * N_EXPERTS,
            g_final=g_fin,
        )
        last = l == DEPTH - 1
        mod_p = mod_all[l, :nb].reshape(nb, 1, 6 * D_MODEL)
        mod_s = jnp.repeat(mod_all[l, nb:nb + ns], dec_len, axis=0).reshape(1, ns * dec_len, 6 * D_MODEL)
        (xp, ffn_p), hp_re, hp_im, _ = _layer(xp, mod_p, lw, zeros_p, zeros_p, seq_len=seq_len, nseq=nb,
                                              sample=False, final_norm=last, ffn=ffn_p)
        xs, hs_re, hs_im, vs = _layer(xs, mod_s, lw, _state_in(state_s5_re[l], 1, ns),
                                      _state_in(state_s5_im[l], 1, ns), seq_len=dec_len, nseq=ns,
                                      sample=True, final_norm=last)
        sp_re.append(_state_out(hp_re, nb))
        sp_im.append(_state_out(hp_im, nb))
        ss_re.append(_state_out(hs_re, ns))
        ss_im.append(_state_out(hs_im, ns))
        v_new.append(vs.reshape(ns, dec_len, GMLP_WIDTH))
    tm = min(MIX_TILE, seq_len)
    yp = _final(xp, ffn_p, g_fin, tm=tm, tiles_per_mod=seq_len // tm)
    return (yp.reshape(nb, seq_len, D_MODEL), xs.reshape(ns, dec_len, D_MODEL),
            jnp.stack(sp_re), jnp.stack(sp_im), jnp.stack(ss_re), jnp.stack(ss_im), jnp.stack(v_new))
```

```python
import functools

import jax
import jax.numpy as jnp
from jax import lax
from jax.experimental import pallas as pl
from jax.experimental.pallas import tpu as pltpu

F32 = jnp.float32
BF16 = jnp.bfloat16

D_MODEL = 1024
DEPTH = 2
CHUNK = 64
GMLP_CHUNK = 128
GMLP_WIDTH = 512
GMLP_HEADS = 4
GMLP_HEAD_DIM = 128
SSM_WIDTH = 512
SSM_GROUP = 16
SSM_GROUPS = 32
SSM_STATE = 64
IN_WIDTH = 1536
N_EXPERTS = 16
EXPERTS_PER_GROUP = 4
N_EXPERT_GROUPS = 4
D_EXPERT = 512
EPS = 1e-6

LANES = 128
S5_CHUNK = 16
GROUP_BLOCKS = 4
GROUPS_PER_BLOCK = SSM_GROUPS // GROUP_BLOCKS
STATE_BLOCK = GROUPS_PER_BLOCK * SSM_STATE
S5_ROW = S5_CHUNK * LANES
ADA_ROWS = 16
PAIRS_PER_GROUP = 6
N_CLASSES = N_EXPERT_GROUPS * PAIRS_PER_GROUP
CLASS_ROWS = 32
_PAIRS = [(a, b) for a in range(EXPERTS_PER_GROUP) for b in range(a + 1, EXPERTS_PER_GROUP)]
CLASS_LO = [g * EXPERTS_PER_GROUP + a for g in range(N_EXPERT_GROUPS) for a, _ in _PAIRS]
CLASS_HI = [g * EXPERTS_PER_GROUP + b for g in range(N_EXPERT_GROUPS) for _, b in _PAIRS]
MOE_TILE = 256
MIX_TILE = 512
RUN_ALIGN = 8
SORT_ROWS = 768
XS_WIDTH = D_MODEL + LANES
VMEM_LIMIT = 56 * 1024 * 1024


def _cparams(*sem):
    return pltpu.CompilerParams(dimension_semantics=sem, vmem_limit_bytes=VMEM_LIMIT)


def _ada_kernel(c_ref, w_ref, b_ref, o_ref):
    c = c_ref[...]
    s = (c * jax.nn.sigmoid(c)).astype(BF16)
    o_ref[0] = jnp.dot(s, w_ref[0].astype(BF16), preferred_element_type=F32) + b_ref[0]


def _ada(c_all, w_ada, b_ada):
    nblk = 6
    return pl.pallas_call(
        _ada_kernel,
        grid=(DEPTH, nblk),
        in_specs=[
            pl.BlockSpec((ADA_ROWS, D_MODEL), lambda l, j: (0, 0)),
            pl.BlockSpec((1, D_MODEL, D_MODEL), lambda l, j: (l, 0, j)),
            pl.BlockSpec((1, 1, D_MODEL), lambda l, j: (l, 0, j)),
        ],
        out_specs=pl.BlockSpec((1, ADA_ROWS, D_MODEL), lambda l, j: (l, 0, j)),
        out_shape=jax.ShapeDtypeStruct((DEPTH, ADA_ROWS, 6 * D_MODEL), F32),
        compiler_params=_cparams("parallel", "parallel"),
        name="ada",
    )(c_all, w_ada, b_ada.reshape(DEPTH, 1, 6 * D_MODEL))


def _prep_kernel(are_ref, aim_ref, ldt_ref, bre_ref, bim_ref, cre_ref, cim_ref,
                 msre_ref, msim_ref, nt_ref, wrev_ref, a16re_ref, a16im_ref):
    a_re = are_ref[0]
    a_im = aim_ref[0]
    dt = jnp.exp(ldt_ref[0])
    rho = a_re * dt
    th = a_im * dt
    kk = jnp.minimum(lax.broadcasted_iota(jnp.int32, (24, STATE_BLOCK), 0), S5_CHUNK).astype(F32)
    mag = jnp.exp(kk * rho)
    pw_re = mag * jnp.cos(kk * th)
    pw_im = mag * jnp.sin(kk * th)

    lb_re = pw_re[1:2]
    lb_im = pw_im[1:2]
    num_re = lb_re - 1.0
    num_im = lb_im
    den = a_re * a_re + a_im * a_im
    coef_re = (num_re * a_re + num_im * a_im) / den
    coef_im = (num_im * a_re - num_re * a_im) / den
    b_re = bre_ref[0]
    b_im = bim_ref[0]
    bb_re = coef_re * b_re - coef_im * b_im
    bb_im = coef_re * b_im + coef_im * b_re

    rows = lax.broadcasted_iota(jnp.int32, (LANES, STATE_BLOCK), 0)
    cols = lax.broadcasted_iota(jnp.int32, (LANES, STATE_BLOCK), 1)
    same_group = (rows >> 4) == (cols >> 6)

    def blockdiag(x16):
        return jnp.where(same_group, jnp.concatenate([x16] * GROUPS_PER_BLOCK, axis=0), 0.0)

    for s in range(S5_CHUNK):
        k = S5_CHUNK - 1 - s
        p_re = pw_re[k:k + 1]
        p_im = pw_im[k:k + 1]
        msre_ref[0, s * LANES:(s + 1) * LANES, :] = blockdiag(p_re * bb_re - p_im * bb_im).astype(BF16)
        msim_ref[0, s * LANES:(s + 1) * LANES, :] = blockdiag(p_re * bb_im + p_im * bb_re).astype(BF16)

    b_hi, b_lo = _split_bf16(jnp.concatenate([blockdiag(bb_re), blockdiag(bb_im)], axis=1))
    c_re = cre_ref[0]
    c_im = cim_ref[0]
    for k in range(S5_CHUNK + 1):
        p_re = pw_re[k:k + 1]
        p_im = pw_im[k:k + 1]
        cl = jnp.concatenate([blockdiag(c_re * p_re - c_im * p_im),
                              -blockdiag(c_re * p_im + c_im * p_re)], axis=1)
        if k >= 1:
            nt_ref[0, (k - 1) * LANES:k * LANES, :] = cl.astype(BF16)
        if k < S5_CHUNK:
            c_hi, c_lo = _split_bf16(cl)
            nt_dims = (((1,), (1,)), ((), ()))
            wl = (lax.dot_general(b_hi, c_hi, nt_dims, preferred_element_type=F32)
                  + lax.dot_general(b_hi, c_lo, nt_dims, preferred_element_type=F32)
                  + lax.dot_general(b_lo, c_hi, nt_dims, preferred_element_type=F32))
            j = S5_CHUNK - 1 - k
            wrev_ref[0, j * LANES:(j + 1) * LANES, :LANES] = wl.astype(BF16)
            if j >= 1:
                wrev_ref[0, (j - 1) * LANES:j * LANES, LANES:] = wl.astype(BF16)
    wrev_ref[0, (S5_CHUNK - 1) * LANES:, LANES:] = jnp.zeros((LANES, LANES), BF16)

    a16re_ref[0] = pw_re[S5_CHUNK:S5_CHUNK + 1]
    a16im_ref[0] = pw_im[S5_CHUNK:S5_CHUNK + 1]


def _s5_prep(a_re, a_im, log_dt, b_re, b_im, c_re, c_im):
    nstate = SSM_GROUPS * SSM_STATE

    def lane_row(v):
        return v.reshape(GROUP_BLOCKS, 1, STATE_BLOCK)

    def rows16(v):
        return v.reshape(SSM_GROUP, GROUP_BLOCKS, STATE_BLOCK).transpose(1, 0, 2)

    ldt = jnp.repeat(log_dt, SSM_STATE).reshape(SSM_GROUPS, SSM_STATE)
    bt_re = rows16(b_re.transpose(2, 0, 1).reshape(SSM_GROUP, nstate))
    bt_im = rows16(b_im.transpose(2, 0, 1).reshape(SSM_GROUP, nstate))
    ct_re = rows16(c_re.transpose(1, 0, 2).reshape(SSM_GROUP, nstate))
    ct_im = rows16(c_im.transpose(1, 0, 2).reshape(SSM_GROUP, nstate))
    row_spec = pl.BlockSpec((1, 1, STATE_BLOCK), lambda g: (g, 0, 0))
    r16_spec = pl.BlockSpec((1, SSM_GROUP, STATE_BLOCK), lambda g: (g, 0, 0))
    return pl.pallas_call(
        _prep_kernel,
        grid=(GROUP_BLOCKS,),
        in_specs=[row_spec, row_spec, row_spec, r16_spec, r16_spec, r16_spec, r16_spec],
        out_specs=[
            pl.BlockSpec((1, S5_ROW, STATE_BLOCK), lambda g: (g, 0, 0)),
            pl.BlockSpec((1, S5_ROW, STATE_BLOCK), lambda g: (g, 0, 0)),
            pl.BlockSpec((1, S5_ROW, 2 * STATE_BLOCK), lambda g: (g, 0, 0)),
            pl.BlockSpec((1, S5_ROW, 2 * LANES), lambda g: (g, 0, 0)),
            row_spec, row_spec,
        ],
        out_shape=[
            jax.ShapeDtypeStruct((GROUP_BLOCKS, S5_ROW, STATE_BLOCK), BF16),
            jax.ShapeDtypeStruct((GROUP_BLOCKS, S5_ROW, STATE_BLOCK), BF16),
            jax.ShapeDtypeStruct((GROUP_BLOCKS, S5_ROW, 2 * STATE_BLOCK), BF16),
            jax.ShapeDtypeStruct((GROUP_BLOCKS, S5_ROW, 2 * LANES), BF16),
            jax.ShapeDtypeStruct((GROUP_BLOCKS, 1, STATE_BLOCK), F32),
            jax.ShapeDtypeStruct((GROUP_BLOCKS, 1, STATE_BLOCK), F32),
        ],
        compiler_params=_cparams("parallel"),
        name="s5_prep",
    )(lane_row(a_re), lane_row(a_im), lane_row(ldt), bt_re, bt_im, ct_re, ct_im)


def _rms(x, g):
    return x * lax.rsqrt(jnp.mean(x * x, axis=-1, keepdims=True) + EPS) * g


def _unsorted_ffn(zs_ref, prow_ref):
    tm = prow_ref.shape[1]
    zs = zs_ref[...].astype(BF16)
    pcol = jnp.concatenate([prow_ref[...].astype(F32), jnp.zeros((LANES - 1, tm), F32)], axis=0).T[:, :1]
    lanes = lax.broadcasted_iota(jnp.int32, (tm, SORT_ROWS), 1)
    pick = jnp.where(lanes == pcol.astype(jnp.int32), 1.0, 0.0).astype(BF16)
    return jnp.dot(pick, zs, preferred_element_type=F32)


def _mixer_in_kernel(x_ref, *refs, tm, fused, want_v):
    if fused:
        zs_ref, prow_ref, gf_ref = refs[:3]
        refs = refs[3:]
    mod_ref, g_ref, win_ref, vg_ref, ws_ref, bs_ref, a_ref, u_ref = refs[:8]
    refs = refs[8:]
    x = x_ref[...]
    if fused:
        x = x + gf_ref[0] * _unsorted_ffn(zs_ref, prow_ref)
        refs[0][...] = x
        refs = refs[1:]
    mod = mod_ref[0]
    shift = mod[:, :D_MODEL]
    scale = mod[:, D_MODEL:]
    h = _rms(x, g_ref[...]) * (1.0 + scale) + shift
    proj = jnp.dot(h.astype(BF16), win_ref[...], preferred_element_type=F32)
    z = jax.nn.gelu(proj[:, :2 * GMLP_WIDTH])
    u = z[:, :GMLP_WIDTH]
    v = z[:, GMLP_WIDTH:]
    vc = v - jnp.mean(v, axis=-1, keepdims=True)
    vn = vc * lax.rsqrt(jnp.mean(vc * vc, axis=-1, keepdims=True) + EPS) * vg_ref[...]
    if want_v:
        refs[0][...] = vn
    vb = vn.astype(BF16)
    bias = bs_ref[...]
    for c in range(tm // GMLP_CHUNK):
        r0 = c * GMLP_CHUNK
        for hh in range(GMLP_HEADS):
            l0 = hh * GMLP_HEAD_DIM
            mixed = jnp.dot(ws_ref[hh], vb[r0:r0 + GMLP_CHUNK, l0:l0 + GMLP_HEAD_DIM],
                            preferred_element_type=F32) + bias[:, l0:l0 + GMLP_HEAD_DIM]
            a_ref[r0:r0 + GMLP_CHUNK, l0:l0 + GMLP_HEAD_DIM] = (
                u[r0:r0 + GMLP_CHUNK, l0:l0 + GMLP_HEAD_DIM] * mixed).astype(BF16)
    for gb in range(GROUP_BLOCKS):
        l0 = 2 * GMLP_WIDTH + gb * LANES
        u_ref[gb] = proj[:, l0:l0 + LANES]


def _mixer_in(x2d, mod, g_mix, w_in, v_gain, ws, bs, *, tm, tiles_per_mod, want_v, ffn=None):
    t = x2d.shape[0]
    rmod = mod.shape[1]
    const2 = lambda i: (0, 0)
    tok = pl.BlockSpec((tm, D_MODEL), lambda i: (i, 0))
    in_specs = [tok]
    args = [x2d]
    if ffn is not None:
        in_specs += _ffn_specs(tm, rmod, tiles_per_mod)
        args += list(ffn)
    in_specs += [
        pl.BlockSpec((1, rmod, 2 * D_MODEL), lambda i: (i // tiles_per_mod, 0, 0)),
        pl.BlockSpec((1, D_MODEL), const2),
        pl.BlockSpec((D_MODEL, IN_WIDTH), const2),
        pl.BlockSpec((1, GMLP_WIDTH), const2),
        pl.BlockSpec((GMLP_HEADS, GMLP_CHUNK, GMLP_CHUNK), lambda i: (0, 0, 0)),
        pl.BlockSpec((GMLP_CHUNK, GMLP_WIDTH), const2),
    ]
    args += [mod, g_mix, w_in, v_gain, ws, bs]
    out_shape = [jax.ShapeDtypeStruct((t, GMLP_WIDTH), BF16),
                 jax.ShapeDtypeStruct((GROUP_BLOCKS, t, LANES), F32)]
    out_specs = [pl.BlockSpec((tm, GMLP_WIDTH), lambda i: (i, 0)),
                 pl.BlockSpec((GROUP_BLOCKS, tm, LANES), lambda i: (0, i, 0))]
    if ffn is not None:
        out_shape.append(jax.ShapeDtypeStruct((t, D_MODEL), F32))
        out_specs.append(tok)
    if want_v:
        out_shape.append(jax.ShapeDtypeStruct((t, GMLP_WIDTH), F32))
        out_specs.append(pl.BlockSpec((tm, GMLP_WIDTH), lambda i: (i, 0)))
    return pl.pallas_call(
        functools.partial(_mixer_in_kernel, tm=tm, fused=ffn is not None, want_v=want_v),
        grid=(t // tm,),
        in_specs=in_specs,
        out_specs=out_specs,
        out_shape=out_shape,
        compiler_params=_cparams("parallel"),
        name="mixer_in",
    )(*args)


def _ffn_specs(tm, rmod, tiles_per_mod):
    return [
        pl.BlockSpec((SORT_ROWS, D_MODEL), lambda i: (i, 0)),
        pl.BlockSpec((1, tm), lambda i: (0, i)),
        pl.BlockSpec((1, rmod, D_MODEL), lambda i: (i // tiles_per_mod, 0, 0)),
    ]


def _final_kernel(x_ref, zs_ref, prow_ref, gf_ref, g_ref, o_ref):
    x = x_ref[...] + gf_ref[0] * _unsorted_ffn(zs_ref, prow_ref)
    o_ref[...] = _rms(x, g_ref[...])


def _final(xmid, ffn, g_final, *, tm, tiles_per_mod):
    t = xmid.shape[0]
    rmod = ffn[2].shape[1]
    tok = pl.BlockSpec((tm, D_MODEL), lambda i: (i, 0))
    return pl.pallas_call(
        _final_kernel,
        grid=(t // tm,),
        in_specs=[tok] + _ffn_specs(tm, rmod, tiles_per_mod) + [pl.BlockSpec((1, D_MODEL), lambda i: (0, 0))],
        out_specs=tok,
        out_shape=jax.ShapeDtypeStruct((t, D_MODEL), F32),
        compiler_params=_cparams("parallel"),
        name="final_norm",
    )(xmid, *ffn, g_final)


def _s5_kernel(u_ref, msre_ref, msim_ref, nt_ref, wrev_ref, a16re_ref, a16im_ref, d_ref,
               h0re_ref, h0im_ref, y_ref, hnre_ref, hnim_ref, *scratch, scan):
    rows = u_ref.shape[1] // S5_CHUNK
    u = jnp.concatenate([u_ref[0, pl.ds(s, rows, stride=S5_CHUNK), :] for s in range(S5_CHUNK)], axis=1)
    ub = u.astype(BF16)
    s_re = jnp.dot(ub, msre_ref[0], preferred_element_type=F32)
    s_im = jnp.dot(ub, msim_ref[0], preferred_element_type=F32)
    a_re = a16re_ref[0]
    a_im = a16im_ref[0]
    h0_re = h0re_ref[0, 0]
    h0_im = h0im_ref[0, 0]
    if scan:
        sre_scr, sim_scr, hre_scr, him_scr = scratch
        sre_scr[...] = s_re
        sim_scr[...] = s_im

        def body(r, carry):
            hr, hi = carry
            hre_scr[pl.ds(r, 1), :] = hr
            him_scr[pl.ds(r, 1), :] = hi
            sr = sre_scr[pl.ds(r, 1), :]
            si = sim_scr[pl.ds(r, 1), :]
            return (a_re * hr - a_im * hi + sr, a_re * hi + a_im * hr + si)

        hn_re, hn_im = lax.fori_loop(0, rows, body, (h0_re, h0_im), unroll=8)
        h_re = hre_scr[...]
        h_im = him_scr[...]
    else:
        h_re = h0_re
        h_im = h0_im
        hn_re = a_re * h_re - a_im * h_im + s_re
        hn_im = a_re * h_im + a_im * h_re + s_im
    hnre_ref[0, 0] = hn_re
    hnim_ref[0, 0] = hn_im
    hcat = jnp.concatenate([h_re, h_im], axis=1).astype(BF16)
    inter = lax.dot_general(hcat, nt_ref[0], (((1,), (1,)), ((), ())),
                            preferred_element_type=F32)
    d = d_ref[0]
    for t in range(0, S5_CHUNK, 2):
        k0 = (S5_CHUNK - 2 - t) * LANES
        pair = jnp.dot(ub[:, :(t + 2) * LANES], wrev_ref[0, k0:, :], preferred_element_type=F32)
        for step, intra in ((t + 1, pair[:, :LANES]), (t, pair[:, LANES:])):
            sl = slice(step * LANES, (step + 1) * LANES)
            y_ref[0, pl.ds(step, rows, stride=S5_CHUNK), :] = intra + inter[:, sl] + d[:, sl] * u[:, sl]


def _s5(u4, prep, d_tiled, h0_re, h0_im, *, rows, nblk, scan):
    ms_re, ms_im, nt, wrev, a16_re, a16_im = prep
    rh = h0_re.shape[2]
    wspec = lambda shape: pl.BlockSpec((1,) + shape, lambda g, b: (g, 0, 0))
    hspec = pl.BlockSpec((1, 1, rh, STATE_BLOCK), lambda g, b: (g, b, 0, 0))
    scratch = [pltpu.VMEM((rows, STATE_BLOCK), F32)] * 4 if scan else []
    return pl.pallas_call(
        functools.partial(_s5_kernel, scan=scan),
        grid=(GROUP_BLOCKS, nblk),
        in_specs=[
            pl.BlockSpec((1, rows * S5_CHUNK, LANES), lambda g, b: (g, b, 0)),
            wspec((S5_ROW, STATE_BLOCK)), wspec((S5_ROW, STATE_BLOCK)),
            wspec((S5_ROW, 2 * STATE_BLOCK)), wspec((S5_ROW, 2 * LANES)),
            wspec((1, STATE_BLOCK)), wspec((1, STATE_BLOCK)), wspec((1, S5_ROW)),
            hspec, hspec,
        ],
        out_specs=[pl.BlockSpec((1, rows * S5_CHUNK, LANES), lambda g, b: (g, b, 0)), hspec, hspec],
        out_shape=[
            jax.ShapeDtypeStruct(u4.shape, F32),
            jax.ShapeDtypeStruct(h0_re.shape, F32),
            jax.ShapeDtypeStruct(h0_re.shape, F32),
        ],
        scratch_shapes=scratch,
        compiler_params=_cparams("parallel", "parallel"),
        name="s5",
    )(u4, ms_re, ms_im, nt, wrev, a16_re, a16_im, d_tiled, h0_re, h0_im)


def _split_bf16(x):
    hi = x.astype(BF16)
    return hi, (x - hi.astype(F32)).astype(BF16)


def _top2_of4(a):
    m1 = jnp.maximum(jnp.maximum(a[0], a[1]), jnp.maximum(a[2], a[3]))
    i1 = jnp.where(a[0] == m1, 0, jnp.where(a[1] == m1, 1, jnp.where(a[2] == m1, 2, 3)))
    b = [jnp.where(i1 == j, -jnp.inf, a[j]) for j in range(4)]
    m2 = jnp.maximum(jnp.maximum(b[0], b[1]), jnp.maximum(b[2], b[3]))
    i2 = jnp.where(b[0] == m2, 0, jnp.where(b[1] == m2, 1, jnp.where(b[2] == m2, 2, 3)))
    return m1, i1, m2, i2


def _route_rows(h2, wr_ref, br_ref):
    h_hi, h_lo = _split_bf16(h2)
    parts = jnp.dot(jnp.concatenate([h_hi, h_lo], axis=1), wr_ref[...], preferred_element_type=F32)
    pt = parts.T
    lt = pt[:N_EXPERTS] + pt[N_EXPERTS:2 * N_EXPERTS] + br_ref[...]
    rows = [lt[e:e + 1] for e in range(N_EXPERTS)]
    mx = functools.reduce(jnp.maximum, rows)
    ex = [jnp.exp(r - mx) for r in rows]
    tot = functools.reduce(lambda p, q: p + q, ex)
    scores = [e / tot for e in ex]
    best = None
    for g in range(N_EXPERT_GROUPS):
        m1, i1, m2, i2 = _top2_of4(scores[g * EXPERTS_PER_GROUP:(g + 1) * EXPERTS_PER_GROUP])
        cand = (m1 + m2, m1, i1 + g * EXPERTS_PER_GROUP, m2, i2 + g * EXPERTS_PER_GROUP)
        if best is None:
            best = cand
        else:
            better = cand[0] > best[0]
            best = tuple(jnp.where(better, c, b) for c, b in zip(cand, best))
    _, v1, e1, v2, e2 = best
    den = v1 + v2
    w1 = v1 / den
    w2 = v2 / den
    first_lo = e1 < e2
    return (jnp.where(first_lo, e1, e2), jnp.where(first_lo, e2, e1),
            jnp.where(first_lo, w1, w2), jnp.where(first_lo, w2, w1))


def _mixer_out_kernel(x_ref, a_ref, y_ref, mod_ref, g_ref, wglu_ref, bglu_ref, wout_ref,
                      wr_ref, br_ref, *rest, route, tiles_per_mod):
    x = x_ref[...]
    mod = mod_ref[0]
    gate_m = mod[:, :D_MODEL]
    shift_f = mod[:, D_MODEL:2 * D_MODEL]
    scale_f = mod[:, 2 * D_MODEL:]
    ys = jax.nn.gelu(jnp.concatenate([y_ref[gb] for gb in range(GROUP_BLOCKS)], axis=1))
    glu = jnp.dot(ys.astype(BF16), wglu_ref[...], preferred_element_type=F32) + bglu_ref[...]
    b_out = ys * jax.nn.sigmoid(glu)
    mixed = jnp.concatenate([a_ref[...], b_out.astype(BF16)], axis=1)
    xmid = x + gate_m * jnp.dot(mixed, wout_ref[...], preferred_element_type=F32)
    h2 = _rms(xmid, g_ref[...]) * (1.0 + scale_f) + shift_f
    lo, hi, glo, ghi = _route_rows(h2, wr_ref, br_ref)
    if not route:
        xmid_ref, h2_ref, lo_ref, hi_ref, glo_ref, ghi_ref = rest
        xmid_ref[...] = xmid
        h2_ref[...] = h2
        lo_ref[...] = lo
        hi_ref[...] = hi
        glo_ref[...] = glo
        ghi_ref[...] = ghi
        return

    tri_ref, ltri_ref, xmid_ref, xs_ref, prow_ref, cnt_ref = rest
    tm = x.shape[0]
    xmid_ref[...] = xmid
    a = lo & (EXPERTS_PER_GROUP - 1)
    b = hi & (EXPERTS_PER_GROUP - 1)
    pair = jnp.where(a == 0, 0, jnp.where(a == 1, 3, 5)) + (b - a - 1)
    cls = (lo >> 2) * PAIRS_PER_GROUP + pair
    onehot = lax.broadcasted_iota(jnp.int32, (CLASS_ROWS, tm), 0) == cls
    prefix = jnp.dot(jnp.where(onehot, 1.0, 0.0).astype(BF16), tri_ref[...],
                     preferred_element_type=F32)
    total = prefix[:, tm - 1:tm]
    cnt_ref[0] = jnp.broadcast_to(total, (CLASS_ROWS, LANES))
    groups = jnp.floor((total + (RUN_ALIGN - 1)) * (1.0 / RUN_ALIGN))
    before = jnp.dot(ltri_ref[...], jnp.broadcast_to(groups, (CLASS_ROWS, LANES)).astype(BF16),
                     preferred_element_type=F32)[:, :1] * RUN_ALIGN
    prow = jnp.sum(jnp.where(onehot, before + prefix - 1.0, 0.0), axis=0, keepdims=True).astype(jnp.int32)
    prow_ref[...] = prow
    pick = jnp.where(lax.broadcasted_iota(jnp.int32, (SORT_ROWS, tm), 0) == prow, 1.0, 0.0).astype(BF16)
    glo_hi, glo_lo = _split_bf16(glo)
    ghi_hi, ghi_lo = _split_bf16(ghi)
    gates = jnp.concatenate([glo_hi.astype(F32), glo_lo.astype(F32), ghi_hi.astype(F32), ghi_lo.astype(F32),
                             jnp.zeros((LANES - 4, tm), F32)], axis=0).T
    payload = jnp.concatenate([h2.astype(BF16), gates.astype(BF16)], axis=1)
    xs_ref[...] = jnp.dot(pick, payload, preferred_element_type=F32)


def _mixer_out(x2d, a_out, y4, mod, g_ffn, w_glu, b_glu, w_out, wr, b_r,
               *, tm, tiles_per_mod, route):
    t = x2d.shape[0]
    rmod = mod.shape[1]
    const2 = lambda i: (0, 0)
    tok = pl.BlockSpec((tm, D_MODEL), lambda i: (i, 0))
    row = pl.BlockSpec((1, tm), lambda i: (0, i))
    in_specs = [
        tok,
        pl.BlockSpec((tm, GMLP_WIDTH), lambda i: (i, 0)),
        pl.BlockSpec((GROUP_BLOCKS, tm, LANES), lambda i: (0, i, 0)),
        pl.BlockSpec((1, rmod, 3 * D_MODEL), lambda i: (i // tiles_per_mod, 0, 0)),
        pl.BlockSpec((1, D_MODEL), const2),
        pl.BlockSpec((SSM_WIDTH, SSM_WIDTH), const2),
        pl.BlockSpec((1, SSM_WIDTH), const2),
        pl.BlockSpec((D_MODEL, D_MODEL), const2),
        pl.BlockSpec((2 * D_MODEL, LANES), const2),
        pl.BlockSpec((N_EXPERTS, 1), const2),
    ]
    args = [x2d, a_out, y4, mod, g_ffn, w_glu, b_glu, w_out, wr, b_r]
    if route:
        ids = jnp.arange(tm)
        cids = jnp.arange(CLASS_ROWS)
        args += [(ids[:, None] <= ids[None, :]).astype(BF16), (cids[None, :] < cids[:, None]).astype(BF16)]
        in_specs += [pl.BlockSpec((tm, tm), const2), pl.BlockSpec((CLASS_ROWS, CLASS_ROWS), const2)]
        out_specs = [tok, pl.BlockSpec((SORT_ROWS, XS_WIDTH), lambda i: (i, 0)), row,
                     pl.BlockSpec((1, CLASS_ROWS, LANES), lambda i: (i, 0, 0))]
        out_shape = [jax.ShapeDtypeStruct((t, D_MODEL), F32),
                     jax.ShapeDtypeStruct((t // tm * SORT_ROWS, XS_WIDTH), F32),
                     jax.ShapeDtypeStruct((1, t), jnp.int32),
                     jax.ShapeDtypeStruct((t // tm, CLASS_ROWS, LANES), F32)]
        scratch = []
    else:
        out_specs = [tok, tok, row, row, row, row]
        out_shape = [jax.ShapeDtypeStruct((t, D_MODEL), F32),
                     jax.ShapeDtypeStruct((t, D_MODEL), F32),
                     jax.ShapeDtypeStruct((1, t), jnp.int32),
                     jax.ShapeDtypeStruct((1, t), jnp.int32),
                     jax.ShapeDtypeStruct((1, t), F32),
                     jax.ShapeDtypeStruct((1, t), F32)]
        scratch = []
    return pl.pallas_call(
        functools.partial(_mixer_out_kernel, route=route, tiles_per_mod=tiles_per_mod),
        grid=(t // tm,),
        in_specs=in_specs,
        out_specs=out_specs,
        out_shape=out_shape,
        scratch_shapes=scratch,
        compiler_params=_cparams("parallel"),
        name="mixer_out",
    )(*args)


def _moe_runs_kernel(tlo_ref, thi_ref, tval_ref, psrc_ref, pdst_ref, plen_ref, kfirst_ref, kend_ref, used_ref,
                     xs_hbm, wgl_ref, wul_ref, wdl_ref, wgh_ref, wuh_ref, wdh_ref,
                     zs_hbm, buf_ref, obuf_ref, sem_in, sem_out, *, nsrc):
    i = pl.program_id(0)
    nt = pl.num_programs(0)
    slot = i % 2
    other = 1 - slot
    nv = tval_ref[i]
    nv_next = jnp.where(i + 1 < nt, tval_ref[jnp.minimum(i + 1, nt - 1)], 0)
    nv_prev = jnp.where(i >= 1, tval_ref[jnp.maximum(i - 1, 0)], 0)
    nv_prev2 = jnp.where(i >= 2, tval_ref[jnp.maximum(i - 2, 0)], 0)

    def rows8(v):
        return pl.multiple_of(v, RUN_ALIGN)

    def for_pieces(tile, fn):
        def body(k, c):
            n = plen_ref[tile * nsrc + k]

            @pl.when(n > 0)
            def _():
                fn(rows8(psrc_ref[tile * nsrc + k]), rows8(pdst_ref[tile * nsrc + k]), rows8(n))

            return c

        lax.fori_loop(kfirst_ref[tile], kend_ref[tile], body, 0)

    def gather_start(tile, sl):
        for_pieces(tile, lambda src, dst, n: pltpu.make_async_copy(
            xs_hbm.at[pl.ds(src, n), :], buf_ref.at[sl, pl.ds(dst, n), :], sem_in.at[sl]).start())

    def gather_wait(sl, n):
        pltpu.make_async_copy(xs_hbm.at[pl.ds(0, rows8(n)), :], buf_ref.at[sl, pl.ds(0, rows8(n)), :],
                              sem_in.at[sl]).wait()

    def scatter_start(tile, sl):
        for_pieces(tile, lambda src, dst, n: pltpu.make_async_copy(
            obuf_ref.at[sl, pl.ds(dst, n), :], zs_hbm.at[pl.ds(src, n), :], sem_out.at[sl]).start())

    def scatter_wait(sl, n):
        pltpu.make_async_copy(obuf_ref.at[sl, pl.ds(0, rows8(n)), :], zs_hbm.at[pl.ds(0, rows8(n)), :],
                              sem_out.at[sl]).wait()

    @pl.when(i == 0)
    def _():
        buf_ref[...] = jnp.zeros_like(buf_ref)
        gather_start(0, 0)
        obuf_ref[1] = jnp.zeros((MOE_TILE, D_MODEL), F32)

        def tail(k, start):
            used = rows8(used_ref[k])
            n = rows8(SORT_ROWS - used)
            copy = pltpu.make_async_copy(obuf_ref.at[1, pl.ds(0, n), :],
                                         zs_hbm.at[pl.ds(rows8(k * SORT_ROWS + used), n), :], sem_out.at[1])

            @pl.when(n > 0)
            def _():
                copy.start() if start else copy.wait()

        for start in (True, False):
            lax.fori_loop(0, nsrc, lambda k, c: (tail(k, start), c)[1], 0)

    @pl.when(nv_next > 0)
    def _():
        gather_start(i + 1, other)

    @pl.when(nv_prev2 > 0)
    def _():
        scatter_wait(slot, nv_prev2)

    @pl.when(nv > 0)
    def _():
        gather_wait(slot, nv)
        buf = buf_ref[slot]
        h = buf[:, :D_MODEL].astype(BF16)
        glo = buf[:, D_MODEL:D_MODEL + 1] + buf[:, D_MODEL + 1:D_MODEL + 2]
        ghi = buf[:, D_MODEL + 2:D_MODEL + 3] + buf[:, D_MODEL + 3:D_MODEL + 4]
        ffn = None
        for wg, wu, wd, gate in ((wgl_ref, wul_ref, wdl_ref, glo), (wgh_ref, wuh_ref, wdh_ref, ghi)):
            he = (jax.nn.silu(jnp.dot(h, wg[0], preferred_element_type=F32))
                  * jnp.dot(h, wu[0], preferred_element_type=F32))
            y = gate * jnp.dot(he.astype(BF16), wd[0], preferred_element_type=F32)
            ffn = y if ffn is None else ffn + y
        obuf_ref[slot] = ffn.astype(BF16).astype(F32)
        scatter_start(i, slot)

    @pl.when(i == nt - 1)
    def _():
        @pl.when(nv_prev > 0)
        def _():
            scatter_wait(other, nv_prev)

        @pl.when(nv > 0)
        def _():
            scatter_wait(slot, nv)


def _moe_runs(xs, counts, wg, wu, wd, *, tm, expert0):
    nmix = counts.shape[0]
    assert SORT_ROWS - tm <= MOE_TILE
    max_rows = nmix * min(SORT_ROWS, tm + N_CLASSES * (RUN_ALIGN - 1))
    ntiles = -(-max_rows // MOE_TILE) + N_CLASSES
    n_kc = counts[:, :N_CLASSES, 0].astype(jnp.int32)
    len_kc = (n_kc + RUN_ALIGN - 1) // RUN_ALIGN * RUN_ALIGN
    off_kc = jnp.cumsum(len_kc, axis=1) - len_kc
    used_k = jnp.sum(len_kc, axis=1)
    start_kc = jnp.cumsum(len_kc, axis=0) - len_kc
    region_c = jnp.sum(len_kc, axis=0)
    tiles_c = (region_c + MOE_TILE - 1) // MOE_TILE
    tile_end = jnp.cumsum(tiles_c)
    tile_start = tile_end - tiles_c
    used = tile_end[-1]
    tid = jnp.arange(ntiles, dtype=jnp.int32)
    tcls = jnp.sum(tile_end[None, :] <= jnp.minimum(tid, used - 1)[:, None], axis=1).astype(jnp.int32)
    lo_row = (tid - tile_start[tcls]) * MOE_TILE
    tval = jnp.where(tid < used, jnp.clip(region_c[tcls] - lo_row, 0, MOE_TILE), 0)
    run_lo = start_kc[:, tcls].T
    run_hi = run_lo + len_kc[:, tcls].T
    piece_lo = jnp.maximum(run_lo, lo_row[:, None])
    piece_hi = jnp.minimum(run_hi, (lo_row + tval)[:, None])
    plen = jnp.maximum(piece_hi - piece_lo, 0)
    psrc = jnp.arange(nmix, dtype=jnp.int32)[None, :] * SORT_ROWS + off_kc[:, tcls].T + (piece_lo - run_lo)
    pdst = piece_lo - lo_row[:, None]
    flat = lambda v: jnp.where(plen > 0, v, 0).reshape(-1).astype(jnp.int32)
    tlo = jnp.asarray(CLASS_LO, jnp.int32)[tcls]
    thi = jnp.asarray(CLASS_HI, jnp.int32)[tcls]
    kfirst = jnp.sum(run_hi <= lo_row[:, None], axis=1).astype(jnp.int32)
    kend = jnp.sum(run_lo < (lo_row + tval)[:, None], axis=1).astype(jnp.int32)
    nprefetch = 9
    wspec_lo = lambda shape: pl.BlockSpec(shape, lambda i, tlo, *_: (expert0 + tlo[i], 0, 0))
    wspec_hi = lambda shape: pl.BlockSpec(shape, lambda i, tlo, thi, *_: (expert0 + thi[i], 0, 0))
    up = (1, D_MODEL, D_EXPERT)
    down = (1, D_EXPERT, D_MODEL)
    zs = pl.pallas_call(
        functools.partial(_moe_runs_kernel, nsrc=nmix),
        grid_spec=pltpu.PrefetchScalarGridSpec(
            num_scalar_prefetch=nprefetch,
            grid=(ntiles,),
            in_specs=[
                pl.BlockSpec(memory_space=pl.ANY),
                wspec_lo(up), wspec_lo(up), wspec_lo(down),
                wspec_hi(up), wspec_hi(up), wspec_hi(down),
            ],
            out_specs=pl.BlockSpec(memory_space=pl.ANY),
            scratch_shapes=[
                pltpu.VMEM((2, MOE_TILE, XS_WIDTH), F32),
                pltpu.VMEM((2, MOE_TILE, D_MODEL), F32),
                pltpu.SemaphoreType.DMA((2,)),
                pltpu.SemaphoreType.DMA((2,)),
            ],
        ),
        out_shape=jax.ShapeDtypeStruct((nmix * SORT_ROWS, D_MODEL), F32),
        compiler_params=_cparams("arbitrary"),
        name="moe_runs",
    )(tlo, thi, tval.astype(jnp.int32), flat(psrc), flat(pdst), flat(plen), kfirst, kend,
      used_k.astype(jnp.int32), xs, wg, wu, wd, wg, wu, wd)
    return zs


def _moe_kernel(h2_ref, gates_ref, wg_ref, wu_ref, wd_ref, xmid_ref, mod_ref, gfin_ref,
                o_ref, acc_ref, *, final_norm):
    e = pl.program_id(1)

    @pl.when(e == 0)
    def _():
        acc_ref[...] = jnp.zeros_like(acc_ref)

    h = h2_ref[...].astype(BF16)
    he = (jax.nn.silu(jnp.dot(h, wg_ref[0], preferred_element_type=F32))
          * jnp.dot(h, wu_ref[0], preferred_element_type=F32))
    y = jnp.dot(he.astype(BF16), wd_ref[0], preferred_element_type=F32)
    gates = gates_ref[...]
    lane = lax.broadcasted_iota(jnp.int32, gates.shape, 1)
    gcol = jnp.sum(jnp.where(lane == e, gates, 0.0), axis=1, keepdims=True)
    acc_ref[...] += gcol * y

    @pl.when(e == N_EXPERTS - 1)
    def _():
        x = xmid_ref[...] + mod_ref[0] * acc_ref[...]
        if final_norm:
            x = _rms(x, gfin_ref[...])
        o_ref[...] = x


def _moe(h2, gates, wg, wu, wd, xmid, mod, g_final, *, tm, tiles_per_mod, expert0, final_norm):
    t = h2.shape[0]
    rmod = mod.shape[1]
    tok = pl.BlockSpec((tm, D_MODEL), lambda i, e: (i, 0))
    return pl.pallas_call(
        functools.partial(_moe_kernel, final_norm=final_norm),
        grid=(t // tm, N_EXPERTS),
        in_specs=[
            tok,
            pl.BlockSpec((tm, N_EXPERTS), lambda i, e: (i, 0)),
            pl.BlockSpec((1, D_MODEL, D_EXPERT), lambda i, e: (expert0 + e, 0, 0)),
            pl.BlockSpec((1, D_MODEL, D_EXPERT), lambda i, e: (expert0 + e, 0, 0)),
            pl.BlockSpec((1, D_EXPERT, D_MODEL), lambda i, e: (expert0 + e, 0, 0)),
            tok,
            pl.BlockSpec((1, rmod, D_MODEL), lambda i, e: (i // tiles_per_mod, 0, 0)),
            pl.BlockSpec((1, D_MODEL), lambda i, e: (0, 0)),
        ],
        out_specs=tok,
        out_shape=jax.ShapeDtypeStruct((t, D_MODEL), F32),
        scratch_shapes=[pltpu.VMEM((tm, D_MODEL), F32)],
        compiler_params=_cparams("parallel", "arbitrary"),
        name="moe",
    )(h2, gates, wg, wu, wd, xmid, mod, g_final)


def _layer(x2d, mod, lw, h0_re, h0_im, *, seq_len, nseq, sample, final_norm, ffn=None):
    t = x2d.shape[0]
    if sample:
        tm, tiles_per_mod, tm_moe = t, 1, t
        s5_rows, s5_blocks = nseq, 1
    else:
        tm = min(MIX_TILE, seq_len)
        tiles_per_mod = seq_len // tm
        s5_rows, s5_blocks = seq_len // S5_CHUNK, nseq
    outs = _mixer_in(x2d, mod[..., :2 * D_MODEL], lw["g_mix"], lw["w_in"], lw["v_gain"],
                     lw["ws_sample"] if sample else lw["ws"], lw["bs_sample"] if sample else lw["bs"],
                     tm=tm, tiles_per_mod=tiles_per_mod, want_v=sample, ffn=ffn)
    a_out, u4 = outs[0], outs[1]
    if ffn is not None:
        x2d = outs[2]
    v_rows = outs[-1] if sample else None
    y4, hn_re, hn_im = _s5(u4, lw["prep"], lw["d_tiled"],
                           h0_re, h0_im, rows=s5_rows, nblk=s5_blocks, scan=not sample)
    mo = _mixer_out(x2d, a_out, y4, mod[..., 2 * D_MODEL:5 * D_MODEL],
                    lw["g_ffn"], lw["w_glu"], lw["b_glu"], lw["w_out"], lw["wr"], lw["b_r"],
                    tm=tm, tiles_per_mod=tiles_per_mod, route=not sample)
    if sample:
        xmid, h2, lo, hi, glo, ghi = mo
        eids = jnp.arange(N_EXPERTS, dtype=jnp.int32)[None, :]
        gates = (jnp.where(lo[0][:, None] == eids, glo[0][:, None], 0.0)
                 + jnp.where(hi[0][:, None] == eids, ghi[0][:, None], 0.0))
        x_new = _moe(h2, gates, lw["wg"], lw["wu"], lw["wd"], xmid, mod[..., 5 * D_MODEL:],
                     lw["g_final"], tm=tm_moe, tiles_per_mod=1, expert0=lw["expert0"],
                     final_norm=final_norm)
        return x_new, hn_re, hn_im, v_rows
    xmid, xs, prow, counts = mo
    zs = _moe_runs(xs, counts, lw["wg"], lw["wu"], lw["wd"], tm=tm, expert0=lw["expert0"])
    return (xmid, (zs, prow, mod[..., 5 * D_MODEL:])), hn_re, hn_im, None


def _state_out(h, nseq):
    return h.reshape(GROUP_BLOCKS, nseq, GROUPS_PER_BLOCK, SSM_STATE).transpose(1, 0, 2, 3).reshape(
        nseq, SSM_GROUPS, SSM_STATE)


def _state_in(h, nblk, rh):
    nseq = h.shape[0]
    return h.reshape(nseq, GROUP_BLOCKS, STATE_BLOCK).transpose(1, 0, 2).reshape(
        GROUP_BLOCKS, nblk, rh, STATE_BLOCK)


def kernel(x_prompt, x_sample, c_prompt, c_sample, state_s5_re, state_s5_im, w_ada, b_ada, g_norm_mix, g_norm_ffn, w_in, gmlp_v_gain, gmlp_w_spatial, gmlp_b_spatial, s5_a_re, s5_a_im, s5_log_dt, s5_b_re, s5_b_im, s5_c_re, s5_c_im, s5_d, s5_w_glu, s5_b_glu, w_out, w_router, b_router, w_gate, w_up, w_down, g_final):
    nb, seq_len, _ = x_prompt.shape
    ns, dec_len, _ = x_sample.shape
    assert dec_len == S5_CHUNK and ns * dec_len == GMLP_CHUNK and nb + ns <= ADA_ROWS
    assert seq_len % GMLP_CHUNK == 0 and MIX_TILE + N_CLASSES * (RUN_ALIGN - 1) <= SORT_ROWS

    c_all = jnp.concatenate([c_prompt, c_sample, jnp.zeros((ADA_ROWS - nb - ns, D_MODEL), F32)], axis=0)
    mod_all = _ada(c_all, w_ada, b_ada)

    pos = jnp.arange(GMLP_CHUNK)
    causal = (pos[None, :] // CHUNK) <= (pos[:, None] // CHUNK)
    wr_hi = w_router.astype(BF16)
    wr_lo = (w_router - wr_hi.astype(F32)).astype(BF16)
    wr = jnp.pad(jnp.concatenate([wr_hi, wr_lo], axis=1), ((0, 0), (0, LANES - 2 * N_EXPERTS)))
    wr = jnp.concatenate([wr, wr], axis=0)
    b_r = b_router.reshape(N_EXPERTS, 1)
    g_fin = g_final.reshape(1, D_MODEL)
    eye_s = jnp.eye(ns, dtype=F32)

    xp = x_prompt.reshape(nb * seq_len, D_MODEL)
    xs = x_sample.reshape(ns * dec_len, D_MODEL)
    zeros_p = jnp.zeros((GROUP_BLOCKS, nb, 1, STATE_BLOCK), F32)
    wg_all = w_gate.astype(BF16).reshape(DEPTH * N_EXPERTS, D_MODEL, D_EXPERT)
    wu_all = w_up.astype(BF16).reshape(DEPTH * N_EXPERTS, D_MODEL, D_EXPERT)
    wd_all = w_down.astype(BF16).reshape(DEPTH * N_EXPERTS, D_EXPERT, D_MODEL)
    sp_re, sp_im, ss_re, ss_im, v_new = [], [], [], [], []
    ffn_p = None
    for l in range(DEPTH):
        ws = jnp.where(causal[None], gmlp_w_spatial[l], 0.0)
        ws_sample = jnp.einsum("ab,hij->haibj", eye_s, ws[:, :dec_len, :dec_len]).reshape(
            GMLP_HEADS, GMLP_CHUNK, GMLP_CHUNK)
        bs = jnp.repeat(gmlp_b_spatial[l].T, GMLP_HEAD_DIM, axis=1)
        lw = dict(
            g_mix=g_norm_mix[l].reshape(1, D_MODEL), g_ffn=g_norm_ffn[l].reshape(1, D_MODEL),
            w_in=w_in[l].astype(BF16), v_gain=gmlp_v_gain[l].reshape(1, GMLP_WIDTH),
            ws=ws.astype(BF16), ws_sample=ws_sample.astype(BF16),
            bs=bs, bs_sample=jnp.tile(bs[:dec_len], (ns, 1)),
            prep=_s5_prep(s5_a_re[l], s5_a_im[l], s5_log_dt[l], s5_b_re[l], s5_b_im[l],
                          s5_c_re[l], s5_c_im[l]),
            d_tiled=jnp.tile(s5_d[l].reshape(GROUP_BLOCKS, 1, LANES), (1, 1, S5_CHUNK)),
            w_glu=s5_w_glu[l].astype(BF16), b_glu=s5_b_glu[l].reshape(1, SSM_WIDTH),
            w_out=w_out[l].astype(BF16), wr=wr, b_r=b_r,
            wg=wg_all, wu=wu_all, wd=wd_all, expert0=l * N_EXPERTS,
            g_final=g_fin,
        )
        last = l == DEPTH - 1
        mod_p = mod_all[l, :nb].reshape(nb, 1, 6 * D_MODEL)
        mod_s = jnp.repeat(mod_all[l, nb:nb + ns], dec_len, axis=0).reshape(1, ns * dec_len, 6 * D_MODEL)
        (xp, ffn_p), hp_re, hp_im, _ = _layer(xp, mod_p, lw, zeros_p, zeros_p, seq_len=seq_len, nseq=nb,
                                              sample=False, final_norm=last, ffn=ffn_p)
        xs, hs_re, hs_im, vs = _layer(xs, mod_s, lw, _state_in(state_s5_re[l], 1, ns),
                                      _state_in(state_s5_im[l], 1, ns), seq_len=dec_len, nseq=ns,
                                      sample=True, final_norm=last)
        sp_re.append(_state_out(hp_re, nb))
        sp_im.append(_state_out(hp_im, nb))
        ss_re.append(_state_out(hs_re, ns))
        ss_im.append(_state_out(hs_im, ns))
        v_new.append(vs.reshape(ns, dec_len, GMLP_WIDTH))
    tm = min(MIX_TILE, seq_len)
    yp = _final(xp, ffn_p, g_fin, tm=tm, tiles_per_mod=seq_len // tm)
    return (yp.reshape(nb, seq_len, D_MODEL), xs.reshape(ns, dec_len, D_MODEL),
            jnp.stack(sp_re), jnp.stack(sp_im), jnp.stack(ss_re), jnp.stack(ss_im), jnp.stack(v_new))
```

```python
import functools

import jax
import jax.numpy as jnp
from jax import lax
from jax.experimental import pallas as pl
from jax.experimental.pallas import tpu as pltpu

F32 = jnp.float32
BF16 = jnp.bfloat16

D_MODEL = 1024
DEPTH = 2
CHUNK = 64
GMLP_CHUNK = 128
GMLP_WIDTH = 512
GMLP_HEADS = 4
GMLP_HEAD_DIM = 128
SSM_WIDTH = 512
SSM_GROUP = 16
SSM_GROUPS = 32
SSM_STATE = 64
IN_WIDTH = 1536
N_EXPERTS = 16
EXPERTS_PER_GROUP = 4
N_EXPERT_GROUPS = 4
D_EXPERT = 512
EPS = 1e-6

LANES = 128
S5_CHUNK = 16
GROUP_BLOCKS = 4
GROUPS_PER_BLOCK = SSM_GROUPS // GROUP_BLOCKS
STATE_BLOCK = GROUPS_PER_BLOCK * SSM_STATE
S5_ROW = S5_CHUNK * LANES
ADA_ROWS = 16
PAIRS_PER_GROUP = 6
N_CLASSES = N_EXPERT_GROUPS * PAIRS_PER_GROUP
CLASS_ROWS = 32
_PAIRS = [(a, b) for a in range(EXPERTS_PER_GROUP) for b in range(a + 1, EXPERTS_PER_GROUP)]
CLASS_LO = [g * EXPERTS_PER_GROUP + a for g in range(N_EXPERT_GROUPS) for a, _ in _PAIRS]
CLASS_HI = [g * EXPERTS_PER_GROUP + b for g in range(N_EXPERT_GROUPS) for _, b in _PAIRS]
MOE_TILE = 256
MIX_TILE = 512
RUN_ALIGN = 8
SORT_ROWS = 768
XS_WIDTH = D_MODEL + LANES
VMEM_LIMIT = 56 * 1024 * 1024


def _cparams(*sem):
    return pltpu.CompilerParams(dimension_semantics=sem, vmem_limit_bytes=VMEM_LIMIT)


def _ada_kernel(c_ref, w_ref, b_ref, o_ref):
    c = c_ref[...]
    s = (c * jax.nn.sigmoid(c)).astype(BF16)
    o_ref[0] = jnp.dot(s, w_ref[0].astype(BF16), preferred_element_type=F32) + b_ref[0]


def _ada(c_all, w_ada, b_ada):
    nblk = 6
    return pl.pallas_call(
        _ada_kernel,
        grid=(DEPTH, nblk),
        in_specs=[
            pl.BlockSpec((ADA_ROWS, D_MODEL), lambda l, j: (0, 0)),
            pl.BlockSpec((1, D_MODEL, D_MODEL), lambda l, j: (l, 0, j)),
            pl.BlockSpec((1, 1, D_MODEL), lambda l, j: (l, 0, j)),
        ],
        out_specs=pl.BlockSpec((1, ADA_ROWS, D_MODEL), lambda l, j: (l, 0, j)),
        out_shape=jax.ShapeDtypeStruct((DEPTH, ADA_ROWS, 6 * D_MODEL), F32),
        compiler_params=_cparams("parallel", "parallel"),
        name="ada",
    )(c_all, w_ada, b_ada.reshape(DEPTH, 1, 6 * D_MODEL))


def _prep_kernel(are_ref, aim_ref, ldt_ref, bre_ref, bim_ref, cre_ref, cim_ref,
                 msre_ref, msim_ref, nt_ref, wrev_ref, a16re_ref, a16im_ref):
    a_re = are_ref[0]
    a_im = aim_ref[0]
    dt = jnp.exp(ldt_ref[0])
    rho = a_re * dt
    th = a_im * dt
    kk = jnp.minimum(lax.broadcasted_iota(jnp.int32, (24, STATE_BLOCK), 0), S5_CHUNK).astype(F32)
    mag = jnp.exp(kk * rho)
    pw_re = mag * jnp.cos(kk * th)
    pw_im = mag * jnp.sin(kk * th)

    lb_re = pw_re[1:2]
    lb_im = pw_im[1:2]
    num_re = lb_re - 1.0
    num_im = lb_im
    den = a_re * a_re + a_im * a_im
    coef_re = (num_re * a_re + num_im * a_im) / den
    coef_im = (num_im * a_re - num_re * a_im) / den
    b_re = bre_ref[0]
    b_im = bim_ref[0]
    bb_re = coef_re * b_re - coef_im * b_im
    bb_im = coef_re * b_im + coef_im * b_re

    rows = lax.broadcasted_iota(jnp.int32, (LANES, STATE_BLOCK), 0)
    cols = lax.broadcasted_iota(jnp.int32, (LANES, STATE_BLOCK), 1)
    same_group = (rows >> 4) == (cols >> 6)

    def blockdiag(x16):
        return jnp.where(same_group, jnp.concatenate([x16] * GROUPS_PER_BLOCK, axis=0), 0.0)

    for s in range(S5_CHUNK):
        k = S5_CHUNK - 1 - s
        p_re = pw_re[k:k + 1]
        p_im = pw_im[k:k + 1]
        msre_ref[0, s * LANES:(s + 1) * LANES, :] = blockdiag(p_re * bb_re - p_im * bb_im).astype(BF16)
        msim_ref[0, s * LANES:(s + 1) * LANES, :] = blockdiag(p_re * bb_im + p_im * bb_re).astype(BF16)

    b_hi, b_lo = _split_bf16(jnp.concatenate([blockdiag(bb_re), blockdiag(bb_im)], axis=1))
    c_re = cre_ref[0]
    c_im = cim_ref[0]
    for k in range(S5_CHUNK + 1):
        p_re = pw_re[k:k + 1]
        p_im = pw_im[k:k + 1]
        cl = jnp.concatenate([blockdiag(c_re * p_re - c_im * p_im),
                              -blockdiag(c_re * p_im + c_im * p_re)], axis=1)
        if k >= 1:
            nt_ref[0, (k - 1) * LANES:k * LANES, :] = cl.astype(BF16)
        if k < S5_CHUNK:
            c_hi, c_lo = _split_bf16(cl)
            nt_dims = (((1,), (1,)), ((), ()))
            wl = (lax.dot_general(b_hi, c_hi, nt_dims, preferred_element_type=F32)
                  + lax.dot_general(b_hi, c_lo, nt_dims, preferred_element_type=F32)
                  + lax.dot_general(b_lo, c_hi, nt_dims, preferred_element_type=F32))
            j = S5_CHUNK - 1 - k
            wrev_ref[0, j * LANES:(j + 1) * LANES, :LANES] = wl.astype(BF16)
            if j >= 1:
                wrev_ref[0, (j - 1) * LANES:j * LANES, LANES:] = wl.astype(BF16)
    wrev_ref[0, (S5_CHUNK - 1) * LANES:, LANES:] = jnp.zeros((LANES, LANES), BF16)

    a16re_ref[0] = pw_re[S5_CHUNK:S5_CHUNK + 1]
    a16im_ref[0] = pw_im[S5_CHUNK:S5_CHUNK + 1]


def _s5_prep(a_re, a_im, log_dt, b_re, b_im, c_re, c_im):
    nstate = SSM_GROUPS * SSM_STATE

    def lane_row(v):
        return v.reshape(GROUP_BLOCKS, 1, STATE_BLOCK)

    def rows16(v):
        return v.reshape(SSM_GROUP, GROUP_BLOCKS, STATE_BLOCK).transpose(1, 0, 2)

    ldt = jnp.repeat(log_dt, SSM_STATE).reshape(SSM_GROUPS, SSM_STATE)
    bt_re = rows16(b_re.transpose(2, 0, 1).reshape(SSM_GROUP, nstate))
    bt_im = rows16(b_im.transpose(2, 0, 1).reshape(SSM_GROUP, nstate))
    ct_re = rows16(c_re.transpose(1, 0, 2).reshape(SSM_GROUP, nstate))
    ct_im = rows16(c_im.transpose(1, 0, 2).reshape(SSM_GROUP, nstate))
    row_spec = pl.BlockSpec((1, 1, STATE_BLOCK), lambda g: (g, 0, 0))
    r16_spec = pl.BlockSpec((1, SSM_GROUP, STATE_BLOCK), lambda g: (g, 0, 0))
    return pl.pallas_call(
        _prep_kernel,
        grid=(GROUP_BLOCKS,),
        in_specs=[row_spec, row_spec, row_spec, r16_spec, r16_spec, r16_spec, r16_spec],
        out_specs=[
            pl.BlockSpec((1, S5_ROW, STATE_BLOCK), lambda g: (g, 0, 0)),
            pl.BlockSpec((1, S5_ROW, STATE_BLOCK), lambda g: (g, 0, 0)),
            pl.BlockSpec((1, S5_ROW, 2 * STATE_BLOCK), lambda g: (g, 0, 0)),
            pl.BlockSpec((1, S5_ROW, 2 * LANES), lambda g: (g, 0, 0)),
            row_spec, row_spec,
        ],
        out_shape=[
            jax.ShapeDtypeStruct((GROUP_BLOCKS, S5_ROW, STATE_BLOCK), BF16),
            jax.ShapeDtypeStruct((GROUP_BLOCKS, S5_ROW, STATE_BLOCK), BF16),
            jax.ShapeDtypeStruct((GROUP_BLOCKS, S5_ROW, 2 * STATE_BLOCK), BF16),
            jax.ShapeDtypeStruct((GROUP_BLOCKS, S5_ROW, 2 * LANES), BF16),
            jax.ShapeDtypeStruct((GROUP_BLOCKS, 1, STATE_BLOCK), F32),
            jax.ShapeDtypeStruct((GROUP_BLOCKS, 1, STATE_BLOCK), F32),
        ],
        compiler_params=_cparams("parallel"),
        name="s5_prep",
    )(lane_row(a_re), lane_row(a_im), lane_row(ldt), bt_re, bt_im, ct_re, ct_im)


def _rms(x, g):
    return x * lax.rsqrt(jnp.mean(x * x, axis=-1, keepdims=True) + EPS) * g


def _unsorted_ffn(zs_ref, prow_ref):
    tm = prow_ref.shape[1]
    zs = zs_ref[...].astype(BF16)
    pcol = jnp.concatenate([prow_ref[...].astype(F32), jnp.zeros((LANES - 1, tm), F32)], axis=0).T[:, :1]
    lanes = lax.broadcasted_iota(jnp.int32, (tm, SORT_ROWS), 1)
    pick = jnp.where(lanes == pcol.astype(jnp.int32), 1.0, 0.0).astype(BF16)
    return jnp.dot(pick, zs, preferred_element_type=F32)


def _mixer_in_kernel(x_ref, *refs, tm, fused, want_v):
    if fused:
        zs_ref, prow_ref, gf_ref = refs[:3]
        refs = refs[3:]
    mod_ref, g_ref, win_ref, vg_ref, ws_ref, bs_ref, a_ref, u_ref = refs[:8]
    refs = refs[8:]
    x = x_ref[...]
    if fused:
        x = x + gf_ref[0] * _unsorted_ffn(zs_ref, prow_ref)
        refs[0][...] = x
        refs = refs[1:]
    mod = mod_ref[0]
    shift = mod[:, :D_MODEL]
    scale = mod[:, D_MODEL:]
    h = _rms(x, g_ref[...]) * (1.0 + scale) + shift
    proj = jnp.dot(h.astype(BF16), win_ref[...], preferred_element_type=F32)
    z = jax.nn.gelu(proj[:, :2 * GMLP_WIDTH])
    u = z[:, :GMLP_WIDTH]
    v = z[:, GMLP_WIDTH:]
    vc = v - jnp.mean(v, axis=-1, keepdims=True)
    vn = vc * lax.rsqrt(jnp.mean(vc * vc, axis=-1, keepdims=True) + EPS) * vg_ref[...]
    if want_v:
        refs[0][...] = vn
    vb = vn.astype(BF16)
    bias = bs_ref[...]
    for c in range(tm // GMLP_CHUNK):
        r0 = c * GMLP_CHUNK
        for hh in range(GMLP_HEADS):
            l0 = hh * GMLP_HEAD_DIM
            mixed = jnp.dot(ws_ref[hh], vb[r0:r0 + GMLP_CHUNK, l0:l0 + GMLP_HEAD_DIM],
                            preferred_element_type=F32) + bias[:, l0:l0 + GMLP_HEAD_DIM]
            a_ref[r0:r0 + GMLP_CHUNK, l0:l0 + GMLP_HEAD_DIM] = (
                u[r0:r0 + GMLP_CHUNK, l0:l0 + GMLP_HEAD_DIM] * mixed).astype(BF16)
    for gb in range(GROUP_BLOCKS):
        l0 = 2 * GMLP_WIDTH + gb * LANES
        u_ref[gb] = proj[:, l0:l0 + LANES]


def _mixer_in(x2d, mod, g_mix, w_in, v_gain, ws, bs, *, tm, tiles_per_mod, want_v, ffn=None):
    t = x2d.shape[0]
    rmod = mod.shape[1]
    const2 = lambda i: (0, 0)
    tok = pl.BlockSpec((tm, D_MODEL), lambda i: (i, 0))
    in_specs = [tok]
    args = [x2d]
    if ffn is not None:
        in_specs += _ffn_specs(tm, rmod, tiles_per_mod)
        args += list(ffn)
    in_specs += [
        pl.BlockSpec((1, rmod, 2 * D_MODEL), lambda i: (i // tiles_per_mod, 0, 0)),
        pl.BlockSpec((1, D_MODEL), const2),
        pl.BlockSpec((D_MODEL, IN_WIDTH), const2),
        pl.BlockSpec((1, GMLP_WIDTH), const2),
        pl.BlockSpec((GMLP_HEADS, GMLP_CHUNK, GMLP_CHUNK), lambda i: (0, 0, 0)),
        pl.BlockSpec((GMLP_CHUNK, GMLP_WIDTH), const2),
    ]
    args += [mod, g_mix, w_in, v_gain, ws, bs]
    out_shape = [jax.ShapeDtypeStruct((t, GMLP_WIDTH), BF16),
                 jax.ShapeDtypeStruct((GROUP_BLOCKS, t, LANES), F32)]
    out_specs = [pl.BlockSpec((tm, GMLP_WIDTH), lambda i: (i, 0)),
                 pl.BlockSpec((GROUP_BLOCKS, tm, LANES), lambda i: (0, i, 0))]
    if ffn is not None:
        out_shape.append(jax.ShapeDtypeStruct((t, D_MODEL), F32))
        out_specs.append(tok)
    if want_v:
        out_shape.append(jax.ShapeDtypeStruct((t, GMLP_WIDTH), F32))
        out_specs.append(pl.BlockSpec((tm, GMLP_WIDTH), lambda i: (i, 0)))
    return pl.pallas_call(
        functools.partial(_mixer_in_kernel, tm=tm, fused=ffn is not None, want_v=want_v),
        grid=(t // tm,),
        in_specs=in_specs,
        out_specs=out_specs,
        out_shape=out_shape,
        compiler_params=_cparams("parallel"),
        name="mixer_in",
    )(*args)


def _ffn_specs(tm, rmod, tiles_per_mod):
    return [
        pl.BlockSpec((SORT_ROWS, D_MODEL), lambda i: (i, 0)),
        pl.BlockSpec((1, tm), lambda i: (0, i)),
        pl.BlockSpec((1, rmod, D_MODEL), lambda i: (i // tiles_per_mod, 0, 0)),
    ]


def _final_kernel(x_ref, zs_ref, prow_ref, gf_ref, g_ref, o_ref):
    x = x_ref[...] + gf_ref[0] * _unsorted_ffn(zs_ref, prow_ref)
    o_ref[...] = _rms(x, g_ref[...])


def _final(xmid, ffn, g_final, *, tm, tiles_per_mod):
    t = xmid.shape[0]
    rmod = ffn[2].shape[1]
    tok = pl.BlockSpec((tm, D_MODEL), lambda i: (i, 0))
    return pl.pallas_call(
        _final_kernel,
        grid=(t // tm,),
        in_specs=[tok] + _ffn_specs(tm, rmod, tiles_per_mod) + [pl.BlockSpec((1, D_MODEL), lambda i: (0, 0))],
        out_specs=tok,
        out_shape=jax.ShapeDtypeStruct((t, D_MODEL), F32),
        compiler_params=_cparams("parallel"),
        name="final_norm",
    )(xmid, *ffn, g_final)


def _s5_kernel(u_ref, msre_ref, msim_ref, nt_ref, wrev_ref, a16re_ref, a16im_ref, d_ref,
               h0re_ref, h0im_ref, y_ref, hnre_ref, hnim_ref, *scratch, scan):
    rows = u_ref.shape[1] // S5_CHUNK
    u = jnp.concatenate([u_ref[0, pl.ds(s, rows, stride=S5_CHUNK), :] for s in range(S5_CHUNK)], axis=1)
    ub = u.astype(BF16)
    s_re = jnp.dot(ub, msre_ref[0], preferred_element_type=F32)
    s_im = jnp.dot(ub, msim_ref[0], preferred_element_type=F32)
    a_re = a16re_ref[0]
    a_im = a16im_ref[0]
    h0_re = h0re_ref[0, 0]
    h0_im = h0im_ref[0, 0]
    if scan:
        sre_scr, sim_scr, hre_scr, him_scr = scratch
        sre_scr[...] = s_re
        sim_scr[...] = s_im

        def body(r, carry):
            hr, hi = carry
            hre_scr[pl.ds(r, 1), :] = hr
            him_scr[pl.ds(r, 1), :] = hi
            sr = sre_scr[pl.ds(r, 1), :]
            si = sim_scr[pl.ds(r, 1), :]
            return (a_re * hr - a_im * hi + sr, a_re * hi + a_im * hr + si)

        hn_re, hn_im = lax.fori_loop(0, rows, body, (h0_re, h0_im), unroll=8)
        h_re = hre_scr[...]
        h_im = him_scr[...]
    else:
        h_re = h0_re
        h_im = h0_im
        hn_re = a_re * h_re - a_im * h_im + s_re
        hn_im = a_re * h_im + a_im * h_re + s_im
    hnre_ref[0, 0] = hn_re
    hnim_ref[0, 0] = hn_im
    hcat = jnp.concatenate([h_re, h_im], axis=1).astype(BF16)
    inter = lax.dot_general(hcat, nt_ref[0], (((1,), (1,)), ((), ())),
                            preferred_element_type=F32)
    d = d_ref[0]
    for t in range(0, S5_CHUNK, 2):
        k0 = (S5_CHUNK - 2 - t) * LANES
        pair = jnp.dot(ub[:, :(t + 2) * LANES], wrev_ref[0, k0:, :], preferred_element_type=F32)
        for step, intra in ((t + 1, pair[:, :LANES]), (t, pair[:, LANES:])):
            sl = slice(step * LANES, (step + 1) * LANES)
            y_ref[0, pl.ds(step, rows, stride=S5_CHUNK), :] = intra + inter[:, sl] + d[:, sl] * u[:, sl]


def _s5(u4, prep, d_tiled, h0_re, h0_im, *, rows, nblk, scan):
    ms_re, ms_im, nt, wrev, a16_re, a16_im = prep
    rh = h0_re.shape[2]
    wspec = lambda shape: pl.BlockSpec((1,) + shape, lambda g, b: (g, 0, 0))
    hspec = pl.BlockSpec((1, 1, rh, STATE_BLOCK), lambda g, b: (g, b, 0, 0))
    scratch = [pltpu.VMEM((rows, STATE_BLOCK), F32)] * 4 if scan else []
    return pl.pallas_call(
        functools.partial(_s5_kernel, scan=scan),
        grid=(GROUP_BLOCKS, nblk),
        in_specs=[
            pl.BlockSpec((1, rows * S5_CHUNK, LANES), lambda g, b: (g, b, 0)),
            wspec((S5_ROW, STATE_BLOCK)), wspec((S5_ROW, STATE_BLOCK)),
            wspec((S5_ROW, 2 * STATE_BLOCK)), wspec((S5_ROW, 2 * LANES)),
            wspec((1, STATE_BLOCK)), wspec((1, STATE_BLOCK)), wspec((1, S5_ROW)),
            hspec, hspec,
        ],
        out_specs=[pl.BlockSpec((1, rows * S5_CHUNK, LANES), lambda g, b: (g, b, 0)), hspec, hspec],
        out_shape=[
            jax.ShapeDtypeStruct(u4.shape, F32),
            jax.ShapeDtypeStruct(h0_re.shape, F32),
            jax.ShapeDtypeStruct(h0_re.shape, F32),
        ],
        scratch_shapes=scratch,
        compiler_params=_cparams("parallel", "parallel"),
        name="s5",
    )(u4, ms_re, ms_im, nt, wrev, a16_re, a16_im, d_tiled, h0_re, h0_im)


def _split_bf16(x):
    hi = x.astype(BF16)
    return hi, (x - hi.astype(F32)).astype(BF16)


def _top2_of4(a):
    m1 = jnp.maximum(jnp.maximum(a[0], a[1]), jnp.maximum(a[2], a[3]))
    i1 = jnp.where(a[0] == m1, 0, jnp.where(a[1] == m1, 1, jnp.where(a[2] == m1, 2, 3)))
    b = [jnp.where(i1 == j, -jnp.inf, a[j]) for j in range(4)]
    m2 = jnp.maximum(jnp.maximum(b[0], b[1]), jnp.maximum(b[2], b[3]))
    i2 = jnp.where(b[0] == m2, 0, jnp.where(b[1] == m2, 1, jnp.where(b[2] == m2, 2, 3)))
    return m1, i1, m2, i2


def _route_rows(h2, wr_ref, br_ref):
    h_hi, h_lo = _split_bf16(h2)
    parts = jnp.dot(jnp.concatenate([h_hi, h_lo], axis=1), wr_ref[...], preferred_element_type=F32)
    pt = parts.T
    lt = pt[:N_EXPERTS] + pt[N_EXPERTS:2 * N_EXPERTS] + br_ref[...]
    rows = [lt[e:e + 1] for e in range(N_EXPERTS)]
    mx = functools.reduce(jnp.maximum, rows)
    ex = [jnp.exp(r - mx) for r in rows]
    tot = functools.reduce(lambda p, q: p + q, ex)
    scores = [e / tot for e in ex]
    best = None
    for g in range(N_EXPERT_GROUPS):
        m1, i1, m2, i2 = _top2_of4(scores[g * EXPERTS_PER_GROUP:(g + 1) * EXPERTS_PER_GROUP])
        cand = (m1 + m2, m1, i1 + g * EXPERTS_PER_GROUP, m2, i2 + g * EXPERTS_PER_GROUP)
        if best is None:
            best = cand
        else:
            better = cand[0] > best[0]
            best = tuple(jnp.where(better, c, b) for c, b in zip(cand, best))
    _, v1, e1, v2, e2 = best
    den = v1 + v2
    w1 = v1 / den
    w2 = v2 / den
    first_lo = e1 < e2
    return (jnp.where(first_lo, e1, e2), jnp.where(first_lo, e2, e1),
            jnp.where(first_lo, w1, w2), jnp.where(first_lo, w2, w1))


def _mixer_out_kernel(x_ref, a_ref, y_ref, mod_ref, g_ref, wglu_ref, bglu_ref, wout_ref,
                      wr_ref, br_ref, *rest, route, tm, nsub):
    for j in range(nsub):
        _mixer_out_tile(j, slice(j * tm, (j + 1) * tm), x_ref, a_ref, y_ref, mod_ref, g_ref, wglu_ref,
                        bglu_ref, wout_ref, wr_ref, br_ref, rest, route)


def _mixer_out_tile(j, r, x_ref, a_ref, y_ref, mod_ref, g_ref, wglu_ref, bglu_ref, wout_ref,
                    wr_ref, br_ref, rest, route):
    x = x_ref[r, :]
    mod = mod_ref[0]
    gate_m = mod[:, :D_MODEL]
    shift_f = mod[:, D_MODEL:2 * D_MODEL]
    scale_f = mod[:, 2 * D_MODEL:]
    ys = jax.nn.gelu(jnp.concatenate([y_ref[gb, r, :] for gb in range(GROUP_BLOCKS)], axis=1))
    glu = jnp.dot(ys.astype(BF16), wglu_ref[...], preferred_element_type=F32) + bglu_ref[...]
    b_out = ys * jax.nn.sigmoid(glu)
    mixed = jnp.concatenate([a_ref[r, :], b_out.astype(BF16)], axis=1)
    xmid = x + gate_m * jnp.dot(mixed, wout_ref[...], preferred_element_type=F32)
    h2 = _rms(xmid, g_ref[...]) * (1.0 + scale_f) + shift_f
    lo, hi, glo, ghi = _route_rows(h2, wr_ref, br_ref)
    if not route:
        xmid_ref, h2_ref, lo_ref, hi_ref, glo_ref, ghi_ref = rest
        xmid_ref[r, :] = xmid
        h2_ref[r, :] = h2
        lo_ref[:, r] = lo
        hi_ref[:, r] = hi
        glo_ref[:, r] = glo
        ghi_ref[:, r] = ghi
        return

    tri_ref, ltri_ref, xmid_ref, xs_ref, prow_ref, cnt_ref = rest
    tm = x.shape[0]
    xmid_ref[r, :] = xmid
    a = lo & (EXPERTS_PER_GROUP - 1)
    b = hi & (EXPERTS_PER_GROUP - 1)
    pair = jnp.where(a == 0, 0, jnp.where(a == 1, 3, 5)) + (b - a - 1)
    cls = (lo >> 2) * PAIRS_PER_GROUP + pair
    onehot = lax.broadcasted_iota(jnp.int32, (CLASS_ROWS, tm), 0) == cls
    prefix = jnp.dot(jnp.where(onehot, 1.0, 0.0).astype(BF16), tri_ref[...],
                     preferred_element_type=F32)
    total = prefix[:, tm - 1:tm]
    cnt_ref[j] = jnp.broadcast_to(total, (CLASS_ROWS, LANES))
    groups = jnp.floor((total + (RUN_ALIGN - 1)) * (1.0 / RUN_ALIGN))
    before = jnp.dot(ltri_ref[...], jnp.broadcast_to(groups, (CLASS_ROWS, LANES)).astype(BF16),
                     preferred_element_type=F32)[:, :1] * RUN_ALIGN
    prow = jnp.sum(jnp.where(onehot, before + prefix - 1.0, 0.0), axis=0, keepdims=True).astype(jnp.int32)
    prow_ref[:, r] = prow
    pick = jnp.where(lax.broadcasted_iota(jnp.int32, (SORT_ROWS, tm), 0) == prow, 1.0, 0.0).astype(BF16)
    glo_hi, glo_lo = _split_bf16(glo)
    ghi_hi, ghi_lo = _split_bf16(ghi)
    gates = jnp.concatenate([glo_hi.astype(F32), glo_lo.astype(F32), ghi_hi.astype(F32), ghi_lo.astype(F32),
                             jnp.zeros((LANES - 4, tm), F32)], axis=0).T
    payload = jnp.concatenate([h2.astype(BF16), gates.astype(BF16)], axis=1)
    xs_ref[j * SORT_ROWS:(j + 1) * SORT_ROWS, :] = jnp.dot(pick, payload, preferred_element_type=F32)


def _mixer_out(x2d, a_out, y4, mod, g_ffn, w_glu, b_glu, w_out, wr, b_r,
               *, tm, tiles_per_mod, route):
    t = x2d.shape[0]
    rmod = mod.shape[1]
    nsub = 2 if route and tiles_per_mod % 2 == 0 else 1
    rows = tm * nsub
    const2 = lambda i: (0, 0)
    tok = pl.BlockSpec((rows, D_MODEL), lambda i: (i, 0))
    row = pl.BlockSpec((1, rows), lambda i: (0, i))
    in_specs = [
        tok,
        pl.BlockSpec((rows, GMLP_WIDTH), lambda i: (i, 0)),
        pl.BlockSpec((GROUP_BLOCKS, rows, LANES), lambda i: (0, i, 0)),
        pl.BlockSpec((1, rmod, 3 * D_MODEL), lambda i: (i // (tiles_per_mod // nsub), 0, 0)),
        pl.BlockSpec((1, D_MODEL), const2),
        pl.BlockSpec((SSM_WIDTH, SSM_WIDTH), const2),
        pl.BlockSpec((1, SSM_WIDTH), const2),
        pl.BlockSpec((D_MODEL, D_MODEL), const2),
        pl.BlockSpec((2 * D_MODEL, LANES), const2),
        pl.BlockSpec((N_EXPERTS, 1), const2),
    ]
    args = [x2d, a_out, y4, mod, g_ffn, w_glu, b_glu, w_out, wr, b_r]
    if route:
        ids = jnp.arange(tm)
        cids = jnp.arange(CLASS_ROWS)
        args += [(ids[:, None] <= ids[None, :]).astype(BF16), (cids[None, :] < cids[:, None]).astype(BF16)]
        in_specs += [pl.BlockSpec((tm, tm), const2), pl.BlockSpec((CLASS_ROWS, CLASS_ROWS), const2)]
        out_specs = [tok, pl.BlockSpec((nsub * SORT_ROWS, XS_WIDTH), lambda i: (i, 0)), row,
                     pl.BlockSpec((nsub, CLASS_ROWS, LANES), lambda i: (i, 0, 0))]
        out_shape = [jax.ShapeDtypeStruct((t, D_MODEL), F32),
                     jax.ShapeDtypeStruct((t // tm * SORT_ROWS, XS_WIDTH), F32),
                     jax.ShapeDtypeStruct((1, t), jnp.int32),
                     jax.ShapeDtypeStruct((t // tm, CLASS_ROWS, LANES), F32)]
    else:
        out_specs = [tok, tok, row, row, row, row]
        out_shape = [jax.ShapeDtypeStruct((t, D_MODEL), F32),
                     jax.ShapeDtypeStruct((t, D_MODEL), F32),
                     jax.ShapeDtypeStruct((1, t), jnp.int32),
                     jax.ShapeDtypeStruct((1, t), jnp.int32),
                     jax.ShapeDtypeStruct((1, t), F32),
                     jax.ShapeDtypeStruct((1, t), F32)]
    return pl.pallas_call(
        functools.partial(_mixer_out_kernel, route=route, tm=tm, nsub=nsub),
        grid=(t // rows,),
        in_specs=in_specs,
        out_specs=out_specs,
        out_shape=out_shape,
        compiler_params=_cparams("parallel"),
        name="mixer_out",
    )(*args)


def _moe_runs_kernel(tlo_ref, thi_ref, tval_ref, psrc_ref, pdst_ref, plen_ref, kfirst_ref, kend_ref, used_ref,
                     xs_hbm, wgl_ref, wul_ref, wdl_ref, wgh_ref, wuh_ref, wdh_ref,
                     zs_hbm, buf_ref, obuf_ref, sem_in, sem_out, *, nsrc):
    i = pl.program_id(0)
    nt = pl.num_programs(0)
    slot = i % 2
    other = 1 - slot
    nv = tval_ref[i]
    nv_next = jnp.where(i + 1 < nt, tval_ref[jnp.minimum(i + 1, nt - 1)], 0)
    nv_prev = jnp.where(i >= 1, tval_ref[jnp.maximum(i - 1, 0)], 0)
    nv_prev2 = jnp.where(i >= 2, tval_ref[jnp.maximum(i - 2, 0)], 0)

    def rows8(v):
        return pl.multiple_of(v, RUN_ALIGN)

    def for_pieces(tile, fn):
        def body(k, c):
            n = plen_ref[tile * nsrc + k]

            @pl.when(n > 0)
            def _():
                fn(rows8(psrc_ref[tile * nsrc + k]), rows8(pdst_ref[tile * nsrc + k]), rows8(n))

            return c

        lax.fori_loop(kfirst_ref[tile], kend_ref[tile], body, 0)

    def gather_start(tile, sl):
        for_pieces(tile, lambda src, dst, n: pltpu.make_async_copy(
            xs_hbm.at[pl.ds(src, n), :], buf_ref.at[sl, pl.ds(dst, n), :], sem_in.at[sl]).start())

    def gather_wait(sl, n):
        pltpu.make_async_copy(xs_hbm.at[pl.ds(0, rows8(n)), :], buf_ref.at[sl, pl.ds(0, rows8(n)), :],
                              sem_in.at[sl]).wait()

    def scatter_start(tile, sl):
        for_pieces(tile, lambda src, dst, n: pltpu.make_async_copy(
            obuf_ref.at[sl, pl.ds(dst, n), :], zs_hbm.at[pl.ds(src, n), :], sem_out.at[sl]).start())

    def scatter_wait(sl, n):
        pltpu.make_async_copy(obuf_ref.at[sl, pl.ds(0, rows8(n)), :], zs_hbm.at[pl.ds(0, rows8(n)), :],
                              sem_out.at[sl]).wait()

    @pl.when(i == 0)
    def _():
        buf_ref[...] = jnp.zeros_like(buf_ref)
        gather_start(0, 0)
        obuf_ref[1] = jnp.zeros((MOE_TILE, D_MODEL), F32)

        def tail(k, start):
            used = rows8(used_ref[k])
            n = rows8(SORT_ROWS - used)
            copy = pltpu.make_async_copy(obuf_ref.at[1, pl.ds(0, n), :],
                                         zs_hbm.at[pl.ds(rows8(k * SORT_ROWS + used), n), :], sem_out.at[1])

            @pl.when(n > 0)
            def _():
                copy.start() if start else copy.wait()

        for start in (True, False):
            lax.fori_loop(0, nsrc, lambda k, c: (tail(k, start), c)[1], 0)

    @pl.when(nv_next > 0)
    def _():
        gather_start(i + 1, other)

    @pl.when(nv_prev2 > 0)
    def _():
        scatter_wait(slot, nv_prev2)

    @pl.when(nv > 0)
    def _():
        gather_wait(slot, nv)
        buf = buf_ref[slot]
        h = buf[:, :D_MODEL].astype(BF16)
        glo = buf[:, D_MODEL:D_MODEL + 1] + buf[:, D_MODEL + 1:D_MODEL + 2]
        ghi = buf[:, D_MODEL + 2:D_MODEL + 3] + buf[:, D_MODEL + 3:D_MODEL + 4]
        ffn = None
        for wg, wu, wd, gate in ((wgl_ref, wul_ref, wdl_ref, glo), (wgh_ref, wuh_ref, wdh_ref, ghi)):
            he = (jax.nn.silu(jnp.dot(h, wg[0], preferred_element_type=F32))
                  * jnp.dot(h, wu[0], preferred_element_type=F32))
            y = gate * jnp.dot(he.astype(BF16), wd[0], preferred_element_type=F32)
            ffn = y if ffn is None else ffn + y
        obuf_ref[slot] = ffn.astype(BF16).astype(F32)
        scatter_start(i, slot)

    @pl.when(i == nt - 1)
    def _():
        @pl.when(nv_prev > 0)
        def _():
            scatter_wait(other, nv_prev)

        @pl.when(nv > 0)
        def _():
            scatter_wait(slot, nv)


def _moe_runs(xs, counts, wg, wu, wd, *, tm, expert0):
    nmix = counts.shape[0]
    assert SORT_ROWS - tm <= MOE_TILE
    max_rows = nmix * min(SORT_ROWS, tm + N_CLASSES * (RUN_ALIGN - 1))
    ntiles = -(-max_rows // MOE_TILE) + N_CLASSES
    n_kc = counts[:, :N_CLASSES, 0].astype(jnp.int32)
    len_kc = (n_kc + RUN_ALIGN - 1) // RUN_ALIGN * RUN_ALIGN
    off_kc = jnp.cumsum(len_kc, axis=1) - len_kc
    used_k = jnp.sum(len_kc, axis=1)
    start_kc = jnp.cumsum(len_kc, axis=0) - len_kc
    region_c = jnp.sum(len_kc, axis=0)
    tiles_c = (region_c + MOE_TILE - 1) // MOE_TILE
    tile_end = jnp.cumsum(tiles_c)
    tile_start = tile_end - tiles_c
    used = tile_end[-1]
    tid = jnp.arange(ntiles, dtype=jnp.int32)
    tcls = jnp.sum(tile_end[None, :] <= jnp.minimum(tid, used - 1)[:, None], axis=1).astype(jnp.int32)
    lo_row = (tid - tile_start[tcls]) * MOE_TILE
    tval = jnp.where(tid < used, jnp.clip(region_c[tcls] - lo_row, 0, MOE_TILE), 0)
    run_lo = start_kc[:, tcls].T
    run_hi = run_lo + len_kc[:, tcls].T
    piece_lo = jnp.maximum(run_lo, lo_row[:, None])
    piece_hi = jnp.minimum(run_hi, (lo_row + tval)[:, None])
    plen = jnp.maximum(piece_hi - piece_lo, 0)
    psrc = jnp.arange(nmix, dtype=jnp.int32)[None, :] * SORT_ROWS + off_kc[:, tcls].T + (piece_lo - run_lo)
    pdst = piece_lo - lo_row[:, None]
    flat = lambda v: jnp.where(plen > 0, v, 0).reshape(-1).astype(jnp.int32)
    tlo = jnp.asarray(CLASS_LO, jnp.int32)[tcls]
    thi = jnp.asarray(CLASS_HI, jnp.int32)[tcls]
    kfirst = jnp.sum(run_hi <= lo_row[:, None], axis=1).astype(jnp.int32)
    kend = jnp.sum(run_lo < (lo_row + tval)[:, None], axis=1).astype(jnp.int32)
    nprefetch = 9
    wspec_lo = lambda shape: pl.BlockSpec(shape, lambda i, tlo, *_: (expert0 + tlo[i], 0, 0))
    wspec_hi = lambda shape: pl.BlockSpec(shape, lambda i, tlo, thi, *_: (expert0 + thi[i], 0, 0))
    up = (1, D_MODEL, D_EXPERT)
    down = (1, D_EXPERT, D_MODEL)
    zs = pl.pallas_call(
        functools.partial(_moe_runs_kernel, nsrc=nmix),
        grid_spec=pltpu.PrefetchScalarGridSpec(
            num_scalar_prefetch=nprefetch,
            grid=(ntiles,),
            in_specs=[
                pl.BlockSpec(memory_space=pl.ANY),
                wspec_lo(up), wspec_lo(up), wspec_lo(down),
                wspec_hi(up), wspec_hi(up), wspec_hi(down),
            ],
            out_specs=pl.BlockSpec(memory_space=pl.ANY),
            scratch_shapes=[
                pltpu.VMEM((2, MOE_TILE, XS_WIDTH), F32),
                pltpu.VMEM((2, MOE_TILE, D_MODEL), F32),
                pltpu.SemaphoreType.DMA((2,)),
                pltpu.SemaphoreType.DMA((2,)),
            ],
        ),
        out_shape=jax.ShapeDtypeStruct((nmix * SORT_ROWS, D_MODEL), F32),
        compiler_params=_cparams("arbitrary"),
        name="moe_runs",
    )(tlo, thi, tval.astype(jnp.int32), flat(psrc), flat(pdst), flat(plen), kfirst, kend,
      used_k.astype(jnp.int32), xs, wg, wu, wd, wg, wu, wd)
    return zs


def _moe_kernel(h2_ref, gates_ref, wg_ref, wu_ref, wd_ref, xmid_ref, mod_ref, gfin_ref,
                o_ref, acc_ref, *, final_norm):
    e = pl.program_id(1)

    @pl.when(e == 0)
    def _():
        acc_ref[...] = jnp.zeros_like(acc_ref)

    h = h2_ref[...].astype(BF16)
    he = (jax.nn.silu(jnp.dot(h, wg_ref[0], preferred_element_type=F32))
          * jnp.dot(h, wu_ref[0], preferred_element_type=F32))
    y = jnp.dot(he.astype(BF16), wd_ref[0], preferred_element_type=F32)
    gates = gates_ref[...]
    lane = lax.broadcasted_iota(jnp.int32, gates.shape, 1)
    gcol = jnp.sum(jnp.where(lane == e, gates, 0.0), axis=1, keepdims=True)
    acc_ref[...] += gcol * y

    @pl.when(e == N_EXPERTS - 1)
    def _():
        x = xmid_ref[...] + mod_ref[0] * acc_ref[...]
        if final_norm:
            x = _rms(x, gfin_ref[...])
        o_ref[...] = x


def _moe(h2, gates, wg, wu, wd, xmid, mod, g_final, *, tm, tiles_per_mod, expert0, final_norm):
    t = h2.shape[0]
    rmod = mod.shape[1]
    tok = pl.BlockSpec((tm, D_MODEL), lambda i, e: (i, 0))
    return pl.pallas_call(
        functools.partial(_moe_kernel, final_norm=final_norm),
        grid=(t // tm, N_EXPERTS),
        in_specs=[
            tok,
            pl.BlockSpec((tm, N_EXPERTS), lambda i, e: (i, 0)),
            pl.BlockSpec((1, D_MODEL, D_EXPERT), lambda i, e: (expert0 + e, 0, 0)),
            pl.BlockSpec((1, D_MODEL, D_EXPERT), lambda i, e: (expert0 + e, 0, 0)),
            pl.BlockSpec((1, D_EXPERT, D_MODEL), lambda i, e: (expert0 + e, 0, 0)),
            tok,
            pl.BlockSpec((1, rmod, D_MODEL), lambda i, e: (i // tiles_per_mod, 0, 0)),
            pl.BlockSpec((1, D_MODEL), lambda i, e: (0, 0)),
        ],
        out_specs=tok,
        out_shape=jax.ShapeDtypeStruct((t, D_MODEL), F32),
        scratch_shapes=[pltpu.VMEM((tm, D_MODEL), F32)],
        compiler_params=_cparams("parallel", "arbitrary"),
        name="moe",
    )(h2, gates, wg, wu, wd, xmid, mod, g_final)


def _layer(x2d, mod, lw, h0_re, h0_im, *, seq_len, nseq, sample, final_norm, ffn=None):
    t = x2d.shape[0]
    if sample:
        tm, tiles_per_mod, tm_moe = t, 1, t
        s5_rows, s5_blocks = nseq, 1
    else:
        tm = min(MIX_TILE, seq_len)
        tiles_per_mod = seq_len // tm
        s5_rows, s5_blocks = seq_len // S5_CHUNK, nseq
    outs = _mixer_in(x2d, mod[..., :2 * D_MODEL], lw["g_mix"], lw["w_in"], lw["v_gain"],
                     lw["ws_sample"] if sample else lw["ws"], lw["bs_sample"] if sample else lw["bs"],
                     tm=tm, tiles_per_mod=tiles_per_mod, want_v=sample, ffn=ffn)
    a_out, u4 = outs[0], outs[1]
    if ffn is not None:
        x2d = outs[2]
    v_rows = outs[-1] if sample else None
    y4, hn_re, hn_im = _s5(u4, lw["prep"], lw["d_tiled"],
                           h0_re, h0_im, rows=s5_rows, nblk=s5_blocks, scan=not sample)
    mo = _mixer_out(x2d, a_out, y4, mod[..., 2 * D_MODEL:5 * D_MODEL],
                    lw["g_ffn"], lw["w_glu"], lw["b_glu"], lw["w_out"], lw["wr"], lw["b_r"],
                    tm=tm, tiles_per_mod=tiles_per_mod, route=not sample)
    if sample:
        xmid, h2, lo, hi, glo, ghi = mo
        eids = jnp.arange(N_EXPERTS, dtype=jnp.int32)[None, :]
        gates = (jnp.where(lo[0][:, None] == eids, glo[0][:, None], 0.0)
                 + jnp.where(hi[0][:, None] == eids, ghi[0][:, None], 0.0))
        x_new = _moe(h2, gates, lw["wg"], lw["wu"], lw["wd"], xmid, mod[..., 5 * D_MODEL:],
                     lw["g_final"], tm=tm_moe, tiles_per_mod=1, expert0=lw["expert0"],
                     final_norm=final_norm)
        return x_new, hn_re, hn_im, v_rows
    xmid, xs, prow, counts = mo
    zs = _moe_runs(xs, counts, lw["wg"], lw["wu"], lw["wd"], tm=tm, expert0=lw["expert0"])
    return (xmid, (zs, prow, mod[..., 5 * D_MODEL:])), hn_re, hn_im, None


def _state_out(h, nseq):
    return h.reshape(GROUP_BLOCKS, nseq, GROUPS_PER_BLOCK, SSM_STATE).transpose(1, 0, 2, 3).reshape(
        nseq, SSM_GROUPS, SSM_STATE)


def _state_in(h, nblk, rh):
    nseq = h.shape[0]
    return h.reshape(nseq, GROUP_BLOCKS, STATE_BLOCK).transpose(1, 0, 2).reshape(
        GROUP_BLOCKS, nblk, rh, STATE_BLOCK)


def kernel(x_prompt, x_sample, c_prompt, c_sample, state_s5_re, state_s5_im, w_ada, b_ada, g_norm_mix, g_norm_ffn, w_in, gmlp_v_gain, gmlp_w_spatial, gmlp_b_spatial, s5_a_re, s5_a_im, s5_log_dt, s5_b_re, s5_b_im, s5_c_re, s5_c_im, s5_d, s5_w_glu, s5_b_glu, w_out, w_router, b_router, w_gate, w_up, w_down, g_final):
    nb, seq_len, _ = x_prompt.shape
    ns, dec_len, _ = x_sample.shape
    assert dec_len == S5_CHUNK and ns * dec_len == GMLP_CHUNK and nb + ns <= ADA_ROWS
    assert seq_len % GMLP_CHUNK == 0 and MIX_TILE + N_CLASSES * (RUN_ALIGN - 1) <= SORT_ROWS

    c_all = jnp.concatenate([c_prompt, c_sample, jnp.zeros((ADA_ROWS - nb - ns, D_MODEL), F32)], axis=0)
    mod_all = _ada(c_all, w_ada, b_ada)

    pos = jnp.arange(GMLP_CHUNK)
    causal = (pos[None, :] // CHUNK) <= (pos[:, None] // CHUNK)
    wr_hi = w_router.astype(BF16)
    wr_lo = (w_router - wr_hi.astype(F32)).astype(BF16)
    wr = jnp.pad(jnp.concatenate([wr_hi, wr_lo], axis=1), ((0, 0), (0, LANES - 2 * N_EXPERTS)))
    wr = jnp.concatenate([wr, wr], axis=0)
    b_r = b_router.reshape(N_EXPERTS, 1)
    g_fin = g_final.reshape(1, D_MODEL)
    eye_s = jnp.eye(ns, dtype=F32)

    xp = x_prompt.reshape(nb * seq_len, D_MODEL)
    xs = x_sample.reshape(ns * dec_len, D_MODEL)
    zeros_p = jnp.zeros((GROUP_BLOCKS, nb, 1, STATE_BLOCK), F32)
    wg_all = w_gate.astype(BF16).reshape(DEPTH * N_EXPERTS, D_MODEL, D_EXPERT)
    wu_all = w_up.astype(BF16).reshape(DEPTH * N_EXPERTS, D_MODEL, D_EXPERT)
    wd_all = w_down.astype(BF16).reshape(DEPTH * N_EXPERTS, D_EXPERT, D_MODEL)
    sp_re, sp_im, ss_re, ss_im, v_new = [], [], [], [], []
    ffn_p = None
    for l in range(DEPTH):
        ws = jnp.where(causal[None], gmlp_w_spatial[l], 0.0)
        ws_sample = jnp.einsum("ab,hij->haibj", eye_s, ws[:, :dec_len, :dec_len]).reshape(
            GMLP_HEADS, GMLP_CHUNK, GMLP_CHUNK)
        bs = jnp.repeat(gmlp_b_spatial[l].T, GMLP_HEAD_DIM, axis=1)
        lw = dict(
            g_mix=g_norm_mix[l].reshape(1, D_MODEL), g_ffn=g_norm_ffn[l].reshape(1, D_MODEL),
            w_in=w_in[l].astype(BF16), v_gain=gmlp_v_gain[l].reshape(1, GMLP_WIDTH),
            ws=ws.astype(BF16), ws_sample=ws_sample.astype(BF16),
            bs=bs, bs_sample=jnp.tile(bs[:dec_len], (ns, 1)),
            prep=_s5_prep(s5_a_re[l], s5_a_im[l], s5_log_dt[l], s5_b_re[l], s5_b_im[l],
                          s5_c_re[l], s5_c_im[l]),
            d_tiled=jnp.tile(s5_d[l].reshape(GROUP_BLOCKS, 1, LANES), (1, 1, S5_CHUNK)),
            w_glu=s5_w_glu[l].astype(BF16), b_glu=s5_b_glu[l].reshape(1, SSM_WIDTH),
            w_out=w_out[l].astype(BF16), wr=wr, b_r=b_r,
            wg=wg_all, wu=wu_all, wd=wd_all, expert0=l * N_EXPERTS,
            g_final=g_fin,
        )
        last = l == DEPTH - 1
        mod_p = mod_all[l, :nb].reshape(nb, 1, 6 * D_MODEL)
        mod_s = jnp.repeat(mod_all[l, nb:nb + ns], dec_len, axis=0).reshape(1, ns * dec_len, 6 * D_MODEL)
        (xp, ffn_p), hp_re, hp_im, _ = _layer(xp, mod_p, lw, zeros_p, zeros_p, seq_len=seq_len, nseq=nb,
                                              sample=False, final_norm=last, ffn=ffn_p)
        xs, hs_re, hs_im, vs = _layer(xs, mod_s, lw, _state_in(state_s5_re[l], 1, ns),
                                      _state_in(state_s5_im[l], 1, ns), seq_len=dec_len, nseq=ns,
                                      sample=True, final_norm=last)
        sp_re.append(_state_out(hp_re, nb))
        sp_im.append(_state_out(hp_im, nb))
        ss_re.append(_state_out(hs_re, ns))
        ss_im.append(_state_out(hs_im, ns))
        v_new.append(vs.reshape(ns, dec_len, GMLP_WIDTH))
    tm = min(MIX_TILE, seq_len)
    yp = _final(xp, ffn_p, g_fin, tm=tm, tiles_per_mod=seq_len // tm)
    return (yp.reshape(nb, seq_len, D_MODEL), xs.reshape(ns, dec_len, D_MODEL),
            jnp.stack(sp_re), jnp.stack(sp_im), jnp.stack(ss_re), jnp.stack(ss_im), jnp.stack(v_new))
```

```python
import functools

import jax
import jax.numpy as jnp
from jax import lax
from jax.experimental import pallas as pl
from jax.experimental.pallas import tpu as pltpu

F32 = jnp.float32
BF16 = jnp.bfloat16

D_MODEL = 1024
DEPTH = 2
CHUNK = 64
GMLP_CHUNK = 128
GMLP_WIDTH = 512
GMLP_HEADS = 4
GMLP_HEAD_DIM = 128
SSM_WIDTH = 512
SSM_GROUP = 16
SSM_GROUPS = 32
SSM_STATE = 64
IN_WIDTH = 1536
N_EXPERTS = 16
EXPERTS_PER_GROUP = 4
N_EXPERT_GROUPS = 4
D_EXPERT = 512
EPS = 1e-6

LANES = 128
S5_CHUNK = 16
GROUP_BLOCKS = 4
GROUPS_PER_BLOCK = SSM_GROUPS // GROUP_BLOCKS
STATE_BLOCK = GROUPS_PER_BLOCK * SSM_STATE
S5_ROW = S5_CHUNK * LANES
ADA_ROWS = 16
PAIRS_PER_GROUP = 6
N_CLASSES = N_EXPERT_GROUPS * PAIRS_PER_GROUP
CLASS_ROWS = 32
_PAIRS = [(a, b) for a in range(EXPERTS_PER_GROUP) for b in range(a + 1, EXPERTS_PER_GROUP)]
CLASS_LO = [g * EXPERTS_PER_GROUP + a for g in range(N_EXPERT_GROUPS) for a, _ in _PAIRS]
CLASS_HI = [g * EXPERTS_PER_GROUP + b for g in range(N_EXPERT_GROUPS) for _, b in _PAIRS]
MOE_TILE = 256
MIX_TILE = 512
RUN_ALIGN = 8
SORT_ROWS = 768
XS_WIDTH = D_MODEL + LANES
VMEM_LIMIT = 56 * 1024 * 1024


def _cparams(*sem):
    return pltpu.CompilerParams(dimension_semantics=sem, vmem_limit_bytes=VMEM_LIMIT)


def _ada_kernel(c_ref, w_ref, b_ref, o_ref):
    c = c_ref[...]
    s = (c * jax.nn.sigmoid(c)).astype(BF16)
    o_ref[0] = jnp.dot(s, w_ref[0].astype(BF16), preferred_element_type=F32) + b_ref[0]


def _ada(c_all, w_ada, b_ada):
    nblk = 6
    return pl.pallas_call(
        _ada_kernel,
        grid=(DEPTH, nblk),
        in_specs=[
            pl.BlockSpec((ADA_ROWS, D_MODEL), lambda l, j: (0, 0)),
            pl.BlockSpec((1, D_MODEL, D_MODEL), lambda l, j: (l, 0, j)),
            pl.BlockSpec((1, 1, D_MODEL), lambda l, j: (l, 0, j)),
        ],
        out_specs=pl.BlockSpec((1, ADA_ROWS, D_MODEL), lambda l, j: (l, 0, j)),
        out_shape=jax.ShapeDtypeStruct((DEPTH, ADA_ROWS, 6 * D_MODEL), F32),
        compiler_params=_cparams("parallel", "parallel"),
        name="ada",
    )(c_all, w_ada, b_ada.reshape(DEPTH, 1, 6 * D_MODEL))


def _prep_kernel(are_ref, aim_ref, ldt_ref, bre_ref, bim_ref, cre_ref, cim_ref,
                 msre_ref, msim_ref, nt_ref, wrev_ref, a16re_ref, a16im_ref):
    a_re = are_ref[0]
    a_im = aim_ref[0]
    dt = jnp.exp(ldt_ref[0])
    rho = a_re * dt
    th = a_im * dt
    kk = jnp.minimum(lax.broadcasted_iota(jnp.int32, (24, STATE_BLOCK), 0), S5_CHUNK).astype(F32)
    mag = jnp.exp(kk * rho)
    pw_re = mag * jnp.cos(kk * th)
    pw_im = mag * jnp.sin(kk * th)

    lb_re = pw_re[1:2]
    lb_im = pw_im[1:2]
    num_re = lb_re - 1.0
    num_im = lb_im
    den = a_re * a_re + a_im * a_im
    coef_re = (num_re * a_re + num_im * a_im) / den
    coef_im = (num_im * a_re - num_re * a_im) / den
    b_re = bre_ref[0]
    b_im = bim_ref[0]
    bb_re = coef_re * b_re - coef_im * b_im
    bb_im = coef_re * b_im + coef_im * b_re

    rows = lax.broadcasted_iota(jnp.int32, (LANES, STATE_BLOCK), 0)
    cols = lax.broadcasted_iota(jnp.int32, (LANES, STATE_BLOCK), 1)
    same_group = (rows >> 4) == (cols >> 6)

    def blockdiag(x16):
        return jnp.where(same_group, jnp.concatenate([x16] * GROUPS_PER_BLOCK, axis=0), 0.0)

    for s in range(S5_CHUNK):
        k = S5_CHUNK - 1 - s
        p_re = pw_re[k:k + 1]
        p_im = pw_im[k:k + 1]
        msre_ref[0, s * LANES:(s + 1) * LANES, :] = blockdiag(p_re * bb_re - p_im * bb_im).astype(BF16)
        msim_ref[0, s * LANES:(s + 1) * LANES, :] = blockdiag(p_re * bb_im + p_im * bb_re).astype(BF16)

    b_hi, b_lo = _split_bf16(jnp.concatenate([blockdiag(bb_re), blockdiag(bb_im)], axis=1))
    c_re = cre_ref[0]
    c_im = cim_ref[0]
    for k in range(S5_CHUNK + 1):
        p_re = pw_re[k:k + 1]
        p_im = pw_im[k:k + 1]
        cl = jnp.concatenate([blockdiag(c_re * p_re - c_im * p_im),
                              -blockdiag(c_re * p_im + c_im * p_re)], axis=1)
        if k >= 1:
            nt_ref[0, (k - 1) * LANES:k * LANES, :] = cl.astype(BF16)
        if k < S5_CHUNK:
            c_hi, c_lo = _split_bf16(cl)
            nt_dims = (((1,), (1,)), ((), ()))
            wl = (lax.dot_general(b_hi, c_hi, nt_dims, preferred_element_type=F32)
                  + lax.dot_general(b_hi, c_lo, nt_dims, preferred_element_type=F32)
                  + lax.dot_general(b_lo, c_hi, nt_dims, preferred_element_type=F32))
            j = S5_CHUNK - 1 - k
            wrev_ref[0, j * LANES:(j + 1) * LANES, :LANES] = wl.astype(BF16)
            if j >= 1:
                wrev_ref[0, (j - 1) * LANES:j * LANES, LANES:] = wl.astype(BF16)
    wrev_ref[0, (S5_CHUNK - 1) * LANES:, LANES:] = jnp.zeros((LANES, LANES), BF16)

    a16re_ref[0] = pw_re[S5_CHUNK:S5_CHUNK + 1]
    a16im_ref[0] = pw_im[S5_CHUNK:S5_CHUNK + 1]


def _s5_prep(a_re, a_im, log_dt, b_re, b_im, c_re, c_im):
    nstate = SSM_GROUPS * SSM_STATE

    def lane_row(v):
        return v.reshape(GROUP_BLOCKS, 1, STATE_BLOCK)

    def rows16(v):
        return v.reshape(SSM_GROUP, GROUP_BLOCKS, STATE_BLOCK).transpose(1, 0, 2)

    ldt = jnp.repeat(log_dt, SSM_STATE).reshape(SSM_GROUPS, SSM_STATE)
    bt_re = rows16(b_re.transpose(2, 0, 1).reshape(SSM_GROUP, nstate))
    bt_im = rows16(b_im.transpose(2, 0, 1).reshape(SSM_GROUP, nstate))
    ct_re = rows16(c_re.transpose(1, 0, 2).reshape(SSM_GROUP, nstate))
    ct_im = rows16(c_im.transpose(1, 0, 2).reshape(SSM_GROUP, nstate))
    row_spec = pl.BlockSpec((1, 1, STATE_BLOCK), lambda g: (g, 0, 0))
    r16_spec = pl.BlockSpec((1, SSM_GROUP, STATE_BLOCK), lambda g: (g, 0, 0))
    return pl.pallas_call(
        _prep_kernel,
        grid=(GROUP_BLOCKS,),
        in_specs=[row_spec, row_spec, row_spec, r16_spec, r16_spec, r16_spec, r16_spec],
        out_specs=[
            pl.BlockSpec((1, S5_ROW, STATE_BLOCK), lambda g: (g, 0, 0)),
            pl.BlockSpec((1, S5_ROW, STATE_BLOCK), lambda g: (g, 0, 0)),
            pl.BlockSpec((1, S5_ROW, 2 * STATE_BLOCK), lambda g: (g, 0, 0)),
            pl.BlockSpec((1, S5_ROW, 2 * LANES), lambda g: (g, 0, 0)),
            row_spec, row_spec,
        ],
        out_shape=[
            jax.ShapeDtypeStruct((GROUP_BLOCKS, S5_ROW, STATE_BLOCK), BF16),
            jax.ShapeDtypeStruct((GROUP_BLOCKS, S5_ROW, STATE_BLOCK), BF16),
            jax.ShapeDtypeStruct((GROUP_BLOCKS, S5_ROW, 2 * STATE_BLOCK), BF16),
            jax.ShapeDtypeStruct((GROUP_BLOCKS, S5_ROW, 2 * LANES), BF16),
            jax.ShapeDtypeStruct((GROUP_BLOCKS, 1, STATE_BLOCK), F32),
            jax.ShapeDtypeStruct((GROUP_BLOCKS, 1, STATE_BLOCK), F32),
        ],
        compiler_params=_cparams("parallel"),
        name="s5_prep",
    )(lane_row(a_re), lane_row(a_im), lane_row(ldt), bt_re, bt_im, ct_re, ct_im)


def _rms(x, g):
    return x * lax.rsqrt(jnp.mean(x * x, axis=-1, keepdims=True) + EPS) * g


def _unsorted_ffn(zs_ref, prow_ref):
    tm = prow_ref.shape[1]
    zs = zs_ref[...].astype(BF16)
    pcol = jnp.concatenate([prow_ref[...].astype(F32), jnp.zeros((LANES - 1, tm), F32)], axis=0).T[:, :1]
    lanes = lax.broadcasted_iota(jnp.int32, (tm, SORT_ROWS), 1)
    pick = jnp.where(lanes == pcol.astype(jnp.int32), 1.0, 0.0).astype(BF16)
    return jnp.dot(pick, zs, preferred_element_type=F32)


def _mixer_in_kernel(x_ref, *refs, tm, fused, want_v):
    if fused:
        zs_ref, prow_ref, gf_ref = refs[:3]
        refs = refs[3:]
    mod_ref, g_ref, win_ref, vg_ref, ws_ref, bs_ref, a_ref, u_ref = refs[:8]
    refs = refs[8:]
    x = x_ref[...]
    if fused:
        x = x + gf_ref[0] * _unsorted_ffn(zs_ref, prow_ref)
        refs[0][...] = x
        refs = refs[1:]
    mod = mod_ref[0]
    shift = mod[:, :D_MODEL]
    scale = mod[:, D_MODEL:]
    h = _rms(x, g_ref[...]) * (1.0 + scale) + shift
    proj = jnp.dot(h.astype(BF16), win_ref[...], preferred_element_type=F32)
    z = jax.nn.gelu(proj[:, :2 * GMLP_WIDTH])
    u = z[:, :GMLP_WIDTH]
    v = z[:, GMLP_WIDTH:]
    vc = v - jnp.mean(v, axis=-1, keepdims=True)
    vn = vc * lax.rsqrt(jnp.mean(vc * vc, axis=-1, keepdims=True) + EPS) * vg_ref[...]
    if want_v:
        refs[0][...] = vn
    vb = vn.astype(BF16)
    bias = bs_ref[...]
    for c in range(tm // GMLP_CHUNK):
        r0 = c * GMLP_CHUNK
        for hh in range(GMLP_HEADS):
            l0 = hh * GMLP_HEAD_DIM
            mixed = jnp.dot(ws_ref[hh], vb[r0:r0 + GMLP_CHUNK, l0:l0 + GMLP_HEAD_DIM],
                            preferred_element_type=F32) + bias[:, l0:l0 + GMLP_HEAD_DIM]
            a_ref[r0:r0 + GMLP_CHUNK, l0:l0 + GMLP_HEAD_DIM] = (
                u[r0:r0 + GMLP_CHUNK, l0:l0 + GMLP_HEAD_DIM] * mixed).astype(BF16)
    for gb in range(GROUP_BLOCKS):
        l0 = 2 * GMLP_WIDTH + gb * LANES
        u_ref[gb] = proj[:, l0:l0 + LANES]


def _mixer_in(x2d, mod, g_mix, w_in, v_gain, ws, bs, *, tm, tiles_per_mod, want_v, ffn=None):
    t = x2d.shape[0]
    rmod = mod.shape[1]
    const2 = lambda i: (0, 0)
    tok = pl.BlockSpec((tm, D_MODEL), lambda i: (i, 0))
    in_specs = [tok]
    args = [x2d]
    if ffn is not None:
        in_specs += _ffn_specs(tm, rmod, tiles_per_mod)
        args += list(ffn)
    in_specs += [
        pl.BlockSpec((1, rmod, 2 * D_MODEL), lambda i: (i // tiles_per_mod, 0, 0)),
        pl.BlockSpec((1, D_MODEL), const2),
        pl.BlockSpec((D_MODEL, IN_WIDTH), const2),
        pl.BlockSpec((1, GMLP_WIDTH), const2),
        pl.BlockSpec((GMLP_HEADS, GMLP_CHUNK, GMLP_CHUNK), lambda i: (0, 0, 0)),
        pl.BlockSpec((GMLP_CHUNK, GMLP_WIDTH), const2),
    ]
    args += [mod, g_mix, w_in, v_gain, ws, bs]
    out_shape = [jax.ShapeDtypeStruct((t, GMLP_WIDTH), BF16),
                 jax.ShapeDtypeStruct((GROUP_BLOCKS, t, LANES), F32)]
    out_specs = [pl.BlockSpec((tm, GMLP_WIDTH), lambda i: (i, 0)),
                 pl.BlockSpec((GROUP_BLOCKS, tm, LANES), lambda i: (0, i, 0))]
    if ffn is not None:
        out_shape.append(jax.ShapeDtypeStruct((t, D_MODEL), F32))
        out_specs.append(tok)
    if want_v:
        out_shape.append(jax.ShapeDtypeStruct((t, GMLP_WIDTH), F32))
        out_specs.append(pl.BlockSpec((tm, GMLP_WIDTH), lambda i: (i, 0)))
    return pl.pallas_call(
        functools.partial(_mixer_in_kernel, tm=tm, fused=ffn is not None, want_v=want_v),
        grid=(t // tm,),
        in_specs=in_specs,
        out_specs=out_specs,
        out_shape=out_shape,
        compiler_params=_cparams("parallel"),
        name="mixer_in",
    )(*args)


def _ffn_specs(tm, rmod, tiles_per_mod):
    return [
        pl.BlockSpec((SORT_ROWS, D_MODEL), lambda i: (i, 0)),
        pl.BlockSpec((1, tm), lambda i: (0, i)),
        pl.BlockSpec((1, rmod, D_MODEL), lambda i: (i // tiles_per_mod, 0, 0)),
    ]


def _final_kernel(x_ref, zs_ref, prow_ref, gf_ref, g_ref, o_ref):
    x = x_ref[...] + gf_ref[0] * _unsorted_ffn(zs_ref, prow_ref)
    o_ref[...] = _rms(x, g_ref[...])


def _final(xmid, ffn, g_final, *, tm, tiles_per_mod):
    t = xmid.shape[0]
    rmod = ffn[2].shape[1]
    tok = pl.BlockSpec((tm, D_MODEL), lambda i: (i, 0))
    return pl.pallas_call(
        _final_kernel,
        grid=(t // tm,),
        in_specs=[tok] + _ffn_specs(tm, rmod, tiles_per_mod) + [pl.BlockSpec((1, D_MODEL), lambda i: (0, 0))],
        out_specs=tok,
        out_shape=jax.ShapeDtypeStruct((t, D_MODEL), F32),
        compiler_params=_cparams("parallel"),
        name="final_norm",
    )(xmid, *ffn, g_final)


def _s5_kernel(u_ref, msre_ref, msim_ref, nt_ref, wrev_ref, a16re_ref, a16im_ref, d_ref,
               h0re_ref, h0im_ref, y_ref, hnre_ref, hnim_ref, *scratch, scan):
    rows = u_ref.shape[1] // S5_CHUNK
    u = jnp.concatenate([u_ref[0, pl.ds(s, rows, stride=S5_CHUNK), :] for s in range(S5_CHUNK)], axis=1)
    ub = u.astype(BF16)
    s_re = jnp.dot(ub, msre_ref[0], preferred_element_type=F32)
    s_im = jnp.dot(ub, msim_ref[0], preferred_element_type=F32)
    a_re = a16re_ref[0]
    a_im = a16im_ref[0]
    h0_re = h0re_ref[0, 0]
    h0_im = h0im_ref[0, 0]
    if scan:
        sre_scr, sim_scr, hre_scr, him_scr = scratch
        sre_scr[...] = s_re
        sim_scr[...] = s_im

        def body(r, carry):
            hr, hi = carry
            hre_scr[pl.ds(r, 1), :] = hr
            him_scr[pl.ds(r, 1), :] = hi
            sr = sre_scr[pl.ds(r, 1), :]
            si = sim_scr[pl.ds(r, 1), :]
            return (a_re * hr - a_im * hi + sr, a_re * hi + a_im * hr + si)

        hn_re, hn_im = lax.fori_loop(0, rows, body, (h0_re, h0_im), unroll=8)
        h_re = hre_scr[...]
        h_im = him_scr[...]
    else:
        h_re = h0_re
        h_im = h0_im
        hn_re = a_re * h_re - a_im * h_im + s_re
        hn_im = a_re * h_im + a_im * h_re + s_im
    hnre_ref[0, 0] = hn_re
    hnim_ref[0, 0] = hn_im
    hcat = jnp.concatenate([h_re, h_im], axis=1).astype(BF16)
    inter = lax.dot_general(hcat, nt_ref[0], (((1,), (1,)), ((), ())),
                            preferred_element_type=F32)
    d = d_ref[0]
    for t in range(0, S5_CHUNK, 2):
        k0 = (S5_CHUNK - 2 - t) * LANES
        pair = jnp.dot(ub[:, :(t + 2) * LANES], wrev_ref[0, k0:, :], preferred_element_type=F32)
        for step, intra in ((t + 1, pair[:, :LANES]), (t, pair[:, LANES:])):
            sl = slice(step * LANES, (step + 1) * LANES)
            y_ref[0, pl.ds(step, rows, stride=S5_CHUNK), :] = intra + inter[:, sl] + d[:, sl] * u[:, sl]


def _s5(u4, prep, d_tiled, h0_re, h0_im, *, rows, nblk, scan):
    ms_re, ms_im, nt, wrev, a16_re, a16_im = prep
    rh = h0_re.shape[2]
    wspec = lambda shape: pl.BlockSpec((1,) + shape, lambda g, b: (g, 0, 0))
    hspec = pl.BlockSpec((1, 1, rh, STATE_BLOCK), lambda g, b: (g, b, 0, 0))
    scratch = [pltpu.VMEM((rows, STATE_BLOCK), F32)] * 4 if scan else []
    return pl.pallas_call(
        functools.partial(_s5_kernel, scan=scan),
        grid=(GROUP_BLOCKS, nblk),
        in_specs=[
            pl.BlockSpec((1, rows * S5_CHUNK, LANES), lambda g, b: (g, b, 0)),
            wspec((S5_ROW, STATE_BLOCK)), wspec((S5_ROW, STATE_BLOCK)),
            wspec((S5_ROW, 2 * STATE_BLOCK)), wspec((S5_ROW, 2 * LANES)),
            wspec((1, STATE_BLOCK)), wspec((1, STATE_BLOCK)), wspec((1, S5_ROW)),
            hspec, hspec,
        ],
        out_specs=[pl.BlockSpec((1, rows * S5_CHUNK, LANES), lambda g, b: (g, b, 0)), hspec, hspec],
        out_shape=[
            jax.ShapeDtypeStruct(u4.shape, F32),
            jax.ShapeDtypeStruct(h0_re.shape, F32),
            jax.ShapeDtypeStruct(h0_re.shape, F32),
        ],
        scratch_shapes=scratch,
        compiler_params=_cparams("parallel", "parallel"),
        name="s5",
    )(u4, ms_re, ms_im, nt, wrev, a16_re, a16_im, d_tiled, h0_re, h0_im)


def _split_bf16(x):
    hi = x.astype(BF16)
    return hi, (x - hi.astype(F32)).astype(BF16)


def _top2_of4(a):
    m1 = jnp.maximum(jnp.maximum(a[0], a[1]), jnp.maximum(a[2], a[3]))
    i1 = jnp.where(a[0] == m1, 0, jnp.where(a[1] == m1, 1, jnp.where(a[2] == m1, 2, 3)))
    b = [jnp.where(i1 == j, -jnp.inf, a[j]) for j in range(4)]
    m2 = jnp.maximum(jnp.maximum(b[0], b[1]), jnp.maximum(b[2], b[3]))
    i2 = jnp.where(b[0] == m2, 0, jnp.where(b[1] == m2, 1, jnp.where(b[2] == m2, 2, 3)))
    return m1, i1, m2, i2


def _route_rows(h2, wr_ref, br_ref):
    h_hi, h_lo = _split_bf16(h2)
    parts = jnp.dot(jnp.concatenate([h_hi, h_lo], axis=1), wr_ref[...], preferred_element_type=F32)
    pt = parts.T
    lt = pt[:N_EXPERTS] + pt[N_EXPERTS:2 * N_EXPERTS] + br_ref[...]
    rows = [lt[e:e + 1] for e in range(N_EXPERTS)]
    mx = functools.reduce(jnp.maximum, rows)
    ex = [jnp.exp(r - mx) for r in rows]
    tot = functools.reduce(lambda p, q: p + q, ex)
    scores = [e / tot for e in ex]
    best = None
    for g in range(N_EXPERT_GROUPS):
        m1, i1, m2, i2 = _top2_of4(scores[g * EXPERTS_PER_GROUP:(g + 1) * EXPERTS_PER_GROUP])
        cand = (m1 + m2, m1, i1 + g * EXPERTS_PER_GROUP, m2, i2 + g * EXPERTS_PER_GROUP)
        if best is None:
            best = cand
        else:
            better = cand[0] > best[0]
            best = tuple(jnp.where(better, c, b) for c, b in zip(cand, best))
    _, v1, e1, v2, e2 = best
    den = v1 + v2
    w1 = v1 / den
    w2 = v2 / den
    first_lo = e1 < e2
    return (jnp.where(first_lo, e1, e2), jnp.where(first_lo, e2, e1),
            jnp.where(first_lo, w1, w2), jnp.where(first_lo, w2, w1))


def _mixer_out_kernel(x_ref, a_ref, y_ref, mod_ref, g_ref, wglu_ref, bglu_ref, wout_ref,
                      wr_ref, br_ref, tri_ref, ltri_ref, *outs, tm, nsub):
    for j in range(nsub):
        _mixer_out_tile(j, slice(j * tm, (j + 1) * tm), x_ref, a_ref, y_ref, mod_ref, g_ref, wglu_ref,
                        bglu_ref, wout_ref, wr_ref, br_ref, tri_ref, ltri_ref, *outs)


def _mixer_out_tile(j, r, x_ref, a_ref, y_ref, mod_ref, g_ref, wglu_ref, bglu_ref, wout_ref,
                    wr_ref, br_ref, tri_ref, ltri_ref, xmid_ref, xs_ref, prow_ref, cnt_ref):
    x = x_ref[r, :]
    mod = mod_ref[0]
    gate_m = mod[:, :D_MODEL]
    shift_f = mod[:, D_MODEL:2 * D_MODEL]
    scale_f = mod[:, 2 * D_MODEL:]
    ys = jax.nn.gelu(jnp.concatenate([y_ref[gb, r, :] for gb in range(GROUP_BLOCKS)], axis=1))
    glu = jnp.dot(ys.astype(BF16), wglu_ref[...], preferred_element_type=F32) + bglu_ref[...]
    b_out = ys * jax.nn.sigmoid(glu)
    mixed = jnp.concatenate([a_ref[r, :], b_out.astype(BF16)], axis=1)
    xmid = x + gate_m * jnp.dot(mixed, wout_ref[...], preferred_element_type=F32)
    h2 = _rms(xmid, g_ref[...]) * (1.0 + scale_f) + shift_f
    lo, hi, glo, ghi = _route_rows(h2, wr_ref, br_ref)
    tm = x.shape[0]
    xmid_ref[r, :] = xmid
    a = lo & (EXPERTS_PER_GROUP - 1)
    b = hi & (EXPERTS_PER_GROUP - 1)
    pair = jnp.where(a == 0, 0, jnp.where(a == 1, 3, 5)) + (b - a - 1)
    cls = (lo >> 2) * PAIRS_PER_GROUP + pair
    onehot = lax.broadcasted_iota(jnp.int32, (CLASS_ROWS, tm), 0) == cls
    prefix = jnp.dot(jnp.where(onehot, 1.0, 0.0).astype(BF16), tri_ref[...],
                     preferred_element_type=F32)
    total = prefix[:, tm - 1:tm]
    cnt_ref[j] = jnp.broadcast_to(total, (CLASS_ROWS, LANES))
    groups = jnp.floor((total + (RUN_ALIGN - 1)) * (1.0 / RUN_ALIGN))
    before = jnp.dot(ltri_ref[...], jnp.broadcast_to(groups, (CLASS_ROWS, LANES)).astype(BF16),
                     preferred_element_type=F32)[:, :1] * RUN_ALIGN
    prow = jnp.sum(jnp.where(onehot, before + prefix - 1.0, 0.0), axis=0, keepdims=True).astype(jnp.int32)
    prow_ref[:, r] = prow
    pick = jnp.where(lax.broadcasted_iota(jnp.int32, (SORT_ROWS, tm), 0) == prow, 1.0, 0.0).astype(BF16)
    glo_hi, glo_lo = _split_bf16(glo)
    ghi_hi, ghi_lo = _split_bf16(ghi)
    gates = jnp.concatenate([glo_hi.astype(F32), glo_lo.astype(F32), ghi_hi.astype(F32), ghi_lo.astype(F32),
                             jnp.zeros((LANES - 4, tm), F32)], axis=0).T
    payload = jnp.concatenate([h2.astype(BF16), gates.astype(BF16)], axis=1)
    xs_ref[j * SORT_ROWS:(j + 1) * SORT_ROWS, :] = jnp.dot(pick, payload, preferred_element_type=F32)


def _mixer_out(x2d, a_out, y4, mod, g_ffn, w_glu, b_glu, w_out, wr, b_r, *, tm, tiles_per_mod):
    t = x2d.shape[0]
    rmod = mod.shape[1]
    nsub = 2 if tiles_per_mod % 2 == 0 else 1
    rows = tm * nsub
    const2 = lambda i: (0, 0)
    tok = pl.BlockSpec((rows, D_MODEL), lambda i: (i, 0))
    ids = jnp.arange(tm)
    cids = jnp.arange(CLASS_ROWS)
    return pl.pallas_call(
        functools.partial(_mixer_out_kernel, tm=tm, nsub=nsub),
        grid=(t // rows,),
        in_specs=[
            tok,
            pl.BlockSpec((rows, GMLP_WIDTH), lambda i: (i, 0)),
            pl.BlockSpec((GROUP_BLOCKS, rows, LANES), lambda i: (0, i, 0)),
            pl.BlockSpec((1, rmod, 3 * D_MODEL), lambda i: (i // (tiles_per_mod // nsub), 0, 0)),
            pl.BlockSpec((1, D_MODEL), const2),
            pl.BlockSpec((SSM_WIDTH, SSM_WIDTH), const2),
            pl.BlockSpec((1, SSM_WIDTH), const2),
            pl.BlockSpec((D_MODEL, D_MODEL), const2),
            pl.BlockSpec((2 * D_MODEL, LANES), const2),
            pl.BlockSpec((N_EXPERTS, 1), const2),
            pl.BlockSpec((tm, tm), const2),
            pl.BlockSpec((CLASS_ROWS, CLASS_ROWS), const2),
        ],
        out_specs=[tok, pl.BlockSpec((nsub * SORT_ROWS, XS_WIDTH), lambda i: (i, 0)),
                   pl.BlockSpec((1, rows), lambda i: (0, i)),
                   pl.BlockSpec((nsub, CLASS_ROWS, LANES), lambda i: (i, 0, 0))],
        out_shape=[jax.ShapeDtypeStruct((t, D_MODEL), F32),
                   jax.ShapeDtypeStruct((t // tm * SORT_ROWS, XS_WIDTH), F32),
                   jax.ShapeDtypeStruct((1, t), jnp.int32),
                   jax.ShapeDtypeStruct((t // tm, CLASS_ROWS, LANES), F32)],
        compiler_params=_cparams("parallel"),
        name="mixer_out",
    )(x2d, a_out, y4, mod, g_ffn, w_glu, b_glu, w_out, wr, b_r,
      (ids[:, None] <= ids[None, :]).astype(BF16), (cids[None, :] < cids[:, None]).astype(BF16))


def _moe_runs_kernel(tlo_ref, thi_ref, tval_ref, psrc_ref, pdst_ref, plen_ref, kfirst_ref, kend_ref, used_ref,
                     xsp_hbm, xss_hbm, wgl_ref, wul_ref, wdl_ref, wgh_ref, wuh_ref, wdh_ref,
                     zsp_hbm, zss_hbm, buf_ref, obuf_ref, sem_in, sem_out, *, nsrc, nsrc_p):
    i = pl.program_id(0)
    nt = pl.num_programs(0)
    slot = i % 2
    other = 1 - slot
    nv = tval_ref[i]
    nv_next = jnp.where(i + 1 < nt, tval_ref[jnp.minimum(i + 1, nt - 1)], 0)
    nv_prev = jnp.where(i >= 1, tval_ref[jnp.maximum(i - 1, 0)], 0)
    nv_prev2 = jnp.where(i >= 2, tval_ref[jnp.maximum(i - 2, 0)], 0)

    def rows8(v):
        return pl.multiple_of(v, RUN_ALIGN)

    def for_pieces(tile, fn):
        def body(k, c):
            n = plen_ref[tile * nsrc + k]

            for sample, (xs_hbm, zs_hbm) in enumerate(((xsp_hbm, zsp_hbm), (xss_hbm, zss_hbm))):
                @pl.when(jnp.logical_and(n > 0, (k >= nsrc_p) == bool(sample)))
                def _():
                    fn(xs_hbm, zs_hbm, rows8(psrc_ref[tile * nsrc + k]), rows8(pdst_ref[tile * nsrc + k]),
                       rows8(n))

            return c

        lax.fori_loop(kfirst_ref[tile], kend_ref[tile], body, 0)

    def gather_start(tile, sl):
        for_pieces(tile, lambda xs_hbm, zs_hbm, src, dst, n: pltpu.make_async_copy(
            xs_hbm.at[pl.ds(src, n), :], buf_ref.at[sl, pl.ds(dst, n), :], sem_in.at[sl]).start())

    def gather_wait(sl, n):
        pltpu.make_async_copy(xsp_hbm.at[pl.ds(0, rows8(n)), :], buf_ref.at[sl, pl.ds(0, rows8(n)), :],
                              sem_in.at[sl]).wait()

    def scatter_start(tile, sl):
        for_pieces(tile, lambda xs_hbm, zs_hbm, src, dst, n: pltpu.make_async_copy(
            obuf_ref.at[sl, pl.ds(dst, n), :], zs_hbm.at[pl.ds(src, n), :], sem_out.at[sl]).start())

    def scatter_wait(sl, n):
        pltpu.make_async_copy(obuf_ref.at[sl, pl.ds(0, rows8(n)), :], zsp_hbm.at[pl.ds(0, rows8(n)), :],
                              sem_out.at[sl]).wait()

    @pl.when(i == 0)
    def _():
        buf_ref[...] = jnp.zeros_like(buf_ref)
        gather_start(0, 0)
        obuf_ref[1] = jnp.zeros((MOE_TILE, D_MODEL), F32)

        def tail(zs_hbm, k0, k, start):
            for first in range(0, SORT_ROWS, MOE_TILE):
                lo_row = rows8(jnp.maximum(used_ref[k], first))
                n = rows8(jnp.maximum(jnp.minimum(first + MOE_TILE, SORT_ROWS) - lo_row, 0))
                copy = pltpu.make_async_copy(obuf_ref.at[1, pl.ds(0, n), :],
                                             zs_hbm.at[pl.ds(rows8((k - k0) * SORT_ROWS + lo_row), n), :],
                                             sem_out.at[1])

                @pl.when(n > 0)
                def _():
                    copy.start() if start else copy.wait()

        for start in (True, False):
            lax.fori_loop(0, nsrc_p, lambda k, c: (tail(zsp_hbm, 0, k, start), c)[1], 0)
            lax.fori_loop(nsrc_p, nsrc, lambda k, c: (tail(zss_hbm, nsrc_p, k, start), c)[1], 0)

    @pl.when(nv_next > 0)
    def _():
        gather_start(i + 1, other)

    @pl.when(nv_prev2 > 0)
    def _():
        scatter_wait(slot, nv_prev2)

    @pl.when(nv > 0)
    def _():
        gather_wait(slot, nv)
        buf = buf_ref[slot]
        h = buf[:, :D_MODEL].astype(BF16)
        glo = buf[:, D_MODEL:D_MODEL + 1] + buf[:, D_MODEL + 1:D_MODEL + 2]
        ghi = buf[:, D_MODEL + 2:D_MODEL + 3] + buf[:, D_MODEL + 3:D_MODEL + 4]
        ffn = None
        for wg, wu, wd, gate in ((wgl_ref, wul_ref, wdl_ref, glo), (wgh_ref, wuh_ref, wdh_ref, ghi)):
            he = (jax.nn.silu(jnp.dot(h, wg[0], preferred_element_type=F32))
                  * jnp.dot(h, wu[0], preferred_element_type=F32))
            y = gate * jnp.dot(he.astype(BF16), wd[0], preferred_element_type=F32)
            ffn = y if ffn is None else ffn + y
        obuf_ref[slot] = ffn.astype(BF16).astype(F32)
        scatter_start(i, slot)

    @pl.when(i == nt - 1)
    def _():
        @pl.when(nv_prev > 0)
        def _():
            scatter_wait(other, nv_prev)

        @pl.when(nv > 0)
        def _():
            scatter_wait(slot, nv)


def _moe_runs(xs_p, counts_p, xs_s, counts_s, wg, wu, wd, *, max_rows, expert0):
    counts = jnp.concatenate([counts_p, counts_s], axis=0)
    nmix = counts.shape[0]
    nmix_p = counts_p.shape[0]
    ntiles = -(-max_rows // MOE_TILE) + N_CLASSES
    n_kc = counts[:, :N_CLASSES, 0].astype(jnp.int32)
    len_kc = (n_kc + RUN_ALIGN - 1) // RUN_ALIGN * RUN_ALIGN
    off_kc = jnp.cumsum(len_kc, axis=1) - len_kc
    used_k = jnp.sum(len_kc, axis=1)
    start_kc = jnp.cumsum(len_kc, axis=0) - len_kc
    region_c = jnp.sum(len_kc, axis=0)
    tiles_c = (region_c + MOE_TILE - 1) // MOE_TILE
    tile_end = jnp.cumsum(tiles_c)
    tile_start = tile_end - tiles_c
    used = tile_end[-1]
    tid = jnp.arange(ntiles, dtype=jnp.int32)
    tcls = jnp.sum(tile_end[None, :] <= jnp.minimum(tid, used - 1)[:, None], axis=1).astype(jnp.int32)
    lo_row = (tid - tile_start[tcls]) * MOE_TILE
    tval = jnp.where(tid < used, jnp.clip(region_c[tcls] - lo_row, 0, MOE_TILE), 0)
    run_lo = start_kc[:, tcls].T
    run_hi = run_lo + len_kc[:, tcls].T
    piece_lo = jnp.maximum(run_lo, lo_row[:, None])
    piece_hi = jnp.minimum(run_hi, (lo_row + tval)[:, None])
    plen = jnp.maximum(piece_hi - piece_lo, 0)
    kk = jnp.arange(nmix, dtype=jnp.int32)
    first_row = jnp.where(kk < nmix_p, kk, kk - nmix_p) * SORT_ROWS
    psrc = first_row[None, :] + off_kc[:, tcls].T + (piece_lo - run_lo)
    pdst = piece_lo - lo_row[:, None]
    flat = lambda v: jnp.where(plen > 0, v, 0).reshape(-1).astype(jnp.int32)
    tlo = jnp.asarray(CLASS_LO, jnp.int32)[tcls]
    thi = jnp.asarray(CLASS_HI, jnp.int32)[tcls]
    kfirst = jnp.sum(run_hi <= lo_row[:, None], axis=1).astype(jnp.int32)
    kend = jnp.sum(run_lo < (lo_row + tval)[:, None], axis=1).astype(jnp.int32)
    nprefetch = 9
    wspec_lo = lambda shape: pl.BlockSpec(shape, lambda i, tlo, *_: (expert0 + tlo[i], 0, 0))
    wspec_hi = lambda shape: pl.BlockSpec(shape, lambda i, tlo, thi, *_: (expert0 + thi[i], 0, 0))
    up = (1, D_MODEL, D_EXPERT)
    down = (1, D_EXPERT, D_MODEL)
    return pl.pallas_call(
        functools.partial(_moe_runs_kernel, nsrc=nmix, nsrc_p=nmix_p),
        grid_spec=pltpu.PrefetchScalarGridSpec(
            num_scalar_prefetch=nprefetch,
            grid=(ntiles,),
            in_specs=[
                pl.BlockSpec(memory_space=pl.ANY), pl.BlockSpec(memory_space=pl.ANY),
                wspec_lo(up), wspec_lo(up), wspec_lo(down),
                wspec_hi(up), wspec_hi(up), wspec_hi(down),
            ],
            out_specs=[pl.BlockSpec(memory_space=pl.ANY), pl.BlockSpec(memory_space=pl.ANY)],
            scratch_shapes=[
                pltpu.VMEM((2, MOE_TILE, XS_WIDTH), F32),
                pltpu.VMEM((2, MOE_TILE, D_MODEL), F32),
                pltpu.SemaphoreType.DMA((2,)),
                pltpu.SemaphoreType.DMA((2,)),
            ],
        ),
        out_shape=[jax.ShapeDtypeStruct((xs_p.shape[0], D_MODEL), F32),
                   jax.ShapeDtypeStruct((xs_s.shape[0], D_MODEL), F32)],
        compiler_params=_cparams("arbitrary"),
        name="moe_runs",
    )(tlo, thi, tval.astype(jnp.int32), flat(psrc), flat(pdst), flat(plen), kfirst, kend,
      used_k.astype(jnp.int32), xs_p, xs_s, wg, wu, wd, wg, wu, wd)


def _sorted_rows_bound(tm):
    return min(SORT_ROWS, tm + N_CLASSES * (RUN_ALIGN - 1))


def _mixers(x2d, mod, lw, h0_re, h0_im, *, seq_len, nseq, sample, ffn):
    t = x2d.shape[0]
    if sample:
        tm, tiles_per_mod = t, 1
        s5_rows, s5_blocks = nseq, 1
    else:
        tm = min(MIX_TILE, seq_len)
        tiles_per_mod = seq_len // tm
        s5_rows, s5_blocks = seq_len // S5_CHUNK, nseq
    outs = _mixer_in(x2d, mod[..., :2 * D_MODEL], lw["g_mix"], lw["w_in"], lw["v_gain"],
                     lw["ws_sample"] if sample else lw["ws"], lw["bs_sample"] if sample else lw["bs"],
                     tm=tm, tiles_per_mod=tiles_per_mod, want_v=sample, ffn=ffn)
    a_out, u4 = outs[0], outs[1]
    if ffn is not None:
        x2d = outs[2]
    v_rows = outs[-1] if sample else None
    y4, hn_re, hn_im = _s5(u4, lw["prep"], lw["d_tiled"],
                           h0_re, h0_im, rows=s5_rows, nblk=s5_blocks, scan=not sample)
    xmid, xs, prow, counts = _mixer_out(
        x2d, a_out, y4, mod[..., 2 * D_MODEL:5 * D_MODEL],
        lw["g_ffn"], lw["w_glu"], lw["b_glu"], lw["w_out"], lw["wr"], lw["b_r"],
        tm=tm, tiles_per_mod=tiles_per_mod)
    return xmid, xs, prow, counts, mod[..., 5 * D_MODEL:], hn_re, hn_im, v_rows


def _state_out(h, nseq):
    return h.reshape(GROUP_BLOCKS, nseq, GROUPS_PER_BLOCK, SSM_STATE).transpose(1, 0, 2, 3).reshape(
        nseq, SSM_GROUPS, SSM_STATE)


def _state_in(h, nblk, rh):
    nseq = h.shape[0]
    return h.reshape(nseq, GROUP_BLOCKS, STATE_BLOCK).transpose(1, 0, 2).reshape(
        GROUP_BLOCKS, nblk, rh, STATE_BLOCK)


def kernel(x_prompt, x_sample, c_prompt, c_sample, state_s5_re, state_s5_im, w_ada, b_ada, g_norm_mix, g_norm_ffn, w_in, gmlp_v_gain, gmlp_w_spatial, gmlp_b_spatial, s5_a_re, s5_a_im, s5_log_dt, s5_b_re, s5_b_im, s5_c_re, s5_c_im, s5_d, s5_w_glu, s5_b_glu, w_out, w_router, b_router, w_gate, w_up, w_down, g_final):
    nb, seq_len, _ = x_prompt.shape
    ns, dec_len, _ = x_sample.shape
    assert dec_len == S5_CHUNK and ns * dec_len == GMLP_CHUNK and nb + ns <= ADA_ROWS
    assert seq_len % GMLP_CHUNK == 0 and MIX_TILE + N_CLASSES * (RUN_ALIGN - 1) <= SORT_ROWS

    c_all = jnp.concatenate([c_prompt, c_sample, jnp.zeros((ADA_ROWS - nb - ns, D_MODEL), F32)], axis=0)
    mod_all = _ada(c_all, w_ada, b_ada)

    pos = jnp.arange(GMLP_CHUNK)
    causal = (pos[None, :] // CHUNK) <= (pos[:, None] // CHUNK)
    wr_hi = w_router.astype(BF16)
    wr_lo = (w_router - wr_hi.astype(F32)).astype(BF16)
    wr = jnp.pad(jnp.concatenate([wr_hi, wr_lo], axis=1), ((0, 0), (0, LANES - 2 * N_EXPERTS)))
    wr = jnp.concatenate([wr, wr], axis=0)
    b_r = b_router.reshape(N_EXPERTS, 1)
    g_fin = g_final.reshape(1, D_MODEL)
    eye_s = jnp.eye(ns, dtype=F32)

    xp = x_prompt.reshape(nb * seq_len, D_MODEL)
    xs = x_sample.reshape(ns * dec_len, D_MODEL)
    zeros_p = jnp.zeros((GROUP_BLOCKS, nb, 1, STATE_BLOCK), F32)
    wg_all = w_gate.astype(BF16).reshape(DEPTH * N_EXPERTS, D_MODEL, D_EXPERT)
    wu_all = w_up.astype(BF16).reshape(DEPTH * N_EXPERTS, D_MODEL, D_EXPERT)
    wd_all = w_down.astype(BF16).reshape(DEPTH * N_EXPERTS, D_EXPERT, D_MODEL)
    sp_re, sp_im, ss_re, ss_im, v_new = [], [], [], [], []
    tm = min(MIX_TILE, seq_len)
    tiles_p = nb * seq_len // tm
    ffn_p = ffn_s = None
    for l in range(DEPTH):
        ws = jnp.where(causal[None], gmlp_w_spatial[l], 0.0)
        ws_sample = jnp.einsum("ab,hij->haibj", eye_s, ws[:, :dec_len, :dec_len]).reshape(
            GMLP_HEADS, GMLP_CHUNK, GMLP_CHUNK)
        bs = jnp.repeat(gmlp_b_spatial[l].T, GMLP_HEAD_DIM, axis=1)
        lw = dict(
            g_mix=g_norm_mix[l].reshape(1, D_MODEL), g_ffn=g_norm_ffn[l].reshape(1, D_MODEL),
            w_in=w_in[l].astype(BF16), v_gain=gmlp_v_gain[l].reshape(1, GMLP_WIDTH),
            ws=ws.astype(BF16), ws_sample=ws_sample.astype(BF16),
            bs=bs, bs_sample=jnp.tile(bs[:dec_len], (ns, 1)),
            prep=_s5_prep(s5_a_re[l], s5_a_im[l], s5_log_dt[l], s5_b_re[l], s5_b_im[l],
                          s5_c_re[l], s5_c_im[l]),
            d_tiled=jnp.tile(s5_d[l].reshape(GROUP_BLOCKS, 1, LANES), (1, 1, S5_CHUNK)),
            w_glu=s5_w_glu[l].astype(BF16), b_glu=s5_b_glu[l].reshape(1, SSM_WIDTH),
            w_out=w_out[l].astype(BF16), wr=wr, b_r=b_r,
        )
        mod_p = mod_all[l, :nb].reshape(nb, 1, 6 * D_MODEL)
        mod_s = jnp.repeat(mod_all[l, nb:nb + ns], dec_len, axis=0).reshape(1, ns * dec_len, 6 * D_MODEL)
        xp, xsort_p, prow_p, counts_p, gf_p, hp_re, hp_im, _ = _mixers(
            xp, mod_p, lw, zeros_p, zeros_p, seq_len=seq_len, nseq=nb, sample=False, ffn=ffn_p)
        xs, xsort_s, prow_s, counts_s, gf_s, hs_re, hs_im, vs = _mixers(
            xs, mod_s, lw, _state_in(state_s5_re[l], 1, ns), _state_in(state_s5_im[l], 1, ns),
            seq_len=dec_len, nseq=ns, sample=True, ffn=ffn_s)
        zs_p, zs_s = _moe_runs(
            xsort_p, counts_p, xsort_s, counts_s, wg_all, wu_all, wd_all, expert0=l * N_EXPERTS,
            max_rows=tiles_p * _sorted_rows_bound(tm) + _sorted_rows_bound(ns * dec_len))
        ffn_p = (zs_p, prow_p, gf_p)
        ffn_s = (zs_s, prow_s, gf_s)
        sp_re.append(_state_out(hp_re, nb))
        sp_im.append(_state_out(hp_im, nb))
        ss_re.append(_state_out(hs_re, ns))
        ss_im.append(_state_out(hs_im, ns))
        v_new.append(vs.reshape(ns, dec_len, GMLP_WIDTH))
    yp = _final(xp, ffn_p, g_fin, tm=tm, tiles_per_mod=seq_len // tm)
    ys = _final(xs, ffn_s, g_fin, tm=ns * dec_len, tiles_per_mod=1)
    return (yp.reshape(nb, seq_len, D_MODEL), ys.reshape(ns, dec_len, D_MODEL),
            jnp.stack(sp_re), jnp.stack(sp_im), jnp.stack(ss_re), jnp.stack(ss_im), jnp.stack(v_new))
```

```python
import functools

import jax
import jax.numpy as jnp
from jax import lax
from jax.experimental import pallas as pl
from jax.experimental.pallas import tpu as pltpu

F32 = jnp.float32
BF16 = jnp.bfloat16

D_MODEL = 1024
DEPTH = 2
CHUNK = 64
GMLP_CHUNK = 128
GMLP_WIDTH = 512
GMLP_HEADS = 4
GMLP_HEAD_DIM = 128
SSM_WIDTH = 512
SSM_GROUP = 16
SSM_GROUPS = 32
SSM_STATE = 64
IN_WIDTH = 1536
N_EXPERTS = 16
EXPERTS_PER_GROUP = 4
N_EXPERT_GROUPS = 4
D_EXPERT = 512
EPS = 1e-6

LANES = 128
S5_CHUNK = 16
GROUP_BLOCKS = 4
GROUPS_PER_BLOCK = SSM_GROUPS // GROUP_BLOCKS
STATE_BLOCK = GROUPS_PER_BLOCK * SSM_STATE
S5_ROW = S5_CHUNK * LANES
ADA_ROWS = 16
PAIRS_PER_GROUP = 6
N_CLASSES = N_EXPERT_GROUPS * PAIRS_PER_GROUP
CLASS_ROWS = 32
_PAIRS = [(a, b) for a in range(EXPERTS_PER_GROUP) for b in range(a + 1, EXPERTS_PER_GROUP)]
CLASS_LO = [g * EXPERTS_PER_GROUP + a for g in range(N_EXPERT_GROUPS) for a, _ in _PAIRS]
CLASS_HI = [g * EXPERTS_PER_GROUP + b for g in range(N_EXPERT_GROUPS) for _, b in _PAIRS]
MOE_TILE = 256
MIX_TILE = 512
RUN_ALIGN = 8
SORT_ROWS = 768
XS_WIDTH = D_MODEL + LANES
VMEM_LIMIT = 56 * 1024 * 1024


def _cparams(*sem):
    return pltpu.CompilerParams(dimension_semantics=sem, vmem_limit_bytes=VMEM_LIMIT)


def _ada_kernel(c_ref, w_ref, b_ref, o_ref):
    c = c_ref[...]
    s = (c * jax.nn.sigmoid(c)).astype(BF16)
    o_ref[0] = jnp.dot(s, w_ref[0].astype(BF16), preferred_element_type=F32) + b_ref[0]


def _ada(c_all, w_ada, b_ada):
    nblk = 6
    return pl.pallas_call(
        _ada_kernel,
        grid=(DEPTH, nblk),
        in_specs=[
            pl.BlockSpec((ADA_ROWS, D_MODEL), lambda l, j: (0, 0)),
            pl.BlockSpec((1, D_MODEL, D_MODEL), lambda l, j: (l, 0, j)),
            pl.BlockSpec((1, 1, D_MODEL), lambda l, j: (l, 0, j)),
        ],
        out_specs=pl.BlockSpec((1, ADA_ROWS, D_MODEL), lambda l, j: (l, 0, j)),
        out_shape=jax.ShapeDtypeStruct((DEPTH, ADA_ROWS, 6 * D_MODEL), F32),
        compiler_params=_cparams("parallel", "parallel"),
        name="ada",
    )(c_all, w_ada, b_ada.reshape(DEPTH, 1, 6 * D_MODEL))


def _prep_kernel(are_ref, aim_ref, ldt_ref, bre_ref, bim_ref, cre_ref, cim_ref,
                 msre_ref, msim_ref, nt_ref, wrev_ref, a16re_ref, a16im_ref):
    a_re = are_ref[0]
    a_im = aim_ref[0]
    dt = jnp.exp(ldt_ref[0])
    rho = a_re * dt
    th = a_im * dt
    kk = jnp.minimum(lax.broadcasted_iota(jnp.int32, (24, STATE_BLOCK), 0), S5_CHUNK).astype(F32)
    mag = jnp.exp(kk * rho)
    pw_re = mag * jnp.cos(kk * th)
    pw_im = mag * jnp.sin(kk * th)

    lb_re = pw_re[1:2]
    lb_im = pw_im[1:2]
    num_re = lb_re - 1.0
    num_im = lb_im
    den = a_re * a_re + a_im * a_im
    coef_re = (num_re * a_re + num_im * a_im) / den
    coef_im = (num_im * a_re - num_re * a_im) / den
    b_re = bre_ref[0]
    b_im = bim_ref[0]
    bb_re = coef_re * b_re - coef_im * b_im
    bb_im = coef_re * b_im + coef_im * b_re

    rows = lax.broadcasted_iota(jnp.int32, (LANES, STATE_BLOCK), 0)
    cols = lax.broadcasted_iota(jnp.int32, (LANES, STATE_BLOCK), 1)
    same_group = (rows >> 4) == (cols >> 6)

    def blockdiag(x16):
        return jnp.where(same_group, jnp.concatenate([x16] * GROUPS_PER_BLOCK, axis=0), 0.0)

    for s in range(S5_CHUNK):
        k = S5_CHUNK - 1 - s
        p_re = pw_re[k:k + 1]
        p_im = pw_im[k:k + 1]
        msre_ref[0, s * LANES:(s + 1) * LANES, :] = blockdiag(p_re * bb_re - p_im * bb_im).astype(BF16)
        msim_ref[0, s * LANES:(s + 1) * LANES, :] = blockdiag(p_re * bb_im + p_im * bb_re).astype(BF16)

    b_hi, b_lo = _split_bf16(jnp.concatenate([blockdiag(bb_re), blockdiag(bb_im)], axis=1))
    c_re = cre_ref[0]
    c_im = cim_ref[0]
    for k in range(S5_CHUNK + 1):
        p_re = pw_re[k:k + 1]
        p_im = pw_im[k:k + 1]
        cl = jnp.concatenate([blockdiag(c_re * p_re - c_im * p_im),
                              -blockdiag(c_re * p_im + c_im * p_re)], axis=1)
        if k >= 1:
            nt_ref[0, (k - 1) * LANES:k * LANES, :] = cl.astype(BF16)
        if k < S5_CHUNK:
            c_hi, c_lo = _split_bf16(cl)
            nt_dims = (((1,), (1,)), ((), ()))
            wl = (lax.dot_general(b_hi, c_hi, nt_dims, preferred_element_type=F32)
                  + lax.dot_general(b_hi, c_lo, nt_dims, preferred_element_type=F32)
                  + lax.dot_general(b_lo, c_hi, nt_dims, preferred_element_type=F32))
            j = S5_CHUNK - 1 - k
            wrev_ref[0, j * LANES:(j + 1) * LANES, :LANES] = wl.astype(BF16)
            if j >= 1:
                wrev_ref[0, (j - 1) * LANES:j * LANES, LANES:] = wl.astype(BF16)
    wrev_ref[0, (S5_CHUNK - 1) * LANES:, LANES:] = jnp.zeros((LANES, LANES), BF16)

    a16re_ref[0] = pw_re[S5_CHUNK:S5_CHUNK + 1]
    a16im_ref[0] = pw_im[S5_CHUNK:S5_CHUNK + 1]


def _s5_prep(a_re, a_im, log_dt, b_re, b_im, c_re, c_im):
    nstate = SSM_GROUPS * SSM_STATE

    def lane_row(v):
        return v.reshape(GROUP_BLOCKS, 1, STATE_BLOCK)

    def rows16(v):
        return v.reshape(SSM_GROUP, GROUP_BLOCKS, STATE_BLOCK).transpose(1, 0, 2)

    ldt = jnp.repeat(log_dt, SSM_STATE).reshape(SSM_GROUPS, SSM_STATE)
    bt_re = rows16(b_re.transpose(2, 0, 1).reshape(SSM_GROUP, nstate))
    bt_im = rows16(b_im.transpose(2, 0, 1).reshape(SSM_GROUP, nstate))
    ct_re = rows16(c_re.transpose(1, 0, 2).reshape(SSM_GROUP, nstate))
    ct_im = rows16(c_im.transpose(1, 0, 2).reshape(SSM_GROUP, nstate))
    row_spec = pl.BlockSpec((1, 1, STATE_BLOCK), lambda g: (g, 0, 0))
    r16_spec = pl.BlockSpec((1, SSM_GROUP, STATE_BLOCK), lambda g: (g, 0, 0))
    return pl.pallas_call(
        _prep_kernel,
        grid=(GROUP_BLOCKS,),
        in_specs=[row_spec, row_spec, row_spec, r16_spec, r16_spec, r16_spec, r16_spec],
        out_specs=[
            pl.BlockSpec((1, S5_ROW, STATE_BLOCK), lambda g: (g, 0, 0)),
            pl.BlockSpec((1, S5_ROW, STATE_BLOCK), lambda g: (g, 0, 0)),
            pl.BlockSpec((1, S5_ROW, 2 * STATE_BLOCK), lambda g: (g, 0, 0)),
            pl.BlockSpec((1, S5_ROW, 2 * LANES), lambda g: (g, 0, 0)),
            row_spec, row_spec,
        ],
        out_shape=[
            jax.ShapeDtypeStruct((GROUP_BLOCKS, S5_ROW, STATE_BLOCK), BF16),
            jax.ShapeDtypeStruct((GROUP_BLOCKS, S5_ROW, STATE_BLOCK), BF16),
            jax.ShapeDtypeStruct((GROUP_BLOCKS, S5_ROW, 2 * STATE_BLOCK), BF16),
            jax.ShapeDtypeStruct((GROUP_BLOCKS, S5_ROW, 2 * LANES), BF16),
            jax.ShapeDtypeStruct((GROUP_BLOCKS, 1, STATE_BLOCK), F32),
            jax.ShapeDtypeStruct((GROUP_BLOCKS, 1, STATE_BLOCK), F32),
        ],
        compiler_params=_cparams("parallel"),
        name="s5_prep",
    )(lane_row(a_re), lane_row(a_im), lane_row(ldt), bt_re, bt_im, ct_re, ct_im)


def _rms(x, g):
    return x * lax.rsqrt(jnp.mean(x * x, axis=-1, keepdims=True) + EPS) * g


def _unsorted_ffn(zs_ref, prow_ref):
    tm = prow_ref.shape[1]
    zs = zs_ref[...].astype(BF16)
    pcol = jnp.concatenate([prow_ref[...].astype(F32), jnp.zeros((LANES - 1, tm), F32)], axis=0).T[:, :1]
    lanes = lax.broadcasted_iota(jnp.int32, (tm, SORT_ROWS), 1)
    pick = jnp.where(lanes == pcol.astype(jnp.int32), 1.0, 0.0).astype(BF16)
    return jnp.dot(pick, zs, preferred_element_type=F32)


def _mixer_in_kernel(x_ref, *refs, tm, fused, want_v):
    if fused:
        zs_ref, prow_ref, gf_ref = refs[:3]
        refs = refs[3:]
    mod_ref, g_ref, win_ref, vg_ref, ws_ref, bs_ref, a_ref, u_ref = refs[:8]
    refs = refs[8:]
    x = x_ref[...]
    if fused:
        x = x + gf_ref[0] * _unsorted_ffn(zs_ref, prow_ref)
        refs[0][...] = x
        refs = refs[1:]
    mod = mod_ref[0]
    shift = mod[:, :D_MODEL]
    scale = mod[:, D_MODEL:]
    h = _rms(x, g_ref[...]) * (1.0 + scale) + shift
    proj = jnp.dot(h.astype(BF16), win_ref[...], preferred_element_type=F32)
    z = jax.nn.gelu(proj[:, :2 * GMLP_WIDTH])
    u = z[:, :GMLP_WIDTH]
    v = z[:, GMLP_WIDTH:]
    vc = v - jnp.mean(v, axis=-1, keepdims=True)
    vn = vc * lax.rsqrt(jnp.mean(vc * vc, axis=-1, keepdims=True) + EPS) * vg_ref[...]
    if want_v:
        refs[0][...] = vn
    vb = vn.astype(BF16)
    bias = bs_ref[...]
    for c in range(tm // GMLP_CHUNK):
        r0 = c * GMLP_CHUNK
        for hh in range(GMLP_HEADS):
            l0 = hh * GMLP_HEAD_DIM
            mixed = jnp.dot(ws_ref[hh], vb[r0:r0 + GMLP_CHUNK, l0:l0 + GMLP_HEAD_DIM],
                            preferred_element_type=F32) + bias[:, l0:l0 + GMLP_HEAD_DIM]
            a_ref[r0:r0 + GMLP_CHUNK, l0:l0 + GMLP_HEAD_DIM] = (
                u[r0:r0 + GMLP_CHUNK, l0:l0 + GMLP_HEAD_DIM] * mixed).astype(BF16)
    for gb in range(GROUP_BLOCKS):
        l0 = 2 * GMLP_WIDTH + gb * LANES
        u_ref[gb] = proj[:, l0:l0 + LANES]


def _mixer_in(x2d, mod, g_mix, w_in, v_gain, ws, bs, *, tm, tiles_per_mod, want_v, ffn=None):
    t = x2d.shape[0]
    rmod = mod.shape[1]
    const2 = lambda i: (0, 0)
    tok = pl.BlockSpec((tm, D_MODEL), lambda i: (i, 0))
    in_specs = [tok]
    args = [x2d]
    if ffn is not None:
        in_specs += _ffn_specs(tm, rmod, tiles_per_mod)
        args += list(ffn)
    in_specs += [
        pl.BlockSpec((1, rmod, 2 * D_MODEL), lambda i: (i // tiles_per_mod, 0, 0)),
        pl.BlockSpec((1, D_MODEL), const2),
        pl.BlockSpec((D_MODEL, IN_WIDTH), const2),
        pl.BlockSpec((1, GMLP_WIDTH), const2),
        pl.BlockSpec((GMLP_HEADS, GMLP_CHUNK, GMLP_CHUNK), lambda i: (0, 0, 0)),
        pl.BlockSpec((GMLP_CHUNK, GMLP_WIDTH), const2),
    ]
    args += [mod, g_mix, w_in, v_gain, ws, bs]
    out_shape = [jax.ShapeDtypeStruct((t, GMLP_WIDTH), BF16),
                 jax.ShapeDtypeStruct((GROUP_BLOCKS, t, LANES), F32)]
    out_specs = [pl.BlockSpec((tm, GMLP_WIDTH), lambda i: (i, 0)),
                 pl.BlockSpec((GROUP_BLOCKS, tm, LANES), lambda i: (0, i, 0))]
    if ffn is not None:
        out_shape.append(jax.ShapeDtypeStruct((t, D_MODEL), F32))
        out_specs.append(tok)
    if want_v:
        out_shape.append(jax.ShapeDtypeStruct((t, GMLP_WIDTH), F32))
        out_specs.append(pl.BlockSpec((tm, GMLP_WIDTH), lambda i: (i, 0)))
    return pl.pallas_call(
        functools.partial(_mixer_in_kernel, tm=tm, fused=ffn is not None, want_v=want_v),
        grid=(t // tm,),
        in_specs=in_specs,
        out_specs=out_specs,
        out_shape=out_shape,
        compiler_params=_cparams("parallel"),
        name="mixer_in",
    )(*args)


def _ffn_specs(tm, rmod, tiles_per_mod):
    return [
        pl.BlockSpec((SORT_ROWS, D_MODEL), lambda i: (i, 0)),
        pl.BlockSpec((1, tm), lambda i: (0, i)),
        pl.BlockSpec((1, rmod, D_MODEL), lambda i: (i // tiles_per_mod, 0, 0)),
    ]


def _final_kernel(x_ref, zs_ref, prow_ref, gf_ref, g_ref, o_ref):
    x = x_ref[...] + gf_ref[0] * _unsorted_ffn(zs_ref, prow_ref)
    o_ref[...] = _rms(x, g_ref[...])


def _final(xmid, ffn, g_final, *, tm, tiles_per_mod):
    t = xmid.shape[0]
    rmod = ffn[2].shape[1]
    tok = pl.BlockSpec((tm, D_MODEL), lambda i: (i, 0))
    return pl.pallas_call(
        _final_kernel,
        grid=(t // tm,),
        in_specs=[tok] + _ffn_specs(tm, rmod, tiles_per_mod) + [pl.BlockSpec((1, D_MODEL), lambda i: (0, 0))],
        out_specs=tok,
        out_shape=jax.ShapeDtypeStruct((t, D_MODEL), F32),
        compiler_params=_cparams("parallel"),
        name="final_norm",
    )(xmid, *ffn, g_final)


def _s5_kernel(u_ref, us_ref, msre_ref, msim_ref, nt_ref, wrev_ref, a16re_ref, a16im_ref, d_ref,
               h0re_ref, h0im_ref, h0sre_ref, h0sim_ref,
               y_ref, ys_ref, hnre_ref, hnim_ref, hnsre_ref, hnsim_ref,
               sre_scr, sim_scr, hre_scr, him_scr):
    rows = u_ref.shape[1] // S5_CHUNK
    srows = us_ref.shape[1] // S5_CHUNK

    def chunk_rows(ref, n):
        return jnp.concatenate([ref[0, pl.ds(s, n, stride=S5_CHUNK), :] for s in range(S5_CHUNK)], axis=1)

    u = jnp.concatenate([chunk_rows(u_ref, rows), chunk_rows(us_ref, srows)], axis=0)
    ub = u.astype(BF16)
    s_re = jnp.dot(ub, msre_ref[0], preferred_element_type=F32)
    s_im = jnp.dot(ub, msim_ref[0], preferred_element_type=F32)
    a_re = a16re_ref[0]
    a_im = a16im_ref[0]
    sre_scr[...] = s_re[:rows]
    sim_scr[...] = s_im[:rows]

    def body(r, carry):
        hr, hi = carry
        hre_scr[pl.ds(r, 1), :] = hr
        him_scr[pl.ds(r, 1), :] = hi
        sr = sre_scr[pl.ds(r, 1), :]
        si = sim_scr[pl.ds(r, 1), :]
        return (a_re * hr - a_im * hi + sr, a_re * hi + a_im * hr + si)

    hn_re, hn_im = lax.fori_loop(0, rows, body, (h0re_ref[0, 0], h0im_ref[0, 0]), unroll=8)
    hnre_ref[0, 0] = hn_re
    hnim_ref[0, 0] = hn_im
    hs_re = h0sre_ref[0, 0]
    hs_im = h0sim_ref[0, 0]
    hnsre_ref[0, 0] = a_re * hs_re - a_im * hs_im + s_re[rows:]
    hnsim_ref[0, 0] = a_re * hs_im + a_im * hs_re + s_im[rows:]
    hcat = jnp.concatenate([jnp.concatenate([hre_scr[...], hs_re], axis=0),
                            jnp.concatenate([him_scr[...], hs_im], axis=0)], axis=1).astype(BF16)
    inter = lax.dot_general(hcat, nt_ref[0], (((1,), (1,)), ((), ())),
                            preferred_element_type=F32)
    d = d_ref[0]
    for t in range(0, S5_CHUNK, 2):
        k0 = (S5_CHUNK - 2 - t) * LANES
        pair = jnp.dot(ub[:, :(t + 2) * LANES], wrev_ref[0, k0:, :], preferred_element_type=F32)
        for step, intra in ((t + 1, pair[:, :LANES]), (t, pair[:, LANES:])):
            sl = slice(step * LANES, (step + 1) * LANES)
            out = intra + inter[:, sl] + d[:, sl] * u[:, sl]
            y_ref[0, pl.ds(step, rows, stride=S5_CHUNK), :] = out[:rows]
            ys_ref[0, pl.ds(step, srows, stride=S5_CHUNK), :] = out[rows:]


def _s5(u4, u4s, prep, d_tiled, h0_re, h0_im, h0s_re, h0s_im, *, rows, nblk):
    ms_re, ms_im, nt, wrev, a16_re, a16_im = prep
    srows = h0s_re.shape[2]
    wspec = lambda shape: pl.BlockSpec((1,) + shape, lambda g, b: (g, 0, 0))
    hspec = pl.BlockSpec((1, 1, 1, STATE_BLOCK), lambda g, b: (g, b, 0, 0))
    hsspec = pl.BlockSpec((1, 1, srows, STATE_BLOCK), lambda g, b: (g, 0, 0, 0))
    uspec = pl.BlockSpec((1, rows * S5_CHUNK, LANES), lambda g, b: (g, b, 0))
    usspec = pl.BlockSpec((1, srows * S5_CHUNK, LANES), lambda g, b: (g, 0, 0))
    return pl.pallas_call(
        _s5_kernel,
        grid=(GROUP_BLOCKS, nblk),
        in_specs=[
            uspec, usspec,
            wspec((S5_ROW, STATE_BLOCK)), wspec((S5_ROW, STATE_BLOCK)),
            wspec((S5_ROW, 2 * STATE_BLOCK)), wspec((S5_ROW, 2 * LANES)),
            wspec((1, STATE_BLOCK)), wspec((1, STATE_BLOCK)), wspec((1, S5_ROW)),
            hspec, hspec, hsspec, hsspec,
        ],
        out_specs=[uspec, usspec, hspec, hspec, hsspec, hsspec],
        out_shape=[
            jax.ShapeDtypeStruct(u4.shape, F32), jax.ShapeDtypeStruct(u4s.shape, F32),
            jax.ShapeDtypeStruct(h0_re.shape, F32), jax.ShapeDtypeStruct(h0_re.shape, F32),
            jax.ShapeDtypeStruct(h0s_re.shape, F32), jax.ShapeDtypeStruct(h0s_re.shape, F32),
        ],
        scratch_shapes=[pltpu.VMEM((rows, STATE_BLOCK), F32)] * 4,
        compiler_params=_cparams("parallel", "arbitrary"),
        name="s5",
    )(u4, u4s, ms_re, ms_im, nt, wrev, a16_re, a16_im, d_tiled, h0_re, h0_im, h0s_re, h0s_im)


def _split_bf16(x):
    hi = x.astype(BF16)
    return hi, (x - hi.astype(F32)).astype(BF16)


def _top2_of4(a):
    m1 = jnp.maximum(jnp.maximum(a[0], a[1]), jnp.maximum(a[2], a[3]))
    i1 = jnp.where(a[0] == m1, 0, jnp.where(a[1] == m1, 1, jnp.where(a[2] == m1, 2, 3)))
    b = [jnp.where(i1 == j, -jnp.inf, a[j]) for j in range(4)]
    m2 = jnp.maximum(jnp.maximum(b[0], b[1]), jnp.maximum(b[2], b[3]))
    i2 = jnp.where(b[0] == m2, 0, jnp.where(b[1] == m2, 1, jnp.where(b[2] == m2, 2, 3)))
    return m1, i1, m2, i2


def _route_rows(h2, wr_ref, br_ref):
    h_hi, h_lo = _split_bf16(h2)
    parts = jnp.dot(jnp.concatenate([h_hi, h_lo], axis=1), wr_ref[...], preferred_element_type=F32)
    pt = parts.T
    lt = pt[:N_EXPERTS] + pt[N_EXPERTS:2 * N_EXPERTS] + br_ref[...]
    rows = [lt[e:e + 1] for e in range(N_EXPERTS)]
    mx = functools.reduce(jnp.maximum, rows)
    ex = [jnp.exp(r - mx) for r in rows]
    tot = functools.reduce(lambda p, q: p + q, ex)
    scores = [e / tot for e in ex]
    best = None
    for g in range(N_EXPERT_GROUPS):
        m1, i1, m2, i2 = _top2_of4(scores[g * EXPERTS_PER_GROUP:(g + 1) * EXPERTS_PER_GROUP])
        cand = (m1 + m2, m1, i1 + g * EXPERTS_PER_GROUP, m2, i2 + g * EXPERTS_PER_GROUP)
        if best is None:
            best = cand
        else:
            better = cand[0] > best[0]
            best = tuple(jnp.where(better, c, b) for c, b in zip(cand, best))
    _, v1, e1, v2, e2 = best
    den = v1 + v2
    w1 = v1 / den
    w2 = v2 / den
    first_lo = e1 < e2
    return (jnp.where(first_lo, e1, e2), jnp.where(first_lo, e2, e1),
            jnp.where(first_lo, w1, w2), jnp.where(first_lo, w2, w1))


def _mixer_out_kernel(x_ref, a_ref, y_ref, mod_ref, g_ref, wglu_ref, bglu_ref, wout_ref,
                      wr_ref, br_ref, tri_ref, ltri_ref, *outs, tm, nsub):
    for j in range(nsub):
        _mixer_out_tile(j, slice(j * tm, (j + 1) * tm), x_ref, a_ref, y_ref, mod_ref, g_ref, wglu_ref,
                        bglu_ref, wout_ref, wr_ref, br_ref, tri_ref, ltri_ref, *outs)


def _mixer_out_tile(j, r, x_ref, a_ref, y_ref, mod_ref, g_ref, wglu_ref, bglu_ref, wout_ref,
                    wr_ref, br_ref, tri_ref, ltri_ref, xmid_ref, xs_ref, prow_ref, cnt_ref):
    x = x_ref[r, :]
    mod = mod_ref[0]
    gate_m = mod[:, :D_MODEL]
    shift_f = mod[:, D_MODEL:2 * D_MODEL]
    scale_f = mod[:, 2 * D_MODEL:]
    ys = jax.nn.gelu(jnp.concatenate([y_ref[gb, r, :] for gb in range(GROUP_BLOCKS)], axis=1))
    glu = jnp.dot(ys.astype(BF16), wglu_ref[...], preferred_element_type=F32) + bglu_ref[...]
    b_out = ys * jax.nn.sigmoid(glu)
    mixed = jnp.concatenate([a_ref[r, :], b_out.astype(BF16)], axis=1)
    xmid = x + gate_m * jnp.dot(mixed, wout_ref[...], preferred_element_type=F32)
    h2 = _rms(xmid, g_ref[...]) * (1.0 + scale_f) + shift_f
    lo, hi, glo, ghi = _route_rows(h2, wr_ref, br_ref)
    tm = x.shape[0]
    xmid_ref[r, :] = xmid
    a = lo & (EXPERTS_PER_GROUP - 1)
    b = hi & (EXPERTS_PER_GROUP - 1)
    pair = jnp.where(a == 0, 0, jnp.where(a == 1, 3, 5)) + (b - a - 1)
    cls = (lo >> 2) * PAIRS_PER_GROUP + pair
    onehot = lax.broadcasted_iota(jnp.int32, (CLASS_ROWS, tm), 0) == cls
    prefix = jnp.dot(jnp.where(onehot, 1.0, 0.0).astype(BF16), tri_ref[...],
                     preferred_element_type=F32)
    total = prefix[:, tm - 1:tm]
    cnt_ref[j] = jnp.broadcast_to(total, (CLASS_ROWS, LANES))
    groups = jnp.floor((total + (RUN_ALIGN - 1)) * (1.0 / RUN_ALIGN))
    before = jnp.dot(ltri_ref[...], jnp.broadcast_to(groups, (CLASS_ROWS, LANES)).astype(BF16),
                     preferred_element_type=F32)[:, :1] * RUN_ALIGN
    prow = jnp.sum(jnp.where(onehot, before + prefix - 1.0, 0.0), axis=0, keepdims=True).astype(jnp.int32)
    prow_ref[:, r] = prow
    pick = jnp.where(lax.broadcasted_iota(jnp.int32, (SORT_ROWS, tm), 0) == prow, 1.0, 0.0).astype(BF16)
    glo_hi, glo_lo = _split_bf16(glo)
    ghi_hi, ghi_lo = _split_bf16(ghi)
    gates = jnp.concatenate([glo_hi.astype(F32), glo_lo.astype(F32), ghi_hi.astype(F32), ghi_lo.astype(F32),
                             jnp.zeros((LANES - 4, tm), F32)], axis=0).T
    payload = jnp.concatenate([h2.astype(BF16), gates.astype(BF16)], axis=1)
    xs_ref[j * SORT_ROWS:(j + 1) * SORT_ROWS, :] = jnp.dot(pick, payload, preferred_element_type=F32)


def _mixer_out(x2d, a_out, y4, mod, g_ffn, w_glu, b_glu, w_out, wr, b_r, *, tm, tiles_per_mod):
    t = x2d.shape[0]
    rmod = mod.shape[1]
    nsub = 2 if tiles_per_mod % 2 == 0 else 1
    rows = tm * nsub
    const2 = lambda i: (0, 0)
    tok = pl.BlockSpec((rows, D_MODEL), lambda i: (i, 0))
    ids = jnp.arange(tm)
    cids = jnp.arange(CLASS_ROWS)
    return pl.pallas_call(
        functools.partial(_mixer_out_kernel, tm=tm, nsub=nsub),
        grid=(t // rows,),
        in_specs=[
            tok,
            pl.BlockSpec((rows, GMLP_WIDTH), lambda i: (i, 0)),
            pl.BlockSpec((GROUP_BLOCKS, rows, LANES), lambda i: (0, i, 0)),
            pl.BlockSpec((1, rmod, 3 * D_MODEL), lambda i: (i // (tiles_per_mod // nsub), 0, 0)),
            pl.BlockSpec((1, D_MODEL), const2),
            pl.BlockSpec((SSM_WIDTH, SSM_WIDTH), const2),
            pl.BlockSpec((1, SSM_WIDTH), const2),
            pl.BlockSpec((D_MODEL, D_MODEL), const2),
            pl.BlockSpec((2 * D_MODEL, LANES), const2),
            pl.BlockSpec((N_EXPERTS, 1), const2),
            pl.BlockSpec((tm, tm), const2),
            pl.BlockSpec((CLASS_ROWS, CLASS_ROWS), const2),
        ],
        out_specs=[tok, pl.BlockSpec((nsub * SORT_ROWS, XS_WIDTH), lambda i: (i, 0)),
                   pl.BlockSpec((1, rows), lambda i: (0, i)),
                   pl.BlockSpec((nsub, CLASS_ROWS, LANES), lambda i: (i, 0, 0))],
        out_shape=[jax.ShapeDtypeStruct((t, D_MODEL), F32),
                   jax.ShapeDtypeStruct((t // tm * SORT_ROWS, XS_WIDTH), F32),
                   jax.ShapeDtypeStruct((1, t), jnp.int32),
                   jax.ShapeDtypeStruct((t // tm, CLASS_ROWS, LANES), F32)],
        compiler_params=_cparams("parallel"),
        name="mixer_out",
    )(x2d, a_out, y4, mod, g_ffn, w_glu, b_glu, w_out, wr, b_r,
      (ids[:, None] <= ids[None, :]).astype(BF16), (cids[None, :] < cids[:, None]).astype(BF16))


def _moe_runs_kernel(tlo_ref, thi_ref, tval_ref, psrc_ref, pdst_ref, plen_ref, kfirst_ref, kend_ref, used_ref,
                     xsp_hbm, xss_hbm, wgl_ref, wul_ref, wdl_ref, wgh_ref, wuh_ref, wdh_ref,
                     zsp_hbm, zss_hbm, buf_ref, obuf_ref, sem_in, sem_out, *, nsrc, nsrc_p):
    i = pl.program_id(0)
    nt = pl.num_programs(0)
    slot = i % 2
    other = 1 - slot
    nv = tval_ref[i]
    nv_next = jnp.where(i + 1 < nt, tval_ref[jnp.minimum(i + 1, nt - 1)], 0)
    nv_prev = jnp.where(i >= 1, tval_ref[jnp.maximum(i - 1, 0)], 0)
    nv_prev2 = jnp.where(i >= 2, tval_ref[jnp.maximum(i - 2, 0)], 0)

    def rows8(v):
        return pl.multiple_of(v, RUN_ALIGN)

    def for_pieces(tile, fn):
        def piece(k, xs_hbm, zs_hbm):
            n = plen_ref[tile * nsrc + k]

            @pl.when(n > 0)
            def _():
                fn(xs_hbm, zs_hbm, rows8(psrc_ref[tile * nsrc + k]), rows8(pdst_ref[tile * nsrc + k]),
                   rows8(n))

        first = kfirst_ref[tile]
        end = kend_ref[tile]
        lax.fori_loop(first, jnp.minimum(end, nsrc_p), lambda k, c: (piece(k, xsp_hbm, zsp_hbm), c)[1], 0)
        lax.fori_loop(jnp.maximum(first, nsrc_p), end, lambda k, c: (piece(k, xss_hbm, zss_hbm), c)[1], 0)

    def gather_start(tile, sl):
        for_pieces(tile, lambda xs_hbm, zs_hbm, src, dst, n: pltpu.make_async_copy(
            xs_hbm.at[pl.ds(src, n), :], buf_ref.at[sl, pl.ds(dst, n), :], sem_in.at[sl]).start())

    def gather_wait(sl, n):
        pltpu.make_async_copy(xsp_hbm.at[pl.ds(0, rows8(n)), :], buf_ref.at[sl, pl.ds(0, rows8(n)), :],
                              sem_in.at[sl]).wait()

    def scatter_start(tile, sl):
        for_pieces(tile, lambda xs_hbm, zs_hbm, src, dst, n: pltpu.make_async_copy(
            obuf_ref.at[sl, pl.ds(dst, n), :], zs_hbm.at[pl.ds(src, n), :], sem_out.at[sl]).start())

    def scatter_wait(sl, n):
        pltpu.make_async_copy(obuf_ref.at[sl, pl.ds(0, rows8(n)), :], zsp_hbm.at[pl.ds(0, rows8(n)), :],
                              sem_out.at[sl]).wait()

    @pl.when(i == 0)
    def _():
        buf_ref[...] = jnp.zeros_like(buf_ref)
        gather_start(0, 0)
        obuf_ref[1] = jnp.zeros((MOE_TILE, D_MODEL), F32)

        def tail(zs_hbm, k0, k, start):
            for first in range(0, SORT_ROWS, MOE_TILE):
                lo_row = rows8(jnp.maximum(used_ref[k], first))
                n = rows8(jnp.maximum(jnp.minimum(first + MOE_TILE, SORT_ROWS) - lo_row, 0))
                copy = pltpu.make_async_copy(obuf_ref.at[1, pl.ds(0, n), :],
                                             zs_hbm.at[pl.ds(rows8((k - k0) * SORT_ROWS + lo_row), n), :],
                                             sem_out.at[1])

                @pl.when(n > 0)
                def _():
                    copy.start() if start else copy.wait()

        for start in (True, False):
            lax.fori_loop(0, nsrc_p, lambda k, c: (tail(zsp_hbm, 0, k, start), c)[1], 0)
            lax.fori_loop(nsrc_p, nsrc, lambda k, c: (tail(zss_hbm, nsrc_p, k, start), c)[1], 0)

    @pl.when(nv_next > 0)
    def _():
        gather_start(i + 1, other)

    @pl.when(nv_prev2 > 0)
    def _():
        scatter_wait(slot, nv_prev2)

    @pl.when(nv > 0)
    def _():
        gather_wait(slot, nv)
        buf = buf_ref[slot]
        h = buf[:, :D_MODEL].astype(BF16)
        glo = buf[:, D_MODEL:D_MODEL + 1] + buf[:, D_MODEL + 1:D_MODEL + 2]
        ghi = buf[:, D_MODEL + 2:D_MODEL + 3] + buf[:, D_MODEL + 3:D_MODEL + 4]
        ffn = None
        for wg, wu, wd, gate in ((wgl_ref, wul_ref, wdl_ref, glo), (wgh_ref, wuh_ref, wdh_ref, ghi)):
            he = (jax.nn.silu(jnp.dot(h, wg[0], preferred_element_type=F32))
                  * jnp.dot(h, wu[0], preferred_element_type=F32))
            y = gate * jnp.dot(he.astype(BF16), wd[0], preferred_element_type=F32)
            ffn = y if ffn is None else ffn + y
        obuf_ref[slot] = ffn.astype(BF16).astype(F32)
        scatter_start(i, slot)

    @pl.when(i == nt - 1)
    def _():
        @pl.when(nv_prev > 0)
        def _():
            scatter_wait(other, nv_prev)

        @pl.when(nv > 0)
        def _():
            scatter_wait(slot, nv)


def _moe_runs(xs_p, counts_p, xs_s, counts_s, wg, wu, wd, *, max_rows, expert0):
    counts = jnp.concatenate([counts_p, counts_s], axis=0)
    nmix = counts.shape[0]
    nmix_p = counts_p.shape[0]
    ntiles = -(-max_rows // MOE_TILE) + N_CLASSES
    n_kc = counts[:, :N_CLASSES, 0].astype(jnp.int32)
    len_kc = (n_kc + RUN_ALIGN - 1) // RUN_ALIGN * RUN_ALIGN
    off_kc = jnp.cumsum(len_kc, axis=1) - len_kc
    used_k = jnp.sum(len_kc, axis=1)
    start_kc = jnp.cumsum(len_kc, axis=0) - len_kc
    region_c = jnp.sum(len_kc, axis=0)
    tiles_c = (region_c + MOE_TILE - 1) // MOE_TILE
    tile_end = jnp.cumsum(tiles_c)
    tile_start = tile_end - tiles_c
    used = tile_end[-1]
    tid = jnp.arange(ntiles, dtype=jnp.int32)
    tcls = jnp.sum(tile_end[None, :] <= jnp.minimum(tid, used - 1)[:, None], axis=1).astype(jnp.int32)
    lo_row = (tid - tile_start[tcls]) * MOE_TILE
    tval = jnp.where(tid < used, jnp.clip(region_c[tcls] - lo_row, 0, MOE_TILE), 0)
    run_lo = start_kc[:, tcls].T
    run_hi = run_lo + len_kc[:, tcls].T
    piece_lo = jnp.maximum(run_lo, lo_row[:, None])
    piece_hi = jnp.minimum(run_hi, (lo_row + tval)[:, None])
    plen = jnp.maximum(piece_hi - piece_lo, 0)
    kk = jnp.arange(nmix, dtype=jnp.int32)
    first_row = jnp.where(kk < nmix_p, kk, kk - nmix_p) * SORT_ROWS
    psrc = first_row[None, :] + off_kc[:, tcls].T + (piece_lo - run_lo)
    pdst = piece_lo - lo_row[:, None]
    flat = lambda v: jnp.where(plen > 0, v, 0).reshape(-1).astype(jnp.int32)
    tlo = jnp.asarray(CLASS_LO, jnp.int32)[tcls]
    thi = jnp.asarray(CLASS_HI, jnp.int32)[tcls]
    kfirst = jnp.sum(run_hi <= lo_row[:, None], axis=1).astype(jnp.int32)
    kend = jnp.sum(run_lo < (lo_row + tval)[:, None], axis=1).astype(jnp.int32)
    nprefetch = 9
    wspec_lo = lambda shape: pl.BlockSpec(shape, lambda i, tlo, *_: (expert0 + tlo[i], 0, 0))
    wspec_hi = lambda shape: pl.BlockSpec(shape, lambda i, tlo, thi, *_: (expert0 + thi[i], 0, 0))
    up = (1, D_MODEL, D_EXPERT)
    down = (1, D_EXPERT, D_MODEL)
    return pl.pallas_call(
        functools.partial(_moe_runs_kernel, nsrc=nmix, nsrc_p=nmix_p),
        grid_spec=pltpu.PrefetchScalarGridSpec(
            num_scalar_prefetch=nprefetch,
            grid=(ntiles,),
            in_specs=[
                pl.BlockSpec(memory_space=pl.ANY), pl.BlockSpec(memory_space=pl.ANY),
                wspec_lo(up), wspec_lo(up), wspec_lo(down),
                wspec_hi(up), wspec_hi(up), wspec_hi(down),
            ],
            out_specs=[pl.BlockSpec(memory_space=pl.ANY), pl.BlockSpec(memory_space=pl.ANY)],
            scratch_shapes=[
                pltpu.VMEM((2, MOE_TILE, XS_WIDTH), F32),
                pltpu.VMEM((2, MOE_TILE, D_MODEL), F32),
                pltpu.SemaphoreType.DMA((2,)),
                pltpu.SemaphoreType.DMA((2,)),
            ],
        ),
        out_shape=[jax.ShapeDtypeStruct((xs_p.shape[0], D_MODEL), F32),
                   jax.ShapeDtypeStruct((xs_s.shape[0], D_MODEL), F32)],
        compiler_params=_cparams("arbitrary"),
        name="moe_runs",
    )(tlo, thi, tval.astype(jnp.int32), flat(psrc), flat(pdst), flat(plen), kfirst, kend,
      used_k.astype(jnp.int32), xs_p, xs_s, wg, wu, wd, wg, wu, wd)


def _sorted_rows_bound(tm):
    return min(SORT_ROWS, tm + N_CLASSES * (RUN_ALIGN - 1))


def _state_out(h, nseq):
    return h.reshape(GROUP_BLOCKS, nseq, GROUPS_PER_BLOCK, SSM_STATE).transpose(1, 0, 2, 3).reshape(
        nseq, SSM_GROUPS, SSM_STATE)


def _state_in(h, nblk, rh):
    nseq = h.shape[0]
    return h.reshape(nseq, GROUP_BLOCKS, STATE_BLOCK).transpose(1, 0, 2).reshape(
        GROUP_BLOCKS, nblk, rh, STATE_BLOCK)


def kernel(x_prompt, x_sample, c_prompt, c_sample, state_s5_re, state_s5_im, w_ada, b_ada, g_norm_mix, g_norm_ffn, w_in, gmlp_v_gain, gmlp_w_spatial, gmlp_b_spatial, s5_a_re, s5_a_im, s5_log_dt, s5_b_re, s5_b_im, s5_c_re, s5_c_im, s5_d, s5_w_glu, s5_b_glu, w_out, w_router, b_router, w_gate, w_up, w_down, g_final):
    nb, seq_len, _ = x_prompt.shape
    ns, dec_len, _ = x_sample.shape
    assert dec_len == S5_CHUNK and ns * dec_len == GMLP_CHUNK and nb + ns <= ADA_ROWS
    assert seq_len % GMLP_CHUNK == 0 and MIX_TILE + N_CLASSES * (RUN_ALIGN - 1) <= SORT_ROWS

    c_all = jnp.concatenate([c_prompt, c_sample, jnp.zeros((ADA_ROWS - nb - ns, D_MODEL), F32)], axis=0)
    mod_all = _ada(c_all, w_ada, b_ada)

    pos = jnp.arange(GMLP_CHUNK)
    causal = (pos[None, :] // CHUNK) <= (pos[:, None] // CHUNK)
    wr_hi = w_router.astype(BF16)
    wr_lo = (w_router - wr_hi.astype(F32)).astype(BF16)
    wr = jnp.pad(jnp.concatenate([wr_hi, wr_lo], axis=1), ((0, 0), (0, LANES - 2 * N_EXPERTS)))
    wr = jnp.concatenate([wr, wr], axis=0)
    b_r = b_router.reshape(N_EXPERTS, 1)
    g_fin = g_final.reshape(1, D_MODEL)
    eye_s = jnp.eye(ns, dtype=F32)

    xp = x_prompt.reshape(nb * seq_len, D_MODEL)
    xs = x_sample.reshape(ns * dec_len, D_MODEL)
    zeros_p = jnp.zeros((GROUP_BLOCKS, nb, 1, STATE_BLOCK), F32)
    wg_all = w_gate.astype(BF16).reshape(DEPTH * N_EXPERTS, D_MODEL, D_EXPERT)
    wu_all = w_up.astype(BF16).reshape(DEPTH * N_EXPERTS, D_MODEL, D_EXPERT)
    wd_all = w_down.astype(BF16).reshape(DEPTH * N_EXPERTS, D_EXPERT, D_MODEL)
    sp_re, sp_im, ss_re, ss_im, v_new = [], [], [], [], []
    tm = min(MIX_TILE, seq_len)
    ts = ns * dec_len
    tiles_p = nb * seq_len // tm
    ffn_p = ffn_s = None
    for l in range(DEPTH):
        ws = jnp.where(causal[None], gmlp_w_spatial[l], 0.0)
        ws_sample = jnp.einsum("ab,hij->haibj", eye_s, ws[:, :dec_len, :dec_len]).reshape(
            GMLP_HEADS, GMLP_CHUNK, GMLP_CHUNK)
        bs = jnp.repeat(gmlp_b_spatial[l].T, GMLP_HEAD_DIM, axis=1)
        lw = dict(
            g_mix=g_norm_mix[l].reshape(1, D_MODEL), g_ffn=g_norm_ffn[l].reshape(1, D_MODEL),
            w_in=w_in[l].astype(BF16), v_gain=gmlp_v_gain[l].reshape(1, GMLP_WIDTH),
            ws=ws.astype(BF16), ws_sample=ws_sample.astype(BF16),
            bs=bs, bs_sample=jnp.tile(bs[:dec_len], (ns, 1)),
            prep=_s5_prep(s5_a_re[l], s5_a_im[l], s5_log_dt[l], s5_b_re[l], s5_b_im[l],
                          s5_c_re[l], s5_c_im[l]),
            d_tiled=jnp.tile(s5_d[l].reshape(GROUP_BLOCKS, 1, LANES), (1, 1, S5_CHUNK)),
            w_glu=s5_w_glu[l].astype(BF16), b_glu=s5_b_glu[l].reshape(1, SSM_WIDTH),
            w_out=w_out[l].astype(BF16), wr=wr, b_r=b_r,
        )
        mod_p = mod_all[l, :nb].reshape(nb, 1, 6 * D_MODEL)
        mod_s = jnp.repeat(mod_all[l, nb:nb + ns], dec_len, axis=0).reshape(1, ns * dec_len, 6 * D_MODEL)
        ins_p = _mixer_in(xp, mod_p[..., :2 * D_MODEL], lw["g_mix"], lw["w_in"], lw["v_gain"], lw["ws"],
                          lw["bs"], tm=tm, tiles_per_mod=seq_len // tm, want_v=False, ffn=ffn_p)
        ins_s = _mixer_in(xs, mod_s[..., :2 * D_MODEL], lw["g_mix"], lw["w_in"], lw["v_gain"], lw["ws_sample"],
                          lw["bs_sample"], tm=ts, tiles_per_mod=1, want_v=True, ffn=ffn_s)
        if l > 0:
            xp, xs = ins_p[2], ins_s[2]
        vs = ins_s[-1]
        y4_p, y4_s, hp_re, hp_im, hs_re, hs_im = _s5(
            ins_p[1], ins_s[1], lw["prep"], lw["d_tiled"], zeros_p, zeros_p,
            _state_in(state_s5_re[l], 1, ns), _state_in(state_s5_im[l], 1, ns),
            rows=seq_len // S5_CHUNK, nblk=nb)
        xp, xsort_p, prow_p, counts_p = _mixer_out(
            xp, ins_p[0], y4_p, mod_p[..., 2 * D_MODEL:5 * D_MODEL], lw["g_ffn"], lw["w_glu"], lw["b_glu"],
            lw["w_out"], lw["wr"], lw["b_r"], tm=tm, tiles_per_mod=seq_len // tm)
        xs, xsort_s, prow_s, counts_s = _mixer_out(
            xs, ins_s[0], y4_s, mod_s[..., 2 * D_MODEL:5 * D_MODEL], lw["g_ffn"], lw["w_glu"], lw["b_glu"],
            lw["w_out"], lw["wr"], lw["b_r"], tm=ts, tiles_per_mod=1)
        gf_p, gf_s = mod_p[..., 5 * D_MODEL:], mod_s[..., 5 * D_MODEL:]
        zs_p, zs_s = _moe_runs(
            xsort_p, counts_p, xsort_s, counts_s, wg_all, wu_all, wd_all, expert0=l * N_EXPERTS,
            max_rows=tiles_p * _sorted_rows_bound(tm) + _sorted_rows_bound(ts))
        ffn_p = (zs_p, prow_p, gf_p)
        ffn_s = (zs_s, prow_s, gf_s)
        sp_re.append(_state_out(hp_re, nb))
        sp_im.append(_state_out(hp_im, nb))
        ss_re.append(_state_out(hs_re, ns))
        ss_im.append(_state_out(hs_im, ns))
        v_new.append(vs.reshape(ns, dec_len, GMLP_WIDTH))
    yp = _final(xp, ffn_p, g_fin, tm=tm, tiles_per_mod=seq_len // tm)
    ys = _final(xs, ffn_s, g_fin, tm=ts, tiles_per_mod=1)
    return (yp.reshape(nb, seq_len, D_MODEL), ys.reshape(ns, dec_len, D_MODEL),
            jnp.stack(sp_re), jnp.stack(sp_im), jnp.stack(ss_re), jnp.stack(ss_im), jnp.stack(v_new))
```

```python
import functools

import jax
import jax.numpy as jnp
from jax import lax
from jax.experimental import pallas as pl
from jax.experimental.pallas import tpu as pltpu

F32 = jnp.float32
BF16 = jnp.bfloat16

D_MODEL = 1024
DEPTH = 2
CHUNK = 64
GMLP_CHUNK = 128
GMLP_WIDTH = 512
GMLP_HEADS = 4
GMLP_HEAD_DIM = 128
SSM_WIDTH = 512
SSM_GROUP = 16
SSM_GROUPS = 32
SSM_STATE = 64
IN_WIDTH = 1536
N_EXPERTS = 16
EXPERTS_PER_GROUP = 4
N_EXPERT_GROUPS = 4
D_EXPERT = 512
EPS = 1e-6

LANES = 128
S5_CHUNK = 16
GROUP_BLOCKS = 4
GROUPS_PER_BLOCK = SSM_GROUPS // GROUP_BLOCKS
STATE_BLOCK = GROUPS_PER_BLOCK * SSM_STATE
S5_ROW = S5_CHUNK * LANES
ADA_ROWS = 16
ADA_BLOCKS = 6
POW_ROWS = 24
GROUP_SHIFT = SSM_GROUP.bit_length() - 1
STATE_SHIFT = SSM_STATE.bit_length() - 1
EXPERT_GROUP_SHIFT = EXPERTS_PER_GROUP.bit_length() - 1
PAIRS_PER_GROUP = 6
N_CLASSES = N_EXPERT_GROUPS * PAIRS_PER_GROUP
CLASS_ROWS = 32
_PAIRS = [(a, b) for a in range(EXPERTS_PER_GROUP) for b in range(a + 1, EXPERTS_PER_GROUP)]
CLASS_LO = [g * EXPERTS_PER_GROUP + a for g in range(N_EXPERT_GROUPS) for a, _ in _PAIRS]
CLASS_HI = [g * EXPERTS_PER_GROUP + b for g in range(N_EXPERT_GROUPS) for _, b in _PAIRS]
MOE_TILE = 256
MIX_TILE = 512
RUN_ALIGN = 8
SORT_ROWS = 768
XS_WIDTH = D_MODEL + LANES
VMEM_LIMIT = 56 * 1024 * 1024


def _cparams(*sem):
    return pltpu.CompilerParams(dimension_semantics=sem, vmem_limit_bytes=VMEM_LIMIT)


def _ada_kernel(c_ref, w_ref, b_ref, o_ref):
    c = c_ref[...]
    s = (c * jax.nn.sigmoid(c)).astype(BF16)
    o_ref[0] = jnp.dot(s, w_ref[0].astype(BF16), preferred_element_type=F32) + b_ref[0]


def _ada(c_all, w_ada, b_ada):
    return pl.pallas_call(
        _ada_kernel,
        grid=(DEPTH, ADA_BLOCKS),
        in_specs=[
            pl.BlockSpec((ADA_ROWS, D_MODEL), lambda l, j: (0, 0)),
            pl.BlockSpec((1, D_MODEL, D_MODEL), lambda l, j: (l, 0, j)),
            pl.BlockSpec((1, 1, D_MODEL), lambda l, j: (l, 0, j)),
        ],
        out_specs=pl.BlockSpec((1, ADA_ROWS, D_MODEL), lambda l, j: (l, 0, j)),
        out_shape=jax.ShapeDtypeStruct((DEPTH, ADA_ROWS, 6 * D_MODEL), F32),
        compiler_params=_cparams("parallel", "parallel"),
        name="ada",
    )(c_all, w_ada, b_ada.reshape(DEPTH, 1, 6 * D_MODEL))


def _prep_kernel(are_ref, aim_ref, ldt_ref, bre_ref, bim_ref, cre_ref, cim_ref,
                 msre_ref, msim_ref, nt_ref, wrev_ref, a16re_ref, a16im_ref):
    a_re = are_ref[0]
    a_im = aim_ref[0]
    dt = jnp.exp(ldt_ref[0])
    rho = a_re * dt
    th = a_im * dt
    kk = jnp.minimum(lax.broadcasted_iota(jnp.int32, (POW_ROWS, STATE_BLOCK), 0), S5_CHUNK).astype(F32)
    mag = jnp.exp(kk * rho)
    pw_re = mag * jnp.cos(kk * th)
    pw_im = mag * jnp.sin(kk * th)

    lb_re = pw_re[1:2]
    lb_im = pw_im[1:2]
    num_re = lb_re - 1.0
    num_im = lb_im
    den = a_re * a_re + a_im * a_im
    coef_re = (num_re * a_re + num_im * a_im) / den
    coef_im = (num_im * a_re - num_re * a_im) / den
    b_re = bre_ref[0]
    b_im = bim_ref[0]
    bb_re = coef_re * b_re - coef_im * b_im
    bb_im = coef_re * b_im + coef_im * b_re

    rows = lax.broadcasted_iota(jnp.int32, (LANES, STATE_BLOCK), 0)
    cols = lax.broadcasted_iota(jnp.int32, (LANES, STATE_BLOCK), 1)
    same_group = (rows >> GROUP_SHIFT) == (cols >> STATE_SHIFT)

    def blockdiag(x16):
        return jnp.where(same_group, jnp.concatenate([x16] * GROUPS_PER_BLOCK, axis=0), 0.0)

    for s in range(S5_CHUNK):
        k = S5_CHUNK - 1 - s
        p_re = pw_re[k:k + 1]
        p_im = pw_im[k:k + 1]
        msre_ref[0, s * LANES:(s + 1) * LANES, :] = blockdiag(p_re * bb_re - p_im * bb_im).astype(BF16)
        msim_ref[0, s * LANES:(s + 1) * LANES, :] = blockdiag(p_re * bb_im + p_im * bb_re).astype(BF16)

    b_hi, b_lo = _split_bf16(jnp.concatenate([blockdiag(bb_re), blockdiag(bb_im)], axis=1))
    c_re = cre_ref[0]
    c_im = cim_ref[0]
    for k in range(S5_CHUNK + 1):
        p_re = pw_re[k:k + 1]
        p_im = pw_im[k:k + 1]
        cl = jnp.concatenate([blockdiag(c_re * p_re - c_im * p_im),
                              -blockdiag(c_re * p_im + c_im * p_re)], axis=1)
        if k >= 1:
            nt_ref[0, (k - 1) * LANES:k * LANES, :] = cl.astype(BF16)
        if k < S5_CHUNK:
            c_hi, c_lo = _split_bf16(cl)
            nt_dims = (((1,), (1,)), ((), ()))
            wl = (lax.dot_general(b_hi, c_hi, nt_dims, preferred_element_type=F32)
                  + lax.dot_general(b_hi, c_lo, nt_dims, preferred_element_type=F32)
                  + lax.dot_general(b_lo, c_hi, nt_dims, preferred_element_type=F32))
            j = S5_CHUNK - 1 - k
            wrev_ref[0, j * LANES:(j + 1) * LANES, :LANES] = wl.astype(BF16)
            if j >= 1:
                wrev_ref[0, (j - 1) * LANES:j * LANES, LANES:] = wl.astype(BF16)
    wrev_ref[0, (S5_CHUNK - 1) * LANES:, LANES:] = jnp.zeros((LANES, LANES), BF16)

    a16re_ref[0] = pw_re[S5_CHUNK:S5_CHUNK + 1]
    a16im_ref[0] = pw_im[S5_CHUNK:S5_CHUNK + 1]


def _s5_prep(a_re, a_im, log_dt, b_re, b_im, c_re, c_im):
    nstate = SSM_GROUPS * SSM_STATE

    def lane_row(v):
        return v.reshape(GROUP_BLOCKS, 1, STATE_BLOCK)

    def rows16(v):
        return v.reshape(SSM_GROUP, GROUP_BLOCKS, STATE_BLOCK).transpose(1, 0, 2)

    ldt = jnp.repeat(log_dt, SSM_STATE).reshape(SSM_GROUPS, SSM_STATE)
    bt_re = rows16(b_re.transpose(2, 0, 1).reshape(SSM_GROUP, nstate))
    bt_im = rows16(b_im.transpose(2, 0, 1).reshape(SSM_GROUP, nstate))
    ct_re = rows16(c_re.transpose(1, 0, 2).reshape(SSM_GROUP, nstate))
    ct_im = rows16(c_im.transpose(1, 0, 2).reshape(SSM_GROUP, nstate))
    row_spec = pl.BlockSpec((1, 1, STATE_BLOCK), lambda g: (g, 0, 0))
    r16_spec = pl.BlockSpec((1, SSM_GROUP, STATE_BLOCK), lambda g: (g, 0, 0))
    return pl.pallas_call(
        _prep_kernel,
        grid=(GROUP_BLOCKS,),
        in_specs=[row_spec, row_spec, row_spec, r16_spec, r16_spec, r16_spec, r16_spec],
        out_specs=[
            pl.BlockSpec((1, S5_ROW, STATE_BLOCK), lambda g: (g, 0, 0)),
            pl.BlockSpec((1, S5_ROW, STATE_BLOCK), lambda g: (g, 0, 0)),
            pl.BlockSpec((1, S5_ROW, 2 * STATE_BLOCK), lambda g: (g, 0, 0)),
            pl.BlockSpec((1, S5_ROW, 2 * LANES), lambda g: (g, 0, 0)),
            row_spec, row_spec,
        ],
        out_shape=[
            jax.ShapeDtypeStruct((GROUP_BLOCKS, S5_ROW, STATE_BLOCK), BF16),
            jax.ShapeDtypeStruct((GROUP_BLOCKS, S5_ROW, STATE_BLOCK), BF16),
            jax.ShapeDtypeStruct((GROUP_BLOCKS, S5_ROW, 2 * STATE_BLOCK), BF16),
            jax.ShapeDtypeStruct((GROUP_BLOCKS, S5_ROW, 2 * LANES), BF16),
            jax.ShapeDtypeStruct((GROUP_BLOCKS, 1, STATE_BLOCK), F32),
            jax.ShapeDtypeStruct((GROUP_BLOCKS, 1, STATE_BLOCK), F32),
        ],
        compiler_params=_cparams("parallel"),
        name="s5_prep",
    )(lane_row(a_re), lane_row(a_im), lane_row(ldt), bt_re, bt_im, ct_re, ct_im)


def _rms(x, g):
    return x * lax.rsqrt(jnp.mean(x * x, axis=-1, keepdims=True) + EPS) * g


def _unsorted_ffn(zs_ref, prow_ref):
    tm = prow_ref.shape[1]
    zs = zs_ref[...].astype(BF16)
    pcol = jnp.concatenate([prow_ref[...].astype(F32), jnp.zeros((LANES - 1, tm), F32)], axis=0).T[:, :1]
    lanes = lax.broadcasted_iota(jnp.int32, (tm, SORT_ROWS), 1)
    pick = jnp.where(lanes == pcol.astype(jnp.int32), 1.0, 0.0).astype(BF16)
    return jnp.dot(pick, zs, preferred_element_type=F32)


def _mixer_in_kernel(x_ref, *refs, tm, nsub, fused, want_v):
    for j in range(nsub):
        _mixer_in_tile(j, slice(j * tm, (j + 1) * tm), x_ref, refs, tm, fused, want_v)


def _mixer_in_tile(j, r, x_ref, refs, tm, fused, want_v):
    if fused:
        zs_ref, prow_ref, gf_ref = refs[:3]
        refs = refs[3:]
    mod_ref, g_ref, win_ref, vg_ref, ws_ref, bs_ref, a_ref, u_ref = refs[:8]
    refs = refs[8:]
    x = x_ref[r, :]
    if fused:
        x = x + gf_ref[0] * _unsorted_ffn(zs_ref.at[j * SORT_ROWS:(j + 1) * SORT_ROWS, :], prow_ref.at[:, r])
        refs[0][r, :] = x
        refs = refs[1:]
    mod = mod_ref[0]
    shift = mod[:, :D_MODEL]
    scale = mod[:, D_MODEL:]
    h = _rms(x, g_ref[...]) * (1.0 + scale) + shift
    proj = jnp.dot(h.astype(BF16), win_ref[...], preferred_element_type=F32)
    z = jax.nn.gelu(proj[:, :2 * GMLP_WIDTH])
    u = z[:, :GMLP_WIDTH]
    v = z[:, GMLP_WIDTH:]
    vc = v - jnp.mean(v, axis=-1, keepdims=True)
    vn = vc * lax.rsqrt(jnp.mean(vc * vc, axis=-1, keepdims=True) + EPS) * vg_ref[...]
    if want_v:
        refs[0][r, :] = vn
    vb = vn.astype(BF16)
    bias = bs_ref[...]
    for c in range(tm // GMLP_CHUNK):
        r0 = c * GMLP_CHUNK
        o0 = j * tm + r0
        for hh in range(GMLP_HEADS):
            l0 = hh * GMLP_HEAD_DIM
            mixed = jnp.dot(ws_ref[hh], vb[r0:r0 + GMLP_CHUNK, l0:l0 + GMLP_HEAD_DIM],
                            preferred_element_type=F32) + bias[:, l0:l0 + GMLP_HEAD_DIM]
            a_ref[o0:o0 + GMLP_CHUNK, l0:l0 + GMLP_HEAD_DIM] = (
                u[r0:r0 + GMLP_CHUNK, l0:l0 + GMLP_HEAD_DIM] * mixed).astype(BF16)
    for gb in range(GROUP_BLOCKS):
        l0 = 2 * GMLP_WIDTH + gb * LANES
        u_ref[gb, r, :] = proj[:, l0:l0 + LANES]


def _mixer_in(x2d, mod, g_mix, w_in, v_gain, ws, bs, *, tm, tiles_per_mod, want_v, ffn=None):
    t = x2d.shape[0]
    rmod = mod.shape[1]
    nsub = 2 if tiles_per_mod % 2 == 0 else 1
    rows = tm * nsub
    mods = tiles_per_mod // nsub
    const2 = lambda i: (0, 0)
    tok = pl.BlockSpec((rows, D_MODEL), lambda i: (i, 0))
    in_specs = [tok]
    args = [x2d]
    if ffn is not None:
        in_specs += _ffn_specs(tm, nsub, rmod, mods)
        args += list(ffn)
    in_specs += [
        pl.BlockSpec((1, rmod, 2 * D_MODEL), lambda i: (i // mods, 0, 0)),
        pl.BlockSpec((1, D_MODEL), const2),
        pl.BlockSpec((D_MODEL, IN_WIDTH), const2),
        pl.BlockSpec((1, GMLP_WIDTH), const2),
        pl.BlockSpec((GMLP_HEADS, GMLP_CHUNK, GMLP_CHUNK), lambda i: (0, 0, 0)),
        pl.BlockSpec((GMLP_CHUNK, GMLP_WIDTH), const2),
    ]
    args += [mod, g_mix, w_in, v_gain, ws, bs]
    out_shape = [jax.ShapeDtypeStruct((t, GMLP_WIDTH), BF16),
                 jax.ShapeDtypeStruct((GROUP_BLOCKS, t, LANES), F32)]
    out_specs = [pl.BlockSpec((rows, GMLP_WIDTH), lambda i: (i, 0)),
                 pl.BlockSpec((GROUP_BLOCKS, rows, LANES), lambda i: (0, i, 0))]
    if ffn is not None:
        out_shape.append(jax.ShapeDtypeStruct((t, D_MODEL), F32))
        out_specs.append(tok)
    if want_v:
        out_shape.append(jax.ShapeDtypeStruct((t, GMLP_WIDTH), F32))
        out_specs.append(pl.BlockSpec((rows, GMLP_WIDTH), lambda i: (i, 0)))
    return pl.pallas_call(
        functools.partial(_mixer_in_kernel, tm=tm, nsub=nsub, fused=ffn is not None, want_v=want_v),
        grid=(t // rows,),
        in_specs=in_specs,
        out_specs=out_specs,
        out_shape=out_shape,
        compiler_params=_cparams("parallel"),
        name="mixer_in",
    )(*args)


def _ffn_specs(tm, nsub, rmod, mods):
    return [
        pl.BlockSpec((nsub * SORT_ROWS, D_MODEL), lambda i: (i, 0)),
        pl.BlockSpec((1, nsub * tm), lambda i: (0, i)),
        pl.BlockSpec((1, rmod, D_MODEL), lambda i: (i // mods, 0, 0)),
    ]


def _final_kernel(x_ref, zs_ref, prow_ref, gf_ref, g_ref, o_ref, *, tm, nsub):
    for j in range(nsub):
        r = slice(j * tm, (j + 1) * tm)
        x = x_ref[r, :] + gf_ref[0] * _unsorted_ffn(zs_ref.at[j * SORT_ROWS:(j + 1) * SORT_ROWS, :],
                                                    prow_ref.at[:, r])
        o_ref[r, :] = _rms(x, g_ref[...])


def _final(xmid, ffn, g_final, *, tm, tiles_per_mod):
    t = xmid.shape[0]
    rmod = ffn[2].shape[1]
    nsub = 2 if tiles_per_mod % 2 == 0 else 1
    tok = pl.BlockSpec((tm * nsub, D_MODEL), lambda i: (i, 0))
    return pl.pallas_call(
        functools.partial(_final_kernel, tm=tm, nsub=nsub),
        grid=(t // (tm * nsub),),
        in_specs=[tok] + _ffn_specs(tm, nsub, rmod, tiles_per_mod // nsub)
        + [pl.BlockSpec((1, D_MODEL), lambda i: (0, 0))],
        out_specs=tok,
        out_shape=jax.ShapeDtypeStruct((t, D_MODEL), F32),
        compiler_params=_cparams("parallel"),
        name="final_norm",
    )(xmid, *ffn, g_final)


def _s5_kernel(u_ref, us_ref, msre_ref, msim_ref, nt_ref, wrev_ref, a16re_ref, a16im_ref, d_ref,
               h0re_ref, h0im_ref, h0sre_ref, h0sim_ref,
               y_ref, ys_ref, hnre_ref, hnim_ref, hnsre_ref, hnsim_ref,
               sre_scr, sim_scr, hre_scr, him_scr):
    rows = u_ref.shape[1] // S5_CHUNK
    srows = us_ref.shape[1] // S5_CHUNK

    def chunk_rows(ref, n):
        return jnp.concatenate([ref[0, pl.ds(s, n, stride=S5_CHUNK), :] for s in range(S5_CHUNK)], axis=1)

    u = jnp.concatenate([chunk_rows(u_ref, rows), chunk_rows(us_ref, srows)], axis=0)
    ub = u.astype(BF16)
    s_re = jnp.dot(ub, msre_ref[0], preferred_element_type=F32)
    s_im = jnp.dot(ub, msim_ref[0], preferred_element_type=F32)
    a_re = a16re_ref[0]
    a_im = a16im_ref[0]
    sre_scr[...] = s_re[:rows]
    sim_scr[...] = s_im[:rows]

    def body(r, carry):
        hr, hi = carry
        hre_scr[pl.ds(r, 1), :] = hr
        him_scr[pl.ds(r, 1), :] = hi
        sr = sre_scr[pl.ds(r, 1), :]
        si = sim_scr[pl.ds(r, 1), :]
        return (a_re * hr - a_im * hi + sr, a_re * hi + a_im * hr + si)

    hn_re, hn_im = lax.fori_loop(0, rows, body, (h0re_ref[0, 0], h0im_ref[0, 0]), unroll=8)
    hnre_ref[0, 0] = hn_re
    hnim_ref[0, 0] = hn_im
    hs_re = h0sre_ref[0, 0]
    hs_im = h0sim_ref[0, 0]
    hnsre_ref[0, 0] = a_re * hs_re - a_im * hs_im + s_re[rows:]
    hnsim_ref[0, 0] = a_re * hs_im + a_im * hs_re + s_im[rows:]
    hcat = jnp.concatenate([jnp.concatenate([hre_scr[...], hs_re], axis=0),
                            jnp.concatenate([him_scr[...], hs_im], axis=0)], axis=1).astype(BF16)
    inter = lax.dot_general(hcat, nt_ref[0], (((1,), (1,)), ((), ())),
                            preferred_element_type=F32)
    d = d_ref[0]
    for t in range(0, S5_CHUNK, 2):
        k0 = (S5_CHUNK - 2 - t) * LANES
        pair = jnp.dot(ub[:, :(t + 2) * LANES], wrev_ref[0, k0:, :], preferred_element_type=F32)
        for step, intra in ((t + 1, pair[:, :LANES]), (t, pair[:, LANES:])):
            sl = slice(step * LANES, (step + 1) * LANES)
            out = intra + inter[:, sl] + d[:, sl] * u[:, sl]
            y_ref[0, pl.ds(step, rows, stride=S5_CHUNK), :] = out[:rows]
            ys_ref[0, pl.ds(step, srows, stride=S5_CHUNK), :] = out[rows:]


def _s5(u4, u4s, prep, d_tiled, h0_re, h0_im, h0s_re, h0s_im, *, rows, nblk):
    ms_re, ms_im, nt, wrev, a16_re, a16_im = prep
    srows = h0s_re.shape[2]
    wspec = lambda shape: pl.BlockSpec((1,) + shape, lambda g, b: (g, 0, 0))
    hspec = pl.BlockSpec((1, 1, 1, STATE_BLOCK), lambda g, b: (g, b, 0, 0))
    hsspec = pl.BlockSpec((1, 1, srows, STATE_BLOCK), lambda g, b: (g, 0, 0, 0))
    uspec = pl.BlockSpec((1, rows * S5_CHUNK, LANES), lambda g, b: (g, b, 0))
    usspec = pl.BlockSpec((1, srows * S5_CHUNK, LANES), lambda g, b: (g, 0, 0))
    return pl.pallas_call(
        _s5_kernel,
        grid=(GROUP_BLOCKS, nblk),
        in_specs=[
            uspec, usspec,
            wspec((S5_ROW, STATE_BLOCK)), wspec((S5_ROW, STATE_BLOCK)),
            wspec((S5_ROW, 2 * STATE_BLOCK)), wspec((S5_ROW, 2 * LANES)),
            wspec((1, STATE_BLOCK)), wspec((1, STATE_BLOCK)), wspec((1, S5_ROW)),
            hspec, hspec, hsspec, hsspec,
        ],
        out_specs=[uspec, usspec, hspec, hspec, hsspec, hsspec],
        out_shape=[
            jax.ShapeDtypeStruct(u4.shape, F32), jax.ShapeDtypeStruct(u4s.shape, F32),
            jax.ShapeDtypeStruct(h0_re.shape, F32), jax.ShapeDtypeStruct(h0_re.shape, F32),
            jax.ShapeDtypeStruct(h0s_re.shape, F32), jax.ShapeDtypeStruct(h0s_re.shape, F32),
        ],
        scratch_shapes=[pltpu.VMEM((rows, STATE_BLOCK), F32)] * 4,
        compiler_params=_cparams("parallel", "arbitrary"),
        name="s5",
    )(u4, u4s, ms_re, ms_im, nt, wrev, a16_re, a16_im, d_tiled, h0_re, h0_im, h0s_re, h0s_im)


def _split_bf16(x):
    hi = x.astype(BF16)
    return hi, (x - hi.astype(F32)).astype(BF16)


def _top2_of4(a):
    m1 = jnp.maximum(jnp.maximum(a[0], a[1]), jnp.maximum(a[2], a[3]))
    i1 = jnp.where(a[0] == m1, 0, jnp.where(a[1] == m1, 1, jnp.where(a[2] == m1, 2, 3)))
    b = [jnp.where(i1 == j, -jnp.inf, a[j]) for j in range(4)]
    m2 = jnp.maximum(jnp.maximum(b[0], b[1]), jnp.maximum(b[2], b[3]))
    i2 = jnp.where(b[0] == m2, 0, jnp.where(b[1] == m2, 1, jnp.where(b[2] == m2, 2, 3)))
    return m1, i1, m2, i2


def _route_rows(h2, wr_ref, br_ref):
    h_hi, h_lo = _split_bf16(h2)
    parts = jnp.dot(jnp.concatenate([h_hi, h_lo], axis=1), wr_ref[...], preferred_element_type=F32)
    pt = parts.T
    lt = pt[:N_EXPERTS] + pt[N_EXPERTS:2 * N_EXPERTS] + br_ref[...]
    rows = [lt[e:e + 1] for e in range(N_EXPERTS)]
    mx = functools.reduce(jnp.maximum, rows)
    ex = [jnp.exp(r - mx) for r in rows]
    tot = functools.reduce(lambda p, q: p + q, ex)
    scores = [e / tot for e in ex]
    best = None
    for g in range(N_EXPERT_GROUPS):
        m1, i1, m2, i2 = _top2_of4(scores[g * EXPERTS_PER_GROUP:(g + 1) * EXPERTS_PER_GROUP])
        cand = (m1 + m2, m1, i1 + g * EXPERTS_PER_GROUP, m2, i2 + g * EXPERTS_PER_GROUP)
        if best is None:
            best = cand
        else:
            better = cand[0] > best[0]
            best = tuple(jnp.where(better, c, b) for c, b in zip(cand, best))
    _, v1, e1, v2, e2 = best
    den = v1 + v2
    w1 = v1 / den
    w2 = v2 / den
    first_lo = e1 < e2
    return (jnp.where(first_lo, e1, e2), jnp.where(first_lo, e2, e1),
            jnp.where(first_lo, w1, w2), jnp.where(first_lo, w2, w1))


def _mixer_out_kernel(x_ref, a_ref, y_ref, mod_ref, g_ref, wglu_ref, bglu_ref, wout_ref,
                      wr_ref, br_ref, tri_ref, ltri_ref, *outs, tm, nsub):
    for j in range(nsub):
        _mixer_out_tile(j, slice(j * tm, (j + 1) * tm), x_ref, a_ref, y_ref, mod_ref, g_ref, wglu_ref,
                        bglu_ref, wout_ref, wr_ref, br_ref, tri_ref, ltri_ref, *outs)


def _mixer_out_tile(j, r, x_ref, a_ref, y_ref, mod_ref, g_ref, wglu_ref, bglu_ref, wout_ref,
                    wr_ref, br_ref, tri_ref, ltri_ref, xmid_ref, xs_ref, prow_ref, cnt_ref):
    x = x_ref[r, :]
    mod = mod_ref[0]
    gate_m = mod[:, :D_MODEL]
    shift_f = mod[:, D_MODEL:2 * D_MODEL]
    scale_f = mod[:, 2 * D_MODEL:]
    ys = jax.nn.gelu(jnp.concatenate([y_ref[gb, r, :] for gb in range(GROUP_BLOCKS)], axis=1))
    glu = jnp.dot(ys.astype(BF16), wglu_ref[...], preferred_element_type=F32) + bglu_ref[...]
    b_out = ys * jax.nn.sigmoid(glu)
    mixed = jnp.concatenate([a_ref[r, :], b_out.astype(BF16)], axis=1)
    xmid = x + gate_m * jnp.dot(mixed, wout_ref[...], preferred_element_type=F32)
    h2 = _rms(xmid, g_ref[...]) * (1.0 + scale_f) + shift_f
    lo, hi, glo, ghi = _route_rows(h2, wr_ref, br_ref)
    tm = x.shape[0]
    xmid_ref[r, :] = xmid
    a = lo & (EXPERTS_PER_GROUP - 1)
    b = hi & (EXPERTS_PER_GROUP - 1)
    pair = jnp.where(a == 0, 0, jnp.where(a == 1, 3, 5)) + (b - a - 1)
    cls = (lo >> EXPERT_GROUP_SHIFT) * PAIRS_PER_GROUP + pair
    onehot = lax.broadcasted_iota(jnp.int32, (CLASS_ROWS, tm), 0) == cls
    prefix = jnp.dot(jnp.where(onehot, 1.0, 0.0).astype(BF16), tri_ref[...],
                     preferred_element_type=F32)
    total = prefix[:, tm - 1:tm]
    cnt_ref[j] = jnp.broadcast_to(total, (CLASS_ROWS, LANES))
    groups = jnp.floor((total + (RUN_ALIGN - 1)) * (1.0 / RUN_ALIGN))
    before = jnp.dot(ltri_ref[...], jnp.broadcast_to(groups, (CLASS_ROWS, LANES)).astype(BF16),
                     preferred_element_type=F32)[:, :1] * RUN_ALIGN
    prow = jnp.sum(jnp.where(onehot, before + prefix - 1.0, 0.0), axis=0, keepdims=True).astype(jnp.int32)
    prow_ref[:, r] = prow
    pick = jnp.where(lax.broadcasted_iota(jnp.int32, (SORT_ROWS, tm), 0) == prow, 1.0, 0.0).astype(BF16)
    glo_hi, glo_lo = _split_bf16(glo)
    ghi_hi, ghi_lo = _split_bf16(ghi)
    gates = jnp.concatenate([glo_hi.astype(F32), glo_lo.astype(F32), ghi_hi.astype(F32), ghi_lo.astype(F32),
                             jnp.zeros((LANES - 4, tm), F32)], axis=0).T
    payload = jnp.concatenate([h2.astype(BF16), gates.astype(BF16)], axis=1)
    xs_ref[j * SORT_ROWS:(j + 1) * SORT_ROWS, :] = jnp.dot(pick, payload, preferred_element_type=F32)


def _mixer_out(x2d, a_out, y4, mod, g_ffn, w_glu, b_glu, w_out, wr, b_r, *, tm, tiles_per_mod):
    t = x2d.shape[0]
    rmod = mod.shape[1]
    nsub = 2 if tiles_per_mod % 2 == 0 else 1
    rows = tm * nsub
    const2 = lambda i: (0, 0)
    tok = pl.BlockSpec((rows, D_MODEL), lambda i: (i, 0))
    ids = jnp.arange(tm)
    cids = jnp.arange(CLASS_ROWS)
    return pl.pallas_call(
        functools.partial(_mixer_out_kernel, tm=tm, nsub=nsub),
        grid=(t // rows,),
        in_specs=[
            tok,
            pl.BlockSpec((rows, GMLP_WIDTH), lambda i: (i, 0)),
            pl.BlockSpec((GROUP_BLOCKS, rows, LANES), lambda i: (0, i, 0)),
            pl.BlockSpec((1, rmod, 3 * D_MODEL), lambda i: (i // (tiles_per_mod // nsub), 0, 0)),
            pl.BlockSpec((1, D_MODEL), const2),
            pl.BlockSpec((SSM_WIDTH, SSM_WIDTH), const2),
            pl.BlockSpec((1, SSM_WIDTH), const2),
            pl.BlockSpec((D_MODEL, D_MODEL), const2),
            pl.BlockSpec((2 * D_MODEL, LANES), const2),
            pl.BlockSpec((N_EXPERTS, 1), const2),
            pl.BlockSpec((tm, tm), const2),
            pl.BlockSpec((CLASS_ROWS, CLASS_ROWS), const2),
        ],
        out_specs=[tok, pl.BlockSpec((nsub * SORT_ROWS, XS_WIDTH), lambda i: (i, 0)),
                   pl.BlockSpec((1, rows), lambda i: (0, i)),
                   pl.BlockSpec((nsub, CLASS_ROWS, LANES), lambda i: (i, 0, 0))],
        out_shape=[jax.ShapeDtypeStruct((t, D_MODEL), F32),
                   jax.ShapeDtypeStruct((t // tm * SORT_ROWS, XS_WIDTH), F32),
                   jax.ShapeDtypeStruct((1, t), jnp.int32),
                   jax.ShapeDtypeStruct((t // tm, CLASS_ROWS, LANES), F32)],
        compiler_params=_cparams("parallel"),
        name="mixer_out",
    )(x2d, a_out, y4, mod, g_ffn, w_glu, b_glu, w_out, wr, b_r,
      (ids[:, None] <= ids[None, :]).astype(BF16), (cids[None, :] < cids[:, None]).astype(BF16))


def _moe_runs_kernel(tlo_ref, thi_ref, tval_ref, psrc_ref, pdst_ref, plen_ref, kfirst_ref, kend_ref, used_ref,
                     xsp_hbm, xss_hbm, wgl_ref, wul_ref, wdl_ref, wgh_ref, wuh_ref, wdh_ref,
                     zsp_hbm, zss_hbm, buf_ref, obuf_ref, sem_in, sem_out, *, nsrc, nsrc_p):
    i = pl.program_id(0)
    nt = pl.num_programs(0)
    slot = i % 2
    other = 1 - slot
    nv = tval_ref[i]
    nv_next = jnp.where(i + 1 < nt, tval_ref[jnp.minimum(i + 1, nt - 1)], 0)
    nv_prev = jnp.where(i >= 1, tval_ref[jnp.maximum(i - 1, 0)], 0)
    nv_prev2 = jnp.where(i >= 2, tval_ref[jnp.maximum(i - 2, 0)], 0)

    def rows8(v):
        return pl.multiple_of(v, RUN_ALIGN)

    def for_pieces(tile, fn):
        def piece(k, xs_hbm, zs_hbm):
            n = plen_ref[tile * nsrc + k]

            @pl.when(n > 0)
            def _():
                fn(xs_hbm, zs_hbm, rows8(psrc_ref[tile * nsrc + k]), rows8(pdst_ref[tile * nsrc + k]),
                   rows8(n))

        first = kfirst_ref[tile]
        end = kend_ref[tile]
        lax.fori_loop(first, jnp.minimum(end, nsrc_p), lambda k, c: (piece(k, xsp_hbm, zsp_hbm), c)[1], 0)
        lax.fori_loop(jnp.maximum(first, nsrc_p), end, lambda k, c: (piece(k, xss_hbm, zss_hbm), c)[1], 0)

    def gather_start(tile, sl):
        for_pieces(tile, lambda xs_hbm, zs_hbm, src, dst, n: pltpu.make_async_copy(
            xs_hbm.at[pl.ds(src, n), :], buf_ref.at[sl, pl.ds(dst, n), :], sem_in.at[sl]).start())

    def gather_wait(sl, n):
        pltpu.make_async_copy(xsp_hbm.at[pl.ds(0, rows8(n)), :], buf_ref.at[sl, pl.ds(0, rows8(n)), :],
                              sem_in.at[sl]).wait()

    def scatter_start(tile, sl):
        for_pieces(tile, lambda xs_hbm, zs_hbm, src, dst, n: pltpu.make_async_copy(
            obuf_ref.at[sl, pl.ds(dst, n), :], zs_hbm.at[pl.ds(src, n), :], sem_out.at[sl]).start())

    def scatter_wait(sl, n):
        pltpu.make_async_copy(obuf_ref.at[sl, pl.ds(0, rows8(n)), :], zsp_hbm.at[pl.ds(0, rows8(n)), :],
                              sem_out.at[sl]).wait()

    @pl.when(i == 0)
    def _():
        buf_ref[...] = jnp.zeros_like(buf_ref)
        gather_start(0, 0)
        obuf_ref[1] = jnp.zeros((MOE_TILE, D_MODEL), F32)

        def tail(zs_hbm, k0, k, start):
            for first in range(0, SORT_ROWS, MOE_TILE):
                lo_row = rows8(jnp.maximum(used_ref[k], first))
                n = rows8(jnp.maximum(jnp.minimum(first + MOE_TILE, SORT_ROWS) - lo_row, 0))
                copy = pltpu.make_async_copy(obuf_ref.at[1, pl.ds(0, n), :],
                                             zs_hbm.at[pl.ds(rows8((k - k0) * SORT_ROWS + lo_row), n), :],
                                             sem_out.at[1])

                @pl.when(n > 0)
                def _():
                    copy.start() if start else copy.wait()

        for start in (True, False):
            lax.fori_loop(0, nsrc_p, lambda k, c: (tail(zsp_hbm, 0, k, start), c)[1], 0)
            lax.fori_loop(nsrc_p, nsrc, lambda k, c: (tail(zss_hbm, nsrc_p, k, start), c)[1], 0)

    @pl.when(nv_next > 0)
    def _():
        gather_start(i + 1, other)

    @pl.when(nv_prev2 > 0)
    def _():
        scatter_wait(slot, nv_prev2)

    @pl.when(nv > 0)
    def _():
        gather_wait(slot, nv)
        buf = buf_ref[slot]
        h = buf[:, :D_MODEL].astype(BF16)
        glo = buf[:, D_MODEL:D_MODEL + 1] + buf[:, D_MODEL + 1:D_MODEL + 2]
        ghi = buf[:, D_MODEL + 2:D_MODEL + 3] + buf[:, D_MODEL + 3:D_MODEL + 4]
        ffn = None
        for wg, wu, wd, gate in ((wgl_ref, wul_ref, wdl_ref, glo), (wgh_ref, wuh_ref, wdh_ref, ghi)):
            he = (jax.nn.silu(jnp.dot(h, wg[0], preferred_element_type=F32))
                  * jnp.dot(h, wu[0], preferred_element_type=F32))
            y = gate * jnp.dot(he.astype(BF16), wd[0], preferred_element_type=F32)
            ffn = y if ffn is None else ffn + y
        obuf_ref[slot] = ffn.astype(BF16).astype(F32)
        scatter_start(i, slot)

    @pl.when(i == nt - 1)
    def _():
        @pl.when(nv_prev > 0)
        def _():
            scatter_wait(other, nv_prev)

        @pl.when(nv > 0)
        def _():
            scatter_wait(slot, nv)


def _moe_runs(xs_p, counts_p, xs_s, counts_s, wg, wu, wd, *, max_rows, expert0):
    counts = jnp.concatenate([counts_p, counts_s], axis=0)
    nmix = counts.shape[0]
    nmix_p = counts_p.shape[0]
    ntiles = -(-max_rows // MOE_TILE) + N_CLASSES
    n_kc = counts[:, :N_CLASSES, 0].astype(jnp.int32)
    len_kc = (n_kc + RUN_ALIGN - 1) // RUN_ALIGN * RUN_ALIGN
    off_kc = jnp.cumsum(len_kc, axis=1) - len_kc
    used_k = jnp.sum(len_kc, axis=1)
    start_kc = jnp.cumsum(len_kc, axis=0) - len_kc
    region_c = jnp.sum(len_kc, axis=0)
    tiles_c = (region_c + MOE_TILE - 1) // MOE_TILE
    tile_end = jnp.cumsum(tiles_c)
    tile_start = tile_end - tiles_c
    used = tile_end[-1]
    tid = jnp.arange(ntiles, dtype=jnp.int32)
    tcls = jnp.sum(tile_end[None, :] <= jnp.minimum(tid, used - 1)[:, None], axis=1).astype(jnp.int32)
    lo_row = (tid - tile_start[tcls]) * MOE_TILE
    tval = jnp.where(tid < used, jnp.clip(region_c[tcls] - lo_row, 0, MOE_TILE), 0)
    run_lo = start_kc[:, tcls].T
    run_hi = run_lo + len_kc[:, tcls].T
    piece_lo = jnp.maximum(run_lo, lo_row[:, None])
    piece_hi = jnp.minimum(run_hi, (lo_row + tval)[:, None])
    plen = jnp.maximum(piece_hi - piece_lo, 0)
    kk = jnp.arange(nmix, dtype=jnp.int32)
    first_row = jnp.where(kk < nmix_p, kk, kk - nmix_p) * SORT_ROWS
    psrc = first_row[None, :] + off_kc[:, tcls].T + (piece_lo - run_lo)
    pdst = piece_lo - lo_row[:, None]
    flat = lambda v: jnp.where(plen > 0, v, 0).reshape(-1).astype(jnp.int32)
    tlo = jnp.asarray(CLASS_LO, jnp.int32)[tcls]
    thi = jnp.asarray(CLASS_HI, jnp.int32)[tcls]
    kfirst = jnp.sum(run_hi <= lo_row[:, None], axis=1).astype(jnp.int32)
    kend = jnp.sum(run_lo < (lo_row + tval)[:, None], axis=1).astype(jnp.int32)
    nprefetch = 9
    wspec_lo = lambda shape: pl.BlockSpec(shape, lambda i, tlo, *_: (expert0 + tlo[i], 0, 0))
    wspec_hi = lambda shape: pl.BlockSpec(shape, lambda i, tlo, thi, *_: (expert0 + thi[i], 0, 0))
    up = (1, D_MODEL, D_EXPERT)
    down = (1, D_EXPERT, D_MODEL)
    return pl.pallas_call(
        functools.partial(_moe_runs_kernel, nsrc=nmix, nsrc_p=nmix_p),
        grid_spec=pltpu.PrefetchScalarGridSpec(
            num_scalar_prefetch=nprefetch,
            grid=(ntiles,),
            in_specs=[
                pl.BlockSpec(memory_space=pl.ANY), pl.BlockSpec(memory_space=pl.ANY),
                wspec_lo(up), wspec_lo(up), wspec_lo(down),
                wspec_hi(up), wspec_hi(up), wspec_hi(down),
            ],
            out_specs=[pl.BlockSpec(memory_space=pl.ANY), pl.BlockSpec(memory_space=pl.ANY)],
            scratch_shapes=[
                pltpu.VMEM((2, MOE_TILE, XS_WIDTH), F32),
                pltpu.VMEM((2, MOE_TILE, D_MODEL), F32),
                pltpu.SemaphoreType.DMA((2,)),
                pltpu.SemaphoreType.DMA((2,)),
            ],
        ),
        out_shape=[jax.ShapeDtypeStruct((xs_p.shape[0], D_MODEL), F32),
                   jax.ShapeDtypeStruct((xs_s.shape[0], D_MODEL), F32)],
        compiler_params=_cparams("arbitrary"),
        name="moe_runs",
    )(tlo, thi, tval.astype(jnp.int32), flat(psrc), flat(pdst), flat(plen), kfirst, kend,
      used_k.astype(jnp.int32), xs_p, xs_s, wg, wu, wd, wg, wu, wd)


def _sorted_rows_bound(tm):
    return min(SORT_ROWS, tm + N_CLASSES * (RUN_ALIGN - 1))


def _state_out(h, nseq):
    return h.reshape(GROUP_BLOCKS, nseq, GROUPS_PER_BLOCK, SSM_STATE).transpose(1, 0, 2, 3).reshape(
        nseq, SSM_GROUPS, SSM_STATE)


def _state_in(h, nblk, rh):
    nseq = h.shape[0]
    return h.reshape(nseq, GROUP_BLOCKS, STATE_BLOCK).transpose(1, 0, 2).reshape(
        GROUP_BLOCKS, nblk, rh, STATE_BLOCK)


def kernel(x_prompt, x_sample, c_prompt, c_sample, state_s5_re, state_s5_im, w_ada, b_ada, g_norm_mix, g_norm_ffn, w_in, gmlp_v_gain, gmlp_w_spatial, gmlp_b_spatial, s5_a_re, s5_a_im, s5_log_dt, s5_b_re, s5_b_im, s5_c_re, s5_c_im, s5_d, s5_w_glu, s5_b_glu, w_out, w_router, b_router, w_gate, w_up, w_down, g_final):
    nb, seq_len, _ = x_prompt.shape
    ns, dec_len, _ = x_sample.shape
    assert dec_len == S5_CHUNK and ns * dec_len == GMLP_CHUNK and nb + ns <= ADA_ROWS
    assert seq_len % GMLP_CHUNK == 0 and MIX_TILE + N_CLASSES * (RUN_ALIGN - 1) <= SORT_ROWS

    c_all = jnp.concatenate([c_prompt, c_sample, jnp.zeros((ADA_ROWS - nb - ns, D_MODEL), F32)], axis=0)
    mod_all = _ada(c_all, w_ada, b_ada)

    pos = jnp.arange(GMLP_CHUNK)
    causal = (pos[None, :] // CHUNK) <= (pos[:, None] // CHUNK)
    wr_hi = w_router.astype(BF16)
    wr_lo = (w_router - wr_hi.astype(F32)).astype(BF16)
    wr = jnp.pad(jnp.concatenate([wr_hi, wr_lo], axis=1), ((0, 0), (0, LANES - 2 * N_EXPERTS)))
    wr = jnp.concatenate([wr, wr], axis=0)
    b_r = b_router.reshape(N_EXPERTS, 1)
    g_fin = g_final.reshape(1, D_MODEL)
    eye_s = jnp.eye(ns, dtype=F32)

    xp = x_prompt.reshape(nb * seq_len, D_MODEL)
    xs = x_sample.reshape(ns * dec_len, D_MODEL)
    zeros_p = jnp.zeros((GROUP_BLOCKS, nb, 1, STATE_BLOCK), F32)
    wg_all = w_gate.astype(BF16).reshape(DEPTH * N_EXPERTS, D_MODEL, D_EXPERT)
    wu_all = w_up.astype(BF16).reshape(DEPTH * N_EXPERTS, D_MODEL, D_EXPERT)
    wd_all = w_down.astype(BF16).reshape(DEPTH * N_EXPERTS, D_EXPERT, D_MODEL)
    sp_re, sp_im, ss_re, ss_im, v_new = [], [], [], [], []
    tm = min(MIX_TILE, seq_len)
    ts = ns * dec_len
    tiles_p = nb * seq_len // tm
    ffn_p = ffn_s = None
    for l in range(DEPTH):
        ws = jnp.where(causal[None], gmlp_w_spatial[l], 0.0)
        ws_sample = jnp.einsum("ab,hij->haibj", eye_s, ws[:, :dec_len, :dec_len]).reshape(
            GMLP_HEADS, GMLP_CHUNK, GMLP_CHUNK)
        bs = jnp.repeat(gmlp_b_spatial[l].T, GMLP_HEAD_DIM, axis=1)
        lw = dict(
            g_mix=g_norm_mix[l].reshape(1, D_MODEL), g_ffn=g_norm_ffn[l].reshape(1, D_MODEL),
            w_in=w_in[l].astype(BF16), v_gain=gmlp_v_gain[l].reshape(1, GMLP_WIDTH),
            ws=ws.astype(BF16), ws_sample=ws_sample.astype(BF16),
            bs=bs, bs_sample=jnp.tile(bs[:dec_len], (ns, 1)),
            prep=_s5_prep(s5_a_re[l], s5_a_im[l], s5_log_dt[l], s5_b_re[l], s5_b_im[l],
                          s5_c_re[l], s5_c_im[l]),
            d_tiled=jnp.tile(s5_d[l].reshape(GROUP_BLOCKS, 1, LANES), (1, 1, S5_CHUNK)),
            w_glu=s5_w_glu[l].astype(BF16), b_glu=s5_b_glu[l].reshape(1, SSM_WIDTH),
            w_out=w_out[l].astype(BF16), wr=wr, b_r=b_r,
        )
        mod_p = mod_all[l, :nb].reshape(nb, 1, 6 * D_MODEL)
        mod_s = jnp.repeat(mod_all[l, nb:nb + ns], dec_len, axis=0).reshape(1, ns * dec_len, 6 * D_MODEL)
        ins_p = _mixer_in(xp, mod_p[..., :2 * D_MODEL], lw["g_mix"], lw["w_in"], lw["v_gain"], lw["ws"],
                          lw["bs"], tm=tm, tiles_per_mod=seq_len // tm, want_v=False, ffn=ffn_p)
        ins_s = _mixer_in(xs, mod_s[..., :2 * D_MODEL], lw["g_mix"], lw["w_in"], lw["v_gain"], lw["ws_sample"],
                          lw["bs_sample"], tm=ts, tiles_per_mod=1, want_v=True, ffn=ffn_s)
        if l > 0:
            xp, xs = ins_p[2], ins_s[2]
        vs = ins_s[-1]
        y4_p, y4_s, hp_re, hp_im, hs_re, hs_im = _s5(
            ins_p[1], ins_s[1], lw["prep"], lw["d_tiled"], zeros_p, zeros_p,
            _state_in(state_s5_re[l], 1, ns), _state_in(state_s5_im[l], 1, ns),
            rows=seq_len // S5_CHUNK, nblk=nb)
        xp, xsort_p, prow_p, counts_p = _mixer_out(
            xp, ins_p[0], y4_p, mod_p[..., 2 * D_MODEL:5 * D_MODEL], lw["g_ffn"], lw["w_glu"], lw["b_glu"],
            lw["w_out"], lw["wr"], lw["b_r"], tm=tm, tiles_per_mod=seq_len // tm)
        xs, xsort_s, prow_s, counts_s = _mixer_out(
            xs, ins_s[0], y4_s, mod_s[..., 2 * D_MODEL:5 * D_MODEL], lw["g_ffn"], lw["w_glu"], lw["b_glu"],
            lw["w_out"], lw["wr"], lw["b_r"], tm=ts, tiles_per_mod=1)
        gf_p, gf_s = mod_p[..., 5 * D_MODEL:], mod_s[..., 5 * D_MODEL:]
        zs_p, zs_s = _moe_runs(
            xsort_p, counts_p, xsort_s, counts_s, wg_all, wu_all, wd_all, expert0=l * N_EXPERTS,
            max_rows=tiles_p * _sorted_rows_bound(tm) + _sorted_rows_bound(ts))
        ffn_p = (zs_p, prow_p, gf_p)
        ffn_s = (zs_s, prow_s, gf_s)
        sp_re.append(_state_out(hp_re, nb))
        sp_im.append(_state_out(hp_im, nb))
        ss_re.append(_state_out(hs_re, ns))
        ss_im.append(_state_out(hs_im, ns))
        v_new.append(vs.reshape(ns, dec_len, GMLP_WIDTH))
    yp = _final(xp, ffn_p, g_fin, tm=tm, tiles_per_mod=seq_len // tm)
    ys = _final(xs, ffn_s, g_fin, tm=ts, tiles_per_mod=1)
    return (yp.reshape(nb, seq_len, D_MODEL), ys.reshape(ns, dec_len, D_MODEL),
            jnp.stack(sp_re), jnp.stack(sp_im), jnp.stack(ss_re), jnp.stack(ss_im), jnp.stack(v_new))
```

```python
import functools

import jax
import jax.numpy as jnp
from jax import lax
from jax.experimental import pallas as pl
from jax.experimental.pallas import tpu as pltpu

F32 = jnp.float32
BF16 = jnp.bfloat16

D_MODEL = 1024
DEPTH = 2
CHUNK = 64
GMLP_CHUNK = 128
GMLP_WIDTH = 512
GMLP_HEADS = 4
GMLP_HEAD_DIM = 128
SSM_WIDTH = 512
SSM_GROUP = 16
SSM_GROUPS = 32
SSM_STATE = 64
IN_WIDTH = 1536
N_EXPERTS = 16
EXPERTS_PER_GROUP = 4
N_EXPERT_GROUPS = 4
D_EXPERT = 512
EPS = 1e-6

LANES = 128
S5_CHUNK = 16
GROUP_BLOCKS = 4
GROUPS_PER_BLOCK = SSM_GROUPS // GROUP_BLOCKS
STATE_BLOCK = GROUPS_PER_BLOCK * SSM_STATE
S5_ROW = S5_CHUNK * LANES
ADA_ROWS = 16
ADA_BLOCKS = 6
POW_ROWS = 24
GROUP_SHIFT = SSM_GROUP.bit_length() - 1
STATE_SHIFT = SSM_STATE.bit_length() - 1
EXPERT_GROUP_SHIFT = EXPERTS_PER_GROUP.bit_length() - 1
PAIRS_PER_GROUP = 6
N_CLASSES = N_EXPERT_GROUPS * PAIRS_PER_GROUP
CLASS_ROWS = 32
_PAIRS = [(a, b) for a in range(EXPERTS_PER_GROUP) for b in range(a + 1, EXPERTS_PER_GROUP)]
CLASS_LO = [g * EXPERTS_PER_GROUP + a for g in range(N_EXPERT_GROUPS) for a, _ in _PAIRS]
CLASS_HI = [g * EXPERTS_PER_GROUP + b for g in range(N_EXPERT_GROUPS) for _, b in _PAIRS]
MOE_TILE = 384
MIX_TILE = 512
RUN_ALIGN = 8
SORT_ROWS = 768
XS_WIDTH = D_MODEL + LANES
VMEM_LIMIT = 56 * 1024 * 1024


def _cparams(*sem):
    return pltpu.CompilerParams(dimension_semantics=sem, vmem_limit_bytes=VMEM_LIMIT)


def _ada_kernel(c_ref, w_ref, b_ref, o_ref):
    c = c_ref[...]
    s = (c * jax.nn.sigmoid(c)).astype(BF16)
    o_ref[0] = jnp.dot(s, w_ref[0].astype(BF16), preferred_element_type=F32) + b_ref[0]


def _ada(c_all, w_ada, b_ada):
    return pl.pallas_call(
        _ada_kernel,
        grid=(DEPTH, ADA_BLOCKS),
        in_specs=[
            pl.BlockSpec((ADA_ROWS, D_MODEL), lambda l, j: (0, 0)),
            pl.BlockSpec((1, D_MODEL, D_MODEL), lambda l, j: (l, 0, j)),
            pl.BlockSpec((1, 1, D_MODEL), lambda l, j: (l, 0, j)),
        ],
        out_specs=pl.BlockSpec((1, ADA_ROWS, D_MODEL), lambda l, j: (l, 0, j)),
        out_shape=jax.ShapeDtypeStruct((DEPTH, ADA_ROWS, 6 * D_MODEL), F32),
        compiler_params=_cparams("parallel", "parallel"),
        name="ada",
    )(c_all, w_ada, b_ada.reshape(DEPTH, 1, 6 * D_MODEL))


def _prep_kernel(are_ref, aim_ref, ldt_ref, bre_ref, bim_ref, cre_ref, cim_ref,
                 msre_ref, msim_ref, nt_ref, wrev_ref, a16re_ref, a16im_ref):
    a_re = are_ref[0]
    a_im = aim_ref[0]
    dt = jnp.exp(ldt_ref[0])
    rho = a_re * dt
    th = a_im * dt
    kk = jnp.minimum(lax.broadcasted_iota(jnp.int32, (POW_ROWS, STATE_BLOCK), 0), S5_CHUNK).astype(F32)
    mag = jnp.exp(kk * rho)
    pw_re = mag * jnp.cos(kk * th)
    pw_im = mag * jnp.sin(kk * th)

    lb_re = pw_re[1:2]
    lb_im = pw_im[1:2]
    num_re = lb_re - 1.0
    num_im = lb_im
    den = a_re * a_re + a_im * a_im
    coef_re = (num_re * a_re + num_im * a_im) / den
    coef_im = (num_im * a_re - num_re * a_im) / den
    b_re = bre_ref[0]
    b_im = bim_ref[0]
    bb_re = coef_re * b_re - coef_im * b_im
    bb_im = coef_re * b_im + coef_im * b_re

    rows = lax.broadcasted_iota(jnp.int32, (LANES, STATE_BLOCK), 0)
    cols = lax.broadcasted_iota(jnp.int32, (LANES, STATE_BLOCK), 1)
    same_group = (rows >> GROUP_SHIFT) == (cols >> STATE_SHIFT)

    def blockdiag(x16):
        return jnp.where(same_group, jnp.concatenate([x16] * GROUPS_PER_BLOCK, axis=0), 0.0)

    for s in range(S5_CHUNK):
        k = S5_CHUNK - 1 - s
        p_re = pw_re[k:k + 1]
        p_im = pw_im[k:k + 1]
        msre_ref[0, s * LANES:(s + 1) * LANES, :] = blockdiag(p_re * bb_re - p_im * bb_im).astype(BF16)
        msim_ref[0, s * LANES:(s + 1) * LANES, :] = blockdiag(p_re * bb_im + p_im * bb_re).astype(BF16)

    b_hi, b_lo = _split_bf16(jnp.concatenate([blockdiag(bb_re), blockdiag(bb_im)], axis=1))
    c_re = cre_ref[0]
    c_im = cim_ref[0]
    for k in range(S5_CHUNK + 1):
        p_re = pw_re[k:k + 1]
        p_im = pw_im[k:k + 1]
        cl = jnp.concatenate([blockdiag(c_re * p_re - c_im * p_im),
                              -blockdiag(c_re * p_im + c_im * p_re)], axis=1)
        if k >= 1:
            nt_ref[0, (k - 1) * LANES:k * LANES, :] = cl.astype(BF16)
        if k < S5_CHUNK:
            c_hi, c_lo = _split_bf16(cl)
            nt_dims = (((1,), (1,)), ((), ()))
            wl = (lax.dot_general(b_hi, c_hi, nt_dims, preferred_element_type=F32)
                  + lax.dot_general(b_hi, c_lo, nt_dims, preferred_element_type=F32)
                  + lax.dot_general(b_lo, c_hi, nt_dims, preferred_element_type=F32))
            j = S5_CHUNK - 1 - k
            wrev_ref[0, j * LANES:(j + 1) * LANES, :LANES] = wl.astype(BF16)
            if j >= 1:
                wrev_ref[0, (j - 1) * LANES:j * LANES, LANES:] = wl.astype(BF16)
    wrev_ref[0, (S5_CHUNK - 1) * LANES:, LANES:] = jnp.zeros((LANES, LANES), BF16)

    a16re_ref[0] = pw_re[S5_CHUNK:S5_CHUNK + 1]
    a16im_ref[0] = pw_im[S5_CHUNK:S5_CHUNK + 1]


def _s5_prep(a_re, a_im, log_dt, b_re, b_im, c_re, c_im):
    nstate = SSM_GROUPS * SSM_STATE

    def lane_row(v):
        return v.reshape(GROUP_BLOCKS, 1, STATE_BLOCK)

    def rows16(v):
        return v.reshape(SSM_GROUP, GROUP_BLOCKS, STATE_BLOCK).transpose(1, 0, 2)

    ldt = jnp.repeat(log_dt, SSM_STATE).reshape(SSM_GROUPS, SSM_STATE)
    bt_re = rows16(b_re.transpose(2, 0, 1).reshape(SSM_GROUP, nstate))
    bt_im = rows16(b_im.transpose(2, 0, 1).reshape(SSM_GROUP, nstate))
    ct_re = rows16(c_re.transpose(1, 0, 2).reshape(SSM_GROUP, nstate))
    ct_im = rows16(c_im.transpose(1, 0, 2).reshape(SSM_GROUP, nstate))
    row_spec = pl.BlockSpec((1, 1, STATE_BLOCK), lambda g: (g, 0, 0))
    r16_spec = pl.BlockSpec((1, SSM_GROUP, STATE_BLOCK), lambda g: (g, 0, 0))
    return pl.pallas_call(
        _prep_kernel,
        grid=(GROUP_BLOCKS,),
        in_specs=[row_spec, row_spec, row_spec, r16_spec, r16_spec, r16_spec, r16_spec],
        out_specs=[
            pl.BlockSpec((1, S5_ROW, STATE_BLOCK), lambda g: (g, 0, 0)),
            pl.BlockSpec((1, S5_ROW, STATE_BLOCK), lambda g: (g, 0, 0)),
            pl.BlockSpec((1, S5_ROW, 2 * STATE_BLOCK), lambda g: (g, 0, 0)),
            pl.BlockSpec((1, S5_ROW, 2 * LANES), lambda g: (g, 0, 0)),
            row_spec, row_spec,
        ],
        out_shape=[
            jax.ShapeDtypeStruct((GROUP_BLOCKS, S5_ROW, STATE_BLOCK), BF16),
            jax.ShapeDtypeStruct((GROUP_BLOCKS, S5_ROW, STATE_BLOCK), BF16),
            jax.ShapeDtypeStruct((GROUP_BLOCKS, S5_ROW, 2 * STATE_BLOCK), BF16),
            jax.ShapeDtypeStruct((GROUP_BLOCKS, S5_ROW, 2 * LANES), BF16),
            jax.ShapeDtypeStruct((GROUP_BLOCKS, 1, STATE_BLOCK), F32),
            jax.ShapeDtypeStruct((GROUP_BLOCKS, 1, STATE_BLOCK), F32),
        ],
        compiler_params=_cparams("parallel"),
        name="s5_prep",
    )(lane_row(a_re), lane_row(a_im), lane_row(ldt), bt_re, bt_im, ct_re, ct_im)


def _rms(x, g):
    return x * lax.rsqrt(jnp.mean(x * x, axis=-1, keepdims=True) + EPS) * g


def _unsorted_ffn(zs_ref, prow_ref):
    tm = prow_ref.shape[1]
    zs = zs_ref[...].astype(BF16)
    pcol = jnp.concatenate([prow_ref[...].astype(F32), jnp.zeros((LANES - 1, tm), F32)], axis=0).T[:, :1]
    lanes = lax.broadcasted_iota(jnp.int32, (tm, SORT_ROWS), 1)
    pick = jnp.where(lanes == pcol.astype(jnp.int32), 1.0, 0.0).astype(BF16)
    return jnp.dot(pick, zs, preferred_element_type=F32)


def _mixer_in_kernel(x_ref, *refs, tm, nsub, fused, want_v):
    for j in range(nsub):
        _mixer_in_tile(j, slice(j * tm, (j + 1) * tm), x_ref, refs, tm, fused, want_v)


def _mixer_in_tile(j, r, x_ref, refs, tm, fused, want_v):
    if fused:
        zs_ref, prow_ref, gf_ref = refs[:3]
        refs = refs[3:]
    mod_ref, g_ref, win_ref, vg_ref, ws_ref, bs_ref, a_ref, u_ref = refs[:8]
    refs = refs[8:]
    x = x_ref[r, :]
    if fused:
        x = x + gf_ref[0] * _unsorted_ffn(zs_ref.at[j * SORT_ROWS:(j + 1) * SORT_ROWS, :], prow_ref.at[:, r])
        refs[0][r, :] = x
        refs = refs[1:]
    mod = mod_ref[0]
    shift = mod[:, :D_MODEL]
    scale = mod[:, D_MODEL:]
    h = _rms(x, g_ref[...]) * (1.0 + scale) + shift
    proj = jnp.dot(h.astype(BF16), win_ref[...], preferred_element_type=F32)
    z = jax.nn.gelu(proj[:, :2 * GMLP_WIDTH])
    u = z[:, :GMLP_WIDTH]
    v = z[:, GMLP_WIDTH:]
    vc = v - jnp.mean(v, axis=-1, keepdims=True)
    vn = vc * lax.rsqrt(jnp.mean(vc * vc, axis=-1, keepdims=True) + EPS) * vg_ref[...]
    if want_v:
        refs[0][r, :] = vn
    vb = vn.astype(BF16)
    bias = bs_ref[...]
    for c in range(tm // GMLP_CHUNK):
        r0 = c * GMLP_CHUNK
        o0 = j * tm + r0
        for hh in range(GMLP_HEADS):
            l0 = hh * GMLP_HEAD_DIM
            mixed = jnp.dot(ws_ref[hh], vb[r0:r0 + GMLP_CHUNK, l0:l0 + GMLP_HEAD_DIM],
                            preferred_element_type=F32) + bias[:, l0:l0 + GMLP_HEAD_DIM]
            a_ref[o0:o0 + GMLP_CHUNK, l0:l0 + GMLP_HEAD_DIM] = (
                u[r0:r0 + GMLP_CHUNK, l0:l0 + GMLP_HEAD_DIM] * mixed).astype(BF16)
    for gb in range(GROUP_BLOCKS):
        l0 = 2 * GMLP_WIDTH + gb * LANES
        u_ref[gb, r, :] = proj[:, l0:l0 + LANES]


def _mixer_in(x2d, mod, g_mix, w_in, v_gain, ws, bs, *, tm, tiles_per_mod, want_v, ffn=None):
    t = x2d.shape[0]
    rmod = mod.shape[1]
    nsub = 2 if tiles_per_mod % 2 == 0 else 1
    rows = tm * nsub
    mods = tiles_per_mod // nsub
    const2 = lambda i: (0, 0)
    tok = pl.BlockSpec((rows, D_MODEL), lambda i: (i, 0))
    in_specs = [tok]
    args = [x2d]
    if ffn is not None:
        in_specs += _ffn_specs(tm, nsub, rmod, mods)
        args += list(ffn)
    in_specs += [
        pl.BlockSpec((1, rmod, 2 * D_MODEL), lambda i: (i // mods, 0, 0)),
        pl.BlockSpec((1, D_MODEL), const2),
        pl.BlockSpec((D_MODEL, IN_WIDTH), const2),
        pl.BlockSpec((1, GMLP_WIDTH), const2),
        pl.BlockSpec((GMLP_HEADS, GMLP_CHUNK, GMLP_CHUNK), lambda i: (0, 0, 0)),
        pl.BlockSpec((GMLP_CHUNK, GMLP_WIDTH), const2),
    ]
    args += [mod, g_mix, w_in, v_gain, ws, bs]
    out_shape = [jax.ShapeDtypeStruct((t, GMLP_WIDTH), BF16),
                 jax.ShapeDtypeStruct((GROUP_BLOCKS, t, LANES), F32)]
    out_specs = [pl.BlockSpec((rows, GMLP_WIDTH), lambda i: (i, 0)),
                 pl.BlockSpec((GROUP_BLOCKS, rows, LANES), lambda i: (0, i, 0))]
    if ffn is not None:
        out_shape.append(jax.ShapeDtypeStruct((t, D_MODEL), F32))
        out_specs.append(tok)
    if want_v:
        out_shape.append(jax.ShapeDtypeStruct((t, GMLP_WIDTH), F32))
        out_specs.append(pl.BlockSpec((rows, GMLP_WIDTH), lambda i: (i, 0)))
    return pl.pallas_call(
        functools.partial(_mixer_in_kernel, tm=tm, nsub=nsub, fused=ffn is not None, want_v=want_v),
        grid=(t // rows,),
        in_specs=in_specs,
        out_specs=out_specs,
        out_shape=out_shape,
        compiler_params=_cparams("parallel"),
        name="mixer_in",
    )(*args)


def _ffn_specs(tm, nsub, rmod, mods):
    return [
        pl.BlockSpec((nsub * SORT_ROWS, D_MODEL), lambda i: (i, 0)),
        pl.BlockSpec((1, nsub * tm), lambda i: (0, i)),
        pl.BlockSpec((1, rmod, D_MODEL), lambda i: (i // mods, 0, 0)),
    ]


def _final_kernel(x_ref, zs_ref, prow_ref, gf_ref, g_ref, o_ref, *, tm, nsub):
    for j in range(nsub):
        r = slice(j * tm, (j + 1) * tm)
        x = x_ref[r, :] + gf_ref[0] * _unsorted_ffn(zs_ref.at[j * SORT_ROWS:(j + 1) * SORT_ROWS, :],
                                                    prow_ref.at[:, r])
        o_ref[r, :] = _rms(x, g_ref[...])


def _final(xmid, ffn, g_final, *, tm, tiles_per_mod):
    t = xmid.shape[0]
    rmod = ffn[2].shape[1]
    nsub = 2 if tiles_per_mod % 2 == 0 else 1
    tok = pl.BlockSpec((tm * nsub, D_MODEL), lambda i: (i, 0))
    return pl.pallas_call(
        functools.partial(_final_kernel, tm=tm, nsub=nsub),
        grid=(t // (tm * nsub),),
        in_specs=[tok] + _ffn_specs(tm, nsub, rmod, tiles_per_mod // nsub)
        + [pl.BlockSpec((1, D_MODEL), lambda i: (0, 0))],
        out_specs=tok,
        out_shape=jax.ShapeDtypeStruct((t, D_MODEL), F32),
        compiler_params=_cparams("parallel"),
        name="final_norm",
    )(xmid, *ffn, g_final)


def _s5_kernel(u_ref, us_ref, msre_ref, msim_ref, nt_ref, wrev_ref, a16re_ref, a16im_ref, d_ref,
               h0re_ref, h0im_ref, h0sre_ref, h0sim_ref,
               y_ref, ys_ref, hnre_ref, hnim_ref, hnsre_ref, hnsim_ref,
               sre_scr, sim_scr, hre_scr, him_scr):
    rows = u_ref.shape[1] // S5_CHUNK
    srows = us_ref.shape[1] // S5_CHUNK

    def chunk_rows(ref, n):
        return jnp.concatenate([ref[0, pl.ds(s, n, stride=S5_CHUNK), :] for s in range(S5_CHUNK)], axis=1)

    u = jnp.concatenate([chunk_rows(u_ref, rows), chunk_rows(us_ref, srows)], axis=0)
    ub = u.astype(BF16)
    s_re = jnp.dot(ub, msre_ref[0], preferred_element_type=F32)
    s_im = jnp.dot(ub, msim_ref[0], preferred_element_type=F32)
    a_re = a16re_ref[0]
    a_im = a16im_ref[0]
    sre_scr[...] = s_re[:rows]
    sim_scr[...] = s_im[:rows]

    def body(r, carry):
        hr, hi = carry
        hre_scr[pl.ds(r, 1), :] = hr
        him_scr[pl.ds(r, 1), :] = hi
        sr = sre_scr[pl.ds(r, 1), :]
        si = sim_scr[pl.ds(r, 1), :]
        return (a_re * hr - a_im * hi + sr, a_re * hi + a_im * hr + si)

    hn_re, hn_im = lax.fori_loop(0, rows, body, (h0re_ref[0, 0], h0im_ref[0, 0]), unroll=8)
    hnre_ref[0, 0] = hn_re
    hnim_ref[0, 0] = hn_im
    hs_re = h0sre_ref[0, 0]
    hs_im = h0sim_ref[0, 0]
    hnsre_ref[0, 0] = a_re * hs_re - a_im * hs_im + s_re[rows:]
    hnsim_ref[0, 0] = a_re * hs_im + a_im * hs_re + s_im[rows:]
    hcat = jnp.concatenate([jnp.concatenate([hre_scr[...], hs_re], axis=0),
                            jnp.concatenate([him_scr[...], hs_im], axis=0)], axis=1).astype(BF16)
    inter = lax.dot_general(hcat, nt_ref[0], (((1,), (1,)), ((), ())),
                            preferred_element_type=F32)
    d = d_ref[0]
    for t in range(0, S5_CHUNK, 2):
        k0 = (S5_CHUNK - 2 - t) * LANES
        pair = jnp.dot(ub[:, :(t + 2) * LANES], wrev_ref[0, k0:, :], preferred_element_type=F32)
        for step, intra in ((t + 1, pair[:, :LANES]), (t, pair[:, LANES:])):
            sl = slice(step * LANES, (step + 1) * LANES)
            out = intra + inter[:, sl] + d[:, sl] * u[:, sl]
            y_ref[0, pl.ds(step, rows, stride=S5_CHUNK), :] = out[:rows]
            ys_ref[0, pl.ds(step, srows, stride=S5_CHUNK), :] = out[rows:]


def _s5(u4, u4s, prep, d_tiled, h0_re, h0_im, h0s_re, h0s_im, *, rows, nblk):
    ms_re, ms_im, nt, wrev, a16_re, a16_im = prep
    srows = h0s_re.shape[2]
    wspec = lambda shape: pl.BlockSpec((1,) + shape, lambda g, b: (g, 0, 0))
    hspec = pl.BlockSpec((1, 1, 1, STATE_BLOCK), lambda g, b: (g, b, 0, 0))
    hsspec = pl.BlockSpec((1, 1, srows, STATE_BLOCK), lambda g, b: (g, 0, 0, 0))
    uspec = pl.BlockSpec((1, rows * S5_CHUNK, LANES), lambda g, b: (g, b, 0))
    usspec = pl.BlockSpec((1, srows * S5_CHUNK, LANES), lambda g, b: (g, 0, 0))
    return pl.pallas_call(
        _s5_kernel,
        grid=(GROUP_BLOCKS, nblk),
        in_specs=[
            uspec, usspec,
            wspec((S5_ROW, STATE_BLOCK)), wspec((S5_ROW, STATE_BLOCK)),
            wspec((S5_ROW, 2 * STATE_BLOCK)), wspec((S5_ROW, 2 * LANES)),
            wspec((1, STATE_BLOCK)), wspec((1, STATE_BLOCK)), wspec((1, S5_ROW)),
            hspec, hspec, hsspec, hsspec,
        ],
        out_specs=[uspec, usspec, hspec, hspec, hsspec, hsspec],
        out_shape=[
            jax.ShapeDtypeStruct(u4.shape, F32), jax.ShapeDtypeStruct(u4s.shape, F32),
            jax.ShapeDtypeStruct(h0_re.shape, F32), jax.ShapeDtypeStruct(h0_re.shape, F32),
            jax.ShapeDtypeStruct(h0s_re.shape, F32), jax.ShapeDtypeStruct(h0s_re.shape, F32),
        ],
        scratch_shapes=[pltpu.VMEM((rows, STATE_BLOCK), F32)] * 4,
        compiler_params=_cparams("parallel", "arbitrary"),
        name="s5",
    )(u4, u4s, ms_re, ms_im, nt, wrev, a16_re, a16_im, d_tiled, h0_re, h0_im, h0s_re, h0s_im)


def _split_bf16(x):
    hi = x.astype(BF16)
    return hi, (x - hi.astype(F32)).astype(BF16)


def _top2_of4(a):
    m1 = jnp.maximum(jnp.maximum(a[0], a[1]), jnp.maximum(a[2], a[3]))
    i1 = jnp.where(a[0] == m1, 0, jnp.where(a[1] == m1, 1, jnp.where(a[2] == m1, 2, 3)))
    b = [jnp.where(i1 == j, -jnp.inf, a[j]) for j in range(4)]
    m2 = jnp.maximum(jnp.maximum(b[0], b[1]), jnp.maximum(b[2], b[3]))
    i2 = jnp.where(b[0] == m2, 0, jnp.where(b[1] == m2, 1, jnp.where(b[2] == m2, 2, 3)))
    return m1, i1, m2, i2


def _route_rows(h2, wr_ref, br_ref):
    h_hi, h_lo = _split_bf16(h2)
    parts = jnp.dot(jnp.concatenate([h_hi, h_lo], axis=1), wr_ref[...], preferred_element_type=F32)
    pt = parts.T
    lt = pt[:N_EXPERTS] + pt[N_EXPERTS:2 * N_EXPERTS] + br_ref[...]
    rows = [lt[e:e + 1] for e in range(N_EXPERTS)]
    mx = functools.reduce(jnp.maximum, rows)
    ex = [jnp.exp(r - mx) for r in rows]
    tot = functools.reduce(lambda p, q: p + q, ex)
    scores = [e / tot for e in ex]
    best = None
    for g in range(N_EXPERT_GROUPS):
        m1, i1, m2, i2 = _top2_of4(scores[g * EXPERTS_PER_GROUP:(g + 1) * EXPERTS_PER_GROUP])
        cand = (m1 + m2, m1, i1 + g * EXPERTS_PER_GROUP, m2, i2 + g * EXPERTS_PER_GROUP)
        if best is None:
            best = cand
        else:
            better = cand[0] > best[0]
            best = tuple(jnp.where(better, c, b) for c, b in zip(cand, best))
    _, v1, e1, v2, e2 = best
    den = v1 + v2
    w1 = v1 / den
    w2 = v2 / den
    first_lo = e1 < e2
    return (jnp.where(first_lo, e1, e2), jnp.where(first_lo, e2, e1),
            jnp.where(first_lo, w1, w2), jnp.where(first_lo, w2, w1))


def _mixer_out_kernel(x_ref, a_ref, y_ref, mod_ref, g_ref, wglu_ref, bglu_ref, wout_ref,
                      wr_ref, br_ref, tri_ref, ltri_ref, *outs, tm, nsub):
    for j in range(nsub):
        _mixer_out_tile(j, slice(j * tm, (j + 1) * tm), x_ref, a_ref, y_ref, mod_ref, g_ref, wglu_ref,
                        bglu_ref, wout_ref, wr_ref, br_ref, tri_ref, ltri_ref, *outs)


def _mixer_out_tile(j, r, x_ref, a_ref, y_ref, mod_ref, g_ref, wglu_ref, bglu_ref, wout_ref,
                    wr_ref, br_ref, tri_ref, ltri_ref, xmid_ref, xs_ref, prow_ref, cnt_ref):
    x = x_ref[r, :]
    mod = mod_ref[0]
    gate_m = mod[:, :D_MODEL]
    shift_f = mod[:, D_MODEL:2 * D_MODEL]
    scale_f = mod[:, 2 * D_MODEL:]
    ys = jax.nn.gelu(jnp.concatenate([y_ref[gb, r, :] for gb in range(GROUP_BLOCKS)], axis=1))
    glu = jnp.dot(ys.astype(BF16), wglu_ref[...], preferred_element_type=F32) + bglu_ref[...]
    b_out = ys * jax.nn.sigmoid(glu)
    mixed = jnp.concatenate([a_ref[r, :], b_out.astype(BF16)], axis=1)
    xmid = x + gate_m * jnp.dot(mixed, wout_ref[...], preferred_element_type=F32)
    h2 = _rms(xmid, g_ref[...]) * (1.0 + scale_f) + shift_f
    lo, hi, glo, ghi = _route_rows(h2, wr_ref, br_ref)
    tm = x.shape[0]
    xmid_ref[r, :] = xmid
    a = lo & (EXPERTS_PER_GROUP - 1)
    b = hi & (EXPERTS_PER_GROUP - 1)
    pair = jnp.where(a == 0, 0, jnp.where(a == 1, 3, 5)) + (b - a - 1)
    cls = (lo >> EXPERT_GROUP_SHIFT) * PAIRS_PER_GROUP + pair
    onehot = lax.broadcasted_iota(jnp.int32, (CLASS_ROWS, tm), 0) == cls
    prefix = jnp.dot(jnp.where(onehot, 1.0, 0.0).astype(BF16), tri_ref[...],
                     preferred_element_type=F32)
    total = prefix[:, tm - 1:tm]
    cnt_ref[j] = jnp.broadcast_to(total, (CLASS_ROWS, LANES))
    groups = jnp.floor((total + (RUN_ALIGN - 1)) * (1.0 / RUN_ALIGN))
    before = jnp.dot(ltri_ref[...], jnp.broadcast_to(groups, (CLASS_ROWS, LANES)).astype(BF16),
                     preferred_element_type=F32)[:, :1] * RUN_ALIGN
    prow = jnp.sum(jnp.where(onehot, before + prefix - 1.0, 0.0), axis=0, keepdims=True).astype(jnp.int32)
    prow_ref[:, r] = prow
    pick = jnp.where(lax.broadcasted_iota(jnp.int32, (SORT_ROWS, tm), 0) == prow, 1.0, 0.0).astype(BF16)
    glo_hi, glo_lo = _split_bf16(glo)
    ghi_hi, ghi_lo = _split_bf16(ghi)
    gates = jnp.concatenate([glo_hi.astype(F32), glo_lo.astype(F32), ghi_hi.astype(F32), ghi_lo.astype(F32),
                             jnp.zeros((LANES - 4, tm), F32)], axis=0).T
    payload = jnp.concatenate([h2.astype(BF16), gates.astype(BF16)], axis=1)
    xs_ref[j * SORT_ROWS:(j + 1) * SORT_ROWS, :] = jnp.dot(pick, payload, preferred_element_type=F32)


def _mixer_out(x2d, a_out, y4, mod, g_ffn, w_glu, b_glu, w_out, wr, b_r, *, tm, tiles_per_mod):
    t = x2d.shape[0]
    rmod = mod.shape[1]
    nsub = 2 if tiles_per_mod % 2 == 0 else 1
    rows = tm * nsub
    const2 = lambda i: (0, 0)
    tok = pl.BlockSpec((rows, D_MODEL), lambda i: (i, 0))
    ids = jnp.arange(tm)
    cids = jnp.arange(CLASS_ROWS)
    return pl.pallas_call(
        functools.partial(_mixer_out_kernel, tm=tm, nsub=nsub),
        grid=(t // rows,),
        in_specs=[
            tok,
            pl.BlockSpec((rows, GMLP_WIDTH), lambda i: (i, 0)),
            pl.BlockSpec((GROUP_BLOCKS, rows, LANES), lambda i: (0, i, 0)),
            pl.BlockSpec((1, rmod, 3 * D_MODEL), lambda i: (i // (tiles_per_mod // nsub), 0, 0)),
            pl.BlockSpec((1, D_MODEL), const2),
            pl.BlockSpec((SSM_WIDTH, SSM_WIDTH), const2),
            pl.BlockSpec((1, SSM_WIDTH), const2),
            pl.BlockSpec((D_MODEL, D_MODEL), const2),
            pl.BlockSpec((2 * D_MODEL, LANES), const2),
            pl.BlockSpec((N_EXPERTS, 1), const2),
            pl.BlockSpec((tm, tm), const2),
            pl.BlockSpec((CLASS_ROWS, CLASS_ROWS), const2),
        ],
        out_specs=[tok, pl.BlockSpec((nsub * SORT_ROWS, XS_WIDTH), lambda i: (i, 0)),
                   pl.BlockSpec((1, rows), lambda i: (0, i)),
                   pl.BlockSpec((nsub, CLASS_ROWS, LANES), lambda i: (i, 0, 0))],
        out_shape=[jax.ShapeDtypeStruct((t, D_MODEL), F32),
                   jax.ShapeDtypeStruct((t // tm * SORT_ROWS, XS_WIDTH), F32),
                   jax.ShapeDtypeStruct((1, t), jnp.int32),
                   jax.ShapeDtypeStruct((t // tm, CLASS_ROWS, LANES), F32)],
        compiler_params=_cparams("parallel"),
        name="mixer_out",
    )(x2d, a_out, y4, mod, g_ffn, w_glu, b_glu, w_out, wr, b_r,
      (ids[:, None] <= ids[None, :]).astype(BF16), (cids[None, :] < cids[:, None]).astype(BF16))


def _moe_runs_kernel(tlo_ref, thi_ref, tval_ref, psrc_ref, pdst_ref, plen_ref, kfirst_ref, kend_ref, used_ref,
                     xsp_hbm, xss_hbm, wgl_ref, wul_ref, wdl_ref, wgh_ref, wuh_ref, wdh_ref,
                     zsp_hbm, zss_hbm, buf_ref, obuf_ref, sem_in, sem_out, *, nsrc, nsrc_p):
    i = pl.program_id(0)
    nt = pl.num_programs(0)
    slot = i % 2
    other = 1 - slot
    nv = tval_ref[i]
    nv_next = jnp.where(i + 1 < nt, tval_ref[jnp.minimum(i + 1, nt - 1)], 0)
    nv_prev = jnp.where(i >= 1, tval_ref[jnp.maximum(i - 1, 0)], 0)
    nv_prev2 = jnp.where(i >= 2, tval_ref[jnp.maximum(i - 2, 0)], 0)

    def rows8(v):
        return pl.multiple_of(v, RUN_ALIGN)

    def for_pieces(tile, fn):
        def piece(k, xs_hbm, zs_hbm):
            n = plen_ref[tile * nsrc + k]

            @pl.when(n > 0)
            def _():
                fn(xs_hbm, zs_hbm, rows8(psrc_ref[tile * nsrc + k]), rows8(pdst_ref[tile * nsrc + k]),
                   rows8(n))

        first = kfirst_ref[tile]
        end = kend_ref[tile]
        lax.fori_loop(first, jnp.minimum(end, nsrc_p), lambda k, c: (piece(k, xsp_hbm, zsp_hbm), c)[1], 0)
        lax.fori_loop(jnp.maximum(first, nsrc_p), end, lambda k, c: (piece(k, xss_hbm, zss_hbm), c)[1], 0)

    def gather_start(tile, sl):
        for_pieces(tile, lambda xs_hbm, zs_hbm, src, dst, n: pltpu.make_async_copy(
            xs_hbm.at[pl.ds(src, n), :], buf_ref.at[sl, pl.ds(dst, n), :], sem_in.at[sl]).start())

    def gather_wait(sl, n):
        pltpu.make_async_copy(xsp_hbm.at[pl.ds(0, rows8(n)), :], buf_ref.at[sl, pl.ds(0, rows8(n)), :],
                              sem_in.at[sl]).wait()

    def scatter_start(tile, sl):
        for_pieces(tile, lambda xs_hbm, zs_hbm, src, dst, n: pltpu.make_async_copy(
            obuf_ref.at[sl, pl.ds(dst, n), :], zs_hbm.at[pl.ds(src, n), :], sem_out.at[sl]).start())

    def scatter_wait(sl, n):
        pltpu.make_async_copy(obuf_ref.at[sl, pl.ds(0, rows8(n)), :], zsp_hbm.at[pl.ds(0, rows8(n)), :],
                              sem_out.at[sl]).wait()

    @pl.when(i == 0)
    def _():
        buf_ref[...] = jnp.zeros_like(buf_ref)
        gather_start(0, 0)
        obuf_ref[1] = jnp.zeros((MOE_TILE, D_MODEL), F32)

        def tail(zs_hbm, k0, k, start):
            for first in range(0, SORT_ROWS, MOE_TILE):
                lo_row = rows8(jnp.maximum(used_ref[k], first))
                n = rows8(jnp.maximum(jnp.minimum(first + MOE_TILE, SORT_ROWS) - lo_row, 0))
                copy = pltpu.make_async_copy(obuf_ref.at[1, pl.ds(0, n), :],
                                             zs_hbm.at[pl.ds(rows8((k - k0) * SORT_ROWS + lo_row), n), :],
                                             sem_out.at[1])

                @pl.when(n > 0)
                def _():
                    copy.start() if start else copy.wait()

        for start in (True, False):
            lax.fori_loop(0, nsrc_p, lambda k, c: (tail(zsp_hbm, 0, k, start), c)[1], 0)
            lax.fori_loop(nsrc_p, nsrc, lambda k, c: (tail(zss_hbm, nsrc_p, k, start), c)[1], 0)

    @pl.when(nv_next > 0)
    def _():
        gather_start(i + 1, other)

    @pl.when(nv_prev2 > 0)
    def _():
        scatter_wait(slot, nv_prev2)

    @pl.when(nv > 0)
    def _():
        gather_wait(slot, nv)
        buf = buf_ref[slot]
        h = buf[:, :D_MODEL].astype(BF16)
        glo = buf[:, D_MODEL:D_MODEL + 1] + buf[:, D_MODEL + 1:D_MODEL + 2]
        ghi = buf[:, D_MODEL + 2:D_MODEL + 3] + buf[:, D_MODEL + 3:D_MODEL + 4]
        ffn = None
        for wg, wu, wd, gate in ((wgl_ref, wul_ref, wdl_ref, glo), (wgh_ref, wuh_ref, wdh_ref, ghi)):
            he = (jax.nn.silu(jnp.dot(h, wg[0], preferred_element_type=F32))
                  * jnp.dot(h, wu[0], preferred_element_type=F32))
            y = gate * jnp.dot(he.astype(BF16), wd[0], preferred_element_type=F32)
            ffn = y if ffn is None else ffn + y
        obuf_ref[slot] = ffn.astype(BF16).astype(F32)
        scatter_start(i, slot)

    @pl.when(i == nt - 1)
    def _():
        @pl.when(nv_prev > 0)
        def _():
            scatter_wait(other, nv_prev)

        @pl.when(nv > 0)
        def _():
            scatter_wait(slot, nv)


def _moe_runs(xs_p, counts_p, xs_s, counts_s, wg, wu, wd, *, max_rows, expert0):
    counts = jnp.concatenate([counts_p, counts_s], axis=0)
    nmix = counts.shape[0]
    nmix_p = counts_p.shape[0]
    ntiles = -(-max_rows // MOE_TILE) + N_CLASSES
    n_kc = counts[:, :N_CLASSES, 0].astype(jnp.int32)
    len_kc = (n_kc + RUN_ALIGN - 1) // RUN_ALIGN * RUN_ALIGN
    off_kc = jnp.cumsum(len_kc, axis=1) - len_kc
    used_k = jnp.sum(len_kc, axis=1)
    start_kc = jnp.cumsum(len_kc, axis=0) - len_kc
    region_c = jnp.sum(len_kc, axis=0)
    tiles_c = (region_c + MOE_TILE - 1) // MOE_TILE
    tile_end = jnp.cumsum(tiles_c)
    tile_start = tile_end - tiles_c
    used = tile_end[-1]
    tid = jnp.arange(ntiles, dtype=jnp.int32)
    tcls = jnp.sum(tile_end[None, :] <= jnp.minimum(tid, used - 1)[:, None], axis=1).astype(jnp.int32)
    lo_row = (tid - tile_start[tcls]) * MOE_TILE
    tval = jnp.where(tid < used, jnp.clip(region_c[tcls] - lo_row, 0, MOE_TILE), 0)
    run_lo = start_kc[:, tcls].T
    run_hi = run_lo + len_kc[:, tcls].T
    piece_lo = jnp.maximum(run_lo, lo_row[:, None])
    piece_hi = jnp.minimum(run_hi, (lo_row + tval)[:, None])
    plen = jnp.maximum(piece_hi - piece_lo, 0)
    kk = jnp.arange(nmix, dtype=jnp.int32)
    first_row = jnp.where(kk < nmix_p, kk, kk - nmix_p) * SORT_ROWS
    psrc = first_row[None, :] + off_kc[:, tcls].T + (piece_lo - run_lo)
    pdst = piece_lo - lo_row[:, None]
    flat = lambda v: jnp.where(plen > 0, v, 0).reshape(-1).astype(jnp.int32)
    tlo = jnp.asarray(CLASS_LO, jnp.int32)[tcls]
    thi = jnp.asarray(CLASS_HI, jnp.int32)[tcls]
    kfirst = jnp.sum(run_hi <= lo_row[:, None], axis=1).astype(jnp.int32)
    kend = jnp.sum(run_lo < (lo_row + tval)[:, None], axis=1).astype(jnp.int32)
    nprefetch = 9
    wspec_lo = lambda shape: pl.BlockSpec(shape, lambda i, tlo, *_: (expert0 + tlo[i], 0, 0))
    wspec_hi = lambda shape: pl.BlockSpec(shape, lambda i, tlo, thi, *_: (expert0 + thi[i], 0, 0))
    up = (1, D_MODEL, D_EXPERT)
    down = (1, D_EXPERT, D_MODEL)
    return pl.pallas_call(
        functools.partial(_moe_runs_kernel, nsrc=nmix, nsrc_p=nmix_p),
        grid_spec=pltpu.PrefetchScalarGridSpec(
            num_scalar_prefetch=nprefetch,
            grid=(ntiles,),
            in_specs=[
                pl.BlockSpec(memory_space=pl.ANY), pl.BlockSpec(memory_space=pl.ANY),
                wspec_lo(up), wspec_lo(up), wspec_lo(down),
                wspec_hi(up), wspec_hi(up), wspec_hi(down),
            ],
            out_specs=[pl.BlockSpec(memory_space=pl.ANY), pl.BlockSpec(memory_space=pl.ANY)],
            scratch_shapes=[
                pltpu.VMEM((2, MOE_TILE, XS_WIDTH), F32),
                pltpu.VMEM((2, MOE_TILE, D_MODEL), F32),
                pltpu.SemaphoreType.DMA((2,)),
                pltpu.SemaphoreType.DMA((2,)),
            ],
        ),
        out_shape=[jax.ShapeDtypeStruct((xs_p.shape[0], D_MODEL), F32),
                   jax.ShapeDtypeStruct((xs_s.shape[0], D_MODEL), F32)],
        compiler_params=_cparams("arbitrary"),
        name="moe_runs",
    )(tlo, thi, tval.astype(jnp.int32), flat(psrc), flat(pdst), flat(plen), kfirst, kend,
      used_k.astype(jnp.int32), xs_p, xs_s, wg, wu, wd, wg, wu, wd)


def _sorted_rows_bound(tm):
    return min(SORT_ROWS, tm + N_CLASSES * (RUN_ALIGN - 1))


def _state_out(h, nseq):
    return h.reshape(GROUP_BLOCKS, nseq, GROUPS_PER_BLOCK, SSM_STATE).transpose(1, 0, 2, 3).reshape(
        nseq, SSM_GROUPS, SSM_STATE)


def _state_in(h, nblk, rh):
    nseq = h.shape[0]
    return h.reshape(nseq, GROUP_BLOCKS, STATE_BLOCK).transpose(1, 0, 2).reshape(
        GROUP_BLOCKS, nblk, rh, STATE_BLOCK)


def kernel(x_prompt, x_sample, c_prompt, c_sample, state_s5_re, state_s5_im, w_ada, b_ada, g_norm_mix, g_norm_ffn, w_in, gmlp_v_gain, gmlp_w_spatial, gmlp_b_spatial, s5_a_re, s5_a_im, s5_log_dt, s5_b_re, s5_b_im, s5_c_re, s5_c_im, s5_d, s5_w_glu, s5_b_glu, w_out, w_router, b_router, w_gate, w_up, w_down, g_final):
    nb, seq_len, _ = x_prompt.shape
    ns, dec_len, _ = x_sample.shape
    assert dec_len == S5_CHUNK and ns * dec_len == GMLP_CHUNK and nb + ns <= ADA_ROWS
    assert seq_len % GMLP_CHUNK == 0 and MIX_TILE + N_CLASSES * (RUN_ALIGN - 1) <= SORT_ROWS

    c_all = jnp.concatenate([c_prompt, c_sample, jnp.zeros((ADA_ROWS - nb - ns, D_MODEL), F32)], axis=0)
    mod_all = _ada(c_all, w_ada, b_ada)

    pos = jnp.arange(GMLP_CHUNK)
    causal = (pos[None, :] // CHUNK) <= (pos[:, None] // CHUNK)
    wr_hi = w_router.astype(BF16)
    wr_lo = (w_router - wr_hi.astype(F32)).astype(BF16)
    wr = jnp.pad(jnp.concatenate([wr_hi, wr_lo], axis=1), ((0, 0), (0, LANES - 2 * N_EXPERTS)))
    wr = jnp.concatenate([wr, wr], axis=0)
    b_r = b_router.reshape(N_EXPERTS, 1)
    g_fin = g_final.reshape(1, D_MODEL)
    eye_s = jnp.eye(ns, dtype=F32)

    xp = x_prompt.reshape(nb * seq_len, D_MODEL)
    xs = x_sample.reshape(ns * dec_len, D_MODEL)
    zeros_p = jnp.zeros((GROUP_BLOCKS, nb, 1, STATE_BLOCK), F32)
    wg_all = w_gate.astype(BF16).reshape(DEPTH * N_EXPERTS, D_MODEL, D_EXPERT)
    wu_all = w_up.astype(BF16).reshape(DEPTH * N_EXPERTS, D_MODEL, D_EXPERT)
    wd_all = w_down.astype(BF16).reshape(DEPTH * N_EXPERTS, D_EXPERT, D_MODEL)
    sp_re, sp_im, ss_re, ss_im, v_new = [], [], [], [], []
    tm = min(MIX_TILE, seq_len)
    ts = ns * dec_len
    tiles_p = nb * seq_len // tm
    ffn_p = ffn_s = None
    for l in range(DEPTH):
        ws = jnp.where(causal[None], gmlp_w_spatial[l], 0.0)
        ws_sample = jnp.einsum("ab,hij->haibj", eye_s, ws[:, :dec_len, :dec_len]).reshape(
            GMLP_HEADS, GMLP_CHUNK, GMLP_CHUNK)
        bs = jnp.repeat(gmlp_b_spatial[l].T, GMLP_HEAD_DIM, axis=1)
        lw = dict(
            g_mix=g_norm_mix[l].reshape(1, D_MODEL), g_ffn=g_norm_ffn[l].reshape(1, D_MODEL),
            w_in=w_in[l].astype(BF16), v_gain=gmlp_v_gain[l].reshape(1, GMLP_WIDTH),
            ws=ws.astype(BF16), ws_sample=ws_sample.astype(BF16),
            bs=bs, bs_sample=jnp.tile(bs[:dec_len], (ns, 1)),
            prep=_s5_prep(s5_a_re[l], s5_a_im[l], s5_log_dt[l], s5_b_re[l], s5_b_im[l],
                          s5_c_re[l], s5_c_im[l]),
            d_tiled=jnp.tile(s5_d[l].reshape(GROUP_BLOCKS, 1, LANES), (1, 1, S5_CHUNK)),
            w_glu=s5_w_glu[l].astype(BF16), b_glu=s5_b_glu[l].reshape(1, SSM_WIDTH),
            w_out=w_out[l].astype(BF16), wr=wr, b_r=b_r,
        )
        mod_p = mod_all[l, :nb].reshape(nb, 1, 6 * D_MODEL)
        mod_s = jnp.repeat(mod_all[l, nb:nb + ns], dec_len, axis=0).reshape(1, ns * dec_len, 6 * D_MODEL)
        ins_p = _mixer_in(xp, mod_p[..., :2 * D_MODEL], lw["g_mix"], lw["w_in"], lw["v_gain"], lw["ws"],
                          lw["bs"], tm=tm, tiles_per_mod=seq_len // tm, want_v=False, ffn=ffn_p)
        ins_s = _mixer_in(xs, mod_s[..., :2 * D_MODEL], lw["g_mix"], lw["w_in"], lw["v_gain"], lw["ws_sample"],
                          lw["bs_sample"], tm=ts, tiles_per_mod=1, want_v=True, ffn=ffn_s)
        if l > 0:
            xp, xs = ins_p[2], ins_s[2]
        vs = ins_s[-1]
        y4_p, y4_s, hp_re, hp_im, hs_re, hs_im = _s5(
            ins_p[1], ins_s[1], lw["prep"], lw["d_tiled"], zeros_p, zeros_p,
            _state_in(state_s5_re[l], 1, ns), _state_in(state_s5_im[l], 1, ns),
            rows=seq_len // S5_CHUNK, nblk=nb)
        xp, xsort_p, prow_p, counts_p = _mixer_out(
            xp, ins_p[0], y4_p, mod_p[..., 2 * D_MODEL:5 * D_MODEL], lw["g_ffn"], lw["w_glu"], lw["b_glu"],
            lw["w_out"], lw["wr"], lw["b_r"], tm=tm, tiles_per_mod=seq_len // tm)
        xs, xsort_s, prow_s, counts_s = _mixer_out(
            xs, ins_s[0], y4_s, mod_s[..., 2 * D_MODEL:5 * D_MODEL], lw["g_ffn"], lw["w_glu"], lw["b_glu"],
            lw["w_out"], lw["wr"], lw["b_r"], tm=ts, tiles_per_mod=1)
        gf_p, gf_s = mod_p[..., 5 * D_MODEL:], mod_s[..., 5 * D_MODEL:]
        zs_p, zs_s = _moe_runs(
            xsort_p, counts_p, xsort_s, counts_s, wg_all, wu_all, wd_all, expert0=l * N_EXPERTS,
            max_rows=tiles_p * _sorted_rows_bound(tm) + _sorted_rows_bound(ts))
        ffn_p = (zs_p, prow_p, gf_p)
        ffn_s = (zs_s, prow_s, gf_s)
        sp_re.append(_state_out(hp_re, nb))
        sp_im.append(_state_out(hp_im, nb))
        ss_re.append(_state_out(hs_re, ns))
        ss_im.append(_state_out(hs_im, ns))
        v_new.append(vs.reshape(ns, dec_len, GMLP_WIDTH))
    yp = _final(xp, ffn_p, g_fin, tm=tm, tiles_per_mod=seq_len // tm)
    ys = _final(xs, ffn_s, g_fin, tm=ts, tiles_per_mod=1)
    return (yp.reshape(nb, seq_len, D_MODEL), ys.reshape(ns, dec_len, D_MODEL),
            jnp.stack(sp_re), jnp.stack(sp_im), jnp.stack(ss_re), jnp.stack(ss_im), jnp.stack(v_new))
```

```python
import functools

import jax
import jax.numpy as jnp
from jax import lax
from jax.experimental import pallas as pl
from jax.experimental.pallas import tpu as pltpu

F32 = jnp.float32
BF16 = jnp.bfloat16

D_MODEL = 1024
DEPTH = 2
CHUNK = 64
GMLP_CHUNK = 128
GMLP_WIDTH = 512
GMLP_HEADS = 4
GMLP_HEAD_DIM = 128
SSM_WIDTH = 512
SSM_GROUP = 16
SSM_GROUPS = 32
SSM_STATE = 64
IN_WIDTH = 1536
N_EXPERTS = 16
EXPERTS_PER_GROUP = 4
N_EXPERT_GROUPS = 4
D_EXPERT = 512
EPS = 1e-6

LANES = 128
S5_CHUNK = 16
GROUP_BLOCKS = 4
GROUPS_PER_BLOCK = SSM_GROUPS // GROUP_BLOCKS
STATE_BLOCK = GROUPS_PER_BLOCK * SSM_STATE
S5_ROW = S5_CHUNK * LANES
ADA_ROWS = 16
ADA_BLOCKS = 6
POW_ROWS = 24
GROUP_SHIFT = SSM_GROUP.bit_length() - 1
STATE_SHIFT = SSM_STATE.bit_length() - 1
EXPERT_GROUP_SHIFT = EXPERTS_PER_GROUP.bit_length() - 1
PAIRS_PER_GROUP = 6
N_CLASSES = N_EXPERT_GROUPS * PAIRS_PER_GROUP
CLASS_ROWS = 32
_PAIRS = [(0, 1), (0, 2), (1, 2), (1, 3), (2, 3), (0, 3)]
assert EXPERTS_PER_GROUP == 4 and len(_PAIRS) == PAIRS_PER_GROUP
CLASS_LO = [g * EXPERTS_PER_GROUP + a for g in range(N_EXPERT_GROUPS) for a, _ in _PAIRS]
CLASS_HI = [g * EXPERTS_PER_GROUP + b for g in range(N_EXPERT_GROUPS) for _, b in _PAIRS]
MOE_TILE = 384
MIX_TILE = 512
RUN_ALIGN = 8
SORT_ROWS = 768
XS_WIDTH = D_MODEL + LANES
VMEM_LIMIT = 56 * 1024 * 1024


def _cparams(*sem):
    return pltpu.CompilerParams(dimension_semantics=sem, vmem_limit_bytes=VMEM_LIMIT)


def _ada_kernel(c_ref, w_ref, b_ref, o_ref):
    c = c_ref[...]
    s = (c * jax.nn.sigmoid(c)).astype(BF16)
    o_ref[0] = jnp.dot(s, w_ref[0].astype(BF16), preferred_element_type=F32) + b_ref[0]


def _ada(c_all, w_ada, b_ada):
    return pl.pallas_call(
        _ada_kernel,
        grid=(DEPTH, ADA_BLOCKS),
        in_specs=[
            pl.BlockSpec((ADA_ROWS, D_MODEL), lambda l, j: (0, 0)),
            pl.BlockSpec((1, D_MODEL, D_MODEL), lambda l, j: (l, 0, j)),
            pl.BlockSpec((1, 1, D_MODEL), lambda l, j: (l, 0, j)),
        ],
        out_specs=pl.BlockSpec((1, ADA_ROWS, D_MODEL), lambda l, j: (l, 0, j)),
        out_shape=jax.ShapeDtypeStruct((DEPTH, ADA_ROWS, 6 * D_MODEL), F32),
        compiler_params=_cparams("parallel", "parallel"),
        name="ada",
    )(c_all, w_ada, b_ada.reshape(DEPTH, 1, 6 * D_MODEL))


def _prep_kernel(are_ref, aim_ref, ldt_ref, bre_ref, bim_ref, cre_ref, cim_ref,
                 msre_ref, msim_ref, nt_ref, wrev_ref, a16re_ref, a16im_ref):
    a_re = are_ref[0]
    a_im = aim_ref[0]
    dt = jnp.exp(ldt_ref[0])
    rho = a_re * dt
    th = a_im * dt
    kk = jnp.minimum(lax.broadcasted_iota(jnp.int32, (POW_ROWS, STATE_BLOCK), 0), S5_CHUNK).astype(F32)
    mag = jnp.exp(kk * rho)
    pw_re = mag * jnp.cos(kk * th)
    pw_im = mag * jnp.sin(kk * th)

    lb_re = pw_re[1:2]
    lb_im = pw_im[1:2]
    num_re = lb_re - 1.0
    num_im = lb_im
    den = a_re * a_re + a_im * a_im
    coef_re = (num_re * a_re + num_im * a_im) / den
    coef_im = (num_im * a_re - num_re * a_im) / den
    b_re = bre_ref[0]
    b_im = bim_ref[0]
    bb_re = coef_re * b_re - coef_im * b_im
    bb_im = coef_re * b_im + coef_im * b_re

    rows = lax.broadcasted_iota(jnp.int32, (LANES, STATE_BLOCK), 0)
    cols = lax.broadcasted_iota(jnp.int32, (LANES, STATE_BLOCK), 1)
    same_group = (rows >> GROUP_SHIFT) == (cols >> STATE_SHIFT)

    def blockdiag(x16):
        return jnp.where(same_group, jnp.concatenate([x16] * GROUPS_PER_BLOCK, axis=0), 0.0)

    for s in range(S5_CHUNK):
        k = S5_CHUNK - 1 - s
        p_re = pw_re[k:k + 1]
        p_im = pw_im[k:k + 1]
        msre_ref[0, s * LANES:(s + 1) * LANES, :] = blockdiag(p_re * bb_re - p_im * bb_im).astype(BF16)
        msim_ref[0, s * LANES:(s + 1) * LANES, :] = blockdiag(p_re * bb_im + p_im * bb_re).astype(BF16)

    b_hi, b_lo = _split_bf16(jnp.concatenate([blockdiag(bb_re), blockdiag(bb_im)], axis=1))
    c_re = cre_ref[0]
    c_im = cim_ref[0]
    for k in range(S5_CHUNK + 1):
        p_re = pw_re[k:k + 1]
        p_im = pw_im[k:k + 1]
        cl = jnp.concatenate([blockdiag(c_re * p_re - c_im * p_im),
                              -blockdiag(c_re * p_im + c_im * p_re)], axis=1)
        if k >= 1:
            nt_ref[0, (k - 1) * LANES:k * LANES, :] = cl.astype(BF16)
        if k < S5_CHUNK:
            c_hi, c_lo = _split_bf16(cl)
            nt_dims = (((1,), (1,)), ((), ()))
            wl = (lax.dot_general(b_hi, c_hi, nt_dims, preferred_element_type=F32)
                  + lax.dot_general(b_hi, c_lo, nt_dims, preferred_element_type=F32)
                  + lax.dot_general(b_lo, c_hi, nt_dims, preferred_element_type=F32))
            j = S5_CHUNK - 1 - k
            wrev_ref[0, j * LANES:(j + 1) * LANES, :LANES] = wl.astype(BF16)
            if j >= 1:
                wrev_ref[0, (j - 1) * LANES:j * LANES, LANES:] = wl.astype(BF16)
    wrev_ref[0, (S5_CHUNK - 1) * LANES:, LANES:] = jnp.zeros((LANES, LANES), BF16)

    a16re_ref[0] = pw_re[S5_CHUNK:S5_CHUNK + 1]
    a16im_ref[0] = pw_im[S5_CHUNK:S5_CHUNK + 1]


def _s5_prep(a_re, a_im, log_dt, b_re, b_im, c_re, c_im):
    nstate = SSM_GROUPS * SSM_STATE

    def lane_row(v):
        return v.reshape(GROUP_BLOCKS, 1, STATE_BLOCK)

    def rows16(v):
        return v.reshape(SSM_GROUP, GROUP_BLOCKS, STATE_BLOCK).transpose(1, 0, 2)

    ldt = jnp.repeat(log_dt, SSM_STATE).reshape(SSM_GROUPS, SSM_STATE)
    bt_re = rows16(b_re.transpose(2, 0, 1).reshape(SSM_GROUP, nstate))
    bt_im = rows16(b_im.transpose(2, 0, 1).reshape(SSM_GROUP, nstate))
    ct_re = rows16(c_re.transpose(1, 0, 2).reshape(SSM_GROUP, nstate))
    ct_im = rows16(c_im.transpose(1, 0, 2).reshape(SSM_GROUP, nstate))
    row_spec = pl.BlockSpec((1, 1, STATE_BLOCK), lambda g: (g, 0, 0))
    r16_spec = pl.BlockSpec((1, SSM_GROUP, STATE_BLOCK), lambda g: (g, 0, 0))
    return pl.pallas_call(
        _prep_kernel,
        grid=(GROUP_BLOCKS,),
        in_specs=[row_spec, row_spec, row_spec, r16_spec, r16_spec, r16_spec, r16_spec],
        out_specs=[
            pl.BlockSpec((1, S5_ROW, STATE_BLOCK), lambda g: (g, 0, 0)),
            pl.BlockSpec((1, S5_ROW, STATE_BLOCK), lambda g: (g, 0, 0)),
            pl.BlockSpec((1, S5_ROW, 2 * STATE_BLOCK), lambda g: (g, 0, 0)),
            pl.BlockSpec((1, S5_ROW, 2 * LANES), lambda g: (g, 0, 0)),
            row_spec, row_spec,
        ],
        out_shape=[
            jax.ShapeDtypeStruct((GROUP_BLOCKS, S5_ROW, STATE_BLOCK), BF16),
            jax.ShapeDtypeStruct((GROUP_BLOCKS, S5_ROW, STATE_BLOCK), BF16),
            jax.ShapeDtypeStruct((GROUP_BLOCKS, S5_ROW, 2 * STATE_BLOCK), BF16),
            jax.ShapeDtypeStruct((GROUP_BLOCKS, S5_ROW, 2 * LANES), BF16),
            jax.ShapeDtypeStruct((GROUP_BLOCKS, 1, STATE_BLOCK), F32),
            jax.ShapeDtypeStruct((GROUP_BLOCKS, 1, STATE_BLOCK), F32),
        ],
        compiler_params=_cparams("parallel"),
        name="s5_prep",
    )(lane_row(a_re), lane_row(a_im), lane_row(ldt), bt_re, bt_im, ct_re, ct_im)


def _rms(x, g):
    return x * lax.rsqrt(jnp.mean(x * x, axis=-1, keepdims=True) + EPS) * g


def _unsorted_ffn(zs_ref, prow_ref):
    tm = prow_ref.shape[1]
    zs = zs_ref[...].astype(BF16)
    pcol = jnp.concatenate([prow_ref[...].astype(F32), jnp.zeros((LANES - 1, tm), F32)], axis=0).T[:, :1]
    lanes = lax.broadcasted_iota(jnp.int32, (tm, SORT_ROWS), 1)
    pick = jnp.where(lanes == pcol.astype(jnp.int32), 1.0, 0.0).astype(BF16)
    return jnp.dot(pick, zs, preferred_element_type=F32)


def _mixer_in_kernel(x_ref, *refs, tm, nsub, fused, want_v):
    for j in range(nsub):
        _mixer_in_tile(j, slice(j * tm, (j + 1) * tm), x_ref, refs, tm, fused, want_v)


def _mixer_in_tile(j, r, x_ref, refs, tm, fused, want_v):
    if fused:
        zs_ref, prow_ref, gf_ref = refs[:3]
        refs = refs[3:]
    mod_ref, g_ref, win_ref, vg_ref, ws_ref, bs_ref, a_ref, u_ref = refs[:8]
    refs = refs[8:]
    x = x_ref[r, :]
    if fused:
        x = x + gf_ref[0] * _unsorted_ffn(zs_ref.at[j * SORT_ROWS:(j + 1) * SORT_ROWS, :], prow_ref.at[:, r])
        refs[0][r, :] = x
        refs = refs[1:]
    mod = mod_ref[0]
    shift = mod[:, :D_MODEL]
    scale = mod[:, D_MODEL:]
    h = _rms(x, g_ref[...]) * (1.0 + scale) + shift
    proj = jnp.dot(h.astype(BF16), win_ref[...], preferred_element_type=F32)
    z = jax.nn.gelu(proj[:, :2 * GMLP_WIDTH])
    u = z[:, :GMLP_WIDTH]
    v = z[:, GMLP_WIDTH:]
    vc = v - jnp.mean(v, axis=-1, keepdims=True)
    vn = vc * lax.rsqrt(jnp.mean(vc * vc, axis=-1, keepdims=True) + EPS) * vg_ref[...]
    if want_v:
        refs[0][r, :] = vn
    vb = vn.astype(BF16)
    bias = bs_ref[...]
    for c in range(tm // GMLP_CHUNK):
        r0 = c * GMLP_CHUNK
        o0 = j * tm + r0
        for hh in range(GMLP_HEADS):
            l0 = hh * GMLP_HEAD_DIM
            mixed = jnp.dot(ws_ref[hh], vb[r0:r0 + GMLP_CHUNK, l0:l0 + GMLP_HEAD_DIM],
                            preferred_element_type=F32) + bias[:, l0:l0 + GMLP_HEAD_DIM]
            a_ref[o0:o0 + GMLP_CHUNK, l0:l0 + GMLP_HEAD_DIM] = (
                u[r0:r0 + GMLP_CHUNK, l0:l0 + GMLP_HEAD_DIM] * mixed).astype(BF16)
    for gb in range(GROUP_BLOCKS):
        l0 = 2 * GMLP_WIDTH + gb * LANES
        u_ref[gb, r, :] = proj[:, l0:l0 + LANES]


def _mixer_in(x2d, mod, g_mix, w_in, v_gain, ws, bs, *, tm, tiles_per_mod, want_v, ffn=None):
    t = x2d.shape[0]
    rmod = mod.shape[1]
    nsub = 2 if tiles_per_mod % 2 == 0 else 1
    rows = tm * nsub
    mods = tiles_per_mod // nsub
    const2 = lambda i: (0, 0)
    tok = pl.BlockSpec((rows, D_MODEL), lambda i: (i, 0))
    in_specs = [tok]
    args = [x2d]
    if ffn is not None:
        in_specs += _ffn_specs(tm, nsub, rmod, mods)
        args += list(ffn)
    in_specs += [
        pl.BlockSpec((1, rmod, 2 * D_MODEL), lambda i: (i // mods, 0, 0)),
        pl.BlockSpec((1, D_MODEL), const2),
        pl.BlockSpec((D_MODEL, IN_WIDTH), const2),
        pl.BlockSpec((1, GMLP_WIDTH), const2),
        pl.BlockSpec((GMLP_HEADS, GMLP_CHUNK, GMLP_CHUNK), lambda i: (0, 0, 0)),
        pl.BlockSpec((GMLP_CHUNK, GMLP_WIDTH), const2),
    ]
    args += [mod, g_mix, w_in, v_gain, ws, bs]
    out_shape = [jax.ShapeDtypeStruct((t, GMLP_WIDTH), BF16),
                 jax.ShapeDtypeStruct((GROUP_BLOCKS, t, LANES), F32)]
    out_specs = [pl.BlockSpec((rows, GMLP_WIDTH), lambda i: (i, 0)),
                 pl.BlockSpec((GROUP_BLOCKS, rows, LANES), lambda i: (0, i, 0))]
    if ffn is not None:
        out_shape.append(jax.ShapeDtypeStruct((t, D_MODEL), F32))
        out_specs.append(tok)
    if want_v:
        out_shape.append(jax.ShapeDtypeStruct((t, GMLP_WIDTH), F32))
        out_specs.append(pl.BlockSpec((rows, GMLP_WIDTH), lambda i: (i, 0)))
    return pl.pallas_call(
        functools.partial(_mixer_in_kernel, tm=tm, nsub=nsub, fused=ffn is not None, want_v=want_v),
        grid=(t // rows,),
        in_specs=in_specs,
        out_specs=out_specs,
        out_shape=out_shape,
        compiler_params=_cparams("parallel"),
        name="mixer_in",
    )(*args)


def _ffn_specs(tm, nsub, rmod, mods):
    return [
        pl.BlockSpec((nsub * SORT_ROWS, D_MODEL), lambda i: (i, 0)),
        pl.BlockSpec((1, nsub * tm), lambda i: (0, i)),
        pl.BlockSpec((1, rmod, D_MODEL), lambda i: (i // mods, 0, 0)),
    ]


def _final_kernel(x_ref, zs_ref, prow_ref, gf_ref, g_ref, o_ref, *, tm, nsub):
    for j in range(nsub):
        r = slice(j * tm, (j + 1) * tm)
        x = x_ref[r, :] + gf_ref[0] * _unsorted_ffn(zs_ref.at[j * SORT_ROWS:(j + 1) * SORT_ROWS, :],
                                                    prow_ref.at[:, r])
        o_ref[r, :] = _rms(x, g_ref[...])


def _final(xmid, ffn, g_final, *, tm, tiles_per_mod):
    t = xmid.shape[0]
    rmod = ffn[2].shape[1]
    nsub = 2 if tiles_per_mod % 2 == 0 else 1
    tok = pl.BlockSpec((tm * nsub, D_MODEL), lambda i: (i, 0))
    return pl.pallas_call(
        functools.partial(_final_kernel, tm=tm, nsub=nsub),
        grid=(t // (tm * nsub),),
        in_specs=[tok] + _ffn_specs(tm, nsub, rmod, tiles_per_mod // nsub)
        + [pl.BlockSpec((1, D_MODEL), lambda i: (0, 0))],
        out_specs=tok,
        out_shape=jax.ShapeDtypeStruct((t, D_MODEL), F32),
        compiler_params=_cparams("parallel"),
        name="final_norm",
    )(xmid, *ffn, g_final)


def _s5_kernel(u_ref, us_ref, msre_ref, msim_ref, nt_ref, wrev_ref, a16re_ref, a16im_ref, d_ref,
               h0re_ref, h0im_ref, h0sre_ref, h0sim_ref,
               y_ref, ys_ref, hnre_ref, hnim_ref, hnsre_ref, hnsim_ref,
               sre_scr, sim_scr, hre_scr, him_scr):
    rows = u_ref.shape[1] // S5_CHUNK
    srows = us_ref.shape[1] // S5_CHUNK

    def chunk_rows(ref, n):
        return jnp.concatenate([ref[0, pl.ds(s, n, stride=S5_CHUNK), :] for s in range(S5_CHUNK)], axis=1)

    u = jnp.concatenate([chunk_rows(u_ref, rows), chunk_rows(us_ref, srows)], axis=0)
    ub = u.astype(BF16)
    s_re = jnp.dot(ub, msre_ref[0], preferred_element_type=F32)
    s_im = jnp.dot(ub, msim_ref[0], preferred_element_type=F32)
    a_re = a16re_ref[0]
    a_im = a16im_ref[0]
    sre_scr[...] = s_re[:rows]
    sim_scr[...] = s_im[:rows]

    def body(r, carry):
        hr, hi = carry
        hre_scr[pl.ds(r, 1), :] = hr
        him_scr[pl.ds(r, 1), :] = hi
        sr = sre_scr[pl.ds(r, 1), :]
        si = sim_scr[pl.ds(r, 1), :]
        return (a_re * hr - a_im * hi + sr, a_re * hi + a_im * hr + si)

    hn_re, hn_im = lax.fori_loop(0, rows, body, (h0re_ref[0, 0], h0im_ref[0, 0]), unroll=8)
    hnre_ref[0, 0] = hn_re
    hnim_ref[0, 0] = hn_im
    hs_re = h0sre_ref[0, 0]
    hs_im = h0sim_ref[0, 0]
    hnsre_ref[0, 0] = a_re * hs_re - a_im * hs_im + s_re[rows:]
    hnsim_ref[0, 0] = a_re * hs_im + a_im * hs_re + s_im[rows:]
    hcat = jnp.concatenate([jnp.concatenate([hre_scr[...], hs_re], axis=0),
                            jnp.concatenate([him_scr[...], hs_im], axis=0)], axis=1).astype(BF16)
    inter = lax.dot_general(hcat, nt_ref[0], (((1,), (1,)), ((), ())),
                            preferred_element_type=F32)
    d = d_ref[0]
    for t in range(0, S5_CHUNK, 2):
        k0 = (S5_CHUNK - 2 - t) * LANES
        pair = jnp.dot(ub[:, :(t + 2) * LANES], wrev_ref[0, k0:, :], preferred_element_type=F32)
        for step, intra in ((t + 1, pair[:, :LANES]), (t, pair[:, LANES:])):
            sl = slice(step * LANES, (step + 1) * LANES)
            out = intra + inter[:, sl] + d[:, sl] * u[:, sl]
            y_ref[0, pl.ds(step, rows, stride=S5_CHUNK), :] = out[:rows]
            ys_ref[0, pl.ds(step, srows, stride=S5_CHUNK), :] = out[rows:]


def _s5(u4, u4s, prep, d_tiled, h0_re, h0_im, h0s_re, h0s_im, *, rows, nblk):
    ms_re, ms_im, nt, wrev, a16_re, a16_im = prep
    srows = h0s_re.shape[2]
    wspec = lambda shape: pl.BlockSpec((1,) + shape, lambda g, b: (g, 0, 0))
    hspec = pl.BlockSpec((1, 1, 1, STATE_BLOCK), lambda g, b: (g, b, 0, 0))
    hsspec = pl.BlockSpec((1, 1, srows, STATE_BLOCK), lambda g, b: (g, 0, 0, 0))
    uspec = pl.BlockSpec((1, rows * S5_CHUNK, LANES), lambda g, b: (g, b, 0))
    usspec = pl.BlockSpec((1, srows * S5_CHUNK, LANES), lambda g, b: (g, 0, 0))
    return pl.pallas_call(
        _s5_kernel,
        grid=(GROUP_BLOCKS, nblk),
        in_specs=[
            uspec, usspec,
            wspec((S5_ROW, STATE_BLOCK)), wspec((S5_ROW, STATE_BLOCK)),
            wspec((S5_ROW, 2 * STATE_BLOCK)), wspec((S5_ROW, 2 * LANES)),
            wspec((1, STATE_BLOCK)), wspec((1, STATE_BLOCK)), wspec((1, S5_ROW)),
            hspec, hspec, hsspec, hsspec,
        ],
        out_specs=[uspec, usspec, hspec, hspec, hsspec, hsspec],
        out_shape=[
            jax.ShapeDtypeStruct(u4.shape, F32), jax.ShapeDtypeStruct(u4s.shape, F32),
            jax.ShapeDtypeStruct(h0_re.shape, F32), jax.ShapeDtypeStruct(h0_re.shape, F32),
            jax.ShapeDtypeStruct(h0s_re.shape, F32), jax.ShapeDtypeStruct(h0s_re.shape, F32),
        ],
        scratch_shapes=[pltpu.VMEM((rows, STATE_BLOCK), F32)] * 4,
        compiler_params=_cparams("parallel", "arbitrary"),
        name="s5",
    )(u4, u4s, ms_re, ms_im, nt, wrev, a16_re, a16_im, d_tiled, h0_re, h0_im, h0s_re, h0s_im)


def _split_bf16(x):
    hi = x.astype(BF16)
    return hi, (x - hi.astype(F32)).astype(BF16)


def _top2_of4(a):
    m1 = jnp.maximum(jnp.maximum(a[0], a[1]), jnp.maximum(a[2], a[3]))
    i1 = jnp.where(a[0] == m1, 0, jnp.where(a[1] == m1, 1, jnp.where(a[2] == m1, 2, 3)))
    b = [jnp.where(i1 == j, -jnp.inf, a[j]) for j in range(4)]
    m2 = jnp.maximum(jnp.maximum(b[0], b[1]), jnp.maximum(b[2], b[3]))
    i2 = jnp.where(b[0] == m2, 0, jnp.where(b[1] == m2, 1, jnp.where(b[2] == m2, 2, 3)))
    return m1, i1, m2, i2


def _route_rows(h2, wr_ref, br_ref):
    h_hi, h_lo = _split_bf16(h2)
    parts = jnp.dot(jnp.concatenate([h_hi, h_lo], axis=1), wr_ref[...], preferred_element_type=F32)
    pt = parts.T
    lt = pt[:N_EXPERTS] + pt[N_EXPERTS:2 * N_EXPERTS] + br_ref[...]
    rows = [lt[e:e + 1] for e in range(N_EXPERTS)]
    mx = functools.reduce(jnp.maximum, rows)
    ex = [jnp.exp(r - mx) for r in rows]
    tot = functools.reduce(lambda p, q: p + q, ex)
    scores = [e / tot for e in ex]
    best = None
    for g in range(N_EXPERT_GROUPS):
        m1, i1, m2, i2 = _top2_of4(scores[g * EXPERTS_PER_GROUP:(g + 1) * EXPERTS_PER_GROUP])
        cand = (m1 + m2, m1, i1 + g * EXPERTS_PER_GROUP, m2, i2 + g * EXPERTS_PER_GROUP)
        if best is None:
            best = cand
        else:
            better = cand[0] > best[0]
            best = tuple(jnp.where(better, c, b) for c, b in zip(cand, best))
    _, v1, e1, v2, e2 = best
    den = v1 + v2
    w1 = v1 / den
    w2 = v2 / den
    first_lo = e1 < e2
    return (jnp.where(first_lo, e1, e2), jnp.where(first_lo, e2, e1),
            jnp.where(first_lo, w1, w2), jnp.where(first_lo, w2, w1))


def _mixer_out_kernel(x_ref, a_ref, y_ref, mod_ref, g_ref, wglu_ref, bglu_ref, wout_ref,
                      wr_ref, br_ref, tri_ref, ltri_ref, *outs, tm, nsub):
    for j in range(nsub):
        _mixer_out_tile(j, slice(j * tm, (j + 1) * tm), x_ref, a_ref, y_ref, mod_ref, g_ref, wglu_ref,
                        bglu_ref, wout_ref, wr_ref, br_ref, tri_ref, ltri_ref, *outs)


def _mixer_out_tile(j, r, x_ref, a_ref, y_ref, mod_ref, g_ref, wglu_ref, bglu_ref, wout_ref,
                    wr_ref, br_ref, tri_ref, ltri_ref, xmid_ref, xs_ref, prow_ref, cnt_ref):
    x = x_ref[r, :]
    mod = mod_ref[0]
    gate_m = mod[:, :D_MODEL]
    shift_f = mod[:, D_MODEL:2 * D_MODEL]
    scale_f = mod[:, 2 * D_MODEL:]
    ys = jax.nn.gelu(jnp.concatenate([y_ref[gb, r, :] for gb in range(GROUP_BLOCKS)], axis=1))
    glu = jnp.dot(ys.astype(BF16), wglu_ref[...], preferred_element_type=F32) + bglu_ref[...]
    b_out = ys * jax.nn.sigmoid(glu)
    mixed = jnp.concatenate([a_ref[r, :], b_out.astype(BF16)], axis=1)
    xmid = x + gate_m * jnp.dot(mixed, wout_ref[...], preferred_element_type=F32)
    h2 = _rms(xmid, g_ref[...]) * (1.0 + scale_f) + shift_f
    lo, hi, glo, ghi = _route_rows(h2, wr_ref, br_ref)
    tm = x.shape[0]
    xmid_ref[r, :] = xmid
    a = lo & (EXPERTS_PER_GROUP - 1)
    b = hi & (EXPERTS_PER_GROUP - 1)
    pair = jnp.where(b - a == 3, 5, a + b - 1)
    cls = (lo >> EXPERT_GROUP_SHIFT) * PAIRS_PER_GROUP + pair
    onehot = lax.broadcasted_iota(jnp.int32, (CLASS_ROWS, tm), 0) == cls
    prefix = jnp.dot(jnp.where(onehot, 1.0, 0.0).astype(BF16), tri_ref[...],
                     preferred_element_type=F32)
    total = prefix[:, tm - 1:tm]
    cnt_ref[j] = jnp.broadcast_to(total, (CLASS_ROWS, LANES))
    groups = jnp.floor((total + (RUN_ALIGN - 1)) * (1.0 / RUN_ALIGN))
    before = jnp.dot(ltri_ref[...], jnp.broadcast_to(groups, (CLASS_ROWS, LANES)).astype(BF16),
                     preferred_element_type=F32)[:, :1] * RUN_ALIGN
    prow = jnp.sum(jnp.where(onehot, before + prefix - 1.0, 0.0), axis=0, keepdims=True).astype(jnp.int32)
    prow_ref[:, r] = prow
    pick = jnp.where(lax.broadcasted_iota(jnp.int32, (SORT_ROWS, tm), 0) == prow, 1.0, 0.0).astype(BF16)
    glo_hi, glo_lo = _split_bf16(glo)
    ghi_hi, ghi_lo = _split_bf16(ghi)
    gates = jnp.concatenate([glo_hi.astype(F32), glo_lo.astype(F32), ghi_hi.astype(F32), ghi_lo.astype(F32),
                             jnp.zeros((LANES - 4, tm), F32)], axis=0).T
    payload = jnp.concatenate([h2.astype(BF16), gates.astype(BF16)], axis=1)
    xs_ref[j * SORT_ROWS:(j + 1) * SORT_ROWS, :] = jnp.dot(pick, payload, preferred_element_type=F32)


def _mixer_out(x2d, a_out, y4, mod, g_ffn, w_glu, b_glu, w_out, wr, b_r, *, tm, tiles_per_mod):
    t = x2d.shape[0]
    rmod = mod.shape[1]
    nsub = 2 if tiles_per_mod % 2 == 0 else 1
    rows = tm * nsub
    const2 = lambda i: (0, 0)
    tok = pl.BlockSpec((rows, D_MODEL), lambda i: (i, 0))
    ids = jnp.arange(tm)
    cids = jnp.arange(CLASS_ROWS)
    return pl.pallas_call(
        functools.partial(_mixer_out_kernel, tm=tm, nsub=nsub),
        grid=(t // rows,),
        in_specs=[
            tok,
            pl.BlockSpec((rows, GMLP_WIDTH), lambda i: (i, 0)),
            pl.BlockSpec((GROUP_BLOCKS, rows, LANES), lambda i: (0, i, 0)),
            pl.BlockSpec((1, rmod, 3 * D_MODEL), lambda i: (i // (tiles_per_mod // nsub), 0, 0)),
            pl.BlockSpec((1, D_MODEL), const2),
            pl.BlockSpec((SSM_WIDTH, SSM_WIDTH), const2),
            pl.BlockSpec((1, SSM_WIDTH), const2),
            pl.BlockSpec((D_MODEL, D_MODEL), const2),
            pl.BlockSpec((2 * D_MODEL, LANES), const2),
            pl.BlockSpec((N_EXPERTS, 1), const2),
            pl.BlockSpec((tm, tm), const2),
            pl.BlockSpec((CLASS_ROWS, CLASS_ROWS), const2),
        ],
        out_specs=[tok, pl.BlockSpec((nsub * SORT_ROWS, XS_WIDTH), lambda i: (i, 0)),
                   pl.BlockSpec((1, rows), lambda i: (0, i)),
                   pl.BlockSpec((nsub, CLASS_ROWS, LANES), lambda i: (i, 0, 0))],
        out_shape=[jax.ShapeDtypeStruct((t, D_MODEL), F32),
                   jax.ShapeDtypeStruct((t // tm * SORT_ROWS, XS_WIDTH), F32),
                   jax.ShapeDtypeStruct((1, t), jnp.int32),
                   jax.ShapeDtypeStruct((t // tm, CLASS_ROWS, LANES), F32)],
        compiler_params=_cparams("parallel"),
        name="mixer_out",
    )(x2d, a_out, y4, mod, g_ffn, w_glu, b_glu, w_out, wr, b_r,
      (ids[:, None] <= ids[None, :]).astype(BF16), (cids[None, :] < cids[:, None]).astype(BF16))


def _moe_runs_kernel(tlo_ref, thi_ref, tval_ref, psrc_ref, pdst_ref, plen_ref, kfirst_ref, kend_ref, used_ref,
                     xsp_hbm, xss_hbm, wgl_ref, wul_ref, wdl_ref, wgh_ref, wuh_ref, wdh_ref,
                     zsp_hbm, zss_hbm, buf_ref, obuf_ref, sem_in, sem_out, *, nsrc, nsrc_p):
    i = pl.program_id(0)
    nt = pl.num_programs(0)
    slot = i % 2
    other = 1 - slot
    nv = tval_ref[i]
    nv_next = jnp.where(i + 1 < nt, tval_ref[jnp.minimum(i + 1, nt - 1)], 0)
    nv_prev = jnp.where(i >= 1, tval_ref[jnp.maximum(i - 1, 0)], 0)
    nv_prev2 = jnp.where(i >= 2, tval_ref[jnp.maximum(i - 2, 0)], 0)

    def rows8(v):
        return pl.multiple_of(v, RUN_ALIGN)

    def for_pieces(tile, fn):
        def piece(k, xs_hbm, zs_hbm):
            n = plen_ref[tile * nsrc + k]

            @pl.when(n > 0)
            def _():
                fn(xs_hbm, zs_hbm, rows8(psrc_ref[tile * nsrc + k]), rows8(pdst_ref[tile * nsrc + k]),
                   rows8(n))

        first = kfirst_ref[tile]
        end = kend_ref[tile]
        lax.fori_loop(first, jnp.minimum(end, nsrc_p), lambda k, c: (piece(k, xsp_hbm, zsp_hbm), c)[1], 0)
        lax.fori_loop(jnp.maximum(first, nsrc_p), end, lambda k, c: (piece(k, xss_hbm, zss_hbm), c)[1], 0)

    def gather_start(tile, sl):
        for_pieces(tile, lambda xs_hbm, zs_hbm, src, dst, n: pltpu.make_async_copy(
            xs_hbm.at[pl.ds(src, n), :], buf_ref.at[sl, pl.ds(dst, n), :], sem_in.at[sl]).start())

    def gather_wait(sl, n):
        pltpu.make_async_copy(xsp_hbm.at[pl.ds(0, rows8(n)), :], buf_ref.at[sl, pl.ds(0, rows8(n)), :],
                              sem_in.at[sl]).wait()

    def scatter_start(tile, sl):
        for_pieces(tile, lambda xs_hbm, zs_hbm, src, dst, n: pltpu.make_async_copy(
            obuf_ref.at[sl, pl.ds(dst, n), :], zs_hbm.at[pl.ds(src, n), :], sem_out.at[sl]).start())

    def scatter_wait(sl, n):
        pltpu.make_async_copy(obuf_ref.at[sl, pl.ds(0, rows8(n)), :], zsp_hbm.at[pl.ds(0, rows8(n)), :],
                              sem_out.at[sl]).wait()

    @pl.when(i == 0)
    def _():
        buf_ref[...] = jnp.zeros_like(buf_ref)
        gather_start(0, 0)
        obuf_ref[1] = jnp.zeros((MOE_TILE, D_MODEL), F32)

        def tail(zs_hbm, k0, k, start):
            for first in range(0, SORT_ROWS, MOE_TILE):
                lo_row = rows8(jnp.maximum(used_ref[k], first))
                n = rows8(jnp.maximum(jnp.minimum(first + MOE_TILE, SORT_ROWS) - lo_row, 0))
                copy = pltpu.make_async_copy(obuf_ref.at[1, pl.ds(0, n), :],
                                             zs_hbm.at[pl.ds(rows8((k - k0) * SORT_ROWS + lo_row), n), :],
                                             sem_out.at[1])

                @pl.when(n > 0)
                def _():
                    copy.start() if start else copy.wait()

        for start in (True, False):
            lax.fori_loop(0, nsrc_p, lambda k, c: (tail(zsp_hbm, 0, k, start), c)[1], 0)
            lax.fori_loop(nsrc_p, nsrc, lambda k, c: (tail(zss_hbm, nsrc_p, k, start), c)[1], 0)

    @pl.when(nv_next > 0)
    def _():
        gather_start(i + 1, other)

    @pl.when(nv_prev2 > 0)
    def _():
        scatter_wait(slot, nv_prev2)

    @pl.when(nv > 0)
    def _():
        gather_wait(slot, nv)
        buf = buf_ref[slot]
        h = buf[:, :D_MODEL].astype(BF16)
        glo = buf[:, D_MODEL:D_MODEL + 1] + buf[:, D_MODEL + 1:D_MODEL + 2]
        ghi = buf[:, D_MODEL + 2:D_MODEL + 3] + buf[:, D_MODEL + 3:D_MODEL + 4]
        ffn = None
        for wg, wu, wd, gate in ((wgl_ref, wul_ref, wdl_ref, glo), (wgh_ref, wuh_ref, wdh_ref, ghi)):
            he = (jax.nn.silu(jnp.dot(h, wg[0], preferred_element_type=F32))
                  * jnp.dot(h, wu[0], preferred_element_type=F32))
            y = gate * jnp.dot(he.astype(BF16), wd[0], preferred_element_type=F32)
            ffn = y if ffn is None else ffn + y
        obuf_ref[slot] = ffn.astype(BF16).astype(F32)
        scatter_start(i, slot)

    @pl.when(i == nt - 1)
    def _():
        @pl.when(nv_prev > 0)
        def _():
            scatter_wait(other, nv_prev)

        @pl.when(nv > 0)
        def _():
            scatter_wait(slot, nv)


def _moe_runs(xs_p, counts_p, xs_s, counts_s, wg, wu, wd, *, max_rows, expert0):
    counts = jnp.concatenate([counts_p, counts_s], axis=0)
    nmix = counts.shape[0]
    nmix_p = counts_p.shape[0]
    ntiles = -(-max_rows // MOE_TILE) + N_CLASSES
    n_kc = counts[:, :N_CLASSES, 0].astype(jnp.int32)
    len_kc = (n_kc + RUN_ALIGN - 1) // RUN_ALIGN * RUN_ALIGN
    off_kc = jnp.cumsum(len_kc, axis=1) - len_kc
    used_k = jnp.sum(len_kc, axis=1)
    start_kc = jnp.cumsum(len_kc, axis=0) - len_kc
    region_c = jnp.sum(len_kc, axis=0)
    tiles_c = (region_c + MOE_TILE - 1) // MOE_TILE
    tile_end = jnp.cumsum(tiles_c)
    tile_start = tile_end - tiles_c
    used = tile_end[-1]
    tid = jnp.arange(ntiles, dtype=jnp.int32)
    tcls = jnp.sum(tile_end[None, :] <= jnp.minimum(tid, used - 1)[:, None], axis=1).astype(jnp.int32)
    by_tile = jnp.stack([tile_start, region_c, jnp.asarray(CLASS_LO, jnp.int32),
                         jnp.asarray(CLASS_HI, jnp.int32)])[:, tcls]
    runs = jnp.stack([start_kc, len_kc, off_kc])[:, :, tcls]
    lo_row = (tid - by_tile[0]) * MOE_TILE
    tval = jnp.where(tid < used, jnp.clip(by_tile[1] - lo_row, 0, MOE_TILE), 0)
    run_lo = runs[0].T
    run_hi = run_lo + runs[1].T
    piece_lo = jnp.maximum(run_lo, lo_row[:, None])
    piece_hi = jnp.minimum(run_hi, (lo_row + tval)[:, None])
    plen = jnp.maximum(piece_hi - piece_lo, 0)
    kk = jnp.arange(nmix, dtype=jnp.int32)
    first_row = jnp.where(kk < nmix_p, kk, kk - nmix_p) * SORT_ROWS
    psrc = first_row[None, :] + runs[2].T + (piece_lo - run_lo)
    pdst = piece_lo - lo_row[:, None]
    flat = lambda v: jnp.where(plen > 0, v, 0).reshape(-1).astype(jnp.int32)
    tlo = by_tile[2]
    thi = by_tile[3]
    kfirst = jnp.sum(run_hi <= lo_row[:, None], axis=1).astype(jnp.int32)
    kend = jnp.sum(run_lo < (lo_row + tval)[:, None], axis=1).astype(jnp.int32)
    nprefetch = 9
    wspec_lo = lambda shape: pl.BlockSpec(shape, lambda i, tlo, *_: (expert0 + tlo[i], 0, 0))
    wspec_hi = lambda shape: pl.BlockSpec(shape, lambda i, tlo, thi, *_: (expert0 + thi[i], 0, 0))
    up = (1, D_MODEL, D_EXPERT)
    down = (1, D_EXPERT, D_MODEL)
    return pl.pallas_call(
        functools.partial(_moe_runs_kernel, nsrc=nmix, nsrc_p=nmix_p),
        grid_spec=pltpu.PrefetchScalarGridSpec(
            num_scalar_prefetch=nprefetch,
            grid=(ntiles,),
            in_specs=[
                pl.BlockSpec(memory_space=pl.ANY), pl.BlockSpec(memory_space=pl.ANY),
                wspec_lo(up), wspec_lo(up), wspec_lo(down),
                wspec_hi(up), wspec_hi(up), wspec_hi(down),
            ],
            out_specs=[pl.BlockSpec(memory_space=pl.ANY), pl.BlockSpec(memory_space=pl.ANY)],
            scratch_shapes=[
                pltpu.VMEM((2, MOE_TILE, XS_WIDTH), F32),
                pltpu.VMEM((2, MOE_TILE, D_MODEL), F32),
                pltpu.SemaphoreType.DMA((2,)),
                pltpu.SemaphoreType.DMA((2,)),
            ],
        ),
        out_shape=[jax.ShapeDtypeStruct((xs_p.shape[0], D_MODEL), F32),
                   jax.ShapeDtypeStruct((xs_s.shape[0], D_MODEL), F32)],
        compiler_params=_cparams("arbitrary"),
        name="moe_runs",
    )(tlo, thi, tval.astype(jnp.int32), flat(psrc), flat(pdst), flat(plen), kfirst, kend,
      used_k.astype(jnp.int32), xs_p, xs_s, wg, wu, wd, wg, wu, wd)


def _sorted_rows_bound(tm):
    return min(SORT_ROWS, tm + N_CLASSES * (RUN_ALIGN - 1))


def _state_out(h, nseq):
    return h.reshape(GROUP_BLOCKS, nseq, GROUPS_PER_BLOCK, SSM_STATE).transpose(1, 0, 2, 3).reshape(
        nseq, SSM_GROUPS, SSM_STATE)


def _state_in(h, nblk, rh):
    nseq = h.shape[0]
    return h.reshape(nseq, GROUP_BLOCKS, STATE_BLOCK).transpose(1, 0, 2).reshape(
        GROUP_BLOCKS, nblk, rh, STATE_BLOCK)


def kernel(x_prompt, x_sample, c_prompt, c_sample, state_s5_re, state_s5_im, w_ada, b_ada, g_norm_mix, g_norm_ffn, w_in, gmlp_v_gain, gmlp_w_spatial, gmlp_b_spatial, s5_a_re, s5_a_im, s5_log_dt, s5_b_re, s5_b_im, s5_c_re, s5_c_im, s5_d, s5_w_glu, s5_b_glu, w_out, w_router, b_router, w_gate, w_up, w_down, g_final):
    nb, seq_len, _ = x_prompt.shape
    ns, dec_len, _ = x_sample.shape
    assert dec_len == S5_CHUNK and ns * dec_len == GMLP_CHUNK and nb + ns <= ADA_ROWS
    assert seq_len % GMLP_CHUNK == 0 and MIX_TILE + N_CLASSES * (RUN_ALIGN - 1) <= SORT_ROWS

    c_all = jnp.concatenate([c_prompt, c_sample, jnp.zeros((ADA_ROWS - nb - ns, D_MODEL), F32)], axis=0)
    mod_all = _ada(c_all, w_ada, b_ada)

    pos = jnp.arange(GMLP_CHUNK)
    causal = (pos[None, :] // CHUNK) <= (pos[:, None] // CHUNK)
    wr_hi = w_router.astype(BF16)
    wr_lo = (w_router - wr_hi.astype(F32)).astype(BF16)
    wr = jnp.pad(jnp.concatenate([wr_hi, wr_lo], axis=1), ((0, 0), (0, LANES - 2 * N_EXPERTS)))
    wr = jnp.concatenate([wr, wr], axis=0)
    b_r = b_router.reshape(N_EXPERTS, 1)
    g_fin = g_final.reshape(1, D_MODEL)
    eye_s = jnp.eye(ns, dtype=F32)

    xp = x_prompt.reshape(nb * seq_len, D_MODEL)
    xs = x_sample.reshape(ns * dec_len, D_MODEL)
    zeros_p = jnp.zeros((GROUP_BLOCKS, nb, 1, STATE_BLOCK), F32)
    wg_all = w_gate.astype(BF16).reshape(DEPTH * N_EXPERTS, D_MODEL, D_EXPERT)
    wu_all = w_up.astype(BF16).reshape(DEPTH * N_EXPERTS, D_MODEL, D_EXPERT)
    wd_all = w_down.astype(BF16).reshape(DEPTH * N_EXPERTS, D_EXPERT, D_MODEL)
    sp_re, sp_im, ss_re, ss_im, v_new = [], [], [], [], []
    tm = min(MIX_TILE, seq_len)
    ts = ns * dec_len
    tiles_p = nb * seq_len // tm
    ffn_p = ffn_s = None
    for l in range(DEPTH):
        ws = jnp.where(causal[None], gmlp_w_spatial[l], 0.0)
        ws_sample = jnp.einsum("ab,hij->haibj", eye_s, ws[:, :dec_len, :dec_len]).reshape(
            GMLP_HEADS, GMLP_CHUNK, GMLP_CHUNK)
        bs = jnp.repeat(gmlp_b_spatial[l].T, GMLP_HEAD_DIM, axis=1)
        lw = dict(
            g_mix=g_norm_mix[l].reshape(1, D_MODEL), g_ffn=g_norm_ffn[l].reshape(1, D_MODEL),
            w_in=w_in[l].astype(BF16), v_gain=gmlp_v_gain[l].reshape(1, GMLP_WIDTH),
            ws=ws.astype(BF16), ws_sample=ws_sample.astype(BF16),
            bs=bs, bs_sample=jnp.tile(bs[:dec_len], (ns, 1)),
            prep=_s5_prep(s5_a_re[l], s5_a_im[l], s5_log_dt[l], s5_b_re[l], s5_b_im[l],
                          s5_c_re[l], s5_c_im[l]),
            d_tiled=jnp.tile(s5_d[l].reshape(GROUP_BLOCKS, 1, LANES), (1, 1, S5_CHUNK)),
            w_glu=s5_w_glu[l].astype(BF16), b_glu=s5_b_glu[l].reshape(1, SSM_WIDTH),
            w_out=w_out[l].astype(BF16), wr=wr, b_r=b_r,
        )
        mod_p = mod_all[l, :nb].reshape(nb, 1, 6 * D_MODEL)
        mod_s = jnp.repeat(mod_all[l, nb:nb + ns], dec_len, axis=0).reshape(1, ns * dec_len, 6 * D_MODEL)
        ins_p = _mixer_in(xp, mod_p[..., :2 * D_MODEL], lw["g_mix"], lw["w_in"], lw["v_gain"], lw["ws"],
                          lw["bs"], tm=tm, tiles_per_mod=seq_len // tm, want_v=False, ffn=ffn_p)
        ins_s = _mixer_in(xs, mod_s[..., :2 * D_MODEL], lw["g_mix"], lw["w_in"], lw["v_gain"], lw["ws_sample"],
                          lw["bs_sample"], tm=ts, tiles_per_mod=1, want_v=True, ffn=ffn_s)
        if l > 0:
            xp, xs = ins_p[2], ins_s[2]
        vs = ins_s[-1]
        y4_p, y4_s, hp_re, hp_im, hs_re, hs_im = _s5(
            ins_p[1], ins_s[1], lw["prep"], lw["d_tiled"], zeros_p, zeros_p,
            _state_in(state_s5_re[l], 1, ns), _state_in(state_s5_im[l], 1, ns),
            rows=seq_len // S5_CHUNK, nblk=nb)
        xp, xsort_p, prow_p, counts_p = _mixer_out(
            xp, ins_p[0], y4_p, mod_p[..., 2 * D_MODEL:5 * D_MODEL], lw["g_ffn"], lw["w_glu"], lw["b_glu"],
            lw["w_out"], lw["wr"], lw["b_r"], tm=tm, tiles_per_mod=seq_len // tm)
        xs, xsort_s, prow_s, counts_s = _mixer_out(
            xs, ins_s[0], y4_s, mod_s[..., 2 * D_MODEL:5 * D_MODEL], lw["g_ffn"], lw["w_glu"], lw["b_glu"],
            lw["w_out"], lw["wr"], lw["b_r"], tm=ts, tiles_per_mod=1)
        gf_p, gf_s = mod_p[..., 5 * D_MODEL:], mod_s[..., 5 * D_MODEL:]
        zs_p, zs_s = _moe_runs(
            xsort_p, counts_p, xsort_s, counts_s, wg_all, wu_all, wd_all, expert0=l * N_EXPERTS,
            max_rows=tiles_p * _sorted_rows_bound(tm) + _sorted_rows_bound(ts))
        ffn_p = (zs_p, prow_p, gf_p)
        ffn_s = (zs_s, prow_s, gf_s)
        sp_re.append(_state_out(hp_re, nb))
        sp_im.append(_state_out(hp_im, nb))
        ss_re.append(_state_out(hs_re, ns))
        ss_im.append(_state_out(hs_im, ns))
        v_new.append(vs.reshape(ns, dec_len, GMLP_WIDTH))
    yp = _final(xp, ffn_p, g_fin, tm=tm, tiles_per_mod=seq_len // tm)
    ys = _final(xs, ffn_s, g_fin, tm=ts, tiles_per_mod=1)
    return (yp.reshape(nb, seq_len, D_MODEL), ys.reshape(ns, dec_len, D_MODEL),
            jnp.stack(sp_re), jnp.stack(sp_im), jnp.stack(ss_re), jnp.stack(ss_im), jnp.stack(v_new))
```

```python
import functools

import jax
import jax.numpy as jnp
from jax import lax
from jax.experimental import pallas as pl
from jax.experimental.pallas import tpu as pltpu

F32 = jnp.float32
BF16 = jnp.bfloat16

D_MODEL = 1024
DEPTH = 2
CHUNK = 64
GMLP_CHUNK = 128
GMLP_WIDTH = 512
GMLP_HEADS = 4
GMLP_HEAD_DIM = 128
SSM_WIDTH = 512
SSM_GROUP = 16
SSM_GROUPS = 32
SSM_STATE = 64
IN_WIDTH = 1536
N_EXPERTS = 16
EXPERTS_PER_GROUP = 4
N_EXPERT_GROUPS = 4
D_EXPERT = 512
EPS = 1e-6

LANES = 128
S5_CHUNK = 16
GROUP_BLOCKS = 4
GROUPS_PER_BLOCK = SSM_GROUPS // GROUP_BLOCKS
STATE_BLOCK = GROUPS_PER_BLOCK * SSM_STATE
S5_ROW = S5_CHUNK * LANES
ADA_ROWS = 16
ADA_BLOCKS = 6
POW_ROWS = 24
GROUP_SHIFT = SSM_GROUP.bit_length() - 1
STATE_SHIFT = SSM_STATE.bit_length() - 1
EXPERT_GROUP_SHIFT = EXPERTS_PER_GROUP.bit_length() - 1
PAIRS_PER_GROUP = 6
N_CLASSES = N_EXPERT_GROUPS * PAIRS_PER_GROUP
CLASS_ROWS = 32
_PAIRS = [(0, 1), (0, 2), (1, 2), (1, 3), (2, 3), (0, 3)]
assert EXPERTS_PER_GROUP == 4 and len(_PAIRS) == PAIRS_PER_GROUP
CLASS_LO = [g * EXPERTS_PER_GROUP + a for g in range(N_EXPERT_GROUPS) for a, _ in _PAIRS]
CLASS_HI = [g * EXPERTS_PER_GROUP + b for g in range(N_EXPERT_GROUPS) for _, b in _PAIRS]
MOE_TILE = 384
MIX_TILE = 512
RUN_ALIGN = 8
SORT_ROWS = 768
XS_WIDTH = D_MODEL + LANES
VMEM_LIMIT = 56 * 1024 * 1024


def _cparams(*sem):
    return pltpu.CompilerParams(dimension_semantics=sem, vmem_limit_bytes=VMEM_LIMIT)


def _ada_kernel(c_ref, w_ref, b_ref, o_ref):
    c = c_ref[...]
    s = (c * jax.nn.sigmoid(c)).astype(BF16)
    o_ref[0] = jnp.dot(s, w_ref[0].astype(BF16), preferred_element_type=F32) + b_ref[0]


def _ada(c_all, w_ada, b_ada):
    return pl.pallas_call(
        _ada_kernel,
        grid=(DEPTH, ADA_BLOCKS),
        in_specs=[
            pl.BlockSpec((ADA_ROWS, D_MODEL), lambda l, j: (0, 0)),
            pl.BlockSpec((1, D_MODEL, D_MODEL), lambda l, j: (l, 0, j)),
            pl.BlockSpec((1, 1, D_MODEL), lambda l, j: (l, 0, j)),
        ],
        out_specs=pl.BlockSpec((1, ADA_ROWS, D_MODEL), lambda l, j: (l, 0, j)),
        out_shape=jax.ShapeDtypeStruct((DEPTH, ADA_ROWS, 6 * D_MODEL), F32),
        compiler_params=_cparams("parallel", "parallel"),
        name="ada",
    )(c_all, w_ada, b_ada.reshape(DEPTH, 1, 6 * D_MODEL))


def _prep_kernel(are_ref, aim_ref, ldt_ref, bre_ref, bim_ref, cre_ref, cim_ref,
                 msre_ref, msim_ref, nt_ref, wrev_ref, a16re_ref, a16im_ref):
    a_re = are_ref[0]
    a_im = aim_ref[0]
    dt = jnp.exp(ldt_ref[0])
    rho = a_re * dt
    th = a_im * dt
    kk = jnp.minimum(lax.broadcasted_iota(jnp.int32, (POW_ROWS, STATE_BLOCK), 0), S5_CHUNK).astype(F32)
    mag = jnp.exp(kk * rho)
    pw_re = mag * jnp.cos(kk * th)
    pw_im = mag * jnp.sin(kk * th)

    lb_re = pw_re[1:2]
    lb_im = pw_im[1:2]
    num_re = lb_re - 1.0
    num_im = lb_im
    den = a_re * a_re + a_im * a_im
    coef_re = (num_re * a_re + num_im * a_im) / den
    coef_im = (num_im * a_re - num_re * a_im) / den
    b_re = bre_ref[0]
    b_im = bim_ref[0]
    bb_re = coef_re * b_re - coef_im * b_im
    bb_im = coef_re * b_im + coef_im * b_re

    rows = lax.broadcasted_iota(jnp.int32, (LANES, STATE_BLOCK), 0)
    cols = lax.broadcasted_iota(jnp.int32, (LANES, STATE_BLOCK), 1)
    same_group = (rows >> GROUP_SHIFT) == (cols >> STATE_SHIFT)

    def blockdiag(x16):
        return jnp.where(same_group, jnp.concatenate([x16] * GROUPS_PER_BLOCK, axis=0), 0.0)

    for s in range(S5_CHUNK):
        k = S5_CHUNK - 1 - s
        p_re = pw_re[k:k + 1]
        p_im = pw_im[k:k + 1]
        msre_ref[0, s * LANES:(s + 1) * LANES, :] = blockdiag(p_re * bb_re - p_im * bb_im).astype(BF16)
        msim_ref[0, s * LANES:(s + 1) * LANES, :] = blockdiag(p_re * bb_im + p_im * bb_re).astype(BF16)

    b_hi, b_lo = _split_bf16(jnp.concatenate([blockdiag(bb_re), blockdiag(bb_im)], axis=1))
    c_re = cre_ref[0]
    c_im = cim_ref[0]
    for k in range(S5_CHUNK + 1):
        p_re = pw_re[k:k + 1]
        p_im = pw_im[k:k + 1]
        cl = jnp.concatenate([blockdiag(c_re * p_re - c_im * p_im),
                              -blockdiag(c_re * p_im + c_im * p_re)], axis=1)
        if k >= 1:
            nt_ref[0, (k - 1) * LANES:k * LANES, :] = cl.astype(BF16)
        if k < S5_CHUNK:
            c_hi, c_lo = _split_bf16(cl)
            nt_dims = (((1,), (1,)), ((), ()))
            wl = (lax.dot_general(b_hi, c_hi, nt_dims, preferred_element_type=F32)
                  + lax.dot_general(b_hi, c_lo, nt_dims, preferred_element_type=F32)
                  + lax.dot_general(b_lo, c_hi, nt_dims, preferred_element_type=F32))
            j = S5_CHUNK - 1 - k
            wrev_ref[0, j * LANES:(j + 1) * LANES, :LANES] = wl.astype(BF16)
            if j >= 1:
                wrev_ref[0, (j - 1) * LANES:j * LANES, LANES:] = wl.astype(BF16)
    wrev_ref[0, (S5_CHUNK - 1) * LANES:, LANES:] = jnp.zeros((LANES, LANES), BF16)

    a16re_ref[0] = pw_re[S5_CHUNK:S5_CHUNK + 1]
    a16im_ref[0] = pw_im[S5_CHUNK:S5_CHUNK + 1]


def _s5_prep(a_re, a_im, log_dt, b_re, b_im, c_re, c_im):
    nstate = SSM_GROUPS * SSM_STATE

    def lane_row(v):
        return v.reshape(GROUP_BLOCKS, 1, STATE_BLOCK)

    def rows16(v):
        return v.reshape(SSM_GROUP, GROUP_BLOCKS, STATE_BLOCK).transpose(1, 0, 2)

    ldt = jnp.repeat(log_dt, SSM_STATE).reshape(SSM_GROUPS, SSM_STATE)
    bt_re = rows16(b_re.transpose(2, 0, 1).reshape(SSM_GROUP, nstate))
    bt_im = rows16(b_im.transpose(2, 0, 1).reshape(SSM_GROUP, nstate))
    ct_re = rows16(c_re.transpose(1, 0, 2).reshape(SSM_GROUP, nstate))
    ct_im = rows16(c_im.transpose(1, 0, 2).reshape(SSM_GROUP, nstate))
    row_spec = pl.BlockSpec((1, 1, STATE_BLOCK), lambda g: (g, 0, 0))
    r16_spec = pl.BlockSpec((1, SSM_GROUP, STATE_BLOCK), lambda g: (g, 0, 0))
    return pl.pallas_call(
        _prep_kernel,
        grid=(GROUP_BLOCKS,),
        in_specs=[row_spec, row_spec, row_spec, r16_spec, r16_spec, r16_spec, r16_spec],
        out_specs=[
            pl.BlockSpec((1, S5_ROW, STATE_BLOCK), lambda g: (g, 0, 0)),
            pl.BlockSpec((1, S5_ROW, STATE_BLOCK), lambda g: (g, 0, 0)),
            pl.BlockSpec((1, S5_ROW, 2 * STATE_BLOCK), lambda g: (g, 0, 0)),
            pl.BlockSpec((1, S5_ROW, 2 * LANES), lambda g: (g, 0, 0)),
            row_spec, row_spec,
        ],
        out_shape=[
            jax.ShapeDtypeStruct((GROUP_BLOCKS, S5_ROW, STATE_BLOCK), BF16),
            jax.ShapeDtypeStruct((GROUP_BLOCKS, S5_ROW, STATE_BLOCK), BF16),
            jax.ShapeDtypeStruct((GROUP_BLOCKS, S5_ROW, 2 * STATE_BLOCK), BF16),
            jax.ShapeDtypeStruct((GROUP_BLOCKS, S5_ROW, 2 * LANES), BF16),
            jax.ShapeDtypeStruct((GROUP_BLOCKS, 1, STATE_BLOCK), F32),
            jax.ShapeDtypeStruct((GROUP_BLOCKS, 1, STATE_BLOCK), F32),
        ],
        compiler_params=_cparams("parallel"),
        name="s5_prep",
    )(lane_row(a_re), lane_row(a_im), lane_row(ldt), bt_re, bt_im, ct_re, ct_im)


def _rms(x, g):
    return x * lax.rsqrt(jnp.mean(x * x, axis=-1, keepdims=True) + EPS) * g


def _unsorted_ffn(zs_ref, prow_ref):
    tm = prow_ref.shape[1]
    zs = zs_ref[...].astype(BF16)
    pcol = jnp.concatenate([prow_ref[...].astype(F32), jnp.zeros((LANES - 1, tm), F32)], axis=0).T[:, :1]
    lanes = lax.broadcasted_iota(jnp.int32, (tm, SORT_ROWS), 1)
    pick = jnp.where(lanes == pcol.astype(jnp.int32), 1.0, 0.0).astype(BF16)
    return jnp.dot(pick, zs, preferred_element_type=F32)


def _mixer_in_kernel(x_ref, *refs, tm, nsub, fused, want_v):
    for j in range(nsub):
        _mixer_in_tile(j, slice(j * tm, (j + 1) * tm), x_ref, refs, tm, fused, want_v)


def _mixer_in_tile(j, r, x_ref, refs, tm, fused, want_v):
    if fused:
        zs_ref, prow_ref, gf_ref = refs[:3]
        refs = refs[3:]
    mod_ref, g_ref, win_ref, vg_ref, ws_ref, bs_ref, a_ref, u_ref = refs[:8]
    refs = refs[8:]
    x = x_ref[r, :]
    if fused:
        x = x + gf_ref[0] * _unsorted_ffn(zs_ref.at[j * SORT_ROWS:(j + 1) * SORT_ROWS, :], prow_ref.at[:, r])
        refs[0][r, :] = x
        refs = refs[1:]
    mod = mod_ref[0]
    shift = mod[:, :D_MODEL]
    scale = mod[:, D_MODEL:]
    h = _rms(x, g_ref[...]) * (1.0 + scale) + shift
    proj = jnp.dot(h.astype(BF16), win_ref[...], preferred_element_type=F32)
    z = jax.nn.gelu(proj[:, :2 * GMLP_WIDTH])
    u = z[:, :GMLP_WIDTH]
    v = z[:, GMLP_WIDTH:]
    vc = v - jnp.mean(v, axis=-1, keepdims=True)
    vn = vc * lax.rsqrt(jnp.mean(vc * vc, axis=-1, keepdims=True) + EPS) * vg_ref[...]
    if want_v:
        refs[0][r, :] = vn
    vb = vn.astype(BF16)
    bias = bs_ref[...]
    for c in range(tm // GMLP_CHUNK):
        r0 = c * GMLP_CHUNK
        o0 = j * tm + r0
        for hh in range(GMLP_HEADS):
            l0 = hh * GMLP_HEAD_DIM
            mixed = jnp.dot(ws_ref[hh], vb[r0:r0 + GMLP_CHUNK, l0:l0 + GMLP_HEAD_DIM],
                            preferred_element_type=F32) + bias[:, l0:l0 + GMLP_HEAD_DIM]
            a_ref[o0:o0 + GMLP_CHUNK, l0:l0 + GMLP_HEAD_DIM] = (
                u[r0:r0 + GMLP_CHUNK, l0:l0 + GMLP_HEAD_DIM] * mixed).astype(BF16)
    for gb in range(GROUP_BLOCKS):
        l0 = 2 * GMLP_WIDTH + gb * LANES
        u_ref[gb, r, :] = proj[:, l0:l0 + LANES]


def _mixer_in(x2d, mod, g_mix, w_in, v_gain, ws, bs, *, tm, tiles_per_mod, want_v, ffn=None):
    t = x2d.shape[0]
    rmod = mod.shape[1]
    nsub = 2 if tiles_per_mod % 2 == 0 else 1
    rows = tm * nsub
    mods = tiles_per_mod // nsub
    const2 = lambda i: (0, 0)
    tok = pl.BlockSpec((rows, D_MODEL), lambda i: (i, 0))
    in_specs = [tok]
    args = [x2d]
    if ffn is not None:
        in_specs += _ffn_specs(tm, nsub, rmod, mods)
        args += list(ffn)
    in_specs += [
        pl.BlockSpec((1, rmod, 2 * D_MODEL), lambda i: (i // mods, 0, 0)),
        pl.BlockSpec((1, D_MODEL), const2),
        pl.BlockSpec((D_MODEL, IN_WIDTH), const2),
        pl.BlockSpec((1, GMLP_WIDTH), const2),
        pl.BlockSpec((GMLP_HEADS, GMLP_CHUNK, GMLP_CHUNK), lambda i: (0, 0, 0)),
        pl.BlockSpec((GMLP_CHUNK, GMLP_WIDTH), const2),
    ]
    args += [mod, g_mix, w_in, v_gain, ws, bs]
    out_shape = [jax.ShapeDtypeStruct((t, GMLP_WIDTH), BF16),
                 jax.ShapeDtypeStruct((GROUP_BLOCKS, t, LANES), F32)]
    out_specs = [pl.BlockSpec((rows, GMLP_WIDTH), lambda i: (i, 0)),
                 pl.BlockSpec((GROUP_BLOCKS, rows, LANES), lambda i: (0, i, 0))]
    if ffn is not None:
        out_shape.append(jax.ShapeDtypeStruct((t, D_MODEL), F32))
        out_specs.append(tok)
    if want_v:
        out_shape.append(jax.ShapeDtypeStruct((t, GMLP_WIDTH), F32))
        out_specs.append(pl.BlockSpec((rows, GMLP_WIDTH), lambda i: (i, 0)))
    return pl.pallas_call(
        functools.partial(_mixer_in_kernel, tm=tm, nsub=nsub, fused=ffn is not None, want_v=want_v),
        grid=(t // rows,),
        in_specs=in_specs,
        out_specs=out_specs,
        out_shape=out_shape,
        compiler_params=_cparams("parallel"),
        name="mixer_in",
    )(*args)


def _ffn_specs(tm, nsub, rmod, mods):
    return [
        pl.BlockSpec((nsub * SORT_ROWS, D_MODEL), lambda i: (i, 0)),
        pl.BlockSpec((1, nsub * tm), lambda i: (0, i)),
        pl.BlockSpec((1, rmod, D_MODEL), lambda i: (i // mods, 0, 0)),
    ]


def _final_kernel(x_ref, zs_ref, prow_ref, gf_ref, g_ref, o_ref, *, tm, nsub):
    for j in range(nsub):
        r = slice(j * tm, (j + 1) * tm)
        x = x_ref[r, :] + gf_ref[0] * _unsorted_ffn(zs_ref.at[j * SORT_ROWS:(j + 1) * SORT_ROWS, :],
                                                    prow_ref.at[:, r])
        o_ref[r, :] = _rms(x, g_ref[...])


def _final(xmid, ffn, g_final, *, tm, tiles_per_mod):
    t = xmid.shape[0]
    rmod = ffn[2].shape[1]
    nsub = 1
    tok = pl.BlockSpec((tm * nsub, D_MODEL), lambda i: (i, 0))
    return pl.pallas_call(
        functools.partial(_final_kernel, tm=tm, nsub=nsub),
        grid=(t // (tm * nsub),),
        in_specs=[tok] + _ffn_specs(tm, nsub, rmod, tiles_per_mod // nsub)
        + [pl.BlockSpec((1, D_MODEL), lambda i: (0, 0))],
        out_specs=tok,
        out_shape=jax.ShapeDtypeStruct((t, D_MODEL), F32),
        compiler_params=_cparams("parallel"),
        name="final_norm",
    )(xmid, *ffn, g_final)


def _s5_kernel(u_ref, us_ref, msre_ref, msim_ref, nt_ref, wrev_ref, a16re_ref, a16im_ref, d_ref,
               h0re_ref, h0im_ref, h0sre_ref, h0sim_ref,
               y_ref, ys_ref, hnre_ref, hnim_ref, hnsre_ref, hnsim_ref,
               sre_scr, sim_scr, hre_scr, him_scr):
    rows = u_ref.shape[1] // S5_CHUNK
    srows = us_ref.shape[1] // S5_CHUNK

    def chunk_rows(ref, n):
        return jnp.concatenate([ref[0, pl.ds(s, n, stride=S5_CHUNK), :] for s in range(S5_CHUNK)], axis=1)

    u = jnp.concatenate([chunk_rows(u_ref, rows), chunk_rows(us_ref, srows)], axis=0)
    ub = u.astype(BF16)
    s_re = jnp.dot(ub, msre_ref[0], preferred_element_type=F32)
    s_im = jnp.dot(ub, msim_ref[0], preferred_element_type=F32)
    a_re = a16re_ref[0]
    a_im = a16im_ref[0]
    sre_scr[...] = s_re[:rows]
    sim_scr[...] = s_im[:rows]

    def body(r, carry):
        hr, hi = carry
        hre_scr[pl.ds(r, 1), :] = hr
        him_scr[pl.ds(r, 1), :] = hi
        sr = sre_scr[pl.ds(r, 1), :]
        si = sim_scr[pl.ds(r, 1), :]
        return (a_re * hr - a_im * hi + sr, a_re * hi + a_im * hr + si)

    hn_re, hn_im = lax.fori_loop(0, rows, body, (h0re_ref[0, 0], h0im_ref[0, 0]), unroll=8)
    hnre_ref[0, 0] = hn_re
    hnim_ref[0, 0] = hn_im
    hs_re = h0sre_ref[0, 0]
    hs_im = h0sim_ref[0, 0]
    hnsre_ref[0, 0] = a_re * hs_re - a_im * hs_im + s_re[rows:]
    hnsim_ref[0, 0] = a_re * hs_im + a_im * hs_re + s_im[rows:]
    hcat = jnp.concatenate([jnp.concatenate([hre_scr[...], hs_re], axis=0),
                            jnp.concatenate([him_scr[...], hs_im], axis=0)], axis=1).astype(BF16)
    inter = lax.dot_general(hcat, nt_ref[0], (((1,), (1,)), ((), ())),
                            preferred_element_type=F32)
    d = d_ref[0]
    for t in range(0, S5_CHUNK, 2):
        k0 = (S5_CHUNK - 2 - t) * LANES
        pair = jnp.dot(ub[:, :(t + 2) * LANES], wrev_ref[0, k0:, :], preferred_element_type=F32)
        for step, intra in ((t + 1, pair[:, :LANES]), (t, pair[:, LANES:])):
            sl = slice(step * LANES, (step + 1) * LANES)
            out = intra + inter[:, sl] + d[:, sl] * u[:, sl]
            y_ref[0, pl.ds(step, rows, stride=S5_CHUNK), :] = out[:rows]
            ys_ref[0, pl.ds(step, srows, stride=S5_CHUNK), :] = out[rows:]


def _s5(u4, u4s, prep, d_tiled, h0_re, h0_im, h0s_re, h0s_im, *, rows, nblk):
    ms_re, ms_im, nt, wrev, a16_re, a16_im = prep
    srows = h0s_re.shape[2]
    wspec = lambda shape: pl.BlockSpec((1,) + shape, lambda g, b: (g, 0, 0))
    hspec = pl.BlockSpec((1, 1, 1, STATE_BLOCK), lambda g, b: (g, b, 0, 0))
    hsspec = pl.BlockSpec((1, 1, srows, STATE_BLOCK), lambda g, b: (g, 0, 0, 0))
    uspec = pl.BlockSpec((1, rows * S5_CHUNK, LANES), lambda g, b: (g, b, 0))
    usspec = pl.BlockSpec((1, srows * S5_CHUNK, LANES), lambda g, b: (g, 0, 0))
    return pl.pallas_call(
        _s5_kernel,
        grid=(GROUP_BLOCKS, nblk),
        in_specs=[
            uspec, usspec,
            wspec((S5_ROW, STATE_BLOCK)), wspec((S5_ROW, STATE_BLOCK)),
            wspec((S5_ROW, 2 * STATE_BLOCK)), wspec((S5_ROW, 2 * LANES)),
            wspec((1, STATE_BLOCK)), wspec((1, STATE_BLOCK)), wspec((1, S5_ROW)),
            hspec, hspec, hsspec, hsspec,
        ],
        out_specs=[uspec, usspec, hspec, hspec, hsspec, hsspec],
        out_shape=[
            jax.ShapeDtypeStruct(u4.shape, F32), jax.ShapeDtypeStruct(u4s.shape, F32),
            jax.ShapeDtypeStruct(h0_re.shape, F32), jax.ShapeDtypeStruct(h0_re.shape, F32),
            jax.ShapeDtypeStruct(h0s_re.shape, F32), jax.ShapeDtypeStruct(h0s_re.shape, F32),
        ],
        scratch_shapes=[pltpu.VMEM((rows, STATE_BLOCK), F32)] * 4,
        compiler_params=_cparams("parallel", "arbitrary"),
        name="s5",
    )(u4, u4s, ms_re, ms_im, nt, wrev, a16_re, a16_im, d_tiled, h0_re, h0_im, h0s_re, h0s_im)


def _split_bf16(x):
    hi = x.astype(BF16)
    return hi, (x - hi.astype(F32)).astype(BF16)


def _top2_of4(a):
    m1 = jnp.maximum(jnp.maximum(a[0], a[1]), jnp.maximum(a[2], a[3]))
    i1 = jnp.where(a[0] == m1, 0, jnp.where(a[1] == m1, 1, jnp.where(a[2] == m1, 2, 3)))
    b = [jnp.where(i1 == j, -jnp.inf, a[j]) for j in range(4)]
    m2 = jnp.maximum(jnp.maximum(b[0], b[1]), jnp.maximum(b[2], b[3]))
    i2 = jnp.where(b[0] == m2, 0, jnp.where(b[1] == m2, 1, jnp.where(b[2] == m2, 2, 3)))
    return m1, i1, m2, i2


def _route_rows(h2, wr_ref, br_ref):
    h_hi, h_lo = _split_bf16(h2)
    parts = jnp.dot(jnp.concatenate([h_hi, h_lo], axis=1), wr_ref[...], preferred_element_type=F32)
    pt = parts.T
    lt = pt[:N_EXPERTS] + pt[N_EXPERTS:2 * N_EXPERTS] + br_ref[...]
    rows = [lt[e:e + 1] for e in range(N_EXPERTS)]
    mx = functools.reduce(jnp.maximum, rows)
    ex = [jnp.exp(r - mx) for r in rows]
    tot = functools.reduce(lambda p, q: p + q, ex)
    scores = [e / tot for e in ex]
    best = None
    for g in range(N_EXPERT_GROUPS):
        m1, i1, m2, i2 = _top2_of4(scores[g * EXPERTS_PER_GROUP:(g + 1) * EXPERTS_PER_GROUP])
        cand = (m1 + m2, m1, i1 + g * EXPERTS_PER_GROUP, m2, i2 + g * EXPERTS_PER_GROUP)
        if best is None:
            best = cand
        else:
            better = cand[0] > best[0]
            best = tuple(jnp.where(better, c, b) for c, b in zip(cand, best))
    _, v1, e1, v2, e2 = best
    den = v1 + v2
    w1 = v1 / den
    w2 = v2 / den
    first_lo = e1 < e2
    return (jnp.where(first_lo, e1, e2), jnp.where(first_lo, e2, e1),
            jnp.where(first_lo, w1, w2), jnp.where(first_lo, w2, w1))


def _mixer_out_kernel(x_ref, a_ref, y_ref, mod_ref, g_ref, wglu_ref, bglu_ref, wout_ref,
                      wr_ref, br_ref, tri_ref, ltri_ref, *outs, tm, nsub):
    for j in range(nsub):
        _mixer_out_tile(j, slice(j * tm, (j + 1) * tm), x_ref, a_ref, y_ref, mod_ref, g_ref, wglu_ref,
                        bglu_ref, wout_ref, wr_ref, br_ref, tri_ref, ltri_ref, *outs)


def _mixer_out_tile(j, r, x_ref, a_ref, y_ref, mod_ref, g_ref, wglu_ref, bglu_ref, wout_ref,
                    wr_ref, br_ref, tri_ref, ltri_ref, xmid_ref, xs_ref, prow_ref, cnt_ref):
    x = x_ref[r, :]
    mod = mod_ref[0]
    gate_m = mod[:, :D_MODEL]
    shift_f = mod[:, D_MODEL:2 * D_MODEL]
    scale_f = mod[:, 2 * D_MODEL:]
    ys = jax.nn.gelu(jnp.concatenate([y_ref[gb, r, :] for gb in range(GROUP_BLOCKS)], axis=1))
    glu = jnp.dot(ys.astype(BF16), wglu_ref[...], preferred_element_type=F32) + bglu_ref[...]
    b_out = ys * jax.nn.sigmoid(glu)
    mixed = jnp.concatenate([a_ref[r, :], b_out.astype(BF16)], axis=1)
    xmid = x + gate_m * jnp.dot(mixed, wout_ref[...], preferred_element_type=F32)
    h2 = _rms(xmid, g_ref[...]) * (1.0 + scale_f) + shift_f
    lo, hi, glo, ghi = _route_rows(h2, wr_ref, br_ref)
    tm = x.shape[0]
    xmid_ref[r, :] = xmid
    a = lo & (EXPERTS_PER_GROUP - 1)
    b = hi & (EXPERTS_PER_GROUP - 1)
    pair = jnp.where(b - a == 3, 5, a + b - 1)
    cls = (lo >> EXPERT_GROUP_SHIFT) * PAIRS_PER_GROUP + pair
    onehot = lax.broadcasted_iota(jnp.int32, (CLASS_ROWS, tm), 0) == cls
    prefix = jnp.dot(jnp.where(onehot, 1.0, 0.0).astype(BF16), tri_ref[...],
                     preferred_element_type=F32)
    total = prefix[:, tm - 1:tm]
    cnt_ref[j] = jnp.broadcast_to(total, (CLASS_ROWS, LANES))
    groups = jnp.floor((total + (RUN_ALIGN - 1)) * (1.0 / RUN_ALIGN))
    before = jnp.dot(ltri_ref[...], jnp.broadcast_to(groups, (CLASS_ROWS, LANES)).astype(BF16),
                     preferred_element_type=F32)[:, :1] * RUN_ALIGN
    prow = jnp.sum(jnp.where(onehot, before + prefix - 1.0, 0.0), axis=0, keepdims=True).astype(jnp.int32)
    prow_ref[:, r] = prow
    pick = jnp.where(lax.broadcasted_iota(jnp.int32, (SORT_ROWS, tm), 0) == prow, 1.0, 0.0).astype(BF16)
    glo_hi, glo_lo = _split_bf16(glo)
    ghi_hi, ghi_lo = _split_bf16(ghi)
    gates = jnp.concatenate([glo_hi.astype(F32), glo_lo.astype(F32), ghi_hi.astype(F32), ghi_lo.astype(F32),
                             jnp.zeros((LANES - 4, tm), F32)], axis=0).T
    payload = jnp.concatenate([h2.astype(BF16), gates.astype(BF16)], axis=1)
    xs_ref[j * SORT_ROWS:(j + 1) * SORT_ROWS, :] = jnp.dot(pick, payload, preferred_element_type=F32)


def _mixer_out(x2d, a_out, y4, mod, g_ffn, w_glu, b_glu, w_out, wr, b_r, *, tm, tiles_per_mod):
    t = x2d.shape[0]
    rmod = mod.shape[1]
    nsub = 2 if tiles_per_mod % 2 == 0 else 1
    rows = tm * nsub
    const2 = lambda i: (0, 0)
    tok = pl.BlockSpec((rows, D_MODEL), lambda i: (i, 0))
    ids = jnp.arange(tm)
    cids = jnp.arange(CLASS_ROWS)
    return pl.pallas_call(
        functools.partial(_mixer_out_kernel, tm=tm, nsub=nsub),
        grid=(t // rows,),
        in_specs=[
            tok,
            pl.BlockSpec((rows, GMLP_WIDTH), lambda i: (i, 0)),
            pl.BlockSpec((GROUP_BLOCKS, rows, LANES), lambda i: (0, i, 0)),
            pl.BlockSpec((1, rmod, 3 * D_MODEL), lambda i: (i // (tiles_per_mod // nsub), 0, 0)),
            pl.BlockSpec((1, D_MODEL), const2),
            pl.BlockSpec((SSM_WIDTH, SSM_WIDTH), const2),
            pl.BlockSpec((1, SSM_WIDTH), const2),
            pl.BlockSpec((D_MODEL, D_MODEL), const2),
            pl.BlockSpec((2 * D_MODEL, LANES), const2),
            pl.BlockSpec((N_EXPERTS, 1), const2),
            pl.BlockSpec((tm, tm), const2),
            pl.BlockSpec((CLASS_ROWS, CLASS_ROWS), const2),
        ],
        out_specs=[tok, pl.BlockSpec((nsub * SORT_ROWS, XS_WIDTH), lambda i: (i, 0)),
                   pl.BlockSpec((1, rows), lambda i: (0, i)),
                   pl.BlockSpec((nsub, CLASS_ROWS, LANES), lambda i: (i, 0, 0))],
        out_shape=[jax.ShapeDtypeStruct((t, D_MODEL), F32),
                   jax.ShapeDtypeStruct((t // tm * SORT_ROWS, XS_WIDTH), F32),
                   jax.ShapeDtypeStruct((1, t), jnp.int32),
                   jax.ShapeDtypeStruct((t // tm, CLASS_ROWS, LANES), F32)],
        compiler_params=_cparams("parallel"),
        name="mixer_out",
    )(x2d, a_out, y4, mod, g_ffn, w_glu, b_glu, w_out, wr, b_r,
      (ids[:, None] <= ids[None, :]).astype(BF16), (cids[None, :] < cids[:, None]).astype(BF16))


def _moe_runs_kernel(tlo_ref, thi_ref, tval_ref, psrc_ref, pdst_ref, plen_ref, kfirst_ref, kend_ref, used_ref,
                     xsp_hbm, xss_hbm, wgl_ref, wul_ref, wdl_ref, wgh_ref, wuh_ref, wdh_ref,
                     zsp_hbm, zss_hbm, buf_ref, obuf_ref, sem_in, sem_out, *, nsrc, nsrc_p):
    i = pl.program_id(0)
    nt = pl.num_programs(0)
    slot = i % 2
    other = 1 - slot
    nv = tval_ref[i]
    nv_next = jnp.where(i + 1 < nt, tval_ref[jnp.minimum(i + 1, nt - 1)], 0)
    nv_prev = jnp.where(i >= 1, tval_ref[jnp.maximum(i - 1, 0)], 0)
    nv_prev2 = jnp.where(i >= 2, tval_ref[jnp.maximum(i - 2, 0)], 0)

    def rows8(v):
        return pl.multiple_of(v, RUN_ALIGN)

    def for_pieces(tile, fn):
        def piece(k, xs_hbm, zs_hbm):
            n = plen_ref[tile * nsrc + k]

            @pl.when(n > 0)
            def _():
                fn(xs_hbm, zs_hbm, rows8(psrc_ref[tile * nsrc + k]), rows8(pdst_ref[tile * nsrc + k]),
                   rows8(n))

        first = kfirst_ref[tile]
        end = kend_ref[tile]
        lax.fori_loop(first, jnp.minimum(end, nsrc_p), lambda k, c: (piece(k, xsp_hbm, zsp_hbm), c)[1], 0)
        lax.fori_loop(jnp.maximum(first, nsrc_p), end, lambda k, c: (piece(k, xss_hbm, zss_hbm), c)[1], 0)

    def gather_start(tile, sl):
        for_pieces(tile, lambda xs_hbm, zs_hbm, src, dst, n: pltpu.make_async_copy(
            xs_hbm.at[pl.ds(src, n), :], buf_ref.at[sl, pl.ds(dst, n), :], sem_in.at[sl]).start())

    def gather_wait(sl, n):
        pltpu.make_async_copy(xsp_hbm.at[pl.ds(0, rows8(n)), :], buf_ref.at[sl, pl.ds(0, rows8(n)), :],
                              sem_in.at[sl]).wait()

    def scatter_start(tile, sl):
        for_pieces(tile, lambda xs_hbm, zs_hbm, src, dst, n: pltpu.make_async_copy(
            obuf_ref.at[sl, pl.ds(dst, n), :], zs_hbm.at[pl.ds(src, n), :], sem_out.at[sl]).start())

    def scatter_wait(sl, n):
        pltpu.make_async_copy(obuf_ref.at[sl, pl.ds(0, rows8(n)), :], zsp_hbm.at[pl.ds(0, rows8(n)), :],
                              sem_out.at[sl]).wait()

    @pl.when(i == 0)
    def _():
        buf_ref[...] = jnp.zeros_like(buf_ref)
        gather_start(0, 0)
        obuf_ref[1] = jnp.zeros((MOE_TILE, D_MODEL), F32)

        def tail(zs_hbm, k0, k, start):
            for first in range(0, SORT_ROWS, MOE_TILE):
                lo_row = rows8(jnp.maximum(used_ref[k], first))
                n = rows8(jnp.maximum(jnp.minimum(first + MOE_TILE, SORT_ROWS) - lo_row, 0))
                copy = pltpu.make_async_copy(obuf_ref.at[1, pl.ds(0, n), :],
                                             zs_hbm.at[pl.ds(rows8((k - k0) * SORT_ROWS + lo_row), n), :],
                                             sem_out.at[1])

                @pl.when(n > 0)
                def _():
                    copy.start() if start else copy.wait()

        for start in (True, False):
            lax.fori_loop(0, nsrc_p, lambda k, c: (tail(zsp_hbm, 0, k, start), c)[1], 0)
            lax.fori_loop(nsrc_p, nsrc, lambda k, c: (tail(zss_hbm, nsrc_p, k, start), c)[1], 0)

    @pl.when(nv_next > 0)
    def _():
        gather_start(i + 1, other)

    @pl.when(nv_prev2 > 0)
    def _():
        scatter_wait(slot, nv_prev2)

    @pl.when(nv > 0)
    def _():
        gather_wait(slot, nv)
        buf = buf_ref[slot]
        h = buf[:, :D_MODEL].astype(BF16)
        glo = buf[:, D_MODEL:D_MODEL + 1] + buf[:, D_MODEL + 1:D_MODEL + 2]
        ghi = buf[:, D_MODEL + 2:D_MODEL + 3] + buf[:, D_MODEL + 3:D_MODEL + 4]
        ffn = None
        for wg, wu, wd, gate in ((wgl_ref, wul_ref, wdl_ref, glo), (wgh_ref, wuh_ref, wdh_ref, ghi)):
            he = (jax.nn.silu(jnp.dot(h, wg[0], preferred_element_type=F32))
                  * jnp.dot(h, wu[0], preferred_element_type=F32))
            y = gate * jnp.dot(he.astype(BF16), wd[0], preferred_element_type=F32)
            ffn = y if ffn is None else ffn + y
        obuf_ref[slot] = ffn.astype(BF16).astype(F32)
        scatter_start(i, slot)

    @pl.when(i == nt - 1)
    def _():
        @pl.when(nv_prev > 0)
        def _():
            scatter_wait(other, nv_prev)

        @pl.when(nv > 0)
        def _():
            scatter_wait(slot, nv)


def _moe_runs(xs_p, counts_p, xs_s, counts_s, wg, wu, wd, *, max_rows, expert0):
    counts = jnp.concatenate([counts_p, counts_s], axis=0)
    nmix = counts.shape[0]
    nmix_p = counts_p.shape[0]
    ntiles = -(-max_rows // MOE_TILE) + N_CLASSES
    n_kc = counts[:, :N_CLASSES, 0].astype(jnp.int32)
    len_kc = (n_kc + RUN_ALIGN - 1) // RUN_ALIGN * RUN_ALIGN
    off_kc = jnp.cumsum(len_kc, axis=1) - len_kc
    used_k = jnp.sum(len_kc, axis=1)
    start_kc = jnp.cumsum(len_kc, axis=0) - len_kc
    region_c = jnp.sum(len_kc, axis=0)
    tiles_c = (region_c + MOE_TILE - 1) // MOE_TILE
    tile_end = jnp.cumsum(tiles_c)
    tile_start = tile_end - tiles_c
    used = tile_end[-1]
    tid = jnp.arange(ntiles, dtype=jnp.int32)
    tcls = jnp.sum(tile_end[None, :] <= jnp.minimum(tid, used - 1)[:, None], axis=1).astype(jnp.int32)
    by_tile = jnp.stack([tile_start, region_c, jnp.asarray(CLASS_LO, jnp.int32),
                         jnp.asarray(CLASS_HI, jnp.int32)])[:, tcls]
    runs = jnp.stack([start_kc, len_kc, off_kc])[:, :, tcls]
    lo_row = (tid - by_tile[0]) * MOE_TILE
    tval = jnp.where(tid < used, jnp.clip(by_tile[1] - lo_row, 0, MOE_TILE), 0)
    run_lo = runs[0].T
    run_hi = run_lo + runs[1].T
    piece_lo = jnp.maximum(run_lo, lo_row[:, None])
    piece_hi = jnp.minimum(run_hi, (lo_row + tval)[:, None])
    plen = jnp.maximum(piece_hi - piece_lo, 0)
    kk = jnp.arange(nmix, dtype=jnp.int32)
    first_row = jnp.where(kk < nmix_p, kk, kk - nmix_p) * SORT_ROWS
    psrc = first_row[None, :] + runs[2].T + (piece_lo - run_lo)
    pdst = piece_lo - lo_row[:, None]
    flat = lambda v: jnp.where(plen > 0, v, 0).reshape(-1).astype(jnp.int32)
    tlo = by_tile[2]
    thi = by_tile[3]
    kfirst = jnp.sum(run_hi <= lo_row[:, None], axis=1).astype(jnp.int32)
    kend = jnp.sum(run_lo < (lo_row + tval)[:, None], axis=1).astype(jnp.int32)
    nprefetch = 9
    wspec_lo = lambda shape: pl.BlockSpec(shape, lambda i, tlo, *_: (expert0 + tlo[i], 0, 0))
    wspec_hi = lambda shape: pl.BlockSpec(shape, lambda i, tlo, thi, *_: (expert0 + thi[i], 0, 0))
    up = (1, D_MODEL, D_EXPERT)
    down = (1, D_EXPERT, D_MODEL)
    return pl.pallas_call(
        functools.partial(_moe_runs_kernel, nsrc=nmix, nsrc_p=nmix_p),
        grid_spec=pltpu.PrefetchScalarGridSpec(
            num_scalar_prefetch=nprefetch,
            grid=(ntiles,),
            in_specs=[
                pl.BlockSpec(memory_space=pl.ANY), pl.BlockSpec(memory_space=pl.ANY),
                wspec_lo(up), wspec_lo(up), wspec_lo(down),
                wspec_hi(up), wspec_hi(up), wspec_hi(down),
            ],
            out_specs=[pl.BlockSpec(memory_space=pl.ANY), pl.BlockSpec(memory_space=pl.ANY)],
            scratch_shapes=[
                pltpu.VMEM((2, MOE_TILE, XS_WIDTH), F32),
                pltpu.VMEM((2, MOE_TILE, D_MODEL), F32),
                pltpu.SemaphoreType.DMA((2,)),
                pltpu.SemaphoreType.DMA((2,)),
            ],
        ),
        out_shape=[jax.ShapeDtypeStruct((xs_p.shape[0], D_MODEL), F32),
                   jax.ShapeDtypeStruct((xs_s.shape[0], D_MODEL), F32)],
        compiler_params=_cparams("arbitrary"),
        name="moe_runs",
    )(tlo, thi, tval.astype(jnp.int32), flat(psrc), flat(pdst), flat(plen), kfirst, kend,
      used_k.astype(jnp.int32), xs_p, xs_s, wg, wu, wd, wg, wu, wd)


def _sorted_rows_bound(tm):
    return min(SORT_ROWS, tm + N_CLASSES * (RUN_ALIGN - 1))


def _state_out(h, nseq):
    return h.reshape(GROUP_BLOCKS, nseq, GROUPS_PER_BLOCK, SSM_STATE).transpose(1, 0, 2, 3).reshape(
        nseq, SSM_GROUPS, SSM_STATE)


def _state_in(h, nblk, rh):
    nseq = h.shape[0]
    return h.reshape(nseq, GROUP_BLOCKS, STATE_BLOCK).transpose(1, 0, 2).reshape(
        GROUP_BLOCKS, nblk, rh, STATE_BLOCK)


def kernel(x_prompt, x_sample, c_prompt, c_sample, state_s5_re, state_s5_im, w_ada, b_ada, g_norm_mix, g_norm_ffn, w_in, gmlp_v_gain, gmlp_w_spatial, gmlp_b_spatial, s5_a_re, s5_a_im, s5_log_dt, s5_b_re, s5_b_im, s5_c_re, s5_c_im, s5_d, s5_w_glu, s5_b_glu, w_out, w_router, b_router, w_gate, w_up, w_down, g_final):
    nb, seq_len, _ = x_prompt.shape
    ns, dec_len, _ = x_sample.shape
    assert dec_len == S5_CHUNK and ns * dec_len == GMLP_CHUNK and nb + ns <= ADA_ROWS
    assert seq_len % GMLP_CHUNK == 0 and MIX_TILE + N_CLASSES * (RUN_ALIGN - 1) <= SORT_ROWS

    c_all = jnp.concatenate([c_prompt, c_sample, jnp.zeros((ADA_ROWS - nb - ns, D_MODEL), F32)], axis=0)
    mod_all = _ada(c_all, w_ada, b_ada)

    pos = jnp.arange(GMLP_CHUNK)
    causal = (pos[None, :] // CHUNK) <= (pos[:, None] // CHUNK)
    wr_hi = w_router.astype(BF16)
    wr_lo = (w_router - wr_hi.astype(F32)).astype(BF16)
    wr = jnp.pad(jnp.concatenate([wr_hi, wr_lo], axis=1), ((0, 0), (0, LANES - 2 * N_EXPERTS)))
    wr = jnp.concatenate([wr, wr], axis=0)
    b_r = b_router.reshape(N_EXPERTS, 1)
    g_fin = g_final.reshape(1, D_MODEL)
    eye_s = jnp.eye(ns, dtype=F32)

    xp = x_prompt.reshape(nb * seq_len, D_MODEL)
    xs = x_sample.reshape(ns * dec_len, D_MODEL)
    zeros_p = jnp.zeros((GROUP_BLOCKS, nb, 1, STATE_BLOCK), F32)
    wg_all = w_gate.astype(BF16).reshape(DEPTH * N_EXPERTS, D_MODEL, D_EXPERT)
    wu_all = w_up.astype(BF16).reshape(DEPTH * N_EXPERTS, D_MODEL, D_EXPERT)
    wd_all = w_down.astype(BF16).reshape(DEPTH * N_EXPERTS, D_EXPERT, D_MODEL)
    sp_re, sp_im, ss_re, ss_im, v_new = [], [], [], [], []
    tm = min(MIX_TILE, seq_len)
    ts = ns * dec_len
    tiles_p = nb * seq_len // tm
    ffn_p = ffn_s = None
    for l in range(DEPTH):
        ws = jnp.where(causal[None], gmlp_w_spatial[l], 0.0)
        ws_sample = jnp.einsum("ab,hij->haibj", eye_s, ws[:, :dec_len, :dec_len]).reshape(
            GMLP_HEADS, GMLP_CHUNK, GMLP_CHUNK)
        bs = jnp.repeat(gmlp_b_spatial[l].T, GMLP_HEAD_DIM, axis=1)
        lw = dict(
            g_mix=g_norm_mix[l].reshape(1, D_MODEL), g_ffn=g_norm_ffn[l].reshape(1, D_MODEL),
            w_in=w_in[l].astype(BF16), v_gain=gmlp_v_gain[l].reshape(1, GMLP_WIDTH),
            ws=ws.astype(BF16), ws_sample=ws_sample.astype(BF16),
            bs=bs, bs_sample=jnp.tile(bs[:dec_len], (ns, 1)),
            prep=_s5_prep(s5_a_re[l], s5_a_im[l], s5_log_dt[l], s5_b_re[l], s5_b_im[l],
                          s5_c_re[l], s5_c_im[l]),
            d_tiled=jnp.tile(s5_d[l].reshape(GROUP_BLOCKS, 1, LANES), (1, 1, S5_CHUNK)),
            w_glu=s5_w_glu[l].astype(BF16), b_glu=s5_b_glu[l].reshape(1, SSM_WIDTH),
            w_out=w_out[l].astype(BF16), wr=wr, b_r=b_r,
        )
        mod_p = mod_all[l, :nb].reshape(nb, 1, 6 * D_MODEL)
        mod_s = jnp.repeat(mod_all[l, nb:nb + ns], dec_len, axis=0).reshape(1, ns * dec_len, 6 * D_MODEL)
        ins_p = _mixer_in(xp, mod_p[..., :2 * D_MODEL], lw["g_mix"], lw["w_in"], lw["v_gain"], lw["ws"],
                          lw["bs"], tm=tm, tiles_per_mod=seq_len // tm, want_v=False, ffn=ffn_p)
        ins_s = _mixer_in(xs, mod_s[..., :2 * D_MODEL], lw["g_mix"], lw["w_in"], lw["v_gain"], lw["ws_sample"],
                          lw["bs_sample"], tm=ts, tiles_per_mod=1, want_v=True, ffn=ffn_s)
        if l > 0:
            xp, xs = ins_p[2], ins_s[2]
        vs = ins_s[-1]
        y4_p, y4_s, hp_re, hp_im, hs_re, hs_im = _s5(
            ins_p[1], ins_s[1], lw["prep"], lw["d_tiled"], zeros_p, zeros_p,
            _state_in(state_s5_re[l], 1, ns), _state_in(state_s5_im[l], 1, ns),
            rows=seq_len // S5_CHUNK, nblk=nb)
        xp, xsort_p, prow_p, counts_p = _mixer_out(
            xp, ins_p[0], y4_p, mod_p[..., 2 * D_MODEL:5 * D_MODEL], lw["g_ffn"], lw["w_glu"], lw["b_glu"],
            lw["w_out"], lw["wr"], lw["b_r"], tm=tm, tiles_per_mod=seq_len // tm)
        xs, xsort_s, prow_s, counts_s = _mixer_out(
            xs, ins_s[0], y4_s, mod_s[..., 2 * D_MODEL:5 * D_MODEL], lw["g_ffn"], lw["w_glu"], lw["b_glu"],
            lw["w_out"], lw["wr"], lw["b_r"], tm=ts, tiles_per_mod=1)
        gf_p, gf_s = mod_p[..., 5 * D_MODEL:], mod_s[..., 5 * D_MODEL:]
        zs_p, zs_s = _moe_runs(
            xsort_p, counts_p, xsort_s, counts_s, wg_all, wu_all, wd_all, expert0=l * N_EXPERTS,
            max_rows=tiles_p * _sorted_rows_bound(tm) + _sorted_rows_bound(ts))
        ffn_p = (zs_p, prow_p, gf_p)
        ffn_s = (zs_s, prow_s, gf_s)
        sp_re.append(_state_out(hp_re, nb))
        sp_im.append(_state_out(hp_im, nb))
        ss_re.append(_state_out(hs_re, ns))
        ss_im.append(_state_out(hs_im, ns))
        v_new.append(vs.reshape(ns, dec_len, GMLP_WIDTH))
    yp = _final(xp, ffn_p, g_fin, tm=tm, tiles_per_mod=seq_len // tm)
    ys = _final(xs, ffn_s, g_fin, tm=ts, tiles_per_mod=1)
    return (yp.reshape(nb, seq_len, D_MODEL), ys.reshape(ns, dec_len, D_MODEL),
            jnp.stack(sp_re), jnp.stack(sp_im), jnp.stack(ss_re), jnp.stack(ss_im), jnp.stack(v_new))
```
